```python
import jax
import jax.numpy as jnp
from jax import lax
import numpy as np

D_MODEL = 1024
BATCH = 16
SEQ = 256
DEPTH = 4
DEC_BATCH = 8
DEC_SEQ = 1024
PAST_LEN = 512

GRID_W = 64
D_HEAD = 64
ML_HEADS = D_MODEL // (4 * D_HEAD)
SWA_HEADS = (3 * D_MODEL) // (8 * D_HEAD)
SWA_KV_HEADS = SWA_HEADS // 3
NAT_HEADS = (3 * D_MODEL) // (8 * D_HEAD)
ML_W = ML_HEADS * D_HEAD
SWA_W = SWA_HEADS * D_HEAD
SWA_KV_W = SWA_KV_HEADS * D_HEAD
NAT_W = NAT_HEADS * D_HEAD
MIX_W = ML_W + SWA_W + NAT_W
N_GATES = 4 * ML_HEADS
IN_SIZES = (ML_W, ML_W, ML_W, ML_W, N_GATES, SWA_W, SWA_KV_W, SWA_KV_W, NAT_W, NAT_W, NAT_W)
IN_DIM = sum(IN_SIZES)
D_FF = 4 * D_MODEL
ML_CHUNK = 64
SWA_WINDOW = 128
SWA_BLOCK = 128
NAT_KH = 8
NAT_KW = 16
NAT_QB = 16
NAT_KC = NAT_QB + NAT_KW
ROPE_BASE = 10000.0
LN_EPS = 1e-5
DN_ALPHA = (2 * DEPTH) ** 0.25
DN_BETA = (8 * DEPTH) ** -0.25

kernel_name = 'hybrid_mlstm_swa_natten_flow_step'


def _layer_norm(x, g, b):
    xf = x.astype(jnp.float32)
    mu = jnp.mean(xf, axis=-1, keepdims=True)
    var = jnp.mean(jnp.square(xf - mu), axis=-1, keepdims=True)
    y = (xf - mu) * lax.rsqrt(var + LN_EPS) * g.astype(jnp.float32) + b.astype(jnp.float32)
    return y.astype(x.dtype)


def _heads(a, n_heads):
    bsz, t, _ = a.shape
    return a.reshape(bsz, t, n_heads, D_HEAD).transpose(0, 2, 1, 3)


def _merge(a):
    bsz, h, t, d = a.shape
    return a.transpose(0, 2, 1, 3).reshape(bsz, t, h * d)


def _rope_axis(x, pos):
    half = x.shape[-1] // 2
    freqs = ROPE_BASE ** (-jnp.arange(half, dtype=jnp.float32) / half)
    ang = pos.astype(jnp.float32)[:, None] * freqs[None, :]
    cos = jnp.cos(ang).astype(x.dtype)
    sin = jnp.sin(ang).astype(x.dtype)
    x1, x2 = x[..., :half], x[..., half:]
    return jnp.concatenate([x1 * cos - x2 * sin, x1 * sin + x2 * cos], axis=-1)


def _rope_2d(x):
    t = jnp.arange(x.shape[-2])
    half = x.shape[-1] // 2
    return jnp.concatenate([_rope_axis(x[..., :half], t // GRID_W),
                            _rope_axis(x[..., half:], t % GRID_W)], axis=-1)


def _softmax_with_sink(s, sink):
    s = s.astype(jnp.float32)
    sk = jnp.broadcast_to(sink.astype(jnp.float32), s.shape[:-1] + (1,))
    p = jax.nn.softmax(jnp.concatenate([s, sk], axis=-1), axis=-1)
    return p[..., :-1]


def _modulation(cvec, w_ada, b_ada):
    return jnp.split(jax.nn.silu(cvec) @ w_ada + b_ada, 6, axis=-1)


def _split_in(z):
    idx = [int(i) for i in np.cumsum(IN_SIZES)[:-1]]
    return jnp.split(z, idx, axis=-1)


def _mlstm_scan(q, k, v, li, lf, c0, n0, m0):
    bsz, nh, t, d = q.shape
    nc = t // ML_CHUNK

    def chunks(a):
        return jnp.moveaxis(a.reshape((bsz, nh, nc, ML_CHUNK) + a.shape[3:]), 2, 0)

    causal = jnp.tril(jnp.ones((ML_CHUNK, ML_CHUNK), dtype=bool))

    def step(carry, xs):
        c_st, n_st, m_st = carry
        qc, kc, vc, lic, lfc = xs
        b = jnp.cumsum(lfc, axis=-1)
        dmat = jnp.where(causal, b[..., :, None] - b[..., None, :] + lic[..., None, :], -jnp.inf)
        prev = b + m_st[..., None]
        mt = jnp.maximum(prev, jnp.max(dmat, axis=-1))
        wprev = jnp.exp(prev - mt)
        s = jnp.einsum('bhtd,bhsd->bhts', qc, kc) * jnp.exp(dmat - mt[..., None])
        num = s @ vc + wprev[..., None] * jnp.einsum('bhde,bhte->bhtd', c_st, qc)
        den = jnp.sum(s, axis=-1) + wprev * jnp.einsum('bhd,bhtd->bht', n_st, qc)
        h = num / jnp.maximum(jnp.abs(den), jnp.exp(-mt))[..., None]
        m_new = mt[..., -1]
        ws = jnp.exp(b[..., -1:] - b + lic - m_new[..., None])
        wc = wprev[..., -1]
        c_new = wc[..., None, None] * c_st + jnp.einsum('bhs,bhsd,bhse->bhde', ws, vc, kc)
        n_new = wc[..., None] * n_st + jnp.einsum('bhs,bhsd->bhd', ws, kc)
        return (c_new, n_new, m_new), h

    (c_f, n_f, m_f), hs = lax.scan(step, (c0, n0, m0), tuple(chunks(a) for a in (q, k, v, li, lf)))
    h = jnp.moveaxis(hs, 0, 2).reshape(bsz, nh, t, d)
    return h, c_f, n_f, m_f


def _mlstm_mixer(mq, mk, mv, mo, mg, fbias, norm_g, c0, n0, m0):
    out_dtype = mq.dtype
    bsz, t, _ = mq.shape
    f32 = jnp.float32
    q = _heads(mq.astype(f32), ML_HEADS)
    k = _heads(mk.astype(f32), ML_HEADS) * (D_HEAD ** -0.5)
    v = _heads(mv.astype(f32), ML_HEADS)
    g = mg.astype(f32).reshape(bsz, t, 4, ML_HEADS).transpose(2, 0, 3, 1)
    fb = fbias.astype(f32)
    li_f, lf_f = g[0], jax.nn.log_sigmoid(g[1] + fb[0][:, None])
    li_b, lf_b = g[2], jax.nn.log_sigmoid(g[3] + fb[1][:, None])
    c0, n0, m0 = c0.astype(f32), n0.astype(f32), m0.astype(f32)
    h_f, c_f, n_f, m_f = _mlstm_scan(q, k, v, li_f, lf_f, c0[:, 0], n0[:, 0], m0[:, 0])
    flip = lambda a: jnp.flip(a, axis=2)
    h_b, c_b, n_b, m_b = _mlstm_scan(flip(q), flip(k), flip(v), flip(li_b), flip(lf_b),
                                     c0[:, 1], n0[:, 1], m0[:, 1])
    h = h_f + flip(h_b)
    mu = jnp.mean(h, axis=-1, keepdims=True)
    var = jnp.mean(jnp.square(h - mu), axis=-1, keepdims=True)
    h = _merge((h - mu) * lax.rsqrt(var + LN_EPS))
    h = h * norm_g.astype(f32) * jax.nn.sigmoid(mo.astype(f32))
    return (h.astype(out_dtype), jnp.stack([c_f, c_b], axis=1),
            jnp.stack([n_f, n_b], axis=1), jnp.stack([m_f, m_b], axis=1))


def _swa_context(q, k, v, sink):
    bsz, hq, lq, d = q.shape
    grp = hq // SWA_KV_HEADS
    qg = q.reshape(bsz, SWA_KV_HEADS, grp, lq, d)
    s = jnp.einsum('bkgqd,bksd->bkgqs', qg, k) * (d ** -0.5)
    p = _softmax_with_sink(s, sink.reshape(1, SWA_KV_HEADS, grp, 1, 1)).astype(v.dtype)
    return jnp.einsum('bkgqs,bksd->bkgqd', p, v).reshape(bsz, hq, lq, d)


def _swa_latent(q, k, v, ck, cv, sink):
    bsz, hq, t, d = q.shape
    kv = SWA_KV_HEADS
    grp = hq // kv
    bl = SWA_BLOCK
    nb = t // bl
    scale = d ** -0.5
    qb = q.reshape(bsz, kv, grp, nb, bl, d)

    def band(a):
        ap = jnp.pad(a, ((0, 0), (0, 0), (bl, bl), (0, 0))).reshape(bsz, kv, nb + 2, bl, d)
        return jnp.concatenate([ap[:, :, 0:nb], ap[:, :, 1:nb + 1], ap[:, :, 2:nb + 2]], axis=3)

    kb, vb = band(k), band(v)
    qpos = np.arange(nb)[:, None, None] * bl + np.arange(bl)[None, :, None]
    kpos = np.arange(nb)[:, None, None] * bl - bl + np.arange(3 * bl)[None, None, :]
    valid = (np.abs(qpos - kpos) <= SWA_WINDOW) & (kpos >= 0) & (kpos < t)
    s_band = jnp.einsum('bkgnqd,bknsd->bkgnqs', qb, kb).astype(jnp.float32) * scale
    s_band = jnp.where(valid, s_band, -jnp.inf)
    s_ctx = jnp.einsum('bkgnqd,bkcd->bkgnqc', qb, ck).astype(jnp.float32) * scale
    p = _softmax_with_sink(jnp.concatenate([s_band, s_ctx], axis=-1),
                           sink.reshape(1, kv, grp, 1, 1, 1)).astype(v.dtype)
    o = (jnp.einsum('bkgnqs,bknsd->bkgnqd', p[..., :3 * bl], vb)
         + jnp.einsum('bkgnqc,bkcd->bkgnqd', p[..., 3 * bl:], cv))
    return o.reshape(bsz, hq, t, d)


def _dense_attention(q, k, v):
    s = jnp.einsum('bhqd,bhsd->bhqs', q, k) * (q.shape[-1] ** -0.5)
    p = jax.nn.softmax(s.astype(jnp.float32), axis=-1).astype(v.dtype)
    return jnp.einsum('bhqs,bhsd->bhqd', p, v)


def _nat_latent(q, k, v, ck, cv, rpb):
    bsz, nh, t, d = q.shape
    rows = t // GRID_W
    kh = min(NAT_KH, rows)
    ncb = GRID_W // NAT_QB
    scale = d ** -0.5
    r = np.arange(rows)
    rows_idx = np.clip(r - kh // 2, 0, rows - kh)[:, None] + np.arange(kh)[None, :]
    cb = np.arange(ncb)
    cols_idx = (np.clip(cb * NAT_QB - NAT_KW // 2, 0, GRID_W - NAT_KC)[:, None]
                + np.arange(NAT_KC)[None, :])
    qcol = cb[:, None] * NAT_QB + np.arange(NAT_QB)[None, :]
    cs = np.clip(qcol - NAT_KW // 2, 0, GRID_W - NAT_KW)
    kc3 = cols_idx[:, None, :]
    col_ok = (kc3 >= cs[..., None]) & (kc3 < cs[..., None] + NAT_KW)
    roff = rows_idx - r[:, None] + NAT_KH - 1
    coff = np.clip(kc3 - qcol[..., None] + NAT_KW - 1, 0, 2 * NAT_KW - 2)
    bias = rpb[:, roff[:, None, None, :, None], coff[None, :, :, None, :]]
    bias = jnp.where(col_ok[None, None, :, :, None, :], bias.astype(jnp.float32), -jnp.inf)
    qg = q.reshape(bsz, nh, rows, ncb, NAT_QB, d)
    ri = rows_idx[:, None, :, None]
    ci = cols_idx[None, :, None, :]
    kg = k.reshape(bsz, nh, rows, GRID_W, d)[:, :, ri, ci]
    vg = v.reshape(bsz, nh, rows, GRID_W, d)[:, :, ri, ci]
    s_nb = jnp.einsum('bhrcqd,bhrckwd->bhrcqkw', qg, kg).astype(jnp.float32) * scale + bias[None]
    n_nb = kh * NAT_KC
    s_nb = s_nb.reshape(bsz, nh, rows, ncb, NAT_QB, n_nb)
    s_ctx = jnp.einsum('bhrcqd,bhsd->bhrcqs', qg, ck).astype(jnp.float32) * scale
    p = jax.nn.softmax(jnp.concatenate([s_nb, s_ctx], axis=-1), axis=-1).astype(v.dtype)
    p_nb = p[..., :n_nb].reshape(bsz, nh, rows, ncb, NAT_QB, kh, NAT_KC)
    o = (jnp.einsum('bhrcqkw,bhrckwd->bhrcqd', p_nb, vg)
         + jnp.einsum('bhrcqs,bhsd->bhrcqd', p[..., n_nb:], cv))
    return o.reshape(bsz, nh, t, d)


def _residual_tail(x, mix_heads, g_a, sh_m, sc_m, g_m, w_out, ln1_g, ln1_b, w_mlp1, w_mlp2, ln2_g, ln2_b):
    mix = jnp.concatenate(mix_heads, axis=-1) @ w_out
    x = _layer_norm(DN_ALPHA * x + g_a * mix, ln1_g, ln1_b)
    h = x * (1.0 + sc_m) + sh_m
    f = jnp.square(jax.nn.relu(h @ w_mlp1)) @ w_mlp2
    return _layer_norm(DN_ALPHA * x + g_m * f, ln2_g, ln2_b)


def _context_layer(x, c_ctx, w_ada, b_ada, w_in, b_in, fbias, norm_g, sink,
                   w_out, ln1_g, ln1_b, w_mlp1, w_mlp2, ln2_g, ln2_b):
    sh_a, sc_a, g_a, sh_m, sc_m, g_m = _modulation(c_ctx, w_ada, b_ada)
    bsz = x.shape[0]
    h = x * (1.0 + sc_a) + sh_a
    mq, mk, mv, mo, mg, sq, sk, sv, nq, nk, nv = _split_in(h @ w_in + b_in)
    zc = jnp.zeros((bsz, 2, ML_HEADS, D_HEAD, D_HEAD), jnp.float32)
    zn = jnp.zeros((bsz, 2, ML_HEADS, D_HEAD), jnp.float32)
    zm = jnp.zeros((bsz, 2, ML_HEADS), jnp.float32)
    y_ml, st_c, st_n, st_m = _mlstm_mixer(mq, mk, mv, mo, mg, fbias, norm_g, zc, zn, zm)
    k_swa, v_swa = _heads(sk, SWA_KV_HEADS), _heads(sv, SWA_KV_HEADS)
    y_swa = _merge(_swa_context(_heads(sq, SWA_HEADS), k_swa, v_swa, sink))
    k_nat, v_nat = _heads(nk, NAT_HEADS), _heads(nv, NAT_HEADS)
    y_nat = _merge(_dense_attention(_heads(nq, NAT_HEADS), k_nat, v_nat))
    x = _residual_tail(x, [y_ml, y_swa, y_nat], g_a, sh_m, sc_m, g_m,
                       w_out, ln1_g, ln1_b, w_mlp1, w_mlp2, ln2_g, ln2_b)
    return x, (k_swa, v_swa, k_nat, v_nat, st_c, st_n, st_m)


def _latent_layer(x, c, ck_swa, cv_swa, ck_nat, cv_nat, st_c, st_n, st_m,
                  w_ada, b_ada, w_in, b_in, fbias, norm_g, sink, rpb,
                  w_out, ln1_g, ln1_b, w_mlp1, w_mlp2, ln2_g, ln2_b):
    mods = _modulation(c, w_ada, b_ada)
    sh_a, sc_a, g_a, sh_m, sc_m, g_m = [m[:, None, :] for m in mods]
    h = x * (1.0 + sc_a) + sh_a
    mq, mk, mv, mo, mg, sq, sk, sv, nq, nk, nv = _split_in(h @ w_in + b_in)
    y_ml, _, _, _ = _mlstm_mixer(mq, mk, mv, mo, mg, fbias, norm_g, st_c, st_n, st_m)
    y_swa = _merge(_swa_latent(_rope_2d(_heads(sq, SWA_HEADS)), _rope_2d(_heads(sk, SWA_KV_HEADS)),
                               _heads(sv, SWA_KV_HEADS), ck_swa, cv_swa, sink))
    y_nat = _merge(_nat_latent(_heads(nq, NAT_HEADS), _heads(nk, NAT_HEADS), _heads(nv, NAT_HEADS),
                               ck_nat, cv_nat, rpb))
    return _residual_tail(x, [y_ml, y_swa, y_nat], g_a, sh_m, sc_m, g_m,
                          w_out, ln1_g, ln1_b, w_mlp1, w_mlp2, ln2_g, ln2_b)


def setup_inputs(seed: int = 0) -> dict:
    key = jax.random.key(seed)
    ks = jax.random.split(key, 26)

    def nrm(k, shape, scale=1.0):
        return scale * jax.random.normal(k, shape, jnp.float32)

    d = D_MODEL
    return {
        'x_prompt': nrm(ks[0], (BATCH, SEQ, d)),
        'x_sample': nrm(ks[1], (DEC_BATCH, DEC_SEQ, d)),
        'cache_swa_k': nrm(ks[2], (DEC_BATCH, DEPTH, SWA_KV_HEADS, PAST_LEN, D_HEAD)),
        'cache_swa_v': nrm(ks[3], (DEC_BATCH, DEPTH, SWA_KV_HEADS, PAST_LEN, D_HEAD)),
        'cache_nat_k': nrm(ks[4], (DEC_BATCH, DEPTH, NAT_HEADS, PAST_LEN, D_HEAD)),
        'cache_nat_v': nrm(ks[5], (DEC_BATCH, DEPTH, NAT_HEADS, PAST_LEN, D_HEAD)),
        'state_mlstm_C': nrm(ks[6], (DEC_BATCH, DEPTH, 2, ML_HEADS, D_HEAD, D_HEAD), 0.1),
        'state_mlstm_n': nrm(ks[7], (DEC_BATCH, DEPTH, 2, ML_HEADS, D_HEAD), 0.5),
        'state_mlstm_m': nrm(ks[8], (DEC_BATCH, DEPTH, 2, ML_HEADS)),
        'c': nrm(ks[9], (DEC_BATCH, d)),
        'c_ctx': nrm(ks[10], (d,)),
        'w_ada': nrm(ks[11], (DEPTH, d, 6 * d), 0.5 * d ** -0.5),
        'b_ada': nrm(ks[12], (DEPTH, 6 * d), 0.02),
        'w_in': nrm(ks[13], (DEPTH, d, IN_DIM), d ** -0.5),
        'b_in': nrm(ks[14], (DEPTH, IN_DIM), 0.02),
        'mlstm_fbias': jnp.linspace(3.0, 6.0, ML_HEADS, dtype=jnp.float32) + nrm(ks[15], (DEPTH, 2, ML_HEADS), 0.1),
        'mlstm_norm_g': 1.0 + nrm(ks[16], (DEPTH, ML_W), 0.02),
        'swa_sink': nrm(ks[17], (DEPTH, SWA_HEADS), 0.5),
        'nat_rpb': nrm(ks[18], (DEPTH, NAT_HEADS, 2 * NAT_KH - 1, 2 * NAT_KW - 1), 0.1),
        'w_out': nrm(ks[19], (DEPTH, MIX_W, d), DN_BETA * MIX_W ** -0.5),
        'ln1_g': 1.0 + nrm(ks[20], (DEPTH, d), 0.02),
        'ln1_b': nrm(ks[21], (DEPTH, d), 0.02),
        'w_mlp1': nrm(ks[22], (DEPTH, d, D_FF), d ** -0.5),
        'w_mlp2': nrm(ks[23], (DEPTH, D_FF, d), DN_BETA * D_FF ** -0.5),
        'ln2_g': 1.0 + nrm(ks[24], (DEPTH, d), 0.02),
        'ln2_b': nrm(ks[25], (DEPTH, d), 0.02),
    }


def reference(x_prompt, x_sample, cache_swa_k, cache_swa_v, cache_nat_k, cache_nat_v,
              state_mlstm_C, state_mlstm_n, state_mlstm_m, c, c_ctx,
              w_ada, b_ada, w_in, b_in, mlstm_fbias, mlstm_norm_g, swa_sink, nat_rpb,
              w_out, ln1_g, ln1_b, w_mlp1, w_mlp2, ln2_g, ln2_b):
    xp = x_prompt
    xs = x_sample
    ks_l, vs_l, kn_l, vn_l, cs_l, ns_l, ms_l = [], [], [], [], [], [], []
    for l in range(DEPTH):
        xp, (k_swa, v_swa, k_nat, v_nat, st_c, st_n, st_m) = _context_layer(
            xp, c_ctx, w_ada[l], b_ada[l], w_in[l], b_in[l], mlstm_fbias[l], mlstm_norm_g[l],
            swa_sink[l], w_out[l], ln1_g[l], ln1_b[l], w_mlp1[l], w_mlp2[l], ln2_g[l], ln2_b[l])
        ks_l.append(k_swa)
        vs_l.append(v_swa)
        kn_l.append(k_nat)
        vn_l.append(v_nat)
        cs_l.append(st_c)
        ns_l.append(st_n)
        ms_l.append(st_m)
        xs = _latent_layer(
            xs, c, cache_swa_k[:, l], cache_swa_v[:, l], cache_nat_k[:, l], cache_nat_v[:, l],
            state_mlstm_C[:, l], state_mlstm_n[:, l], state_mlstm_m[:, l],
            w_ada[l], b_ada[l], w_in[l], b_in[l], mlstm_fbias[l], mlstm_norm_g[l], swa_sink[l], nat_rpb[l],
            w_out[l], ln1_g[l], ln1_b[l], w_mlp1[l], w_mlp2[l], ln2_g[l], ln2_b[l])
    new_swa_k = jnp.stack(ks_l, axis=1)
    new_swa_v = jnp.stack(vs_l, axis=1)
    new_nat_k = jnp.stack(kn_l, axis=1)
    new_nat_v = jnp.stack(vn_l, axis=1)
    new_mlstm_C = jnp.stack(cs_l, axis=1)
    new_mlstm_n = jnp.stack(ns_l, axis=1)
    new_mlstm_m = jnp.stack(ms_l, axis=1)
    return (xp, xs, new_swa_k, new_swa_v, new_nat_k, new_nat_v, new_mlstm_C, new_mlstm_n, new_mlstm_m)
```

```python
import functools

import numpy as np
import jax
import jax.numpy as jnp
from jax import lax
from jax.experimental import pallas as pl
from jax.experimental.pallas import tpu as pltpu

D_MODEL = 1024
BATCH = 16
SEQ = 256
DEPTH = 4
DEC_BATCH = 8
DEC_SEQ = 1024
PAST_LEN = 512
GRID_W = 64
D_HEAD = 64
ML_HEADS = 4
SWA_HEADS = 6
SWA_KV_HEADS = 2
SWA_GROUP = SWA_HEADS // SWA_KV_HEADS
NAT_HEADS = 6
ML_W = ML_HEADS * D_HEAD
SWA_W = SWA_HEADS * D_HEAD
SWA_KV_W = SWA_KV_HEADS * D_HEAD
NAT_W = NAT_HEADS * D_HEAD
N_GATES = 4 * ML_HEADS
D_FF = 4 * D_MODEL
SWA_WINDOW = 128
SWA_BLOCK = 128
NAT_KH = 8
NAT_KW = 16
ROPE_BASE = 10000.0
LN_EPS = 1e-5
DN_ALPHA = (2 * DEPTH) ** 0.25
ATT_SCALE = D_HEAD ** -0.5

LANES = 128
Z_ML_W = 4 * ML_W
Z_SWA_W = SWA_W + 2 * SWA_KV_W
Z_NAT_W = 3 * NAT_W
Z_G_W = LANES
Z_W = Z_ML_W + Z_SWA_W + Z_NAT_W + Z_G_W
MOD_ROWS = 16
CTX_MOD_ROW = DEC_BATCH
ROW_TILE = 256
ML_CHUNK = 256
VMEM_LIMIT = 56 * 1024 * 1024

BF16 = jnp.bfloat16
F32 = jnp.float32
NEG_INF = float("-inf")


def _dot(a, b):
    return jnp.dot(a.astype(BF16), b.astype(BF16), preferred_element_type=F32)


def _dot_nt(a, b):
    return lax.dot_general(a.astype(BF16), b.astype(BF16), (((1,), (1,)), ((), ())),
                           preferred_element_type=F32)


def _dot_split3(tri, x):
    hi = x.astype(BF16)
    r1 = x - hi.astype(F32)
    mid = r1.astype(BF16)
    lo = (r1 - mid.astype(F32)).astype(BF16)
    return (jnp.dot(tri, hi, preferred_element_type=F32)
            + jnp.dot(tri, mid, preferred_element_type=F32)
            + jnp.dot(tri, lo, preferred_element_type=F32))


def _sigmoid(x):
    return 1.0 / (1.0 + jnp.exp(-x))


def _log_sigmoid(x):
    return jnp.minimum(x, 0.0) - jnp.log1p(jnp.exp(-jnp.abs(x)))


def _layer_norm(x, g, b):
    mu = jnp.mean(x, axis=-1, keepdims=True)
    xc = x - mu
    var = jnp.mean(xc * xc, axis=-1, keepdims=True)
    return xc * lax.rsqrt(var + LN_EPS) * g + b


def _params(n_axes=1):
    return pltpu.CompilerParams(dimension_semantics=("arbitrary",) * n_axes,
                                vmem_limit_bytes=VMEM_LIMIT)


def _mods_kernel(c_ref, w_ref, b_ref, o_ref):
    c = c_ref[...]
    o_ref[...] = _dot(c * _sigmoid(c), w_ref[...]) + b_ref[...]


def _mods_call(cvec, w_ada, b_ada):
    n_col = 6
    return pl.pallas_call(
        _mods_kernel,
        grid=(DEPTH, n_col),
        in_specs=[
            pl.BlockSpec((MOD_ROWS, D_MODEL), lambda l, j: (0, 0)),
            pl.BlockSpec((None, D_MODEL, D_MODEL), lambda l, j: (l, 0, j)),
            pl.BlockSpec((None, 1, D_MODEL), lambda l, j: (l, 0, j)),
        ],
        out_specs=pl.BlockSpec((None, MOD_ROWS, D_MODEL), lambda l, j: (l, 0, j)),
        out_shape=jax.ShapeDtypeStruct((DEPTH, MOD_ROWS, 6 * D_MODEL), F32),
        compiler_params=_params(2),
        name="mods",
    )(cvec, w_ada, b_ada.reshape(DEPTH, 1, 6 * D_MODEL))


def _inproj_kernel(x_ref, mod_ref, w_ref, b_ref, zml_ref, zswa_ref, znat_ref, zg_ref):
    h = (x_ref[...] * (1.0 + mod_ref[1:2, :]) + mod_ref[0:1, :]).astype(BF16)
    o = 0
    for ref in (zml_ref, zswa_ref, znat_ref, zg_ref):
        w = ref.shape[-1]
        ref[...] = jnp.dot(h, w_ref[:, o:o + w], preferred_element_type=F32) + b_ref[:, o:o + w]
        o += w


def _inproj_call(x, mods_l, mod_row_of_tile, w_in, b_in):
    rows = x.shape[0]
    widths = (Z_ML_W, Z_SWA_W, Z_NAT_W, Z_G_W)
    return pl.pallas_call(
        _inproj_kernel,
        grid=(rows // ROW_TILE,),
        in_specs=[
            pl.BlockSpec((ROW_TILE, D_MODEL), lambda i: (i, 0)),
            pl.BlockSpec((None, 6, D_MODEL), lambda i: (mod_row_of_tile(i), 0, 0)),
            pl.BlockSpec((D_MODEL, Z_W), lambda i: (0, 0)),
            pl.BlockSpec((1, Z_W), lambda i: (0, 0)),
        ],
        out_specs=[pl.BlockSpec((ROW_TILE, w), lambda i: (i, 0)) for w in widths],
        out_shape=[jax.ShapeDtypeStruct((rows, w), F32) for w in widths],
        compiler_params=_params(1),
        name="inproj",
    )(x, mods_l, w_in, b_in)


def _tail_kernel(x_ref, mml_ref, mswa_ref, mnat_ref, mod_ref, wout_ref, ln1g_ref, ln1b_ref,
                 w1_ref, w2_ref, ln2g_ref, ln2b_ref, o_ref):
    g_a, sh_m, sc_m, g_m = (mod_ref[i:i + 1, :] for i in (2, 3, 4, 5))
    mix = jnp.concatenate([mml_ref[...], mswa_ref[...], mnat_ref[...]], axis=1)
    proj = jnp.dot(mix, wout_ref[...], preferred_element_type=F32)
    y = _layer_norm(DN_ALPHA * x_ref[...] + g_a * proj, ln1g_ref[...], ln1b_ref[...])
    h = (y * (1.0 + sc_m) + sh_m).astype(BF16)
    f = jnp.maximum(jnp.dot(h, w1_ref[...], preferred_element_type=F32), 0.0)
    f = jnp.dot((f * f).astype(BF16), w2_ref[...], preferred_element_type=F32)
    o_ref[...] = _layer_norm(DN_ALPHA * y + g_m * f, ln2g_ref[...], ln2b_ref[...])


def _tail_call(x, mix_ml, mix_swa, mix_nat, mods_l, mod_row_of_tile, w_out, ln1_g, ln1_b,
               w_mlp1, w_mlp2, ln2_g, ln2_b):
    rows = x.shape[0]
    row_spec = lambda w: pl.BlockSpec((ROW_TILE, w), lambda i: (i, 0))
    full = lambda a: pl.BlockSpec(a.shape, lambda i: (0,) * a.ndim)
    vec = lambda a: a.reshape(1, D_MODEL)
    consts = (w_out, vec(ln1_g), vec(ln1_b), w_mlp1, w_mlp2, vec(ln2_g), vec(ln2_b))
    return pl.pallas_call(
        _tail_kernel,
        grid=(rows // ROW_TILE,),
        in_specs=[row_spec(D_MODEL), row_spec(ML_W), row_spec(SWA_W), row_spec(NAT_W),
                  pl.BlockSpec((None, 6, D_MODEL), lambda i: (mod_row_of_tile(i), 0, 0))]
                 + [full(a) for a in consts],
        out_specs=row_spec(D_MODEL),
        out_shape=jax.ShapeDtypeStruct((rows, D_MODEL), F32),
        compiler_params=_params(1),
        name="tail",
    )(x, mix_ml, mix_swa, mix_nat, mods_l, *consts)


def _mlstm_gate_sums(g_ref, fb_ref, rows):
    n = ML_CHUNK
    g = g_ref[rows, :]
    ls = _log_sigmoid(g + fb_ref[...])
    r = lax.broadcasted_iota(jnp.int32, (n, n), 0)
    c = lax.broadcasted_iota(jnp.int32, (n, n), 1)
    lower = (c <= r).astype(BF16)
    upper = (c >= r).astype(BF16)
    cum = _dot_split3(lower, ls)
    suf = _dot_split3(upper, ls)
    return g, g.T, cum, cum.T, suf, suf.T


def _mlstm_direction(q, k, v, vt, bcol, brow, li_col, li_row, mask, last, c_st, n_st, m_st):
    dmat = jnp.where(mask, bcol - brow + li_row, NEG_INF)
    prev = bcol + m_st
    mt = jnp.maximum(prev, jnp.max(dmat, axis=1, keepdims=True))
    wprev = jnp.exp(prev - mt)
    s = _dot_nt(q, k) * jnp.exp(dmat - mt)
    num = _dot(s, v)
    den = jnp.sum(s, axis=1, keepdims=True) + wprev * jnp.sum(q * n_st, axis=1, keepdims=True)
    if c_st is not None:
        num = num + wprev * _dot_nt(q, c_st)
    h = num * (1.0 / jnp.maximum(jnp.abs(den), jnp.exp(-mt)))
    m_new = mt[last:last + 1, :]
    b_last_col = bcol[last:last + 1, :]
    ws_col = jnp.exp(b_last_col - bcol + li_col - m_new)
    ws_row = jnp.exp(b_last_col - brow + li_row - m_new)
    wc = wprev[last:last + 1, :]
    c_new = _dot(vt * ws_row, k)
    if c_st is not None:
        c_new = c_new + wc * c_st
    n_new = wc * n_st + jnp.sum(ws_col * k, axis=0, keepdims=True)
    return h, c_new, n_new, m_new


def _mlstm_chunk(zml_ref, g_ref, fb_ref, rows, direction, state):
    n = ML_CHUNK
    g, gt, cum, cumt, suf, suft = _mlstm_gate_sums(g_ref, fb_ref, rows)
    r = lax.broadcasted_iota(jnp.int32, (n, n), 0)
    c = lax.broadcasted_iota(jnp.int32, (n, n), 1)
    if direction == 0:
        bc, bt, mask, last = cum, cumt, c <= r, n - 1
    else:
        bc, bt, mask, last = suf, suft, c >= r, 0
    i_col = 2 * ML_HEADS * direction
    f_col = i_col + ML_HEADS
    hs, new_state = [], []
    for hd in range(ML_HEADS):
        sl = slice(hd * D_HEAD, (hd + 1) * D_HEAD)
        q = zml_ref[rows, hd * D_HEAD:(hd + 1) * D_HEAD]
        k = zml_ref[rows, ML_W + hd * D_HEAD:ML_W + (hd + 1) * D_HEAD] * ATT_SCALE
        pair = hd // 2
        v2 = zml_ref[rows, 2 * ML_W + pair * LANES:2 * ML_W + (pair + 1) * LANES]
        v = v2[:, (hd % 2) * D_HEAD:(hd % 2 + 1) * D_HEAD]
        vt = v2.T[(hd % 2) * D_HEAD:(hd % 2 + 1) * D_HEAD, :]
        c_st, n_st, m_st = state[hd]
        h, c_new, n_new, m_new = _mlstm_direction(
            q, k, v, vt,
            bc[:, f_col + hd:f_col + hd + 1], bt[f_col + hd:f_col + hd + 1, :],
            g[:, i_col + hd:i_col + hd + 1], gt[i_col + hd:i_col + hd + 1, :],
            mask, last, c_st, n_st, m_st)
        del sl
        hs.append(h)
        new_state.append((c_new, n_new, m_new))
    return hs, new_state


def _mlstm_finish(h, o_gate, norm_g):
    outs = []
    for hd in range(ML_HEADS):
        x = h[:, hd * D_HEAD:(hd + 1) * D_HEAD]
        mu = jnp.mean(x, axis=-1, keepdims=True)
        xc = x - mu
        var = jnp.mean(xc * xc, axis=-1, keepdims=True)
        outs.append(xc * lax.rsqrt(var + LN_EPS))
    hn = jnp.concatenate(outs, axis=1)
    return hn * norm_g * _sigmoid(o_gate)


def _store_state(c_ref, n_ref, m_ref, idx, c_new, n_new, m_new):
    c_ref[idx] = c_new
    n_ref[idx:idx + 1, :] = n_new
    m_ref[idx:idx + 1, :] = jnp.broadcast_to(m_new, (1, LANES))


def _softmax_rows(s, sink_col):
    m = jnp.max(s, axis=1, keepdims=True)
    if sink_col is not None:
        m = jnp.maximum(m, sink_col)
    p = jnp.exp(s - m)
    den = jnp.sum(p, axis=1, keepdims=True)
    if sink_col is not None:
        den = den + jnp.exp(sink_col - m)
    return p, 1.0 / den


def _ctx_mixers_kernel(zml_ref, zswa_ref, znat_ref, zg_ref, fb_ref, ng_ref, sink_ref,
                       mml_ref, mswa_ref, mnat_ref, sk_ref, sv_ref, nk_ref, nv_ref,
                       c_ref, n_ref, m_ref):
    rows = slice(0, SEQ)
    zero_state = [(None, jnp.zeros((1, D_HEAD), F32), jnp.zeros((1, 1), F32))] * ML_HEADS
    h_sum = None
    for direction in range(2):
        hs, new_state = _mlstm_chunk(zml_ref, zg_ref, fb_ref, rows, direction, zero_state)
        h_dir = jnp.concatenate(hs, axis=1)
        h_sum = h_dir if h_sum is None else h_sum + h_dir
        for hd in range(ML_HEADS):
            _store_state(c_ref, n_ref, m_ref, direction * ML_HEADS + hd, *new_state[hd])
    mml_ref[...] = _mlstm_finish(h_sum, zml_ref[:, 3 * ML_W:4 * ML_W], ng_ref[...]).astype(BF16)

    for kv in range(SWA_KV_HEADS):
        k = zswa_ref[:, SWA_W + kv * D_HEAD:SWA_W + (kv + 1) * D_HEAD]
        v = zswa_ref[:, SWA_W + SWA_KV_W + kv * D_HEAD:SWA_W + SWA_KV_W + (kv + 1) * D_HEAD]
        sk_ref[kv] = k
        sv_ref[kv] = v
        q = jnp.concatenate(
            [zswa_ref[:, (kv * SWA_GROUP + g) * D_HEAD:(kv * SWA_GROUP + g + 1) * D_HEAD]
             for g in range(SWA_GROUP)], axis=0)
        s = _dot_nt(q, k) * ATT_SCALE
        p, rden = _softmax_rows(s, sink_ref[kv * SWA_GROUP * SEQ:(kv + 1) * SWA_GROUP * SEQ, :])
        o = _dot(p, v) * rden
        for g in range(SWA_GROUP):
            hq = kv * SWA_GROUP + g
            mswa_ref[:, hq * D_HEAD:(hq + 1) * D_HEAD] = o[g * SEQ:(g + 1) * SEQ, :].astype(BF16)

    for hd in range(NAT_HEADS):
        q = znat_ref[:, hd * D_HEAD:(hd + 1) * D_HEAD]
        k = znat_ref[:, NAT_W + hd * D_HEAD:NAT_W + (hd + 1) * D_HEAD]
        v = znat_ref[:, 2 * NAT_W + hd * D_HEAD:2 * NAT_W + (hd + 1) * D_HEAD]
        nk_ref[hd] = k
        nv_ref[hd] = v
        p, rden = _softmax_rows(_dot_nt(q, k) * ATT_SCALE, None)
        mnat_ref[:, hd * D_HEAD:(hd + 1) * D_HEAD] = (_dot(p, v) * rden).astype(BF16)


def _ctx_mixers_call(z_ml, z_swa, z_nat, z_g, fb_row, norm_g, sink_col):
    rows = z_ml.shape[0]
    row_spec = lambda w: pl.BlockSpec((SEQ, w), lambda b: (b, 0))
    full = lambda a: pl.BlockSpec(a.shape, lambda b: (0,) * a.ndim)
    kv_spec = lambda h: pl.BlockSpec((None, h, SEQ, D_HEAD), lambda b: (b, 0, 0, 0))
    kv_shape = lambda h: jax.ShapeDtypeStruct((BATCH, h, SEQ, D_HEAD), F32)
    n_st = 2 * ML_HEADS
    return pl.pallas_call(
        _ctx_mixers_kernel,
        grid=(BATCH,),
        in_specs=[row_spec(Z_ML_W), row_spec(Z_SWA_W), row_spec(Z_NAT_W), row_spec(Z_G_W),
                  full(fb_row), full(norm_g), full(sink_col)],
        out_specs=[row_spec(ML_W), row_spec(SWA_W), row_spec(NAT_W),
                   kv_spec(SWA_KV_HEADS), kv_spec(SWA_KV_HEADS),
                   kv_spec(NAT_HEADS), kv_spec(NAT_HEADS),
                   pl.BlockSpec((None, n_st, D_HEAD, D_HEAD), lambda b: (b, 0, 0, 0)),
                   pl.BlockSpec((None, n_st, D_HEAD), lambda b: (b, 0, 0)),
                   pl.BlockSpec((None, n_st, LANES), lambda b: (b, 0, 0))],
        out_shape=[jax.ShapeDtypeStruct((rows, ML_W), BF16),
                   jax.ShapeDtypeStruct((rows, SWA_W), BF16),
                   jax.ShapeDtypeStruct((rows, NAT_W), BF16),
                   kv_shape(SWA_KV_HEADS), kv_shape(SWA_KV_HEADS),
                   kv_shape(NAT_HEADS), kv_shape(NAT_HEADS),
                   jax.ShapeDtypeStruct((BATCH, n_st, D_HEAD, D_HEAD), F32),
                   jax.ShapeDtypeStruct((BATCH, n_st, D_HEAD), F32),
                   jax.ShapeDtypeStruct((BATCH, n_st, LANES), F32)],
        compiler_params=_params(1),
        name="ctx_mixers",
    )(z_ml, z_swa, z_nat, z_g, fb_row, norm_g, sink_col)


def _lat_mlstm_kernel(zml_ref, zg_ref, fb_ref, ng_ref, c0_ref, n0_ref, m0_ref, mml_ref,
                      hf_ref, hb_ref, c_scr, n_scr, m_scr):
    n_chunks = DEC_SEQ // ML_CHUNK
    c_scr[...] = c0_ref[...]
    n_scr[...] = n0_ref[...]
    m_scr[...] = jnp.broadcast_to(m0_ref[...], m_scr.shape)

    def body(i, carry):
        for direction, h_ref in ((0, hf_ref), (1, hb_ref)):
            chunk = i if direction == 0 else n_chunks - 1 - i
            rows = pl.ds(pl.multiple_of(chunk * ML_CHUNK, ML_CHUNK), ML_CHUNK)
            base = direction * ML_HEADS
            state = [(c_scr[base + hd], n_scr[base + hd:base + hd + 1, :],
                      m_scr[base + hd:base + hd + 1, 0:1]) for hd in range(ML_HEADS)]
            hs, new_state = _mlstm_chunk(zml_ref, zg_ref, fb_ref, rows, direction, state)
            h_ref[rows, :] = jnp.concatenate(hs, axis=1)
            for hd in range(ML_HEADS):
                _store_state(c_scr, n_scr, m_scr, base + hd, *new_state[hd])
        return carry

    lax.fori_loop(0, n_chunks, body, 0)
    h = hf_ref[...] + hb_ref[...]
    mml_ref[...] = _mlstm_finish(h, zml_ref[:, 3 * ML_W:4 * ML_W], ng_ref[...]).astype(BF16)


def _lat_mlstm_call(z_ml, z_g, fb_row, norm_g, state_c, state_n, state_m, layer):
    n_st = 2 * ML_HEADS
    row_spec = lambda w: pl.BlockSpec((DEC_SEQ, w), lambda b: (b, 0))
    full = lambda a: pl.BlockSpec(a.shape, lambda b: (0,) * a.ndim)
    return pl.pallas_call(
        _lat_mlstm_kernel,
        grid=(DEC_BATCH,),
        in_specs=[row_spec(Z_ML_W), row_spec(Z_G_W), full(fb_row), full(norm_g),
                  pl.BlockSpec((None, None, n_st, D_HEAD, D_HEAD), lambda b: (b, layer, 0, 0, 0)),
                  pl.BlockSpec((None, None, n_st, D_HEAD), lambda b: (b, layer, 0, 0)),
                  pl.BlockSpec((None, None, n_st, 1), lambda b: (b, layer, 0, 0))],
        out_specs=row_spec(ML_W),
        out_shape=jax.ShapeDtypeStruct((DEC_BATCH * DEC_SEQ, ML_W), BF16),
        scratch_shapes=[pltpu.VMEM((DEC_SEQ, ML_W), F32), pltpu.VMEM((DEC_SEQ, ML_W), F32),
                        pltpu.VMEM((n_st, D_HEAD, D_HEAD), F32), pltpu.VMEM((n_st, D_HEAD), F32),
                        pltpu.VMEM((n_st, LANES), F32)],
        compiler_params=_params(1),
        name="lat_mlstm",
    )(z_ml, z_g, fb_row, norm_g, state_c, state_n, state_m)


def _rope(x, cos, sin_signed):
    lane = lax.broadcasted_iota(jnp.int32, x.shape, 1)
    first = (lane & 31) < 16
    partner = jnp.where(first, pltpu.roll(x, LANES - 16, 1), pltpu.roll(x, 16, 1))
    return x * cos + partner * sin_signed


def _lat_swa_kernel(zswa_ref, cos_ref, sin_ref, ck_ref, cv_ref, sink_ref, mswa_ref,
                    q_scr, k_scr):
    cos = cos_ref[...]
    sin = sin_ref[...]
    for j in range(SWA_W // LANES):
        q_scr[:, j * LANES:(j + 1) * LANES] = _rope(
            zswa_ref[:, j * LANES:(j + 1) * LANES], cos, sin).astype(BF16)
    k_scr[...] = _rope(zswa_ref[:, SWA_W:SWA_W + SWA_KV_W], cos, sin).astype(BF16)

    bl = SWA_BLOCK
    nb = DEC_SEQ // bl
    n_q = SWA_GROUP * bl
    for kv in range(SWA_KV_HEADS):
        ck = ck_ref[kv].astype(BF16)
        cv = cv_ref[kv].astype(BF16)
        sink_col = sink_ref[kv * n_q:(kv + 1) * n_q, :]
        v_col = SWA_W + SWA_KV_W + kv * D_HEAD
        for blk in range(nb):
            lo = max(blk - 1, 0) * bl
            hi = min(blk + 2, nb) * bl
            n_band = hi - lo
            q = jnp.concatenate(
                [q_scr[blk * bl:(blk + 1) * bl,
                       (kv * SWA_GROUP + g) * D_HEAD:(kv * SWA_GROUP + g + 1) * D_HEAD]
                 for g in range(SWA_GROUP)], axis=0)
            k = jnp.concatenate([k_scr[lo:hi, kv * D_HEAD:(kv + 1) * D_HEAD], ck], axis=0)
            v = jnp.concatenate(
                [zswa_ref[lo:hi, v_col:v_col + D_HEAD].astype(BF16), cv], axis=0)
            n_keys = n_band + PAST_LEN
            s = _dot_nt(q, k) * ATT_SCALE
            row = lax.broadcasted_iota(jnp.int32, (n_q, n_keys), 0)
            col = lax.broadcasted_iota(jnp.int32, (n_q, n_keys), 1)
            qpos = blk * bl + (row & (bl - 1))
            kpos = lo + col
            valid = (col >= n_band) | (jnp.abs(qpos - kpos) <= SWA_WINDOW)
            p, rden = _softmax_rows(jnp.where(valid, s, NEG_INF), sink_col)
            o = _dot(p, v) * rden
            for g in range(SWA_GROUP):
                hq = kv * SWA_GROUP + g
                mswa_ref[blk * bl:(blk + 1) * bl, hq * D_HEAD:(hq + 1) * D_HEAD] = (
                    o[g * bl:(g + 1) * bl, :].astype(BF16))


def _lat_swa_call(z_swa, cos_t, sin_t, cache_k, cache_v, sink_col, layer):
    row_spec = lambda w: pl.BlockSpec((DEC_SEQ, w), lambda b: (b, 0))
    full = lambda a: pl.BlockSpec(a.shape, lambda b: (0,) * a.ndim)
    cache_spec = pl.BlockSpec((None, None, SWA_KV_HEADS, PAST_LEN, D_HEAD),
                              lambda b: (b, layer, 0, 0, 0))
    return pl.pallas_call(
        _lat_swa_kernel,
        grid=(DEC_BATCH,),
        in_specs=[row_spec(Z_SWA_W), full(cos_t), full(sin_t), cache_spec, cache_spec,
                  full(sink_col)],
        out_specs=row_spec(SWA_W),
        out_shape=jax.ShapeDtypeStruct((DEC_BATCH * DEC_SEQ, SWA_W), BF16),
        scratch_shapes=[pltpu.VMEM((DEC_SEQ, SWA_W), BF16), pltpu.VMEM((DEC_SEQ, SWA_KV_W), BF16)],
        compiler_params=_params(1),
        name="lat_swa",
    )(z_swa, cos_t, sin_t, cache_k, cache_v, sink_col)


NAT_ROWS = DEC_SEQ // GRID_W
NAT_NB_KEYS = NAT_KH * GRID_W


def _nat_row_start(r):
    return min(max(r - NAT_KH // 2, 0), NAT_ROWS - NAT_KH)


def _lat_nat_kernel(q_ref, k_ref, v_ref, ck_ref, cv_ref, bias_ref, mnat_ref):
    for hd in range(2):
        lanes = slice(hd * D_HEAD, (hd + 1) * D_HEAD)
        ck = ck_ref[hd].astype(BF16)
        cv = cv_ref[hd].astype(BF16)
        for r in range(NAT_ROWS):
            r0 = _nat_row_start(r)
            q = q_ref[r * GRID_W:(r + 1) * GRID_W, lanes]
            k = k_ref[r0 * GRID_W:r0 * GRID_W + NAT_NB_KEYS, lanes]
            v = v_ref[r0 * GRID_W:r0 * GRID_W + NAT_NB_KEYS, lanes]
            s_nb = _dot_nt(q, k) * ATT_SCALE + bias_ref[hd, r - r0]
            s_ctx = _dot_nt(q, ck) * ATT_SCALE
            m = jnp.maximum(jnp.max(s_nb, axis=1, keepdims=True),
                            jnp.max(s_ctx, axis=1, keepdims=True))
            p_nb = jnp.exp(s_nb - m)
            p_ctx = jnp.exp(s_ctx - m)
            den = jnp.sum(p_nb, axis=1, keepdims=True) + jnp.sum(p_ctx, axis=1, keepdims=True)
            o = (_dot(p_nb, v) + _dot(p_ctx, cv)) * (1.0 / den)
            mnat_ref[r * GRID_W:(r + 1) * GRID_W, lanes] = o.astype(BF16)


def _lat_nat_call(z_nat, cache_k, cache_v, bias, layer):
    n_pair = NAT_HEADS // 2
    col_spec = lambda off: pl.BlockSpec((DEC_SEQ, LANES), lambda b, p: (b, off + p))
    cache_spec = pl.BlockSpec((None, None, 2, PAST_LEN, D_HEAD), lambda b, p: (b, layer, p, 0, 0))
    return pl.pallas_call(
        _lat_nat_kernel,
        grid=(DEC_BATCH, n_pair),
        in_specs=[col_spec(0), col_spec(n_pair), col_spec(2 * n_pair), cache_spec, cache_spec,
                  pl.BlockSpec((2, NAT_KH, GRID_W, NAT_NB_KEYS), lambda b, p: (p, 0, 0, 0))],
        out_specs=pl.BlockSpec((DEC_SEQ, LANES), lambda b, p: (b, p)),
        out_shape=jax.ShapeDtypeStruct((DEC_BATCH * DEC_SEQ, NAT_W), BF16),
        compiler_params=_params(2),
        name="lat_nat",
    )(z_nat, z_nat, z_nat, cache_k, cache_v, bias)


def _nat_bias_table(rpb):
    dd = np.arange(NAT_KH)[:, None, None, None]
    q = np.arange(GRID_W)[None, :, None, None]
    j = np.arange(NAT_KH)[None, None, :, None]
    kc = np.arange(GRID_W)[None, None, None, :]
    roff = np.broadcast_to(j - dd + NAT_KH - 1, (NAT_KH, GRID_W, NAT_KH, GRID_W))
    coff = np.broadcast_to(np.clip(kc - q + NAT_KW - 1, 0, 2 * NAT_KW - 2),
                           (NAT_KH, GRID_W, NAT_KH, GRID_W))
    cs = np.clip(q - NAT_KW // 2, 0, GRID_W - NAT_KW)
    ok = np.broadcast_to((kc >= cs) & (kc < cs + NAT_KW), (NAT_KH, GRID_W, NAT_KH, GRID_W))
    bias = jnp.where(ok[None], rpb[:, roff, coff].astype(F32), NEG_INF)
    return bias.reshape(NAT_HEADS, NAT_KH, GRID_W, NAT_NB_KEYS)


def _rope_tables():
    t = np.arange(DEC_SEQ)[:, None]
    d = np.arange(LANES)[None, :] % D_HEAD
    pos = np.where(d < D_HEAD // 2, t // GRID_W, t % GRID_W).astype(np.float64)
    quarter = D_HEAD // 4
    freq = ROPE_BASE ** (-(d % quarter).astype(np.float64) / quarter)
    ang = (pos.astype(np.float32) * freq.astype(np.float32)).astype(np.float32)
    sign = np.where((d % (2 * quarter)) < quarter, -1.0, 1.0)
    return (jnp.asarray(np.cos(ang), dtype=F32), jnp.asarray(np.sin(ang) * sign, dtype=F32))


def _permute_in_columns(a):
    g0 = Z_ML_W
    s0 = g0 + N_GATES
    pad = jnp.zeros(a.shape[:-1] + (Z_G_W - N_GATES,), a.dtype)
    return jnp.concatenate([a[..., :g0], a[..., s0:], a[..., g0:s0], pad], axis=-1)


def kernel(x_prompt, x_sample, cache_swa_k, cache_swa_v, cache_nat_k, cache_nat_v, state_mlstm_C,
           state_mlstm_n, state_mlstm_m, c, c_ctx, w_ada, b_ada, w_in, b_in, mlstm_fbias,
           mlstm_norm_g, swa_sink, nat_rpb, w_out, ln1_g, ln1_b, w_mlp1, w_mlp2, ln2_g, ln2_b):
    xp = x_prompt.reshape(BATCH * SEQ, D_MODEL)
    xs = x_sample.reshape(DEC_BATCH * DEC_SEQ, D_MODEL)
    cvec = jnp.concatenate(
        [c, c_ctx[None, :], jnp.zeros((MOD_ROWS - DEC_BATCH - 1, D_MODEL), F32)], axis=0)
    mods = _mods_call(cvec, w_ada, b_ada).reshape(DEPTH, MOD_ROWS, 6, D_MODEL)

    w_in_p = _permute_in_columns(w_in).astype(BF16)
    b_in_p = _permute_in_columns(b_in).reshape(DEPTH, 1, Z_W)
    w_out_b = w_out.astype(BF16)
    w_mlp1_b = w_mlp1.astype(BF16)
    w_mlp2_b = w_mlp2.astype(BF16)
    cos_t, sin_t = _rope_tables()

    n_st = 2 * ML_HEADS
    state_c = state_mlstm_C.reshape(DEC_BATCH, DEPTH, n_st, D_HEAD, D_HEAD)
    state_n = state_mlstm_n.reshape(DEC_BATCH, DEPTH, n_st, D_HEAD)
    state_m = state_mlstm_m.reshape(DEC_BATCH, DEPTH, n_st, 1)

    ctx_row = lambda i: CTX_MOD_ROW
    lat_row = lambda i: i // (DEC_SEQ // ROW_TILE)

    outs = [[] for _ in range(7)]
    for l in range(DEPTH):
        fb_row = jnp.zeros((1, Z_G_W), F32)
        fb_row = fb_row.at[0, ML_HEADS:2 * ML_HEADS].set(mlstm_fbias[l, 0])
        fb_row = fb_row.at[0, 3 * ML_HEADS:4 * ML_HEADS].set(mlstm_fbias[l, 1])
        norm_g = mlstm_norm_g[l].reshape(1, ML_W)
        sink_ctx = jnp.repeat(swa_sink[l], SEQ).reshape(SWA_HEADS * SEQ, 1)
        sink_lat = jnp.repeat(swa_sink[l], SWA_BLOCK).reshape(SWA_HEADS * SWA_BLOCK, 1)
        tail_w = (w_out_b[l], ln1_g[l], ln1_b[l], w_mlp1_b[l], w_mlp2_b[l], ln2_g[l], ln2_b[l])

        z_ml, z_swa, z_nat, z_g = _inproj_call(xp, mods[l], ctx_row, w_in_p[l], b_in_p[l])
        (m_ml, m_swa, m_nat, k_swa, v_swa, k_nat, v_nat, st_c, st_n, st_m) = _ctx_mixers_call(
            z_ml, z_swa, z_nat, z_g, fb_row, norm_g, sink_ctx)
        xp = _tail_call(xp, m_ml, m_swa, m_nat, mods[l], ctx_row, *tail_w)
        for lst, a in zip(outs, (k_swa, v_swa, k_nat, v_nat,
                                 st_c.reshape(BATCH, 2, ML_HEADS, D_HEAD, D_HEAD),
                                 st_n.reshape(BATCH, 2, ML_HEADS, D_HEAD),
                                 st_m[:, :, 0].reshape(BATCH, 2, ML_HEADS))):
            lst.append(a)

        z_ml, z_swa, z_nat, z_g = _inproj_call(xs, mods[l], lat_row, w_in_p[l], b_in_p[l])
        m_ml = _lat_mlstm_call(z_ml, z_g, fb_row, norm_g, state_c, state_n, state_m, l)
        m_swa = _lat_swa_call(z_swa, cos_t, sin_t, cache_swa_k, cache_swa_v, sink_lat, l)
        m_nat = _lat_nat_call(z_nat, cache_nat_k, cache_nat_v, _nat_bias_table(nat_rpb[l]), l)
        xs = _tail_call(xs, m_ml, m_swa, m_nat, mods[l], lat_row, *tail_w)

    stacked = [jnp.stack(lst, axis=1) for lst in outs]
    return (xp.reshape(BATCH, SEQ, D_MODEL), xs.reshape(DEC_BATCH, DEC_SEQ, D_MODEL), *stacked)
```

```python
import numpy as np
import jax
import jax.numpy as jnp
from jax import lax
from jax.experimental import pallas as pl
from jax.experimental.pallas import tpu as pltpu

D_MODEL = 1024
BATCH = 16
SEQ = 256
DEPTH = 4
DEC_BATCH = 8
DEC_SEQ = 1024
PAST_LEN = 512
GRID_W = 64
D_HEAD = 64
ML_HEADS = 4
SWA_HEADS = 6
SWA_KV_HEADS = 2
SWA_GROUP = SWA_HEADS // SWA_KV_HEADS
NAT_HEADS = 6
ML_W = ML_HEADS * D_HEAD
SWA_W = SWA_HEADS * D_HEAD
SWA_KV_W = SWA_KV_HEADS * D_HEAD
NAT_W = NAT_HEADS * D_HEAD
N_GATES = 4 * ML_HEADS
D_FF = 4 * D_MODEL
SWA_WINDOW = 128
SWA_BLOCK = 128
NAT_KH = 8
NAT_KW = 16
ROPE_BASE = 10000.0
LN_EPS = 1e-5
DN_ALPHA = (2 * DEPTH) ** 0.25
ATT_SCALE = D_HEAD ** -0.5

LANES = 128
SUBLANES = 8
Z_ML_W = 4 * ML_W
Z_SWA_W = SWA_W + 2 * SWA_KV_W
Z_NAT_W = 3 * NAT_W
Z_G_W = LANES
Z_W = Z_ML_W + Z_SWA_W + Z_NAT_W + Z_G_W
MOD_ROWS = 16
CTX_MOD_ROW = DEC_BATCH
ROW_TILE = 256
ML_CHUNK = 256
VMEM_LIMIT = 56 * 1024 * 1024

BF16 = jnp.bfloat16
F32 = jnp.float32
NEG_INF = float("-inf")


def _dot(a, b):
    return jnp.dot(a.astype(BF16), b.astype(BF16), preferred_element_type=F32)


def _dot_nt(a, b):
    return lax.dot_general(a.astype(BF16), b.astype(BF16), (((1,), (1,)), ((), ())),
                           preferred_element_type=F32)


def _dot_split3(tri, x):
    hi = x.astype(BF16)
    r1 = x - hi.astype(F32)
    mid = r1.astype(BF16)
    lo = (r1 - mid.astype(F32)).astype(BF16)
    return (jnp.dot(tri, hi, preferred_element_type=F32)
            + jnp.dot(tri, mid, preferred_element_type=F32)
            + jnp.dot(tri, lo, preferred_element_type=F32))


def _sigmoid(x):
    return 1.0 / (1.0 + jnp.exp(-x))


def _log_sigmoid(x):
    return jnp.minimum(x, 0.0) - jnp.log1p(jnp.exp(-jnp.abs(x)))


def _layer_norm(x, g, b):
    mu = jnp.mean(x, axis=-1, keepdims=True)
    xc = x - mu
    var = jnp.mean(xc * xc, axis=-1, keepdims=True)
    return xc * lax.rsqrt(var + LN_EPS) * g + b


def _params(n_axes=1):
    return pltpu.CompilerParams(dimension_semantics=("arbitrary",) * n_axes,
                                vmem_limit_bytes=VMEM_LIMIT)


def _mods_kernel(c_ref, w_ref, b_ref, o_ref):
    c = c_ref[...]
    o_ref[...] = _dot(c * _sigmoid(c), w_ref[...]) + b_ref[...]


def _mods_call(cvec, w_ada, b_ada):
    n_col = 6
    return pl.pallas_call(
        _mods_kernel,
        grid=(DEPTH, n_col),
        in_specs=[
            pl.BlockSpec((MOD_ROWS, D_MODEL), lambda l, j: (0, 0)),
            pl.BlockSpec((None, D_MODEL, D_MODEL), lambda l, j: (l, 0, j)),
            pl.BlockSpec((None, 1, D_MODEL), lambda l, j: (l, 0, j)),
        ],
        out_specs=pl.BlockSpec((None, MOD_ROWS, D_MODEL), lambda l, j: (l, 0, j)),
        out_shape=jax.ShapeDtypeStruct((DEPTH, MOD_ROWS, 6 * D_MODEL), F32),
        compiler_params=_params(2),
        name="mods",
    )(cvec, w_ada, b_ada.reshape(DEPTH, 1, 6 * D_MODEL))


def _inproj_kernel(x_ref, mod_ref, w_ref, b_ref, zml_ref, zswa_ref, znat_ref, zg_ref):
    h = (x_ref[...] * (1.0 + mod_ref[1:2, :]) + mod_ref[0:1, :]).astype(BF16)
    o = 0
    for ref in (zml_ref, zswa_ref, znat_ref, zg_ref):
        w = ref.shape[-1]
        ref[...] = jnp.dot(h, w_ref[:, o:o + w], preferred_element_type=F32) + b_ref[:, o:o + w]
        o += w


def _inproj_call(x, mods_l, mod_row_of_tile, w_in, b_in):
    rows = x.shape[0]
    widths = (Z_ML_W, Z_SWA_W, Z_NAT_W, Z_G_W)
    return pl.pallas_call(
        _inproj_kernel,
        grid=(rows // ROW_TILE,),
        in_specs=[
            pl.BlockSpec((ROW_TILE, D_MODEL), lambda i: (i, 0)),
            pl.BlockSpec((None, 6, D_MODEL), lambda i: (mod_row_of_tile(i), 0, 0)),
            pl.BlockSpec((D_MODEL, Z_W), lambda i: (0, 0)),
            pl.BlockSpec((1, Z_W), lambda i: (0, 0)),
        ],
        out_specs=[pl.BlockSpec((ROW_TILE, w), lambda i: (i, 0)) for w in widths],
        out_shape=[jax.ShapeDtypeStruct((rows, w), F32) for w in widths],
        compiler_params=_params(1),
        name="inproj",
    )(x, mods_l, w_in, b_in)


def _tail_kernel(x_ref, mml_ref, mswa_ref, mnat_ref, mod_ref, wout_ref, ln1g_ref, ln1b_ref,
                 w1_ref, w2_ref, ln2g_ref, ln2b_ref, o_ref):
    g_a, sh_m, sc_m, g_m = (mod_ref[i:i + 1, :] for i in (2, 3, 4, 5))
    mix = jnp.concatenate([mml_ref[...], mswa_ref[...], mnat_ref[...]], axis=1)
    proj = jnp.dot(mix, wout_ref[...], preferred_element_type=F32)
    y = _layer_norm(DN_ALPHA * x_ref[...] + g_a * proj, ln1g_ref[...], ln1b_ref[...])
    h = (y * (1.0 + sc_m) + sh_m).astype(BF16)
    f = jnp.maximum(jnp.dot(h, w1_ref[...], preferred_element_type=F32), 0.0)
    f = jnp.dot((f * f).astype(BF16), w2_ref[...], preferred_element_type=F32)
    o_ref[...] = _layer_norm(DN_ALPHA * y + g_m * f, ln2g_ref[...], ln2b_ref[...])


def _tail_call(x, mix_ml, mix_swa, mix_nat, mods_l, mod_row_of_tile, w_out, ln1_g, ln1_b,
               w_mlp1, w_mlp2, ln2_g, ln2_b):
    rows = x.shape[0]
    row_spec = lambda w: pl.BlockSpec((ROW_TILE, w), lambda i: (i, 0))
    full = lambda a: pl.BlockSpec(a.shape, lambda i: (0,) * a.ndim)
    vec = lambda a: a.reshape(1, D_MODEL)
    consts = (w_out, vec(ln1_g), vec(ln1_b), w_mlp1, w_mlp2, vec(ln2_g), vec(ln2_b))
    return pl.pallas_call(
        _tail_kernel,
        grid=(rows // ROW_TILE,),
        in_specs=[row_spec(D_MODEL), row_spec(ML_W), row_spec(SWA_W), row_spec(NAT_W),
                  pl.BlockSpec((None, 6, D_MODEL), lambda i: (mod_row_of_tile(i), 0, 0))]
                 + [full(a) for a in consts],
        out_specs=row_spec(D_MODEL),
        out_shape=jax.ShapeDtypeStruct((rows, D_MODEL), F32),
        compiler_params=_params(1),
        name="tail",
    )(x, mix_ml, mix_swa, mix_nat, mods_l, *consts)


def _mlstm_direction(qt, k, vt, a_col, b_row, li_row, neg, last, c_st, n_st, m_st):
    a_m = a_col + neg
    c_row = jnp.maximum(m_st, jnp.max(a_m, axis=0, keepdims=True))
    decay = jnp.exp(a_m - c_row)
    wprev = jnp.exp(m_st - c_row)
    mt = b_row + c_row
    qt_b = qt.astype(BF16)
    st = _dot(k, qt_b) * decay
    den = jnp.sum(st, axis=0, keepdims=True)
    num = _dot(vt, st)
    if c_st is not None:
        cn = _dot(jnp.concatenate([c_st, jnp.broadcast_to(n_st, (SUBLANES, D_HEAD))], axis=0),
                  qt_b)
        num = num + wprev * cn[0:D_HEAD, :]
        den = den + wprev * cn[D_HEAD:D_HEAD + 1, :]
    ht = num * (1.0 / jnp.maximum(jnp.abs(den), jnp.exp(-mt)))
    m_new = mt[:, last:last + 1]
    ws = jnp.exp(b_row[:, last:last + 1] - b_row + li_row - m_new)
    upd = _dot(jnp.concatenate([vt * ws, jnp.broadcast_to(ws, (SUBLANES, ws.shape[1]))], axis=0),
               k)
    c_new = upd[0:D_HEAD, :]
    n_new = upd[D_HEAD:D_HEAD + 1, :]
    if c_st is not None:
        wc = wprev[:, last:last + 1]
        c_new = c_new + wc * c_st
        n_new = n_new + wc * n_st
    return ht, c_new, n_new, m_new


def _mlstm_chunk(zml_ref, g_ref, fb_ref, rows, direction, state):
    n = ML_CHUNK
    g = g_ref[rows, :]
    ls = _log_sigmoid(g + fb_ref[...])
    r = lax.broadcasted_iota(jnp.int32, (n, n), 0)
    c = lax.broadcasted_iota(jnp.int32, (n, n), 1)
    if direction == 0:
        tri, neg, last = c <= r, jnp.where(r <= c, 0.0, NEG_INF), n - 1
    else:
        tri, neg, last = c >= r, jnp.where(r >= c, 0.0, NEG_INF), 0
    b = _dot_split3(tri.astype(BF16), ls)
    a = pltpu.roll(g, ML_HEADS, 1) - b
    bt = b.T
    gt = g.T
    i_col = 2 * ML_HEADS * direction
    f_col = i_col + ML_HEADS
    hts, new_state = [], []
    for pair in range(ML_HEADS // 2):
        qt2 = zml_ref[rows, pair * LANES:(pair + 1) * LANES].T
        vt2 = zml_ref[rows, 2 * ML_W + pair * LANES:2 * ML_W + (pair + 1) * LANES].T
        for sub in range(2):
            hd = 2 * pair + sub
            feat = slice(sub * D_HEAD, (sub + 1) * D_HEAD)
            k = zml_ref[rows, ML_W + hd * D_HEAD:ML_W + (hd + 1) * D_HEAD] * ATT_SCALE
            ht, c_new, n_new, m_new = _mlstm_direction(
                qt2[feat, :], k, vt2[feat, :],
                a[:, f_col + hd:f_col + hd + 1], bt[f_col + hd:f_col + hd + 1, :],
                gt[i_col + hd:i_col + hd + 1, :], neg, last, *state[hd])
            hts.append(ht)
            new_state.append((c_new, n_new, m_new))
    return jnp.concatenate(hts, axis=0), new_state


def _mlstm_finish(ht, o_gate, norm_g):
    outs = []
    for hd in range(ML_HEADS):
        x = ht[hd * D_HEAD:(hd + 1) * D_HEAD, :]
        mu = jnp.mean(x, axis=0, keepdims=True)
        xc = x - mu
        var = jnp.mean(xc * xc, axis=0, keepdims=True)
        outs.append(xc * lax.rsqrt(var + LN_EPS))
    hn = jnp.concatenate(outs, axis=0).T
    return hn * norm_g * _sigmoid(o_gate)


def _store_state(c_ref, n_ref, m_ref, idx, c_new, n_new, m_new):
    c_ref[idx] = c_new
    n_ref[idx:idx + 1, :] = n_new
    m_ref[idx:idx + 1, :] = jnp.broadcast_to(m_new, (1, LANES))


def _softmax_rows(s, sink_col):
    m = jnp.max(s, axis=1, keepdims=True)
    if sink_col is not None:
        m = jnp.maximum(m, sink_col)
    p = jnp.exp(s - m)
    den = jnp.sum(p, axis=1, keepdims=True)
    if sink_col is not None:
        den = den + jnp.exp(sink_col - m)
    return p, 1.0 / den


def _ctx_mixers_kernel(zml_ref, zswa_ref, znat_ref, zg_ref, fb_ref, ng_ref, sink_ref,
                       mml_ref, mswa_ref, mnat_ref, sk_ref, sv_ref, nk_ref, nv_ref,
                       c_ref, n_ref, m_ref):
    rows = slice(0, SEQ)
    zero_state = [(None, jnp.zeros((1, D_HEAD), F32), jnp.zeros((1, 1), F32))] * ML_HEADS
    ht_sum = None
    for direction in range(2):
        ht, new_state = _mlstm_chunk(zml_ref, zg_ref, fb_ref, rows, direction, zero_state)
        ht_sum = ht if ht_sum is None else ht_sum + ht
        for hd in range(ML_HEADS):
            _store_state(c_ref, n_ref, m_ref, direction * ML_HEADS + hd, *new_state[hd])
    mml_ref[...] = _mlstm_finish(ht_sum, zml_ref[:, 3 * ML_W:4 * ML_W], ng_ref[...]).astype(BF16)

    for kv in range(SWA_KV_HEADS):
        k = zswa_ref[:, SWA_W + kv * D_HEAD:SWA_W + (kv + 1) * D_HEAD]
        v = zswa_ref[:, SWA_W + SWA_KV_W + kv * D_HEAD:SWA_W + SWA_KV_W + (kv + 1) * D_HEAD]
        sk_ref[kv] = k
        sv_ref[kv] = v
        q = jnp.concatenate(
            [zswa_ref[:, (kv * SWA_GROUP + g) * D_HEAD:(kv * SWA_GROUP + g + 1) * D_HEAD]
             for g in range(SWA_GROUP)], axis=0) * ATT_SCALE
        p, rden = _softmax_rows(
            _dot_nt(q, k), sink_ref[kv * SWA_GROUP * SEQ:(kv + 1) * SWA_GROUP * SEQ, :])
        o = _dot(p, v) * rden
        for g in range(SWA_GROUP):
            hq = kv * SWA_GROUP + g
            mswa_ref[:, hq * D_HEAD:(hq + 1) * D_HEAD] = o[g * SEQ:(g + 1) * SEQ, :].astype(BF16)

    for hd in range(NAT_HEADS):
        q = znat_ref[:, hd * D_HEAD:(hd + 1) * D_HEAD] * ATT_SCALE
        k = znat_ref[:, NAT_W + hd * D_HEAD:NAT_W + (hd + 1) * D_HEAD]
        v = znat_ref[:, 2 * NAT_W + hd * D_HEAD:2 * NAT_W + (hd + 1) * D_HEAD]
        nk_ref[hd] = k
        nv_ref[hd] = v
        p, rden = _softmax_rows(_dot_nt(q, k), None)
        mnat_ref[:, hd * D_HEAD:(hd + 1) * D_HEAD] = (_dot(p, v) * rden).astype(BF16)


def _ctx_mixers_call(z_ml, z_swa, z_nat, z_g, fb_row, norm_g, sink_col):
    rows = z_ml.shape[0]
    row_spec = lambda w: pl.BlockSpec((SEQ, w), lambda b: (b, 0))
    full = lambda a: pl.BlockSpec(a.shape, lambda b: (0,) * a.ndim)
    kv_spec = lambda h: pl.BlockSpec((None, h, SEQ, D_HEAD), lambda b: (b, 0, 0, 0))
    kv_shape = lambda h: jax.ShapeDtypeStruct((BATCH, h, SEQ, D_HEAD), F32)
    n_st = 2 * ML_HEADS
    return pl.pallas_call(
        _ctx_mixers_kernel,
        grid=(BATCH,),
        in_specs=[row_spec(Z_ML_W), row_spec(Z_SWA_W), row_spec(Z_NAT_W), row_spec(Z_G_W),
                  full(fb_row), full(norm_g), full(sink_col)],
        out_specs=[row_spec(ML_W), row_spec(SWA_W), row_spec(NAT_W),
                   kv_spec(SWA_KV_HEADS), kv_spec(SWA_KV_HEADS),
                   kv_spec(NAT_HEADS), kv_spec(NAT_HEADS),
                   pl.BlockSpec((None, n_st, D_HEAD, D_HEAD), lambda b: (b, 0, 0, 0)),
                   pl.BlockSpec((None, n_st, D_HEAD), lambda b: (b, 0, 0)),
                   pl.BlockSpec((None, n_st, LANES), lambda b: (b, 0, 0))],
        out_shape=[jax.ShapeDtypeStruct((rows, ML_W), BF16),
                   jax.ShapeDtypeStruct((rows, SWA_W), BF16),
                   jax.ShapeDtypeStruct((rows, NAT_W), BF16),
                   kv_shape(SWA_KV_HEADS), kv_shape(SWA_KV_HEADS),
                   kv_shape(NAT_HEADS), kv_shape(NAT_HEADS),
                   jax.ShapeDtypeStruct((BATCH, n_st, D_HEAD, D_HEAD), F32),
                   jax.ShapeDtypeStruct((BATCH, n_st, D_HEAD), F32),
                   jax.ShapeDtypeStruct((BATCH, n_st, LANES), F32)],
        compiler_params=_params(1),
        name="ctx_mixers",
    )(z_ml, z_swa, z_nat, z_g, fb_row, norm_g, sink_col)


def _lat_mlstm_kernel(zml_ref, zg_ref, fb_ref, ng_ref, c0_ref, n0_ref, m0_ref, mml_ref,
                      ht_scr, c_scr, n_scr, m_scr):
    n_chunks = DEC_SEQ // ML_CHUNK
    c_scr[...] = c0_ref[...]
    n_scr[...] = n0_ref[...]
    m_scr[...] = jnp.broadcast_to(m0_ref[...], m_scr.shape)

    def body(i, carry):
        for direction in range(2):
            chunk = i if direction == 0 else n_chunks - 1 - i
            rows = pl.ds(pl.multiple_of(chunk * ML_CHUNK, ML_CHUNK), ML_CHUNK)
            base = direction * ML_HEADS
            state = [(c_scr[base + hd], n_scr[base + hd:base + hd + 1, :],
                      m_scr[base + hd:base + hd + 1, 0:1]) for hd in range(ML_HEADS)]
            ht, new_state = _mlstm_chunk(zml_ref, zg_ref, fb_ref, rows, direction, state)
            ht_scr[direction, chunk] = ht
            for hd in range(ML_HEADS):
                _store_state(c_scr, n_scr, m_scr, base + hd, *new_state[hd])
        return carry

    lax.fori_loop(0, n_chunks, body, 0)
    for chunk in range(n_chunks):
        rows = slice(chunk * ML_CHUNK, (chunk + 1) * ML_CHUNK)
        mml_ref[rows, :] = _mlstm_finish(ht_scr[0, chunk] + ht_scr[1, chunk],
                                         zml_ref[rows, 3 * ML_W:4 * ML_W],
                                         ng_ref[...]).astype(BF16)


def _lat_mlstm_call(z_ml, z_g, fb_row, norm_g, state_c, state_n, state_m, layer):
    n_st = 2 * ML_HEADS
    n_chunks = DEC_SEQ // ML_CHUNK
    row_spec = lambda w: pl.BlockSpec((DEC_SEQ, w), lambda b: (b, 0))
    full = lambda a: pl.BlockSpec(a.shape, lambda b: (0,) * a.ndim)
    return pl.pallas_call(
        _lat_mlstm_kernel,
        grid=(DEC_BATCH,),
        in_specs=[row_spec(Z_ML_W), row_spec(Z_G_W), full(fb_row), full(norm_g),
                  pl.BlockSpec((None, None, n_st, D_HEAD, D_HEAD), lambda b: (b, layer, 0, 0, 0)),
                  pl.BlockSpec((None, None, n_st, D_HEAD), lambda b: (b, layer, 0, 0)),
                  pl.BlockSpec((None, None, n_st, 1), lambda b: (b, layer, 0, 0))],
        out_specs=row_spec(ML_W),
        out_shape=jax.ShapeDtypeStruct((DEC_BATCH * DEC_SEQ, ML_W), BF16),
        scratch_shapes=[pltpu.VMEM((2, n_chunks, ML_W, ML_CHUNK), F32),
                        pltpu.VMEM((n_st, D_HEAD, D_HEAD), F32), pltpu.VMEM((n_st, D_HEAD), F32),
                        pltpu.VMEM((n_st, LANES), F32)],
        compiler_params=_params(1),
        name="lat_mlstm",
    )(z_ml, z_g, fb_row, norm_g, state_c, state_n, state_m)


def _rope(x, cos, sin_signed):
    lane = lax.broadcasted_iota(jnp.int32, x.shape, 1)
    first = (lane & 31) < 16
    partner = jnp.where(first, pltpu.roll(x, LANES - 16, 1), pltpu.roll(x, 16, 1))
    return x * cos + partner * sin_signed


def _lat_swa_kernel(zswa_ref, cos_ref, sin_ref, ck_ref, cv_ref, sink_ref, mswa_ref,
                    q_scr, k_scr):
    cos = cos_ref[...]
    sin = sin_ref[...]
    for j in range(SWA_W // LANES):
        q_scr[:, j * LANES:(j + 1) * LANES] = (_rope(
            zswa_ref[:, j * LANES:(j + 1) * LANES], cos, sin) * ATT_SCALE).astype(BF16)
    k_scr[...] = _rope(zswa_ref[:, SWA_W:SWA_W + SWA_KV_W], cos, sin).astype(BF16)

    bl = SWA_BLOCK
    nb = DEC_SEQ // bl
    n_q = SWA_GROUP * bl
    for kv in range(SWA_KV_HEADS):
        ck = ck_ref[kv].astype(BF16)
        cv = cv_ref[kv].astype(BF16)
        sink_col = sink_ref[kv * n_q:(kv + 1) * n_q, :]
        v_col = SWA_W + SWA_KV_W + kv * D_HEAD
        for blk in range(nb):
            lo = max(blk - 1, 0) * bl
            hi = min(blk + 2, nb) * bl
            n_band = hi - lo
            q = jnp.concatenate(
                [q_scr[blk * bl:(blk + 1) * bl,
                       (kv * SWA_GROUP + g) * D_HEAD:(kv * SWA_GROUP + g + 1) * D_HEAD]
                 for g in range(SWA_GROUP)], axis=0)
            k = jnp.concatenate([k_scr[lo:hi, kv * D_HEAD:(kv + 1) * D_HEAD], ck], axis=0)
            v = jnp.concatenate(
                [zswa_ref[lo:hi, v_col:v_col + D_HEAD].astype(BF16), cv], axis=0)
            n_keys = n_band + PAST_LEN
            s = _dot_nt(q, k)
            row = lax.broadcasted_iota(jnp.int32, (n_q, n_keys), 0)
            col = lax.broadcasted_iota(jnp.int32, (n_q, n_keys), 1)
            qpos = blk * bl + (row & (bl - 1))
            kpos = lo + col
            valid = (col >= n_band) | (jnp.abs(qpos - kpos) <= SWA_WINDOW)
            p, rden = _softmax_rows(jnp.where(valid, s, NEG_INF), sink_col)
            o = _dot(p, v) * rden
            for g in range(SWA_GROUP):
                hq = kv * SWA_GROUP + g
                mswa_ref[blk * bl:(blk + 1) * bl, hq * D_HEAD:(hq + 1) * D_HEAD] = (
                    o[g * bl:(g + 1) * bl, :].astype(BF16))


def _lat_swa_call(z_swa, cos_t, sin_t, cache_k, cache_v, sink_col, layer):
    row_spec = lambda w: pl.BlockSpec((DEC_SEQ, w), lambda b: (b, 0))
    full = lambda a: pl.BlockSpec(a.shape, lambda b: (0,) * a.ndim)
    cache_spec = pl.BlockSpec((None, None, SWA_KV_HEADS, PAST_LEN, D_HEAD),
                              lambda b: (b, layer, 0, 0, 0))
    return pl.pallas_call(
        _lat_swa_kernel,
        grid=(DEC_BATCH,),
        in_specs=[row_spec(Z_SWA_W), full(cos_t), full(sin_t), cache_spec, cache_spec,
                  full(sink_col)],
        out_specs=row_spec(SWA_W),
        out_shape=jax.ShapeDtypeStruct((DEC_BATCH * DEC_SEQ, SWA_W), BF16),
        scratch_shapes=[pltpu.VMEM((DEC_SEQ, SWA_W), BF16), pltpu.VMEM((DEC_SEQ, SWA_KV_W), BF16)],
        compiler_params=_params(1),
        name="lat_swa",
    )(z_swa, cos_t, sin_t, cache_k, cache_v, sink_col)


NAT_ROWS = DEC_SEQ // GRID_W
NAT_GROUP_ROWS = 4
NAT_GROUP_Q = NAT_GROUP_ROWS * GRID_W
NAT_RPB_ROWS = 2 * NAT_KH


def _nat_row_start(r):
    return min(max(r - NAT_KH // 2, 0), NAT_ROWS - NAT_KH)


def _nat_groups():
    groups, off = [], 0
    for g in range(NAT_ROWS // NAT_GROUP_ROWS):
        w0 = _nat_row_start(g * NAT_GROUP_ROWS)
        w1 = _nat_row_start((g + 1) * NAT_GROUP_ROWS - 1) + NAT_KH
        n_rows = w1 - w0 + (w1 - w0) % 2
        w0 = min(w0, NAT_ROWS - n_rows)
        groups.append((w0, n_rows, off))
        off += n_rows * GRID_W
    return groups, off


NAT_GROUPS, NAT_BIAS_W = _nat_groups()


def _nat_build_bias(rpb_ref, bias_scr):
    shape = (GRID_W, LANES)
    q = lax.broadcasted_iota(jnp.int32, shape, 0)
    lane = lax.broadcasted_iota(jnp.int32, shape, 1)
    kc = lane & (GRID_W - 1)
    cs = jnp.clip(q - NAT_KW // 2, 0, GRID_W - NAT_KW)
    ok = (kc >= cs) & (kc < cs + NAT_KW)
    ok_lo = ok & (lane < GRID_W)
    ok_hi = ok & (lane >= GRID_W)
    neg_tile = jnp.full(shape, NEG_INF, F32)
    for hd in range(2):
        lo, hi = [], []
        for ro in range(2 * NAT_KH - 1):
            x = jnp.broadcast_to(rpb_ref[hd, ro:ro + 1, :], shape)
            lo.append(jnp.where(ok_lo, pltpu.roll(x, 0, 1, stride=1, stride_axis=0), NEG_INF))
            hi.append(jnp.where(ok_hi, pltpu.roll(x, GRID_W, 1, stride=1, stride_axis=0), NEG_INF))
        for g, (w0, n_rows, off) in enumerate(NAT_GROUPS):
            for rr in range(NAT_GROUP_ROWS):
                r = g * NAT_GROUP_ROWS + rr
                r0 = _nat_row_start(r)
                for jj in range(n_rows // 2):
                    kra, krb = w0 + 2 * jj, w0 + 2 * jj + 1
                    parts = []
                    if r0 <= kra < r0 + NAT_KH:
                        parts.append(lo[kra - r + NAT_KH - 1])
                    if r0 <= krb < r0 + NAT_KH:
                        parts.append(hi[krb - r + NAT_KH - 1])
                    tile = neg_tile if not parts else (
                        parts[0] if len(parts) == 1 else jnp.maximum(parts[0], parts[1]))
                    bias_scr[hd, rr * GRID_W:(rr + 1) * GRID_W,
                             off + jj * LANES:off + (jj + 1) * LANES] = tile


def _lat_nat_kernel(q_ref, k_ref, v_ref, ck_ref, cv_ref, rpb_ref, mnat_ref, bias_scr):
    @pl.when(pl.program_id(1) == 0)
    def _():
        _nat_build_bias(rpb_ref, bias_scr)

    for hd in range(2):
        lanes = slice(hd * D_HEAD, (hd + 1) * D_HEAD)
        ck = ck_ref[hd].astype(BF16)
        cv = cv_ref[hd].astype(BF16)
        for g, (w0, n_rows, off) in enumerate(NAT_GROUPS):
            q = (q_ref[g * NAT_GROUP_Q:(g + 1) * NAT_GROUP_Q, lanes] * ATT_SCALE).astype(BF16)
            k = k_ref[w0 * GRID_W:(w0 + n_rows) * GRID_W, lanes]
            v = v_ref[w0 * GRID_W:(w0 + n_rows) * GRID_W, lanes]
            s_nb = _dot_nt(q, k) + bias_scr[hd, :, off:off + n_rows * GRID_W]
            s_ctx = _dot_nt(q, ck)
            m = jnp.maximum(jnp.max(s_nb, axis=1, keepdims=True),
                            jnp.max(s_ctx, axis=1, keepdims=True))
            p_nb = jnp.exp(s_nb - m)
            p_ctx = jnp.exp(s_ctx - m)
            den = jnp.sum(p_nb, axis=1, keepdims=True) + jnp.sum(p_ctx, axis=1, keepdims=True)
            o = (_dot(p_nb, v) + _dot(p_ctx, cv)) * (1.0 / den)
            mnat_ref[g * NAT_GROUP_Q:(g + 1) * NAT_GROUP_Q, lanes] = o.astype(BF16)


def _lat_nat_call(z_nat, cache_k, cache_v, rpb_lanes, layer):
    n_pair = NAT_HEADS // 2
    col_spec = lambda off: pl.BlockSpec((DEC_SEQ, LANES), lambda p, b: (b, off + p))
    cache_spec = pl.BlockSpec((None, None, 2, PAST_LEN, D_HEAD), lambda p, b: (b, layer, p, 0, 0))
    return pl.pallas_call(
        _lat_nat_kernel,
        grid=(n_pair, DEC_BATCH),
        in_specs=[col_spec(0), col_spec(n_pair), col_spec(2 * n_pair), cache_spec, cache_spec,
                  pl.BlockSpec((2, NAT_RPB_ROWS, LANES), lambda p, b: (p, 0, 0))],
        out_specs=pl.BlockSpec((DEC_SEQ, LANES), lambda p, b: (b, p)),
        out_shape=jax.ShapeDtypeStruct((DEC_BATCH * DEC_SEQ, NAT_W), BF16),
        scratch_shapes=[pltpu.VMEM((2, NAT_GROUP_Q, NAT_BIAS_W), F32)],
        compiler_params=_params(2),
        name="lat_nat",
    )(z_nat, z_nat, z_nat, cache_k, cache_v, rpb_lanes)


def _nat_rpb_lanes(rpb):
    n_off = 2 * NAT_KW - 1
    padded = jnp.concatenate(
        [rpb.astype(F32), jnp.zeros(rpb.shape[:-1] + (LANES - n_off,), F32)], axis=-1)
    rolled = jnp.roll(padded, -(NAT_KW - 1), axis=-1)
    pad_rows = jnp.zeros(rpb.shape[:-2] + (NAT_RPB_ROWS - rpb.shape[-2], LANES), F32)
    return jnp.concatenate([rolled, pad_rows], axis=-2)


def _rope_tables():
    t = np.arange(DEC_SEQ)[:, None]
    d = np.arange(LANES)[None, :] % D_HEAD
    pos = np.where(d < D_HEAD // 2, t // GRID_W, t % GRID_W).astype(np.float64)
    quarter = D_HEAD // 4
    freq = ROPE_BASE ** (-(d % quarter).astype(np.float64) / quarter)
    ang = (pos.astype(np.float32) * freq.astype(np.float32)).astype(np.float32)
    sign = np.where((d % (2 * quarter)) < quarter, -1.0, 1.0)
    return (jnp.asarray(np.cos(ang), dtype=F32), jnp.asarray(np.sin(ang) * sign, dtype=F32))


def _permute_in_columns(a):
    g0 = Z_ML_W
    s0 = g0 + N_GATES
    pad = jnp.zeros(a.shape[:-1] + (Z_G_W - N_GATES,), a.dtype)
    return jnp.concatenate([a[..., :g0], a[..., s0:], a[..., g0:s0], pad], axis=-1)


def kernel(x_prompt, x_sample, cache_swa_k, cache_swa_v, cache_nat_k, cache_nat_v, state_mlstm_C,
           state_mlstm_n, state_mlstm_m, c, c_ctx, w_ada, b_ada, w_in, b_in, mlstm_fbias,
           mlstm_norm_g, swa_sink, nat_rpb, w_out, ln1_g, ln1_b, w_mlp1, w_mlp2, ln2_g, ln2_b):
    xp = x_prompt.reshape(BATCH * SEQ, D_MODEL)
    xs = x_sample.reshape(DEC_BATCH * DEC_SEQ, D_MODEL)
    cvec = jnp.concatenate(
        [c, c_ctx[None, :], jnp.zeros((MOD_ROWS - DEC_BATCH - 1, D_MODEL), F32)], axis=0)
    mods = _mods_call(cvec, w_ada, b_ada).reshape(DEPTH, MOD_ROWS, 6, D_MODEL)

    w_in_p = _permute_in_columns(w_in).astype(BF16)
    b_in_p = _permute_in_columns(b_in).reshape(DEPTH, 1, Z_W)
    w_out_b = w_out.astype(BF16)
    w_mlp1_b = w_mlp1.astype(BF16)
    w_mlp2_b = w_mlp2.astype(BF16)
    cos_t, sin_t = _rope_tables()
    rpb_lanes = _nat_rpb_lanes(nat_rpb)

    n_st = 2 * ML_HEADS
    state_c = state_mlstm_C.reshape(DEC_BATCH, DEPTH, n_st, D_HEAD, D_HEAD)
    state_n = state_mlstm_n.reshape(DEC_BATCH, DEPTH, n_st, D_HEAD)
    state_m = state_mlstm_m.reshape(DEC_BATCH, DEPTH, n_st, 1)

    ctx_row = lambda i: CTX_MOD_ROW
    lat_row = lambda i: i // (DEC_SEQ // ROW_TILE)

    outs = [[] for _ in range(7)]
    for l in range(DEPTH):
        fb_row = jnp.zeros((1, Z_G_W), F32)
        fb_row = fb_row.at[0, ML_HEADS:2 * ML_HEADS].set(mlstm_fbias[l, 0])
        fb_row = fb_row.at[0, 3 * ML_HEADS:4 * ML_HEADS].set(mlstm_fbias[l, 1])
        norm_g = mlstm_norm_g[l].reshape(1, ML_W)
        sink_ctx = jnp.repeat(swa_sink[l], SEQ).reshape(SWA_HEADS * SEQ, 1)
        sink_lat = jnp.repeat(swa_sink[l], SWA_BLOCK).reshape(SWA_HEADS * SWA_BLOCK, 1)
        tail_w = (w_out_b[l], ln1_g[l], ln1_b[l], w_mlp1_b[l], w_mlp2_b[l], ln2_g[l], ln2_b[l])

        z_ml, z_swa, z_nat, z_g = _inproj_call(xp, mods[l], ctx_row, w_in_p[l], b_in_p[l])
        (m_ml, m_swa, m_nat, k_swa, v_swa, k_nat, v_nat, st_c, st_n, st_m) = _ctx_mixers_call(
            z_ml, z_swa, z_nat, z_g, fb_row, norm_g, sink_ctx)
        xp = _tail_call(xp, m_ml, m_swa, m_nat, mods[l], ctx_row, *tail_w)
        for lst, a in zip(outs, (k_swa, v_swa, k_nat, v_nat,
                                 st_c.reshape(BATCH, 2, ML_HEADS, D_HEAD, D_HEAD),
                                 st_n.reshape(BATCH, 2, ML_HEADS, D_HEAD),
                                 st_m[:, :, 0].reshape(BATCH, 2, ML_HEADS))):
            lst.append(a)

        z_ml, z_swa, z_nat, z_g = _inproj_call(xs, mods[l], lat_row, w_in_p[l], b_in_p[l])
        m_ml = _lat_mlstm_call(z_ml, z_g, fb_row, norm_g, state_c, state_n, state_m, l)
        m_swa = _lat_swa_call(z_swa, cos_t, sin_t, cache_swa_k, cache_swa_v, sink_lat, l)
        m_nat = _lat_nat_call(z_nat, cache_nat_k, cache_nat_v, rpb_lanes[l], l)
        xs = _tail_call(xs, m_ml, m_swa, m_nat, mods[l], lat_row, *tail_w)

    stacked = [jnp.stack(lst, axis=1) for lst in outs]
    return (xp.reshape(BATCH, SEQ, D_MODEL), xs.reshape(DEC_BATCH, DEC_SEQ, D_MODEL), *stacked)
```

```python
import numpy as np
import jax
import jax.numpy as jnp
from jax import lax
from jax.experimental import pallas as pl
from jax.experimental.pallas import tpu as pltpu

D_MODEL = 1024
BATCH = 16
SEQ = 256
DEPTH = 4
DEC_BATCH = 8
DEC_SEQ = 1024
PAST_LEN = 512
GRID_W = 64
D_HEAD = 64
ML_HEADS = 4
SWA_HEADS = 6
SWA_KV_HEADS = 2
SWA_GROUP = SWA_HEADS // SWA_KV_HEADS
NAT_HEADS = 6
ML_W = ML_HEADS * D_HEAD
SWA_W = SWA_HEADS * D_HEAD
SWA_KV_W = SWA_KV_HEADS * D_HEAD
NAT_W = NAT_HEADS * D_HEAD
N_GATES = 4 * ML_HEADS
D_FF = 4 * D_MODEL
SWA_WINDOW = 128
SWA_BLOCK = 128
NAT_KH = 8
NAT_KW = 16
ROPE_BASE = 10000.0
LN_EPS = 1e-5
DN_ALPHA = (2 * DEPTH) ** 0.25
ATT_SCALE = D_HEAD ** -0.5

LANES = 128
SUBLANES = 8
Z_ML_W = 4 * ML_W
Z_SWA_W = SWA_W + 2 * SWA_KV_W
Z_NAT_W = 3 * NAT_W
Z_G_W = LANES
Z_W = Z_ML_W + Z_SWA_W + Z_NAT_W + Z_G_W
MOD_ROWS = 16
CTX_MOD_ROW = DEC_BATCH
ROW_TILE = 256
ML_CHUNK = 256
VMEM_LIMIT = 56 * 1024 * 1024

BF16 = jnp.bfloat16
F32 = jnp.float32
NEG_INF = float("-inf")


def _dot(a, b):
    return jnp.dot(a.astype(BF16), b.astype(BF16), preferred_element_type=F32)


def _dot_split3(tri, x):
    hi = x.astype(BF16)
    r1 = x - hi.astype(F32)
    mid = r1.astype(BF16)
    lo = (r1 - mid.astype(F32)).astype(BF16)
    return (jnp.dot(tri, hi, preferred_element_type=F32)
            + jnp.dot(tri, mid, preferred_element_type=F32)
            + jnp.dot(tri, lo, preferred_element_type=F32))


def _sigmoid(x):
    return 1.0 / (1.0 + jnp.exp(-x))


def _log_sigmoid(x):
    return jnp.minimum(x, 0.0) - jnp.log1p(jnp.exp(-jnp.abs(x)))


def _layer_norm(x, g, b):
    mu = jnp.mean(x, axis=-1, keepdims=True)
    xc = x - mu
    var = jnp.mean(xc * xc, axis=-1, keepdims=True)
    return xc * lax.rsqrt(var + LN_EPS) * g + b


def _params(n_axes=1):
    return pltpu.CompilerParams(dimension_semantics=("arbitrary",) * n_axes,
                                vmem_limit_bytes=VMEM_LIMIT)


def _mods_kernel(c_ref, w_ref, b_ref, o_ref):
    c = c_ref[...]
    o_ref[...] = _dot(c * _sigmoid(c), w_ref[...]) + b_ref[...]


def _mods_call(cvec, w_ada, b_ada):
    n_col = 6
    return pl.pallas_call(
        _mods_kernel,
        grid=(DEPTH, n_col),
        in_specs=[
            pl.BlockSpec((MOD_ROWS, D_MODEL), lambda l, j: (0, 0)),
            pl.BlockSpec((None, D_MODEL, D_MODEL), lambda l, j: (l, 0, j)),
            pl.BlockSpec((None, 1, D_MODEL), lambda l, j: (l, 0, j)),
        ],
        out_specs=pl.BlockSpec((None, MOD_ROWS, D_MODEL), lambda l, j: (l, 0, j)),
        out_shape=jax.ShapeDtypeStruct((DEPTH, MOD_ROWS, 6 * D_MODEL), F32),
        compiler_params=_params(2),
        name="mods",
    )(cvec, w_ada, b_ada.reshape(DEPTH, 1, 6 * D_MODEL))


def _inproj_kernel(x_ref, mod_ref, w_ref, b_ref, zml_ref, zswa_ref, znat_ref, zg_ref):
    h = (x_ref[...] * (1.0 + mod_ref[1:2, :]) + mod_ref[0:1, :]).astype(BF16)
    o = 0
    for ref in (zml_ref, zswa_ref, znat_ref, zg_ref):
        w = ref.shape[-1]
        ref[...] = jnp.dot(h, w_ref[:, o:o + w], preferred_element_type=F32) + b_ref[:, o:o + w]
        o += w


def _inproj_call(x, mods_l, mod_row_of_tile, w_in, b_in):
    rows = x.shape[0]
    widths = (Z_ML_W, Z_SWA_W, Z_NAT_W, Z_G_W)
    return pl.pallas_call(
        _inproj_kernel,
        grid=(rows // ROW_TILE,),
        in_specs=[
            pl.BlockSpec((ROW_TILE, D_MODEL), lambda i: (i, 0)),
            pl.BlockSpec((None, 6, D_MODEL), lambda i: (mod_row_of_tile(i), 0, 0)),
            pl.BlockSpec((D_MODEL, Z_W), lambda i: (0, 0)),
            pl.BlockSpec((1, Z_W), lambda i: (0, 0)),
        ],
        out_specs=[pl.BlockSpec((ROW_TILE, w), lambda i: (i, 0)) for w in widths],
        out_shape=[jax.ShapeDtypeStruct((rows, w), F32) for w in widths],
        compiler_params=_params(1),
        name="inproj",
    )(x, mods_l, w_in, b_in)


def _tail_kernel(x_ref, mml_ref, mswa_ref, mnat_ref, mod_ref, wout_ref, ln1g_ref, ln1b_ref,
                 w1_ref, w2_ref, ln2g_ref, ln2b_ref, o_ref):
    g_a, sh_m, sc_m, g_m = (mod_ref[i:i + 1, :] for i in (2, 3, 4, 5))
    mix = jnp.concatenate([mml_ref[...], mswa_ref[...], mnat_ref[...]], axis=1)
    proj = jnp.dot(mix, wout_ref[...], preferred_element_type=F32)
    y = _layer_norm(DN_ALPHA * x_ref[...] + g_a * proj, ln1g_ref[...], ln1b_ref[...])
    h = (y * (1.0 + sc_m) + sh_m).astype(BF16)
    f = jnp.maximum(jnp.dot(h, w1_ref[...], preferred_element_type=F32), 0.0)
    f = jnp.dot((f * f).astype(BF16), w2_ref[...], preferred_element_type=F32)
    o_ref[...] = _layer_norm(DN_ALPHA * y + g_m * f, ln2g_ref[...], ln2b_ref[...])


def _tail_call(x, mix_ml, mix_swa, mix_nat, mods_l, mod_row_of_tile, w_out, ln1_g, ln1_b,
               w_mlp1, w_mlp2, ln2_g, ln2_b):
    rows = x.shape[0]
    row_spec = lambda w: pl.BlockSpec((ROW_TILE, w), lambda i: (i, 0))
    full = lambda a: pl.BlockSpec(a.shape, lambda i: (0,) * a.ndim)
    vec = lambda a: a.reshape(1, D_MODEL)
    consts = (w_out, vec(ln1_g), vec(ln1_b), w_mlp1, w_mlp2, vec(ln2_g), vec(ln2_b))
    return pl.pallas_call(
        _tail_kernel,
        grid=(rows // ROW_TILE,),
        in_specs=[row_spec(D_MODEL), row_spec(ML_W), row_spec(SWA_W), row_spec(NAT_W),
                  pl.BlockSpec((None, 6, D_MODEL), lambda i: (mod_row_of_tile(i), 0, 0))]
                 + [full(a) for a in consts],
        out_specs=row_spec(D_MODEL),
        out_shape=jax.ShapeDtypeStruct((rows, D_MODEL), F32),
        compiler_params=_params(1),
        name="tail",
    )(x, mix_ml, mix_swa, mix_nat, mods_l, *consts)


def _mlstm_direction(qt, k, vt, a_col, b_row, li_row, neg, last, c_st, n_st, m_st):
    a_m = a_col + neg
    c_row = jnp.maximum(m_st, jnp.max(a_m, axis=0, keepdims=True))
    decay = jnp.exp(a_m - c_row)
    wprev = jnp.exp(m_st - c_row)
    mt = b_row + c_row
    qt_b = qt.astype(BF16)
    st = _dot(k, qt_b) * decay
    den = jnp.sum(st, axis=0, keepdims=True)
    num = _dot(vt, st)
    if c_st is not None:
        cn = _dot(jnp.concatenate([c_st, jnp.broadcast_to(n_st, (SUBLANES, D_HEAD))], axis=0),
                  qt_b)
        num = num + wprev * cn[0:D_HEAD, :]
        den = den + wprev * cn[D_HEAD:D_HEAD + 1, :]
    ht = num * (1.0 / jnp.maximum(jnp.abs(den), jnp.exp(-mt)))
    m_new = mt[:, last:last + 1]
    ws = jnp.exp(b_row[:, last:last + 1] - b_row + li_row - m_new)
    upd = _dot(jnp.concatenate([vt * ws, jnp.broadcast_to(ws, (SUBLANES, ws.shape[1]))], axis=0),
               k)
    c_new = upd[0:D_HEAD, :]
    n_new = upd[D_HEAD:D_HEAD + 1, :]
    if c_st is not None:
        wc = wprev[:, last:last + 1]
        c_new = c_new + wc * c_st
        n_new = n_new + wc * n_st
    return ht, c_new, n_new, m_new


def _mlstm_chunk(zml_ref, g_ref, fb_ref, rows, direction, state):
    n = ML_CHUNK
    g = g_ref[rows, :]
    ls = _log_sigmoid(g + fb_ref[...])
    r = lax.broadcasted_iota(jnp.int32, (n, n), 0)
    c = lax.broadcasted_iota(jnp.int32, (n, n), 1)
    if direction == 0:
        tri, neg, last = c <= r, jnp.where(r <= c, 0.0, NEG_INF), n - 1
    else:
        tri, neg, last = c >= r, jnp.where(r >= c, 0.0, NEG_INF), 0
    b = _dot_split3(tri.astype(BF16), ls)
    a = pltpu.roll(g, ML_HEADS, 1) - b
    bt = b.T
    gt = g.T
    i_col = 2 * ML_HEADS * direction
    f_col = i_col + ML_HEADS
    hts, new_state = [], []
    for pair in range(ML_HEADS // 2):
        qt2 = zml_ref[rows, pair * LANES:(pair + 1) * LANES].T
        vt2 = zml_ref[rows, 2 * ML_W + pair * LANES:2 * ML_W + (pair + 1) * LANES].T
        for sub in range(2):
            hd = 2 * pair + sub
            feat = slice(sub * D_HEAD, (sub + 1) * D_HEAD)
            k = zml_ref[rows, ML_W + hd * D_HEAD:ML_W + (hd + 1) * D_HEAD] * ATT_SCALE
            ht, c_new, n_new, m_new = _mlstm_direction(
                qt2[feat, :], k, vt2[feat, :],
                a[:, f_col + hd:f_col + hd + 1], bt[f_col + hd:f_col + hd + 1, :],
                gt[i_col + hd:i_col + hd + 1, :], neg, last, *state[hd])
            hts.append(ht)
            new_state.append((c_new, n_new, m_new))
    return jnp.concatenate(hts, axis=0), new_state


def _mlstm_finish(ht, o_gate, norm_g):
    outs = []
    for hd in range(ML_HEADS):
        x = ht[hd * D_HEAD:(hd + 1) * D_HEAD, :]
        mu = jnp.mean(x, axis=0, keepdims=True)
        xc = x - mu
        var = jnp.mean(xc * xc, axis=0, keepdims=True)
        outs.append(xc * lax.rsqrt(var + LN_EPS))
    hn = jnp.concatenate(outs, axis=0).T
    return hn * norm_g * _sigmoid(o_gate)


def _store_state(c_ref, n_ref, m_ref, idx, c_new, n_new, m_new):
    c_ref[idx] = c_new
    n_ref[idx:idx + 1, :] = n_new
    m_ref[idx:idx + 1, :] = jnp.broadcast_to(m_new, (1, LANES))


ONES_ROWS = 16


def _with_ones(vt):
    ones = jnp.ones((ONES_ROWS, vt.shape[1]), BF16)
    return jnp.concatenate([vt.astype(BF16), ones], axis=0)


def _attend_t(qt, parts, sink_row=None):
    sts = []
    for k, _, bias_t in parts:
        st = jnp.dot(k, qt, preferred_element_type=F32)
        sts.append(st if bias_t is None else st + bias_t)
    m = None
    for st in sts:
        mx = jnp.max(st, axis=0, keepdims=True)
        m = mx if m is None else jnp.maximum(m, mx)
    if sink_row is not None:
        m = jnp.maximum(m, sink_row)
    ot = None
    for st, (_, vt_ones, _) in zip(sts, parts):
        part = jnp.dot(vt_ones, jnp.exp(st - m).astype(BF16), preferred_element_type=F32)
        ot = part if ot is None else ot + part
    den = ot[D_HEAD:D_HEAD + 1, :]
    if sink_row is not None:
        den = den + jnp.exp(sink_row - m)
    return ot[0:D_HEAD, :] * (1.0 / den)


def _head_rows(pairs_t, hd):
    return pairs_t[hd // 2][(hd % 2) * D_HEAD:(hd % 2 + 1) * D_HEAD, :]


def _store_heads_t(o_ref, ots):
    for p in range(len(ots) // 2):
        o_ref[:, p * LANES:(p + 1) * LANES] = jnp.concatenate(
            [ots[2 * p], ots[2 * p + 1]], axis=0).T.astype(o_ref.dtype)


N_STATE = 2 * ML_HEADS
N_CTX_STATE_OUTS = 7


def _ctx_mixers_kernel(zml_ref, zswa_ref, znat_ref, zg_ref, fb_ref, ng_ref, sink_ref, *rest):
    (mml_ref, mswa_ref, mnat_ref, sk_ref, sv_ref, nk_ref, nv_ref,
     c_ref, n_ref, m_ref) = rest[-(3 + N_CTX_STATE_OUTS):]
    rows = slice(0, SEQ)
    zero_state = [(None, jnp.zeros((1, D_HEAD), F32), jnp.zeros((1, 1), F32))] * ML_HEADS
    ht_sum = None
    for direction in range(2):
        ht, new_state = _mlstm_chunk(zml_ref, zg_ref, fb_ref, rows, direction, zero_state)
        ht_sum = ht if ht_sum is None else ht_sum + ht
        for hd in range(ML_HEADS):
            _store_state(c_ref, n_ref, m_ref, direction * ML_HEADS + hd, *new_state[hd])
    mml_ref[...] = _mlstm_finish(ht_sum, zml_ref[:, 3 * ML_W:4 * ML_W], ng_ref[...]).astype(BF16)

    qts = [(zswa_ref[:, p * LANES:(p + 1) * LANES] * ATT_SCALE).T.astype(BF16)
           for p in range(SWA_W // LANES)]
    vts = [zswa_ref[:, SWA_W + SWA_KV_W:SWA_W + 2 * SWA_KV_W].T]
    ots = []
    for kv in range(SWA_KV_HEADS):
        k = zswa_ref[:, SWA_W + kv * D_HEAD:SWA_W + (kv + 1) * D_HEAD]
        sk_ref[kv] = k
        sv_ref[kv] = zswa_ref[:, SWA_W + SWA_KV_W + kv * D_HEAD:SWA_W + SWA_KV_W + (kv + 1) * D_HEAD]
        qt = jnp.concatenate([_head_rows(qts, kv * SWA_GROUP + g) for g in range(SWA_GROUP)], axis=1)
        ot = _attend_t(qt, [(k.astype(BF16), _with_ones(_head_rows(vts, kv)), None)],
                       sink_ref[kv:kv + 1, :])
        ots += [ot[:, g * SEQ:(g + 1) * SEQ] for g in range(SWA_GROUP)]
    _store_heads_t(mswa_ref, ots)

    n_pair = NAT_W // LANES
    qts = [(znat_ref[:, p * LANES:(p + 1) * LANES] * ATT_SCALE).T.astype(BF16)
           for p in range(n_pair)]
    vts = [znat_ref[:, 2 * NAT_W + p * LANES:2 * NAT_W + (p + 1) * LANES].T for p in range(n_pair)]
    ots = []
    for hd in range(NAT_HEADS):
        k = znat_ref[:, NAT_W + hd * D_HEAD:NAT_W + (hd + 1) * D_HEAD]
        nk_ref[hd] = k
        nv_ref[hd] = znat_ref[:, 2 * NAT_W + hd * D_HEAD:2 * NAT_W + (hd + 1) * D_HEAD]
        ots.append(_attend_t(_head_rows(qts, hd),
                             [(k.astype(BF16), _with_ones(_head_rows(vts, hd)), None)]))
    _store_heads_t(mnat_ref, ots)


def _ctx_mixers_call(z_ml, z_swa, z_nat, z_g, fb_row, norm_g, sink_rows, layer, prev_state):
    rows = z_ml.shape[0]
    row_spec = lambda w: pl.BlockSpec((SEQ, w), lambda b: (b, 0))
    full = lambda a: pl.BlockSpec(a.shape, lambda b: (0,) * a.ndim)
    state_dims = ((SWA_KV_HEADS, SEQ, D_HEAD), (SWA_KV_HEADS, SEQ, D_HEAD),
                  (NAT_HEADS, SEQ, D_HEAD), (NAT_HEADS, SEQ, D_HEAD),
                  (N_STATE, D_HEAD, D_HEAD), (N_STATE, D_HEAD), (N_STATE, LANES))
    state_spec = lambda d: pl.BlockSpec((None, None) + d, lambda b: (b, layer) + (0,) * len(d))
    prev = () if prev_state is None else tuple(prev_state)
    n_in = 7
    return pl.pallas_call(
        _ctx_mixers_kernel,
        grid=(BATCH,),
        in_specs=[row_spec(Z_ML_W), row_spec(Z_SWA_W), row_spec(Z_NAT_W), row_spec(Z_G_W),
                  full(fb_row), full(norm_g), full(sink_rows)]
                 + [pl.BlockSpec(memory_space=pl.ANY)] * len(prev),
        out_specs=[row_spec(ML_W), row_spec(SWA_W), row_spec(NAT_W)]
                  + [state_spec(d) for d in state_dims],
        out_shape=[jax.ShapeDtypeStruct((rows, ML_W), BF16),
                   jax.ShapeDtypeStruct((rows, SWA_W), BF16),
                   jax.ShapeDtypeStruct((rows, NAT_W), BF16)]
                  + [jax.ShapeDtypeStruct((BATCH, DEPTH) + d, F32) for d in state_dims],
        input_output_aliases={n_in + i: 3 + i for i in range(len(prev))},
        compiler_params=_params(1),
        name="ctx_mixers",
    )(z_ml, z_swa, z_nat, z_g, fb_row, norm_g, sink_rows, *prev)


def _lat_mlstm_kernel(zml_ref, zg_ref, fb_ref, ng_ref, c0_ref, n0_ref, m0_ref, mml_ref,
                      ht_scr, c_scr, n_scr, m_scr):
    n_chunks = DEC_SEQ // ML_CHUNK
    c_scr[...] = c0_ref[...]
    n_scr[...] = n0_ref[...]
    m_scr[...] = jnp.broadcast_to(m0_ref[...], m_scr.shape)

    def body(i, carry):
        for direction in range(2):
            chunk = i if direction == 0 else n_chunks - 1 - i
            rows = pl.ds(pl.multiple_of(chunk * ML_CHUNK, ML_CHUNK), ML_CHUNK)
            base = direction * ML_HEADS
            state = [(c_scr[base + hd], n_scr[base + hd:base + hd + 1, :],
                      m_scr[base + hd:base + hd + 1, 0:1]) for hd in range(ML_HEADS)]
            ht, new_state = _mlstm_chunk(zml_ref, zg_ref, fb_ref, rows, direction, state)
            ht_scr[direction, chunk] = ht
            for hd in range(ML_HEADS):
                _store_state(c_scr, n_scr, m_scr, base + hd, *new_state[hd])
        return carry

    lax.fori_loop(0, n_chunks, body, 0)
    for chunk in range(n_chunks):
        rows = slice(chunk * ML_CHUNK, (chunk + 1) * ML_CHUNK)
        mml_ref[rows, :] = _mlstm_finish(ht_scr[0, chunk] + ht_scr[1, chunk],
                                         zml_ref[rows, 3 * ML_W:4 * ML_W],
                                         ng_ref[...]).astype(BF16)


def _lat_mlstm_call(z_ml, z_g, fb_row, norm_g, state_c, state_n, state_m, layer):
    n_st = 2 * ML_HEADS
    n_chunks = DEC_SEQ // ML_CHUNK
    row_spec = lambda w: pl.BlockSpec((DEC_SEQ, w), lambda b: (b, 0))
    full = lambda a: pl.BlockSpec(a.shape, lambda b: (0,) * a.ndim)
    return pl.pallas_call(
        _lat_mlstm_kernel,
        grid=(DEC_BATCH,),
        in_specs=[row_spec(Z_ML_W), row_spec(Z_G_W), full(fb_row), full(norm_g),
                  pl.BlockSpec((None, None, n_st, D_HEAD, D_HEAD), lambda b: (b, layer, 0, 0, 0)),
                  pl.BlockSpec((None, None, n_st, D_HEAD), lambda b: (b, layer, 0, 0)),
                  pl.BlockSpec((None, None, n_st, 1), lambda b: (b, layer, 0, 0))],
        out_specs=row_spec(ML_W),
        out_shape=jax.ShapeDtypeStruct((DEC_BATCH * DEC_SEQ, ML_W), BF16),
        scratch_shapes=[pltpu.VMEM((2, n_chunks, ML_W, ML_CHUNK), F32),
                        pltpu.VMEM((n_st, D_HEAD, D_HEAD), F32), pltpu.VMEM((n_st, D_HEAD), F32),
                        pltpu.VMEM((n_st, LANES), F32)],
        compiler_params=_params(1),
        name="lat_mlstm",
    )(z_ml, z_g, fb_row, norm_g, state_c, state_n, state_m)


def _rope(x, cos, sin_signed):
    lane = lax.broadcasted_iota(jnp.int32, x.shape, 1)
    first = (lane & 31) < 16
    partner = jnp.where(first, pltpu.roll(x, LANES - 16, 1), pltpu.roll(x, 16, 1))
    return x * cos + partner * sin_signed


def _lat_swa_kernel(zswa_ref, cos_ref, sin_ref, ck_ref, cv_ref, sink_ref, mswa_ref,
                    qt_scr, k_scr, vt_scr, ot_scr):
    cos = cos_ref[...]
    sin = sin_ref[...]
    for j in range(SWA_W // LANES):
        qt_scr[j * LANES:(j + 1) * LANES, :] = (_rope(
            zswa_ref[:, j * LANES:(j + 1) * LANES], cos, sin) * ATT_SCALE).T.astype(BF16)
    k_scr[...] = _rope(zswa_ref[:, SWA_W:SWA_W + SWA_KV_W], cos, sin).astype(BF16)
    vt_scr[...] = zswa_ref[:, SWA_W + SWA_KV_W:SWA_W + 2 * SWA_KV_W].T.astype(BF16)
    cvt = jnp.concatenate([cv_ref[kv] for kv in range(SWA_KV_HEADS)], axis=1).T

    bl = SWA_BLOCK
    nb = DEC_SEQ // bl
    n_q = SWA_GROUP * bl
    kj = lax.broadcasted_iota(jnp.int32, (3 * bl, n_q), 0)
    qi = lax.broadcasted_iota(jnp.int32, (3 * bl, n_q), 1) & (bl - 1)
    keep = ((kj >= qi) & (kj < 2 * bl)) | ((kj >= 2 * bl) & (kj - 2 * bl <= qi))
    band_bias = jnp.where(keep, 0.0, NEG_INF)

    for kv in range(SWA_KV_HEADS):
        head = slice(kv * D_HEAD, (kv + 1) * D_HEAD)
        ck = ck_ref[kv].astype(BF16)
        cvt_ones = _with_ones(cvt[head, :])
        sink_row = sink_ref[kv:kv + 1, :]
        for blk in range(nb):
            lo_b, hi_b = max(blk - 1, 0), min(blk + 2, nb)
            keys = slice(lo_b * bl, hi_b * bl)
            bias = band_bias[(lo_b - blk + 1) * bl:(hi_b - blk + 1) * bl, :]
            qcols = slice(blk * bl, (blk + 1) * bl)
            qt = jnp.concatenate(
                [qt_scr[(kv * SWA_GROUP + g) * D_HEAD:(kv * SWA_GROUP + g + 1) * D_HEAD, qcols]
                 for g in range(SWA_GROUP)], axis=1)
            ot = _attend_t(qt, [(k_scr[keys, head], _with_ones(vt_scr[head, keys]), bias),
                                (ck, cvt_ones, None)], sink_row)
            for g in range(SWA_GROUP):
                hq = kv * SWA_GROUP + g
                ot_scr[hq * D_HEAD:(hq + 1) * D_HEAD, qcols] = ot[:, g * bl:(g + 1) * bl]
    for p in range(SWA_W // LANES):
        mswa_ref[:, p * LANES:(p + 1) * LANES] = ot_scr[p * LANES:(p + 1) * LANES, :].T.astype(BF16)


def _lat_swa_call(z_swa, cos_t, sin_t, cache_k, cache_v, sink_rows, layer):
    row_spec = lambda w: pl.BlockSpec((DEC_SEQ, w), lambda b: (b, 0))
    full = lambda a: pl.BlockSpec(a.shape, lambda b: (0,) * a.ndim)
    cache_spec = pl.BlockSpec((None, None, SWA_KV_HEADS, PAST_LEN, D_HEAD),
                              lambda b: (b, layer, 0, 0, 0))
    return pl.pallas_call(
        _lat_swa_kernel,
        grid=(DEC_BATCH,),
        in_specs=[row_spec(Z_SWA_W), full(cos_t), full(sin_t), cache_spec, cache_spec,
                  full(sink_rows)],
        out_specs=row_spec(SWA_W),
        out_shape=jax.ShapeDtypeStruct((DEC_BATCH * DEC_SEQ, SWA_W), BF16),
        scratch_shapes=[pltpu.VMEM((SWA_W, DEC_SEQ), BF16), pltpu.VMEM((DEC_SEQ, SWA_KV_W), BF16),
                        pltpu.VMEM((SWA_KV_W, DEC_SEQ), BF16), pltpu.VMEM((SWA_W, DEC_SEQ), F32)],
        compiler_params=_params(1),
        name="lat_swa",
    )(z_swa, cos_t, sin_t, cache_k, cache_v, sink_rows)


NAT_ROWS = DEC_SEQ // GRID_W
NAT_GROUP_ROWS = 4
NAT_GROUP_Q = NAT_GROUP_ROWS * GRID_W
NAT_RPB_ROWS = 2 * NAT_KH


def _nat_row_start(r):
    return min(max(r - NAT_KH // 2, 0), NAT_ROWS - NAT_KH)


def _nat_groups():
    groups, off = [], 0
    for g in range(NAT_ROWS // NAT_GROUP_ROWS):
        w0 = _nat_row_start(g * NAT_GROUP_ROWS)
        w1 = _nat_row_start((g + 1) * NAT_GROUP_ROWS - 1) + NAT_KH
        n_rows = w1 - w0 + (w1 - w0) % 2
        w0 = min(w0, NAT_ROWS - n_rows)
        groups.append((w0, n_rows, off))
        off += n_rows * GRID_W
    return groups, off


NAT_GROUPS, NAT_BIAS_KEYS = _nat_groups()


def _nat_build_bias(rpb_ref, bias_scr):
    shape = (GRID_W, LANES)
    q = lax.broadcasted_iota(jnp.int32, shape, 0)
    lane = lax.broadcasted_iota(jnp.int32, shape, 1)
    kc = lane & (GRID_W - 1)
    cs = jnp.clip(q - NAT_KW // 2, 0, GRID_W - NAT_KW)
    ok = (kc >= cs) & (kc < cs + NAT_KW)
    ok_lo = ok & (lane < GRID_W)
    ok_hi = ok & (lane >= GRID_W)
    neg_tile = jnp.full(shape, NEG_INF, F32)
    for hd in range(2):
        lo, hi = [], []
        for ro in range(2 * NAT_KH - 1):
            x = jnp.broadcast_to(rpb_ref[hd, ro:ro + 1, :], shape)
            lo.append(jnp.where(ok_lo, pltpu.roll(x, 0, 1, stride=1, stride_axis=0), NEG_INF))
            hi.append(jnp.where(ok_hi, pltpu.roll(x, GRID_W, 1, stride=1, stride_axis=0), NEG_INF))
        for g, (w0, n_rows, off) in enumerate(NAT_GROUPS):
            for jj in range(n_rows // 2):
                kra, krb = w0 + 2 * jj, w0 + 2 * jj + 1
                tiles = []
                for rr in range(NAT_GROUP_ROWS):
                    r = g * NAT_GROUP_ROWS + rr
                    r0 = _nat_row_start(r)
                    parts = []
                    if r0 <= kra < r0 + NAT_KH:
                        parts.append(lo[kra - r + NAT_KH - 1])
                    if r0 <= krb < r0 + NAT_KH:
                        parts.append(hi[krb - r + NAT_KH - 1])
                    tiles.append(neg_tile if not parts else (
                        parts[0] if len(parts) == 1 else jnp.maximum(parts[0], parts[1])))
                bias_scr[hd, off + jj * LANES:off + (jj + 1) * LANES, :] = (
                    jnp.concatenate(tiles, axis=0).T)


def _lat_nat_kernel(q_ref, k_ref, v_ref, ck_ref, cv_ref, rpb_ref, mnat_ref, bias_scr):
    @pl.when(pl.program_id(1) == 0)
    def _():
        _nat_build_bias(rpb_ref, bias_scr)

    qt = (q_ref[...] * ATT_SCALE).T.astype(BF16)
    vt = v_ref[...].T.astype(BF16)
    cvt = jnp.concatenate([cv_ref[0], cv_ref[1]], axis=1).T
    ots = []
    for hd in range(2):
        head = slice(hd * D_HEAD, (hd + 1) * D_HEAD)
        ck = ck_ref[hd].astype(BF16)
        cvt_ones = _with_ones(cvt[head, :])
        ot_groups = []
        for g, (w0, n_rows, off) in enumerate(NAT_GROUPS):
            keys = slice(w0 * GRID_W, (w0 + n_rows) * GRID_W)
            parts = [(k_ref[keys, head].astype(BF16), _with_ones(vt[head, keys]),
                      bias_scr[hd, off:off + n_rows * GRID_W, :]),
                     (ck, cvt_ones, None)]
            ot_groups.append(_attend_t(qt[head, g * NAT_GROUP_Q:(g + 1) * NAT_GROUP_Q], parts))
        ots.append(jnp.concatenate(ot_groups, axis=1))
    mnat_ref[...] = jnp.concatenate(ots, axis=0).T.astype(BF16)


def _lat_nat_call(z_nat, cache_k, cache_v, rpb_lanes, layer):
    n_pair = NAT_HEADS // 2
    col_spec = lambda off: pl.BlockSpec((DEC_SEQ, LANES), lambda p, b: (b, off + p))
    cache_spec = pl.BlockSpec((None, None, 2, PAST_LEN, D_HEAD), lambda p, b: (b, layer, p, 0, 0))
    return pl.pallas_call(
        _lat_nat_kernel,
        grid=(n_pair, DEC_BATCH),
        in_specs=[col_spec(0), col_spec(n_pair), col_spec(2 * n_pair), cache_spec, cache_spec,
                  pl.BlockSpec((2, NAT_RPB_ROWS, LANES), lambda p, b: (p, 0, 0))],
        out_specs=pl.BlockSpec((DEC_SEQ, LANES), lambda p, b: (b, p)),
        out_shape=jax.ShapeDtypeStruct((DEC_BATCH * DEC_SEQ, NAT_W), BF16),
        scratch_shapes=[pltpu.VMEM((2, NAT_BIAS_KEYS, NAT_GROUP_Q), F32)],
        compiler_params=_params(2),
        name="lat_nat",
    )(z_nat, z_nat, z_nat, cache_k, cache_v, rpb_lanes)


def _nat_rpb_lanes(rpb):
    n_off = 2 * NAT_KW - 1
    padded = jnp.concatenate(
        [rpb.astype(F32), jnp.zeros(rpb.shape[:-1] + (LANES - n_off,), F32)], axis=-1)
    rolled = jnp.roll(padded, -(NAT_KW - 1), axis=-1)
    pad_rows = jnp.zeros(rpb.shape[:-2] + (NAT_RPB_ROWS - rpb.shape[-2], LANES), F32)
    return jnp.concatenate([rolled, pad_rows], axis=-2)


def _rope_tables():
    t = np.arange(DEC_SEQ)[:, None]
    d = np.arange(LANES)[None, :] % D_HEAD
    pos = np.where(d < D_HEAD // 2, t // GRID_W, t % GRID_W).astype(np.float64)
    quarter = D_HEAD // 4
    freq = ROPE_BASE ** (-(d % quarter).astype(np.float64) / quarter)
    ang = (pos.astype(np.float32) * freq.astype(np.float32)).astype(np.float32)
    sign = np.where((d % (2 * quarter)) < quarter, -1.0, 1.0)
    return (jnp.asarray(np.cos(ang), dtype=F32), jnp.asarray(np.sin(ang) * sign, dtype=F32))


def _permute_in_columns(a):
    g0 = Z_ML_W
    s0 = g0 + N_GATES
    pad = jnp.zeros(a.shape[:-1] + (Z_G_W - N_GATES,), a.dtype)
    return jnp.concatenate([a[..., :g0], a[..., s0:], a[..., g0:s0], pad], axis=-1)


def kernel(x_prompt, x_sample, cache_swa_k, cache_swa_v, cache_nat_k, cache_nat_v, state_mlstm_C,
           state_mlstm_n, state_mlstm_m, c, c_ctx, w_ada, b_ada, w_in, b_in, mlstm_fbias,
           mlstm_norm_g, swa_sink, nat_rpb, w_out, ln1_g, ln1_b, w_mlp1, w_mlp2, ln2_g, ln2_b):
    xp = x_prompt.reshape(BATCH * SEQ, D_MODEL)
    xs = x_sample.reshape(DEC_BATCH * DEC_SEQ, D_MODEL)
    cvec = jnp.concatenate(
        [c, c_ctx[None, :], jnp.zeros((MOD_ROWS - DEC_BATCH - 1, D_MODEL), F32)], axis=0)
    mods = _mods_call(cvec, w_ada, b_ada).reshape(DEPTH, MOD_ROWS, 6, D_MODEL)

    w_in_p = _permute_in_columns(w_in).astype(BF16)
    b_in_p = _permute_in_columns(b_in).reshape(DEPTH, 1, Z_W)
    w_out_b = w_out.astype(BF16)
    w_mlp1_b = w_mlp1.astype(BF16)
    w_mlp2_b = w_mlp2.astype(BF16)
    cos_t, sin_t = _rope_tables()
    rpb_lanes = _nat_rpb_lanes(nat_rpb)

    n_st = 2 * ML_HEADS
    state_c = state_mlstm_C.reshape(DEC_BATCH, DEPTH, n_st, D_HEAD, D_HEAD)
    state_n = state_mlstm_n.reshape(DEC_BATCH, DEPTH, n_st, D_HEAD)
    state_m = state_mlstm_m.reshape(DEC_BATCH, DEPTH, n_st, 1)

    ctx_row = lambda i: CTX_MOD_ROW
    lat_row = lambda i: i // (DEC_SEQ // ROW_TILE)

    ctx_state = None
    for l in range(DEPTH):
        fb_row = jnp.zeros((1, Z_G_W), F32)
        fb_row = fb_row.at[0, ML_HEADS:2 * ML_HEADS].set(mlstm_fbias[l, 0])
        fb_row = fb_row.at[0, 3 * ML_HEADS:4 * ML_HEADS].set(mlstm_fbias[l, 1])
        norm_g = mlstm_norm_g[l].reshape(1, ML_W)
        sink_ctx = jnp.repeat(swa_sink[l], SEQ).reshape(SWA_KV_HEADS, SWA_GROUP * SEQ)
        sink_lat = jnp.repeat(swa_sink[l], SWA_BLOCK).reshape(SWA_KV_HEADS, SWA_GROUP * SWA_BLOCK)
        tail_w = (w_out_b[l], ln1_g[l], ln1_b[l], w_mlp1_b[l], w_mlp2_b[l], ln2_g[l], ln2_b[l])

        z_ml, z_swa, z_nat, z_g = _inproj_call(xp, mods[l], ctx_row, w_in_p[l], b_in_p[l])
        m_ml, m_swa, m_nat, *ctx_state = _ctx_mixers_call(
            z_ml, z_swa, z_nat, z_g, fb_row, norm_g, sink_ctx, l, ctx_state)
        xp = _tail_call(xp, m_ml, m_swa, m_nat, mods[l], ctx_row, *tail_w)

        z_ml, z_swa, z_nat, z_g = _inproj_call(xs, mods[l], lat_row, w_in_p[l], b_in_p[l])
        m_ml = _lat_mlstm_call(z_ml, z_g, fb_row, norm_g, state_c, state_n, state_m, l)
        m_swa = _lat_swa_call(z_swa, cos_t, sin_t, cache_swa_k, cache_swa_v, sink_lat, l)
        m_nat = _lat_nat_call(z_nat, cache_nat_k, cache_nat_v, rpb_lanes[l], l)
        xs = _tail_call(xs, m_ml, m_swa, m_nat, mods[l], lat_row, *tail_w)

    k_swa, v_swa, k_nat, v_nat, st_c, st_n, st_m = ctx_state
    return (xp.reshape(BATCH, SEQ, D_MODEL), xs.reshape(DEC_BATCH, DEC_SEQ, D_MODEL),
            k_swa, v_swa, k_nat, v_nat,
            st_c.reshape(BATCH, DEPTH, 2, ML_HEADS, D_HEAD, D_HEAD),
            st_n.reshape(BATCH, DEPTH, 2, ML_HEADS, D_HEAD),
            st_m[..., 0].reshape(BATCH, DEPTH, 2, ML_HEADS))
```

```python
import numpy as np
import jax
import jax.numpy as jnp
from jax import lax
from jax.experimental import pallas as pl
from jax.experimental.pallas import tpu as pltpu

D_MODEL = 1024
BATCH = 16
SEQ = 256
DEPTH = 4
DEC_BATCH = 8
DEC_SEQ = 1024
PAST_LEN = 512
GRID_W = 64
D_HEAD = 64
ML_HEADS = 4
SWA_HEADS = 6
SWA_KV_HEADS = 2
SWA_GROUP = SWA_HEADS // SWA_KV_HEADS
NAT_HEADS = 6
ML_W = ML_HEADS * D_HEAD
SWA_W = SWA_HEADS * D_HEAD
SWA_KV_W = SWA_KV_HEADS * D_HEAD
NAT_W = NAT_HEADS * D_HEAD
N_GATES = 4 * ML_HEADS
D_FF = 4 * D_MODEL
SWA_WINDOW = 128
SWA_BLOCK = 128
NAT_KH = 8
NAT_KW = 16
ROPE_BASE = 10000.0
LN_EPS = 1e-5
DN_ALPHA = (2 * DEPTH) ** 0.25
ATT_SCALE = D_HEAD ** -0.5

LANES = 128
SUBLANES = 8
Z_ML_W = 4 * ML_W
Z_SWA_W = SWA_W + 2 * SWA_KV_W
Z_NAT_W = 3 * NAT_W
Z_G_W = LANES
Z_W = Z_ML_W + Z_SWA_W + Z_NAT_W + Z_G_W
MOD_ROWS = 16
CTX_MOD_ROW = DEC_BATCH
ROW_TILE = 256
TAIL_TILE = 512
ML_CHUNK = 256
VMEM_LIMIT = 56 * 1024 * 1024

BF16 = jnp.bfloat16
F32 = jnp.float32
NEG_INF = float("-inf")


def _dot(a, b):
    return jnp.dot(a.astype(BF16), b.astype(BF16), preferred_element_type=F32)


def _dot_nt(a, b):
    return lax.dot_general(a.astype(BF16), b.astype(BF16), (((1,), (1,)), ((), ())),
                           preferred_element_type=F32)


def _dot_split3(tri, x):
    hi = x.astype(BF16)
    r1 = x - hi.astype(F32)
    mid = r1.astype(BF16)
    lo = (r1 - mid.astype(F32)).astype(BF16)
    return (jnp.dot(tri, hi, preferred_element_type=F32)
            + jnp.dot(tri, mid, preferred_element_type=F32)
            + jnp.dot(tri, lo, preferred_element_type=F32))


def _sigmoid(x):
    return 1.0 / (1.0 + jnp.exp(-x))


def _log_sigmoid(x):
    return jnp.minimum(x, 0.0) - jnp.log1p(jnp.exp(-jnp.abs(x)))


def _layer_norm(x, g, b):
    mu = jnp.mean(x, axis=-1, keepdims=True)
    xc = x - mu
    var = jnp.mean(xc * xc, axis=-1, keepdims=True)
    return xc * lax.rsqrt(var + LN_EPS) * g + b


def _params(n_axes=1):
    return pltpu.CompilerParams(dimension_semantics=("arbitrary",) * n_axes,
                                vmem_limit_bytes=VMEM_LIMIT)


def _layer_spec(a, layer, single_buffer=False):
    kwargs = {"pipeline_mode": pl.Buffered(1)} if single_buffer else {}
    return pl.BlockSpec((None,) + a.shape[1:], lambda *_: (layer,) + (0,) * (a.ndim - 1), **kwargs)


def _mods_kernel(c_ref, w_ref, b_ref, o_ref):
    c = c_ref[...]
    o_ref[...] = _dot(c * _sigmoid(c), w_ref[...]) + b_ref[...]


def _mods_call(cvec, w_ada, b_ada):
    n_col = 6
    return pl.pallas_call(
        _mods_kernel,
        grid=(DEPTH, n_col),
        in_specs=[
            pl.BlockSpec((MOD_ROWS, D_MODEL), lambda l, j: (0, 0)),
            pl.BlockSpec((None, D_MODEL, D_MODEL), lambda l, j: (l, 0, j)),
            pl.BlockSpec((None, 1, D_MODEL), lambda l, j: (l, 0, j)),
        ],
        out_specs=pl.BlockSpec((None, MOD_ROWS, D_MODEL), lambda l, j: (l, 0, j)),
        out_shape=jax.ShapeDtypeStruct((DEPTH, MOD_ROWS, 6 * D_MODEL), F32),
        compiler_params=_params(2),
        name="mods",
    )(cvec, w_ada, b_ada.reshape(DEPTH, 1, 6 * D_MODEL))


def _inproj_kernel(x_ref, mod_ref, w_ref, b_ref, zml_ref, zswa_ref, znat_ref, zg_ref):
    h = (x_ref[...] * (1.0 + mod_ref[1:2, :]) + mod_ref[0:1, :]).astype(BF16)
    o = 0
    for ref in (zml_ref, zswa_ref, znat_ref, zg_ref):
        w = ref.shape[-1]
        ref[...] = jnp.dot(h, w_ref[:, o:o + w], preferred_element_type=F32) + b_ref[:, o:o + w]
        o += w


def _mod_spec(layer, mod_row_of_tile):
    return pl.BlockSpec((None, 6, D_MODEL),
                        lambda i: (layer * MOD_ROWS + mod_row_of_tile(i), 0, 0))


def _inproj_call(x, mods, mod_row_of_tile, w_in, b_in, layer):
    rows = x.shape[0]
    widths = (Z_ML_W, Z_SWA_W, Z_NAT_W, Z_G_W)
    return pl.pallas_call(
        _inproj_kernel,
        grid=(rows // ROW_TILE,),
        in_specs=[
            pl.BlockSpec((ROW_TILE, D_MODEL), lambda i: (i, 0)),
            _mod_spec(layer, mod_row_of_tile),
            _layer_spec(w_in, layer),
            _layer_spec(b_in, layer),
        ],
        out_specs=[pl.BlockSpec((ROW_TILE, w), lambda i: (i, 0)) for w in widths],
        out_shape=[jax.ShapeDtypeStruct((rows, w), F32) for w in widths],
        compiler_params=_params(1),
        name="inproj",
    )(x, mods, w_in, b_in)


def _tail_kernel(x_ref, mml_ref, mswa_ref, mnat_ref, mod_ref, wout_ref, ln1g_ref, ln1b_ref,
                 w1_ref, w2_ref, ln2g_ref, ln2b_ref, o_ref):
    g_a, sh_m, sc_m, g_m = (mod_ref[i:i + 1, :] for i in (2, 3, 4, 5))
    mix = jnp.concatenate([mml_ref[...], mswa_ref[...], mnat_ref[...]], axis=1)
    proj = jnp.dot(mix, wout_ref[...], preferred_element_type=F32)
    y = _layer_norm(DN_ALPHA * x_ref[...] + g_a * proj, ln1g_ref[...], ln1b_ref[...])
    h = (y * (1.0 + sc_m) + sh_m).astype(BF16)
    f = jnp.maximum(jnp.dot(h, w1_ref[...], preferred_element_type=F32), 0.0)
    f = jnp.dot((f * f).astype(BF16), w2_ref[...], preferred_element_type=F32)
    o_ref[...] = _layer_norm(DN_ALPHA * y + g_m * f, ln2g_ref[...], ln2b_ref[...])


def _tail_call(x, mix_ml, mix_swa, mix_nat, mods, mod_row_of_tile, consts, layer):
    rows = x.shape[0]
    row_spec = lambda w: pl.BlockSpec((TAIL_TILE, w), lambda i: (i, 0))
    return pl.pallas_call(
        _tail_kernel,
        grid=(rows // TAIL_TILE,),
        in_specs=[row_spec(D_MODEL), row_spec(ML_W), row_spec(SWA_W), row_spec(NAT_W),
                  _mod_spec(layer, mod_row_of_tile)]
                 + [_layer_spec(a, layer, single_buffer=True) for a in consts],
        out_specs=row_spec(D_MODEL),
        out_shape=jax.ShapeDtypeStruct((rows, D_MODEL), F32),
        compiler_params=_params(1),
        name="tail",
    )(x, mix_ml, mix_swa, mix_nat, mods, *consts)


def _mlstm_direction(q, k, vt, a_col, b_row, li_row, neg, last, c_st, n_st, m_st):
    a_m = a_col + neg
    c_row = jnp.maximum(m_st, jnp.max(a_m, axis=0, keepdims=True))
    decay = jnp.exp(a_m - c_row)
    wprev = jnp.exp(m_st - c_row)
    mt = b_row + c_row
    st = _dot_nt(k, q) * decay
    den = jnp.sum(st, axis=0, keepdims=True)
    num = _dot(vt, st)
    if c_st is not None:
        cn = _dot_nt(jnp.concatenate([c_st, jnp.broadcast_to(n_st, (SUBLANES, D_HEAD))], axis=0),
                     q)
        num = num + wprev * cn[0:D_HEAD, :]
        den = den + wprev * cn[D_HEAD:D_HEAD + 1, :]
    ht = num * (1.0 / jnp.maximum(jnp.abs(den), jnp.exp(-mt)))
    m_new = mt[:, last:last + 1]
    ws = jnp.exp(b_row[:, last:last + 1] - b_row + li_row - m_new)
    upd = _dot(jnp.concatenate([vt * ws, jnp.broadcast_to(ws, (SUBLANES, ws.shape[1]))], axis=0),
               k)
    c_new = upd[0:D_HEAD, :]
    n_new = upd[D_HEAD:D_HEAD + 1, :]
    if c_st is not None:
        wc = wprev[:, last:last + 1]
        c_new = c_new + wc * c_st
        n_new = n_new + wc * n_st
    return ht, c_new, n_new, m_new


def _mlstm_chunk(zml_ref, g_ref, fb_ref, rows, direction, state):
    n = ML_CHUNK
    g = g_ref[rows, :]
    ls = _log_sigmoid(g + fb_ref[...])
    r = lax.broadcasted_iota(jnp.int32, (n, n), 0)
    c = lax.broadcasted_iota(jnp.int32, (n, n), 1)
    if direction == 0:
        tri, neg, last = c <= r, jnp.where(r <= c, 0.0, NEG_INF), n - 1
    else:
        tri, neg, last = c >= r, jnp.where(r >= c, 0.0, NEG_INF), 0
    b = _dot_split3(tri.astype(BF16), ls)
    a = pltpu.roll(g, ML_HEADS, 1) - b
    bt = b.T
    gt = g.T
    i_col = 2 * ML_HEADS * direction
    f_col = i_col + ML_HEADS
    hts, new_state = [], []
    for pair in range(ML_HEADS // 2):
        vt2 = zml_ref[rows, 2 * ML_W + pair * LANES:2 * ML_W + (pair + 1) * LANES].T
        for sub in range(2):
            hd = 2 * pair + sub
            feat = slice(sub * D_HEAD, (sub + 1) * D_HEAD)
            q = zml_ref[rows, hd * D_HEAD:(hd + 1) * D_HEAD].astype(BF16)
            k = zml_ref[rows, ML_W + hd * D_HEAD:ML_W + (hd + 1) * D_HEAD] * ATT_SCALE
            ht, c_new, n_new, m_new = _mlstm_direction(
                q, k, vt2[feat, :],
                a[:, f_col + hd:f_col + hd + 1], bt[f_col + hd:f_col + hd + 1, :],
                gt[i_col + hd:i_col + hd + 1, :], neg, last, *state[hd])
            hts.append(ht)
            new_state.append((c_new, n_new, m_new))
    return jnp.concatenate(hts, axis=0), new_state


def _mlstm_finish(ht, o_gate, norm_g):
    outs = []
    for hd in range(ML_HEADS):
        x = ht[hd * D_HEAD:(hd + 1) * D_HEAD, :]
        mu = jnp.mean(x, axis=0, keepdims=True)
        xc = x - mu
        var = jnp.mean(xc * xc, axis=0, keepdims=True)
        outs.append(xc * lax.rsqrt(var + LN_EPS))
    hn = jnp.concatenate(outs, axis=0).T
    return hn * norm_g * _sigmoid(o_gate)


def _store_state(c_ref, n_ref, m_ref, idx, c_new, n_new, m_new):
    c_ref[idx] = c_new
    n_ref[idx:idx + 1, :] = n_new
    m_ref[idx:idx + 1, :] = jnp.broadcast_to(m_new, (1, LANES))


ONES_ROWS = 16


def _with_ones(vt):
    ones = jnp.ones((ONES_ROWS, vt.shape[1]), BF16)
    return jnp.concatenate([vt.astype(BF16), ones], axis=0)


def _attend_t(q, parts, sink_row=None):
    sts = []
    for k, _, bias_t in parts:
        st = _dot_nt(k, q)
        sts.append(st if bias_t is None else st + bias_t)
    m = None
    for st in sts:
        mx = jnp.max(st, axis=0, keepdims=True)
        m = mx if m is None else jnp.maximum(m, mx)
    if sink_row is not None:
        m = jnp.maximum(m, sink_row)
    ot = None
    for st, (_, vt_ones, _) in zip(sts, parts):
        part = jnp.dot(vt_ones, jnp.exp(st - m).astype(BF16), preferred_element_type=F32)
        ot = part if ot is None else ot + part
    den = ot[D_HEAD:D_HEAD + 1, :]
    if sink_row is not None:
        den = den + jnp.exp(sink_row - m)
    return ot[0:D_HEAD, :] * (1.0 / den)


def _head_rows(pairs_t, hd):
    return pairs_t[hd // 2][(hd % 2) * D_HEAD:(hd % 2 + 1) * D_HEAD, :]


def _store_heads_t(o_ref, ots):
    for p in range(len(ots) // 2):
        o_ref[:, p * LANES:(p + 1) * LANES] = jnp.concatenate(
            [ots[2 * p], ots[2 * p + 1]], axis=0).astype(o_ref.dtype).T


N_STATE = 2 * ML_HEADS
N_CTX_STATE_OUTS = 7


def _ctx_mixers_kernel(zml_ref, zswa_ref, znat_ref, zg_ref, fb_ref, ng_ref, sink_ref, *rest):
    (mml_ref, mswa_ref, mnat_ref, sk_ref, sv_ref, nk_ref, nv_ref,
     c_ref, n_ref, m_ref) = rest[-(3 + N_CTX_STATE_OUTS):]
    rows = slice(0, SEQ)
    zero_state = [(None, jnp.zeros((1, D_HEAD), F32), jnp.zeros((1, 1), F32))] * ML_HEADS
    ht_sum = None
    for direction in range(2):
        ht, new_state = _mlstm_chunk(zml_ref, zg_ref, fb_ref, rows, direction, zero_state)
        ht_sum = ht if ht_sum is None else ht_sum + ht
        for hd in range(ML_HEADS):
            _store_state(c_ref, n_ref, m_ref, direction * ML_HEADS + hd, *new_state[hd])
    mml_ref[...] = _mlstm_finish(ht_sum, zml_ref[:, 3 * ML_W:4 * ML_W], ng_ref[...]).astype(BF16)

    scaled_q = lambda ref, hd: (ref[:, hd * D_HEAD:(hd + 1) * D_HEAD] * ATT_SCALE).astype(BF16)
    vts = [zswa_ref[:, SWA_W + SWA_KV_W:SWA_W + 2 * SWA_KV_W].astype(BF16).T]
    ots = []
    for kv in range(SWA_KV_HEADS):
        k = zswa_ref[:, SWA_W + kv * D_HEAD:SWA_W + (kv + 1) * D_HEAD]
        sk_ref[kv] = k
        sv_ref[kv] = zswa_ref[:, SWA_W + SWA_KV_W + kv * D_HEAD:SWA_W + SWA_KV_W + (kv + 1) * D_HEAD]
        q = jnp.concatenate([scaled_q(zswa_ref, kv * SWA_GROUP + g) for g in range(SWA_GROUP)],
                            axis=0)
        ot = _attend_t(q, [(k.astype(BF16), _with_ones(_head_rows(vts, kv)), None)],
                       sink_ref[kv:kv + 1, :])
        ots += [ot[:, g * SEQ:(g + 1) * SEQ] for g in range(SWA_GROUP)]
    _store_heads_t(mswa_ref, ots)

    vts = [znat_ref[:, 2 * NAT_W + p * LANES:2 * NAT_W + (p + 1) * LANES].astype(BF16).T
           for p in range(NAT_W // LANES)]
    ots = []
    for hd in range(NAT_HEADS):
        k = znat_ref[:, NAT_W + hd * D_HEAD:NAT_W + (hd + 1) * D_HEAD]
        nk_ref[hd] = k
        nv_ref[hd] = znat_ref[:, 2 * NAT_W + hd * D_HEAD:2 * NAT_W + (hd + 1) * D_HEAD]
        ots.append(_attend_t(scaled_q(znat_ref, hd),
                             [(k.astype(BF16), _with_ones(_head_rows(vts, hd)), None)]))
    _store_heads_t(mnat_ref, ots)


def _ctx_mixers_call(z_ml, z_swa, z_nat, z_g, fb_row, norm_g, sink_rows, layer, prev_state):
    rows = z_ml.shape[0]
    row_spec = lambda w: pl.BlockSpec((SEQ, w), lambda b: (b, 0))
    state_dims =((SWA_KV_HEADS, SEQ, D_HEAD), (SWA_KV_HEADS, SEQ, D_HEAD),
                  (NAT_HEADS, SEQ, D_HEAD), (NAT_HEADS, SEQ, D_HEAD),
                  (N_STATE, D_HEAD, D_HEAD), (N_STATE, D_HEAD), (N_STATE, LANES))
    state_spec = lambda d: pl.BlockSpec((None, None) + d, lambda b: (b, layer) + (0,) * len(d))
    prev = () if prev_state is None else tuple(prev_state)
    n_in = 7
    return pl.pallas_call(
        _ctx_mixers_kernel,
        grid=(BATCH,),
        in_specs=[row_spec(Z_ML_W), row_spec(Z_SWA_W), row_spec(Z_NAT_W), row_spec(Z_G_W),
                  _layer_spec(fb_row, layer), _layer_spec(norm_g, layer),
                  _layer_spec(sink_rows, layer)]
                 + [pl.BlockSpec(memory_space=pl.ANY)] * len(prev),
        out_specs=[row_spec(ML_W), row_spec(SWA_W), row_spec(NAT_W)]
                  + [state_spec(d) for d in state_dims],
        out_shape=[jax.ShapeDtypeStruct((rows, ML_W), BF16),
                   jax.ShapeDtypeStruct((rows, SWA_W), BF16),
                   jax.ShapeDtypeStruct((rows, NAT_W), BF16)]
                  + [jax.ShapeDtypeStruct((BATCH, DEPTH) + d, F32) for d in state_dims],
        input_output_aliases={n_in + i: 3 + i for i in range(len(prev))},
        compiler_params=_params(1),
        name="ctx_mixers",
    )(z_ml, z_swa, z_nat, z_g, fb_row, norm_g, sink_rows, *prev)


def _lat_mlstm_kernel(zml_ref, zg_ref, fb_ref, ng_ref, c0_ref, n0_ref, m0_ref, mml_ref,
                      ht_scr, c_scr, n_scr, m_scr):
    n_chunks = DEC_SEQ // ML_CHUNK
    c_scr[...] = c0_ref[...]
    n_scr[...] = n0_ref[...]
    m_scr[...] = jnp.broadcast_to(m0_ref[...], m_scr.shape)

    def body(i, carry):
        for direction in range(2):
            chunk = i if direction == 0 else n_chunks - 1 - i
            rows = pl.ds(pl.multiple_of(chunk * ML_CHUNK, ML_CHUNK), ML_CHUNK)
            base = direction * ML_HEADS
            state = [(c_scr[base + hd], n_scr[base + hd:base + hd + 1, :],
                      m_scr[base + hd:base + hd + 1, 0:1]) for hd in range(ML_HEADS)]
            ht, new_state = _mlstm_chunk(zml_ref, zg_ref, fb_ref, rows, direction, state)
            ht_scr[direction, chunk] = ht
            for hd in range(ML_HEADS):
                _store_state(c_scr, n_scr, m_scr, base + hd, *new_state[hd])
        return carry

    lax.fori_loop(0, n_chunks, body, 0)
    for chunk in range(n_chunks):
        rows = slice(chunk * ML_CHUNK, (chunk + 1) * ML_CHUNK)
        mml_ref[rows, :] = _mlstm_finish(ht_scr[0, chunk] + ht_scr[1, chunk],
                                         zml_ref[rows, 3 * ML_W:4 * ML_W],
                                         ng_ref[...]).astype(BF16)


def _lat_mlstm_call(z_ml, z_g, fb_row, norm_g, state_c, state_n, state_m, layer):
    n_st = 2 * ML_HEADS
    n_chunks = DEC_SEQ // ML_CHUNK
    row_spec = lambda w: pl.BlockSpec((DEC_SEQ, w), lambda b: (b, 0))
    return pl.pallas_call(
        _lat_mlstm_kernel,
        grid=(DEC_BATCH,),
        in_specs=[row_spec(Z_ML_W), row_spec(Z_G_W), _layer_spec(fb_row, layer),
                  _layer_spec(norm_g, layer),
                  pl.BlockSpec((None, None, n_st, D_HEAD, D_HEAD), lambda b: (b, layer, 0, 0, 0)),
                  pl.BlockSpec((None, None, n_st, D_HEAD), lambda b: (b, layer, 0, 0)),
                  pl.BlockSpec((None, None, n_st, 1), lambda b: (b, layer, 0, 0))],
        out_specs=row_spec(ML_W),
        out_shape=jax.ShapeDtypeStruct((DEC_BATCH * DEC_SEQ, ML_W), BF16),
        scratch_shapes=[pltpu.VMEM((2, n_chunks, ML_W, ML_CHUNK), F32),
                        pltpu.VMEM((n_st, D_HEAD, D_HEAD), F32), pltpu.VMEM((n_st, D_HEAD), F32),
                        pltpu.VMEM((n_st, LANES), F32)],
        compiler_params=_params(1),
        name="lat_mlstm",
    )(z_ml, z_g, fb_row, norm_g, state_c, state_n, state_m)


def _rope(x, cos, sin_signed):
    lane = lax.broadcasted_iota(jnp.int32, x.shape, 1)
    first = (lane & 31) < 16
    partner = jnp.where(first, pltpu.roll(x, LANES - 16, 1), pltpu.roll(x, 16, 1))
    return x * cos + partner * sin_signed


def _lat_swa_kernel(zswa_ref, cos_ref, sin_ref, ck_ref, cv_ref, sink_ref, mswa_ref,
                    q_scr, k_scr, vt_scr, ot_scr, bias_scr, ck_scr, cvt_scr, st_scr, m_scr):
    cos = cos_ref[...]
    sin = sin_ref[...]
    for j in range(SWA_W // LANES):
        q_scr[:, j * LANES:(j + 1) * LANES] = (_rope(
            zswa_ref[:, j * LANES:(j + 1) * LANES], cos, sin) * ATT_SCALE).astype(BF16)
    k_scr[...] = _rope(zswa_ref[:, SWA_W:SWA_W + SWA_KV_W], cos, sin).astype(BF16)
    vt_scr[...] = zswa_ref[:, SWA_W + SWA_KV_W:SWA_W + 2 * SWA_KV_W].astype(BF16).T
    cvt = jnp.concatenate([cv_ref[kv] for kv in range(SWA_KV_HEADS)], axis=1).astype(BF16).T

    bl = SWA_BLOCK
    nb = DEC_SEQ // bl
    n_q = SWA_GROUP * bl
    kj = lax.broadcasted_iota(jnp.int32, (3 * bl, n_q), 0)
    qi = lax.broadcasted_iota(jnp.int32, (3 * bl, n_q), 1) & (bl - 1)
    keep = ((kj >= qi) & (kj < 2 * bl)) | ((kj >= 2 * bl) & (kj - 2 * bl <= qi))
    bias_scr[...] = jnp.where(keep, 0.0, NEG_INF)
    for kv in range(SWA_KV_HEADS):
        ck_scr[kv] = ck_ref[kv].astype(BF16)
        cvt_scr[kv] = _with_ones(cvt[kv * D_HEAD:(kv + 1) * D_HEAD, :])

    def band(blk):
        lo_b, hi_b = max(blk - 1, 0), min(blk + 2, nb)
        return slice(lo_b * bl, hi_b * bl), slice((lo_b - blk + 1) * bl, (hi_b - blk + 1) * bl)

    def score_stage(kv, blk, slot):
        head = slice(kv * D_HEAD, (kv + 1) * D_HEAD)
        keys, bias_rows = band(blk)
        n_band = keys.stop - keys.start
        q = jnp.concatenate(
            [q_scr[blk * bl:(blk + 1) * bl,
                   (kv * SWA_GROUP + g) * D_HEAD:(kv * SWA_GROUP + g + 1) * D_HEAD]
             for g in range(SWA_GROUP)], axis=0)
        s_ctx = _dot_nt(ck_scr[kv], q)
        s_band = _dot_nt(k_scr[keys, head], q) + bias_scr[bias_rows, :]
        st_scr[slot, 0:PAST_LEN, :] = s_ctx
        st_scr[slot, PAST_LEN:PAST_LEN + n_band, :] = s_band
        m = jnp.maximum(jnp.max(s_ctx, axis=0, keepdims=True),
                        jnp.max(s_band, axis=0, keepdims=True))
        m_scr[slot] = jnp.broadcast_to(jnp.maximum(m, sink_ref[kv:kv + 1, :]), m_scr.shape[1:])

    def value_stage(kv, blk, slot):
        head = slice(kv * D_HEAD, (kv + 1) * D_HEAD)
        keys, _ = band(blk)
        n_band = keys.stop - keys.start
        m = m_scr[slot, 0:1, :]
        p_ctx = jnp.exp(st_scr[slot, 0:PAST_LEN, :] - m).astype(BF16)
        p_band = jnp.exp(st_scr[slot, PAST_LEN:PAST_LEN + n_band, :] - m).astype(BF16)
        ot = (jnp.dot(cvt_scr[kv], p_ctx, preferred_element_type=F32)
              + jnp.dot(_with_ones(vt_scr[head, keys]), p_band, preferred_element_type=F32))
        den = ot[D_HEAD:D_HEAD + 1, :] + jnp.exp(sink_ref[kv:kv + 1, :] - m)
        o = (ot[0:D_HEAD, :] * (1.0 / den)).astype(BF16)
        for g in range(SWA_GROUP):
            hq = kv * SWA_GROUP + g
            ot_scr[hq * D_HEAD:(hq + 1) * D_HEAD, blk * bl:(blk + 1) * bl] = o[:, g * bl:(g + 1) * bl]

    problems = [(kv, blk) for kv in range(SWA_KV_HEADS) for blk in range(nb)]
    for i in range(len(problems) + 1):
        def stage(i=i):
            if i > 0:
                value_stage(*problems[i - 1], (i - 1) % 2)
            if i < len(problems):
                score_stage(*problems[i], i % 2)
        stage()
    for p in range(SWA_W // LANES):
        mswa_ref[:, p * LANES:(p + 1) * LANES] = ot_scr[p * LANES:(p + 1) * LANES, :].T


def _lat_swa_call(z_swa, cos_t, sin_t, cache_k, cache_v, sink_rows, layer):
    row_spec = lambda w: pl.BlockSpec((DEC_SEQ, w), lambda b: (b, 0))
    full = lambda a: pl.BlockSpec(a.shape, lambda b: (0,) * a.ndim)
    cache_spec = pl.BlockSpec((None, None, SWA_KV_HEADS, PAST_LEN, D_HEAD),
                              lambda b: (b, layer, 0, 0, 0))
    n_q = SWA_GROUP * SWA_BLOCK
    return pl.pallas_call(
        _lat_swa_kernel,
        grid=(DEC_BATCH,),
        in_specs=[row_spec(Z_SWA_W), full(cos_t), full(sin_t), cache_spec, cache_spec,
                  _layer_spec(sink_rows, layer)],
        out_specs=row_spec(SWA_W),
        out_shape=jax.ShapeDtypeStruct((DEC_BATCH * DEC_SEQ, SWA_W), BF16),
        scratch_shapes=[pltpu.VMEM((DEC_SEQ, SWA_W), BF16), pltpu.VMEM((DEC_SEQ, SWA_KV_W), BF16),
                        pltpu.VMEM((SWA_KV_W, DEC_SEQ), BF16), pltpu.VMEM((SWA_W, DEC_SEQ), BF16),
                        pltpu.VMEM((3 * SWA_BLOCK, n_q), F32),
                        pltpu.VMEM((SWA_KV_HEADS, PAST_LEN, D_HEAD), BF16),
                        pltpu.VMEM((SWA_KV_HEADS, D_HEAD + ONES_ROWS, PAST_LEN), BF16),
                        pltpu.VMEM((2, PAST_LEN + 3 * SWA_BLOCK, n_q), F32),
                        pltpu.VMEM((2, SUBLANES, n_q), F32)],
        compiler_params=_params(1),
        name="lat_swa",
    )(z_swa, cos_t, sin_t, cache_k, cache_v, sink_rows)


NAT_ROWS = DEC_SEQ // GRID_W
NAT_GROUP_ROWS = 4
NAT_GROUP_Q = NAT_GROUP_ROWS * GRID_W
NAT_RPB_ROWS = 2 * NAT_KH


def _nat_row_start(r):
    return min(max(r - NAT_KH // 2, 0), NAT_ROWS - NAT_KH)


def _nat_groups():
    groups, off = [], 0
    for g in range(NAT_ROWS // NAT_GROUP_ROWS):
        w0 = _nat_row_start(g * NAT_GROUP_ROWS)
        w1 = _nat_row_start((g + 1) * NAT_GROUP_ROWS - 1) + NAT_KH
        n_rows = w1 - w0 + (w1 - w0) % 2
        w0 = min(w0, NAT_ROWS - n_rows)
        groups.append((w0, n_rows, off))
        off += n_rows * GRID_W
    return groups, off


NAT_GROUPS, NAT_BIAS_KEYS = _nat_groups()


def _nat_build_bias(rpb_ref, bias_scr):
    shape = (GRID_W, LANES)
    q = lax.broadcasted_iota(jnp.int32, shape, 0)
    lane = lax.broadcasted_iota(jnp.int32, shape, 1)
    kc = lane & (GRID_W - 1)
    cs = jnp.clip(q - NAT_KW // 2, 0, GRID_W - NAT_KW)
    ok = (kc >= cs) & (kc < cs + NAT_KW)
    ok_lo = ok & (lane < GRID_W)
    ok_hi = ok & (lane >= GRID_W)
    neg_tile = jnp.full(shape, NEG_INF, F32)
    for hd in range(2):
        lo, hi = [], []
        for ro in range(2 * NAT_KH - 1):
            x = jnp.broadcast_to(rpb_ref[hd, ro:ro + 1, :], shape)
            lo.append(jnp.where(ok_lo, pltpu.roll(x, 0, 1, stride=1, stride_axis=0), NEG_INF))
            hi.append(jnp.where(ok_hi, pltpu.roll(x, GRID_W, 1, stride=1, stride_axis=0), NEG_INF))
        for g, (w0, n_rows, off) in enumerate(NAT_GROUPS):
            for jj in range(n_rows // 2):
                kra, krb = w0 + 2 * jj, w0 + 2 * jj + 1
                tiles = []
                for rr in range(NAT_GROUP_ROWS):
                    r = g * NAT_GROUP_ROWS + rr
                    r0 = _nat_row_start(r)
                    parts = []
                    if r0 <= kra < r0 + NAT_KH:
                        parts.append(lo[kra - r + NAT_KH - 1])
                    if r0 <= krb < r0 + NAT_KH:
                        parts.append(hi[krb - r + NAT_KH - 1])
                    tiles.append(neg_tile if not parts else (
                        parts[0] if len(parts) == 1 else jnp.maximum(parts[0], parts[1])))
                bias_scr[hd, off + jj * LANES:off + (jj + 1) * LANES, :] = (
                    jnp.concatenate(tiles, axis=0).T)


def _lat_nat_kernel(q_ref, k_ref, v_ref, ck_ref, cv_ref, rpb_ref, mnat_ref, bias_scr):
    @pl.when(pl.program_id(1) == 0)
    def _():
        _nat_build_bias(rpb_ref, bias_scr)

    vt = v_ref[...].astype(BF16).T
    cvt = jnp.concatenate([cv_ref[0], cv_ref[1]], axis=1).astype(BF16).T
    ots = []
    for hd in range(2):
        head = slice(hd * D_HEAD, (hd + 1) * D_HEAD)
        ck = ck_ref[hd].astype(BF16)
        cvt_ones = _with_ones(cvt[head, :])
        ot_groups = []
        for g, (w0, n_rows, off) in enumerate(NAT_GROUPS):
            keys = slice(w0 * GRID_W, (w0 + n_rows) * GRID_W)
            parts = [(k_ref[keys, head].astype(BF16), _with_ones(vt[head, keys]),
                      bias_scr[hd, off:off + n_rows * GRID_W, :]),
                     (ck, cvt_ones, None)]
            q = (q_ref[g * NAT_GROUP_Q:(g + 1) * NAT_GROUP_Q, head] * ATT_SCALE).astype(BF16)
            ot_groups.append(_attend_t(q, parts))
        ots.append(jnp.concatenate(ot_groups, axis=1))
    mnat_ref[...] = jnp.concatenate(ots, axis=0).astype(BF16).T


def _lat_nat_call(z_nat, cache_k, cache_v, rpb_lanes, layer):
    n_pair = NAT_HEADS // 2
    col_spec = lambda off: pl.BlockSpec((DEC_SEQ, LANES), lambda p, b: (b, off + p))
    cache_spec = pl.BlockSpec((None, None, 2, PAST_LEN, D_HEAD), lambda p, b: (b, layer, p, 0, 0))
    return pl.pallas_call(
        _lat_nat_kernel,
        grid=(n_pair, DEC_BATCH),
        in_specs=[col_spec(0), col_spec(n_pair), col_spec(2 * n_pair), cache_spec, cache_spec,
                  pl.BlockSpec((None, 2, NAT_RPB_ROWS, LANES), lambda p, b: (layer, p, 0, 0))],
        out_specs=pl.BlockSpec((DEC_SEQ, LANES), lambda p, b: (b, p)),
        out_shape=jax.ShapeDtypeStruct((DEC_BATCH * DEC_SEQ, NAT_W), BF16),
        scratch_shapes=[pltpu.VMEM((2, NAT_BIAS_KEYS, NAT_GROUP_Q), F32)],
        compiler_params=_params(2),
        name="lat_nat",
    )(z_nat, z_nat, z_nat, cache_k, cache_v, rpb_lanes)


def _nat_rpb_lanes(rpb):
    n_off = 2 * NAT_KW - 1
    padded = jnp.concatenate(
        [rpb.astype(F32), jnp.zeros(rpb.shape[:-1] + (LANES - n_off,), F32)], axis=-1)
    rolled = jnp.roll(padded, -(NAT_KW - 1), axis=-1)
    pad_rows = jnp.zeros(rpb.shape[:-2] + (NAT_RPB_ROWS - rpb.shape[-2], LANES), F32)
    return jnp.concatenate([rolled, pad_rows], axis=-2)


def _rope_tables():
    t = np.arange(DEC_SEQ)[:, None]
    d = np.arange(LANES)[None, :] % D_HEAD
    pos = np.where(d < D_HEAD // 2, t // GRID_W, t % GRID_W).astype(np.float64)
    quarter = D_HEAD // 4
    freq = ROPE_BASE ** (-(d % quarter).astype(np.float64) / quarter)
    ang = (pos.astype(np.float32) * freq.astype(np.float32)).astype(np.float32)
    sign = np.where((d % (2 * quarter)) < quarter, -1.0, 1.0)
    return (jnp.asarray(np.cos(ang), dtype=F32), jnp.asarray(np.sin(ang) * sign, dtype=F32))


def _permute_in_columns(a):
    g0 = Z_ML_W
    s0 = g0 + N_GATES
    pad = jnp.zeros(a.shape[:-1] + (Z_G_W - N_GATES,), a.dtype)
    return jnp.concatenate([a[..., :g0], a[..., s0:], a[..., g0:s0], pad], axis=-1)


def kernel(x_prompt, x_sample, cache_swa_k, cache_swa_v, cache_nat_k, cache_nat_v, state_mlstm_C,
           state_mlstm_n, state_mlstm_m, c, c_ctx, w_ada, b_ada, w_in, b_in, mlstm_fbias,
           mlstm_norm_g, swa_sink, nat_rpb, w_out, ln1_g, ln1_b, w_mlp1, w_mlp2, ln2_g, ln2_b):
    xp = x_prompt.reshape(BATCH * SEQ, D_MODEL)
    xs = x_sample.reshape(DEC_BATCH * DEC_SEQ, D_MODEL)
    cvec = jnp.concatenate(
        [c, c_ctx[None, :], jnp.zeros((MOD_ROWS - DEC_BATCH - 1, D_MODEL), F32)], axis=0)
    mods = _mods_call(cvec, w_ada, b_ada).reshape(DEPTH * MOD_ROWS, 6, D_MODEL)

    w_in_p = _permute_in_columns(w_in).astype(BF16)
    b_in_p = _permute_in_columns(b_in).reshape(DEPTH, 1, Z_W)
    vec = lambda a: a.reshape(DEPTH, 1, D_MODEL)
    tail_w = (w_out.astype(BF16), vec(ln1_g), vec(ln1_b), w_mlp1.astype(BF16),
              w_mlp2.astype(BF16), vec(ln2_g), vec(ln2_b))
    cos_t, sin_t = _rope_tables()
    rpb_lanes = _nat_rpb_lanes(nat_rpb)
    fb_rows = jnp.zeros((DEPTH, 1, Z_G_W), F32)
    fb_rows = fb_rows.at[:, 0, ML_HEADS:2 * ML_HEADS].set(mlstm_fbias[:, 0])
    fb_rows = fb_rows.at[:, 0, 3 * ML_HEADS:4 * ML_HEADS].set(mlstm_fbias[:, 1])
    norm_g = mlstm_norm_g.reshape(DEPTH, 1, ML_W)
    sink_ctx = jnp.repeat(swa_sink, SEQ, axis=1).reshape(DEPTH, SWA_KV_HEADS, SWA_GROUP * SEQ)
    sink_lat = jnp.repeat(swa_sink, SWA_BLOCK, axis=1).reshape(
        DEPTH, SWA_KV_HEADS, SWA_GROUP * SWA_BLOCK)

    n_st = 2 * ML_HEADS
    state_c = state_mlstm_C.reshape(DEC_BATCH, DEPTH, n_st, D_HEAD, D_HEAD)
    state_n = state_mlstm_n.reshape(DEC_BATCH, DEPTH, n_st, D_HEAD)
    state_m = state_mlstm_m.reshape(DEC_BATCH, DEPTH, n_st, 1)

    ctx_row = lambda tile: (lambda i: CTX_MOD_ROW)
    lat_row = lambda tile: (lambda i: i // (DEC_SEQ // tile))

    ctx_state = None
    for l in range(DEPTH):
        z_ml, z_swa, z_nat, z_g = _inproj_call(xp, mods, ctx_row(ROW_TILE), w_in_p, b_in_p, l)
        m_ml, m_swa, m_nat, *ctx_state = _ctx_mixers_call(
            z_ml, z_swa, z_nat, z_g, fb_rows, norm_g, sink_ctx, l, ctx_state)
        xp = _tail_call(xp, m_ml, m_swa, m_nat, mods, ctx_row(TAIL_TILE), tail_w, l)

        z_ml, z_swa, z_nat, z_g = _inproj_call(xs, mods, lat_row(ROW_TILE), w_in_p, b_in_p, l)
        m_ml = _lat_mlstm_call(z_ml, z_g, fb_rows, norm_g, state_c, state_n, state_m, l)
        m_swa = _lat_swa_call(z_swa, cos_t, sin_t, cache_swa_k, cache_swa_v, sink_lat, l)
        m_nat = _lat_nat_call(z_nat, cache_nat_k, cache_nat_v, rpb_lanes, l)
        xs = _tail_call(xs, m_ml, m_swa, m_nat, mods, lat_row(TAIL_TILE), tail_w, l)

    k_swa, v_swa, k_nat, v_nat, st_c, st_n, st_m = ctx_state
    return (xp.reshape(BATCH, SEQ, D_MODEL), xs.reshape(DEC_BATCH, DEC_SEQ, D_MODEL),
            k_swa, v_swa, k_nat, v_nat,
            st_c.reshape(BATCH, DEPTH, 2, ML_HEADS, D_HEAD, D_HEAD),
            st_n.reshape(BATCH, DEPTH, 2, ML_HEADS, D_HEAD),
            st_m[..., 0].reshape(BATCH, DEPTH, 2, ML_HEADS))
```

```python
import numpy as np
import jax
import jax.numpy as jnp
from jax import lax
from jax.experimental import pallas as pl
from jax.experimental.pallas import tpu as pltpu

D_MODEL = 1024
BATCH = 16
SEQ = 256
DEPTH = 4
DEC_BATCH = 8
DEC_SEQ = 1024
PAST_LEN = 512
GRID_W = 64
D_HEAD = 64
ML_HEADS = 4
SWA_HEADS = 6
SWA_KV_HEADS = 2
SWA_GROUP = SWA_HEADS // SWA_KV_HEADS
NAT_HEADS = 6
ML_W = ML_HEADS * D_HEAD
SWA_W = SWA_HEADS * D_HEAD
SWA_KV_W = SWA_KV_HEADS * D_HEAD
NAT_W = NAT_HEADS * D_HEAD
N_GATES = 4 * ML_HEADS
IN_DIM = 4 * ML_W + N_GATES + SWA_W + 2 * SWA_KV_W + 3 * NAT_W
D_FF = 4 * D_MODEL
SWA_WINDOW = 128
SWA_BLOCK = 128
NAT_KH = 8
NAT_KW = 16
ROPE_BASE = 10000.0
LN_EPS = 1e-5
DN_ALPHA = (2 * DEPTH) ** 0.25
ATT_SCALE = D_HEAD ** -0.5

LANES = 128
SUBLANES = 8
Z_ML_W = 4 * ML_W
Z_SWA_W = SWA_W + 2 * SWA_KV_W
Z_NAT_W = 3 * NAT_W
Z_G_W = LANES
Z_W = Z_ML_W + Z_SWA_W + Z_NAT_W + Z_G_W
MOD_ROWS = 16
CTX_MOD_ROW = DEC_BATCH
ROW_TILE = 256
TAIL_TILE = 512
ML_CHUNK = 256
VMEM_LIMIT = 56 * 1024 * 1024

BF16 = jnp.bfloat16
F32 = jnp.float32
NEG_INF = float("-inf")


def _dot(a, b):
    return jnp.dot(a.astype(BF16), b.astype(BF16), preferred_element_type=F32)


def _dot_nt(a, b):
    return lax.dot_general(a.astype(BF16), b.astype(BF16), (((1,), (1,)), ((), ())),
                           preferred_element_type=F32)


def _dot_split3(tri, x):
    hi = x.astype(BF16)
    r1 = x - hi.astype(F32)
    mid = r1.astype(BF16)
    lo = (r1 - mid.astype(F32)).astype(BF16)
    return (jnp.dot(tri, hi, preferred_element_type=F32)
            + jnp.dot(tri, mid, preferred_element_type=F32)
            + jnp.dot(tri, lo, preferred_element_type=F32))


def _sigmoid(x):
    return 1.0 / (1.0 + jnp.exp(-x))


def _log_sigmoid(x):
    return jnp.minimum(x, 0.0) - jnp.log1p(jnp.exp(-jnp.abs(x)))


def _layer_norm(x, g, b):
    mu = jnp.mean(x, axis=-1, keepdims=True)
    xc = x - mu
    var = jnp.mean(xc * xc, axis=-1, keepdims=True)
    return xc * lax.rsqrt(var + LN_EPS) * g + b


def _params(n_axes=1):
    return pltpu.CompilerParams(dimension_semantics=("arbitrary",) * n_axes,
                                vmem_limit_bytes=VMEM_LIMIT)


def _layer_spec(a, layer, single_buffer=False):
    kwargs = {"pipeline_mode": pl.Buffered(1)} if single_buffer else {}
    return pl.BlockSpec((None,) + a.shape[1:], lambda *_: (layer,) + (0,) * (a.ndim - 1), **kwargs)


def _mods_kernel(c_ref, w_ref, b_ref, o_ref):
    c = c_ref[...]
    o_ref[...] = _dot(c * _sigmoid(c), w_ref[...]) + b_ref[...]


def _mods_call(cvec, w_ada, b_ada):
    n_col = 6
    return pl.pallas_call(
        _mods_kernel,
        grid=(DEPTH, n_col),
        in_specs=[
            pl.BlockSpec((MOD_ROWS, D_MODEL), lambda l, j: (0, 0)),
            pl.BlockSpec((None, D_MODEL, D_MODEL), lambda l, j: (l, 0, j)),
            pl.BlockSpec((None, 1, D_MODEL), lambda l, j: (l, 0, j)),
        ],
        out_specs=pl.BlockSpec((None, MOD_ROWS, D_MODEL), lambda l, j: (l, 0, j)),
        out_shape=jax.ShapeDtypeStruct((DEPTH, MOD_ROWS, 6 * D_MODEL), F32),
        compiler_params=_params(2),
        name="mods",
    )(cvec, w_ada, b_ada.reshape(DEPTH, 1, 6 * D_MODEL))


def _inproj_kernel(x_ref, mod_ref, wt_ref, b_ref, zml_ref, zswa_ref, znat_ref, zg_ref, w_scr):
    @pl.when(pl.program_id(0) == 0)
    def _():
        rest = IN_DIM - Z_ML_W - N_GATES
        w_scr[0:Z_ML_W, :] = wt_ref[0:Z_ML_W, :].astype(BF16)
        w_scr[Z_ML_W:Z_ML_W + rest, :] = wt_ref[Z_ML_W + N_GATES:IN_DIM, :].astype(BF16)
        w_scr[Z_ML_W + rest:IN_DIM, :] = wt_ref[Z_ML_W:Z_ML_W + N_GATES, :].astype(BF16)
        w_scr[IN_DIM:Z_W, :] = jnp.zeros((Z_W - IN_DIM, D_MODEL), BF16)

    h = (x_ref[...] * (1.0 + mod_ref[1:2, :]) + mod_ref[0:1, :]).astype(BF16)
    o = 0
    for ref in (zml_ref, zswa_ref, znat_ref, zg_ref):
        w = ref.shape[-1]
        ref[...] = _dot_nt(h, w_scr[o:o + w, :]) + b_ref[:, o:o + w]
        o += w


def _mod_spec(layer, mod_row_of_tile):
    return pl.BlockSpec((None, 6, D_MODEL),
                        lambda i: (layer * MOD_ROWS + mod_row_of_tile(i), 0, 0))


def _inproj_call(x, mods, mod_row_of_tile, w_in_t, b_in, layer):
    rows = x.shape[0]
    widths = (Z_ML_W, Z_SWA_W, Z_NAT_W, Z_G_W)
    return pl.pallas_call(
        _inproj_kernel,
        grid=(rows // ROW_TILE,),
        in_specs=[
            pl.BlockSpec((ROW_TILE, D_MODEL), lambda i: (i, 0)),
            _mod_spec(layer, mod_row_of_tile),
            _layer_spec(w_in_t, layer, single_buffer=True),
            _layer_spec(b_in, layer),
        ],
        out_specs=[pl.BlockSpec((ROW_TILE, w), lambda i: (i, 0)) for w in widths],
        out_shape=[jax.ShapeDtypeStruct((rows, w), F32) for w in widths],
        scratch_shapes=[pltpu.VMEM((Z_W, D_MODEL), BF16)],
        compiler_params=_params(1),
        name="inproj",
    )(x, mods, w_in_t, b_in)


def _tail_kernel(x_ref, mml_ref, mswa_ref, mnat_ref, mod_ref, wout_ref, ln1g_ref, ln1b_ref,
                 w1_ref, w2_ref, ln2g_ref, ln2b_ref, o_ref):
    g_a, sh_m, sc_m, g_m = (mod_ref[i:i + 1, :] for i in (2, 3, 4, 5))
    mix = jnp.concatenate([mml_ref[...], mswa_ref[...], mnat_ref[...]], axis=1)
    proj = jnp.dot(mix, wout_ref[...], preferred_element_type=F32)
    y = _layer_norm(DN_ALPHA * x_ref[...] + g_a * proj, ln1g_ref[...], ln1b_ref[...])
    h = (y * (1.0 + sc_m) + sh_m).astype(BF16)
    f = jnp.maximum(jnp.dot(h, w1_ref[...], preferred_element_type=F32), 0.0)
    f = jnp.dot((f * f).astype(BF16), w2_ref[...], preferred_element_type=F32)
    o_ref[...] = _layer_norm(DN_ALPHA * y + g_m * f, ln2g_ref[...], ln2b_ref[...])


def _tail_call(x, mix_ml, mix_swa, mix_nat, mods, mod_row_of_tile, consts, layer):
    rows = x.shape[0]
    row_spec = lambda w: pl.BlockSpec((TAIL_TILE, w), lambda i: (i, 0))
    return pl.pallas_call(
        _tail_kernel,
        grid=(rows // TAIL_TILE,),
        in_specs=[row_spec(D_MODEL), row_spec(ML_W), row_spec(SWA_W), row_spec(NAT_W),
                  _mod_spec(layer, mod_row_of_tile)]
                 + [_layer_spec(a, layer, single_buffer=True) for a in consts],
        out_specs=row_spec(D_MODEL),
        out_shape=jax.ShapeDtypeStruct((rows, D_MODEL), F32),
        compiler_params=_params(1),
        name="tail",
    )(x, mix_ml, mix_swa, mix_nat, mods, *consts)


def _mlstm_direction(q, k, vt, a_col, b_row, li_row, neg, last, c_st, n_st, m_st):
    a_m = a_col + neg
    c_row = jnp.maximum(m_st, jnp.max(a_m, axis=0, keepdims=True))
    decay = jnp.exp(a_m - c_row)
    wprev = jnp.exp(m_st - c_row)
    mt = b_row + c_row
    st = _dot_nt(k, q) * decay
    den = jnp.sum(st, axis=0, keepdims=True)
    num = _dot(vt, st)
    if c_st is not None:
        cn = _dot_nt(jnp.concatenate([c_st, jnp.broadcast_to(n_st, (SUBLANES, D_HEAD))], axis=0),
                     q)
        num = num + wprev * cn[0:D_HEAD, :]
        den = den + wprev * cn[D_HEAD:D_HEAD + 1, :]
    ht = num * (1.0 / jnp.maximum(jnp.abs(den), jnp.exp(-mt)))
    m_new = mt[:, last:last + 1]
    ws = jnp.exp(b_row[:, last:last + 1] - b_row + li_row - m_new)
    upd = _dot(jnp.concatenate([vt * ws, jnp.broadcast_to(ws, (SUBLANES, ws.shape[1]))], axis=0),
               k)
    c_new = upd[0:D_HEAD, :]
    n_new = upd[D_HEAD:D_HEAD + 1, :]
    if c_st is not None:
        wc = wprev[:, last:last + 1]
        c_new = c_new + wc * c_st
        n_new = n_new + wc * n_st
    return ht, c_new, n_new, m_new


def _mlstm_chunk(zml_ref, g_ref, fb_ref, rows, direction, state):
    n = ML_CHUNK
    g = g_ref[rows, :]
    ls = _log_sigmoid(g + fb_ref[...])
    r = lax.broadcasted_iota(jnp.int32, (n, n), 0)
    c = lax.broadcasted_iota(jnp.int32, (n, n), 1)
    if direction == 0:
        tri, neg, last = c <= r, jnp.where(r <= c, 0.0, NEG_INF), n - 1
    else:
        tri, neg, last = c >= r, jnp.where(r >= c, 0.0, NEG_INF), 0
    b = _dot_split3(tri.astype(BF16), ls)
    a = pltpu.roll(g, ML_HEADS, 1) - b
    bt = b.T
    gt = g.T
    i_col = 2 * ML_HEADS * direction
    f_col = i_col + ML_HEADS
    hts, new_state = [], []
    for pair in range(ML_HEADS // 2):
        vt2 = zml_ref[rows, 2 * ML_W + pair * LANES:2 * ML_W + (pair + 1) * LANES].T
        for sub in range(2):
            hd = 2 * pair + sub
            feat = slice(sub * D_HEAD, (sub + 1) * D_HEAD)
            q = zml_ref[rows, hd * D_HEAD:(hd + 1) * D_HEAD].astype(BF16)
            k = zml_ref[rows, ML_W + hd * D_HEAD:ML_W + (hd + 1) * D_HEAD] * ATT_SCALE
            ht, c_new, n_new, m_new = _mlstm_direction(
                q, k, vt2[feat, :],
                a[:, f_col + hd:f_col + hd + 1], bt[f_col + hd:f_col + hd + 1, :],
                gt[i_col + hd:i_col + hd + 1, :], neg, last, *state[hd])
            hts.append(ht)
            new_state.append((c_new, n_new, m_new))
    return jnp.concatenate(hts, axis=0), new_state


def _mlstm_finish(ht, o_gate, norm_g):
    outs = []
    for hd in range(ML_HEADS):
        x = ht[hd * D_HEAD:(hd + 1) * D_HEAD, :]
        mu = jnp.mean(x, axis=0, keepdims=True)
        xc = x - mu
        var = jnp.mean(xc * xc, axis=0, keepdims=True)
        outs.append(xc * lax.rsqrt(var + LN_EPS))
    hn = jnp.concatenate(outs, axis=0).T
    return hn * norm_g * _sigmoid(o_gate)


def _store_state(c_ref, n_ref, m_ref, idx, c_new, n_new, m_new):
    c_ref[idx] = c_new
    n_ref[idx:idx + 1, :] = n_new
    m_ref[idx:idx + 1, :] = jnp.broadcast_to(m_new, (1, LANES))


ONES_ROWS = 16


def _with_ones(vt):
    ones = jnp.ones((ONES_ROWS, vt.shape[1]), BF16)
    return jnp.concatenate([vt.astype(BF16), ones], axis=0)


def _attend_t(q, parts, sink_row=None):
    sts = []
    for k, _, bias_t in parts:
        st = _dot_nt(k, q)
        sts.append(st if bias_t is None else st + bias_t)
    m = None
    for st in sts:
        mx = jnp.max(st, axis=0, keepdims=True)
        m = mx if m is None else jnp.maximum(m, mx)
    if sink_row is not None:
        m = jnp.maximum(m, sink_row)
    ot = None
    for st, (_, vt_ones, _) in zip(sts, parts):
        part = jnp.dot(vt_ones, jnp.exp(st - m).astype(BF16), preferred_element_type=F32)
        ot = part if ot is None else ot + part
    den = ot[D_HEAD:D_HEAD + 1, :]
    if sink_row is not None:
        den = den + jnp.exp(sink_row - m)
    return ot[0:D_HEAD, :] * (1.0 / den)


def _head_rows(pairs_t, hd):
    return pairs_t[hd // 2][(hd % 2) * D_HEAD:(hd % 2 + 1) * D_HEAD, :]


def _store_heads_t(o_ref, ots):
    for p in range(len(ots) // 2):
        o_ref[:, p * LANES:(p + 1) * LANES] = jnp.concatenate(
            [ots[2 * p], ots[2 * p + 1]], axis=0).astype(o_ref.dtype).T


N_STATE = 2 * ML_HEADS
N_CTX_STATE_OUTS = 7


def _ctx_mixers_kernel(zml_ref, zswa_ref, znat_ref, zg_ref, fb_ref, ng_ref, sink_ref, *rest):
    (mml_ref, mswa_ref, mnat_ref, sk_ref, sv_ref, nk_ref, nv_ref,
     c_ref, n_ref, m_ref) = rest[-(3 + N_CTX_STATE_OUTS):]
    rows = slice(0, SEQ)
    zero_state = [(None, jnp.zeros((1, D_HEAD), F32), jnp.zeros((1, 1), F32))] * ML_HEADS
    ht_sum = None
    for direction in range(2):
        ht, new_state = _mlstm_chunk(zml_ref, zg_ref, fb_ref, rows, direction, zero_state)
        ht_sum = ht if ht_sum is None else ht_sum + ht
        for hd in range(ML_HEADS):
            _store_state(c_ref, n_ref, m_ref, direction * ML_HEADS + hd, *new_state[hd])
    mml_ref[...] = _mlstm_finish(ht_sum, zml_ref[:, 3 * ML_W:4 * ML_W], ng_ref[...]).astype(BF16)

    scaled_q = lambda ref, hd: (ref[:, hd * D_HEAD:(hd + 1) * D_HEAD] * ATT_SCALE).astype(BF16)
    vts = [zswa_ref[:, SWA_W + SWA_KV_W:SWA_W + 2 * SWA_KV_W].astype(BF16).T]
    ots = []
    for kv in range(SWA_KV_HEADS):
        k = zswa_ref[:, SWA_W + kv * D_HEAD:SWA_W + (kv + 1) * D_HEAD]
        sk_ref[kv] = k
        sv_ref[kv] = zswa_ref[:, SWA_W + SWA_KV_W + kv * D_HEAD:SWA_W + SWA_KV_W + (kv + 1) * D_HEAD]
        q = jnp.concatenate([scaled_q(zswa_ref, kv * SWA_GROUP + g) for g in range(SWA_GROUP)],
                            axis=0)
        ot = _attend_t(q, [(k.astype(BF16), _with_ones(_head_rows(vts, kv)), None)],
                       sink_ref[kv:kv + 1, :])
        ots += [ot[:, g * SEQ:(g + 1) * SEQ] for g in range(SWA_GROUP)]
    _store_heads_t(mswa_ref, ots)

    vts = [znat_ref[:, 2 * NAT_W + p * LANES:2 * NAT_W + (p + 1) * LANES].astype(BF16).T
           for p in range(NAT_W // LANES)]
    ots = []
    for hd in range(NAT_HEADS):
        k = znat_ref[:, NAT_W + hd * D_HEAD:NAT_W + (hd + 1) * D_HEAD]
        nk_ref[hd] = k
        nv_ref[hd] = znat_ref[:, 2 * NAT_W + hd * D_HEAD:2 * NAT_W + (hd + 1) * D_HEAD]
        ots.append(_attend_t(scaled_q(znat_ref, hd),
                             [(k.astype(BF16), _with_ones(_head_rows(vts, hd)), None)]))
    _store_heads_t(mnat_ref, ots)


def _ctx_mixers_call(z_ml, z_swa, z_nat, z_g, fb_row, norm_g, sink_rows, layer, prev_state):
    rows = z_ml.shape[0]
    row_spec = lambda w: pl.BlockSpec((SEQ, w), lambda b: (b, 0))
    state_dims =((SWA_KV_HEADS, SEQ, D_HEAD), (SWA_KV_HEADS, SEQ, D_HEAD),
                  (NAT_HEADS, SEQ, D_HEAD), (NAT_HEADS, SEQ, D_HEAD),
                  (N_STATE, D_HEAD, D_HEAD), (N_STATE, D_HEAD), (N_STATE, LANES))
    state_spec = lambda d: pl.BlockSpec((None, None) + d, lambda b: (b, layer) + (0,) * len(d))
    prev = () if prev_state is None else tuple(prev_state)
    n_in = 7
    return pl.pallas_call(
        _ctx_mixers_kernel,
        grid=(BATCH,),
        in_specs=[row_spec(Z_ML_W), row_spec(Z_SWA_W), row_spec(Z_NAT_W), row_spec(Z_G_W),
                  _layer_spec(fb_row, layer), _layer_spec(norm_g, layer),
                  _layer_spec(sink_rows, layer)]
                 + [pl.BlockSpec(memory_space=pl.ANY)] * len(prev),
        out_specs=[row_spec(ML_W), row_spec(SWA_W), row_spec(NAT_W)]
                  + [state_spec(d) for d in state_dims],
        out_shape=[jax.ShapeDtypeStruct((rows, ML_W), BF16),
                   jax.ShapeDtypeStruct((rows, SWA_W), BF16),
                   jax.ShapeDtypeStruct((rows, NAT_W), BF16)]
                  + [jax.ShapeDtypeStruct((BATCH, DEPTH) + d, F32) for d in state_dims],
        input_output_aliases={n_in + i: 3 + i for i in range(len(prev))},
        compiler_params=_params(1),
        name="ctx_mixers",
    )(z_ml, z_swa, z_nat, z_g, fb_row, norm_g, sink_rows, *prev)


def _lat_mlstm_kernel(zml_ref, zg_ref, fb_ref, ng_ref, c0_ref, n0_ref, m0_ref, mml_ref,
                      ht_scr, c_scr, n_scr, m_scr):
    n_chunks = DEC_SEQ // ML_CHUNK
    c_scr[...] = c0_ref[...]
    n_scr[...] = n0_ref[...]
    m_scr[...] = jnp.broadcast_to(m0_ref[...], m_scr.shape)

    def body(i, carry):
        for direction in range(2):
            chunk = i if direction == 0 else n_chunks - 1 - i
            rows = pl.ds(pl.multiple_of(chunk * ML_CHUNK, ML_CHUNK), ML_CHUNK)
            base = direction * ML_HEADS
            state = [(c_scr[base + hd], n_scr[base + hd:base + hd + 1, :],
                      m_scr[base + hd:base + hd + 1, 0:1]) for hd in range(ML_HEADS)]
            ht, new_state = _mlstm_chunk(zml_ref, zg_ref, fb_ref, rows, direction, state)
            ht_scr[direction, chunk] = ht
            for hd in range(ML_HEADS):
                _store_state(c_scr, n_scr, m_scr, base + hd, *new_state[hd])
        return carry

    lax.fori_loop(0, n_chunks, body, 0)
    for chunk in range(n_chunks):
        rows = slice(chunk * ML_CHUNK, (chunk + 1) * ML_CHUNK)
        mml_ref[rows, :] = _mlstm_finish(ht_scr[0, chunk] + ht_scr[1, chunk],
                                         zml_ref[rows, 3 * ML_W:4 * ML_W],
                                         ng_ref[...]).astype(BF16)


def _lat_mlstm_call(z_ml, z_g, fb_row, norm_g, state_c, state_n, state_m, layer):
    n_st = 2 * ML_HEADS
    n_chunks = DEC_SEQ // ML_CHUNK
    row_spec = lambda w: pl.BlockSpec((DEC_SEQ, w), lambda b: (b, 0))
    return pl.pallas_call(
        _lat_mlstm_kernel,
        grid=(DEC_BATCH,),
        in_specs=[row_spec(Z_ML_W), row_spec(Z_G_W), _layer_spec(fb_row, layer),
                  _layer_spec(norm_g, layer),
                  pl.BlockSpec((None, None, n_st, D_HEAD, D_HEAD), lambda b: (b, layer, 0, 0, 0)),
                  pl.BlockSpec((None, None, n_st, D_HEAD), lambda b: (b, layer, 0, 0)),
                  pl.BlockSpec((None, None, n_st, 1), lambda b: (b, layer, 0, 0))],
        out_specs=row_spec(ML_W),
        out_shape=jax.ShapeDtypeStruct((DEC_BATCH * DEC_SEQ, ML_W), BF16),
        scratch_shapes=[pltpu.VMEM((2, n_chunks, ML_W, ML_CHUNK), F32),
                        pltpu.VMEM((n_st, D_HEAD, D_HEAD), F32), pltpu.VMEM((n_st, D_HEAD), F32),
                        pltpu.VMEM((n_st, LANES), F32)],
        compiler_params=_params(1),
        name="lat_mlstm",
    )(z_ml, z_g, fb_row, norm_g, state_c, state_n, state_m)


def _rope(x, cos, sin_signed):
    lane = lax.broadcasted_iota(jnp.int32, x.shape, 1)
    first = (lane & 31) < 16
    partner = jnp.where(first, pltpu.roll(x, LANES - 16, 1), pltpu.roll(x, 16, 1))
    return x * cos + partner * sin_signed


def _lat_swa_kernel(zswa_ref, cos_ref, sin_ref, ckt_ref, cvt_ref, sink_ref, mswa_ref,
                    q_scr, k_scr, vt_scr, ot_scr, bias_scr, ck_scr, cvt_scr, st_scr, m_scr):
    cos = cos_ref[...]
    sin = sin_ref[...]
    for j in range(SWA_W // LANES):
        q_scr[:, j * LANES:(j + 1) * LANES] = (_rope(
            zswa_ref[:, j * LANES:(j + 1) * LANES], cos, sin) * ATT_SCALE).astype(BF16)
    k_scr[...] = _rope(zswa_ref[:, SWA_W:SWA_W + SWA_KV_W], cos, sin).astype(BF16)
    vt_scr[...] = zswa_ref[:, SWA_W + SWA_KV_W:SWA_W + 2 * SWA_KV_W].astype(BF16).T

    bl = SWA_BLOCK
    nb = DEC_SEQ // bl
    n_q = SWA_GROUP * bl
    kj = lax.broadcasted_iota(jnp.int32, (3 * bl, n_q), 0)
    qi = lax.broadcasted_iota(jnp.int32, (3 * bl, n_q), 1) & (bl - 1)
    keep = ((kj >= qi) & (kj < 2 * bl)) | ((kj >= 2 * bl) & (kj - 2 * bl <= qi))
    bias_scr[...] = jnp.where(keep, 0.0, NEG_INF)
    ck_scr[...] = jnp.concatenate([ckt_ref[kv] for kv in range(SWA_KV_HEADS)],
                                  axis=0).astype(BF16).T
    for kv in range(SWA_KV_HEADS):
        cvt_scr[kv] = _with_ones(cvt_ref[kv])

    def band(blk):
        lo_b, hi_b = max(blk - 1, 0), min(blk + 2, nb)
        return slice(lo_b * bl, hi_b * bl), slice((lo_b - blk + 1) * bl, (hi_b - blk + 1) * bl)

    def score_stage(kv, blk, slot):
        head = slice(kv * D_HEAD, (kv + 1) * D_HEAD)
        keys, bias_rows = band(blk)
        n_band = keys.stop - keys.start
        q = jnp.concatenate(
            [q_scr[blk * bl:(blk + 1) * bl,
                   (kv * SWA_GROUP + g) * D_HEAD:(kv * SWA_GROUP + g + 1) * D_HEAD]
             for g in range(SWA_GROUP)], axis=0)
        s_ctx = _dot_nt(ck_scr[:, head], q)
        s_band = _dot_nt(k_scr[keys, head], q) + bias_scr[bias_rows, :]
        st_scr[slot, 0:PAST_LEN, :] = s_ctx
        st_scr[slot, PAST_LEN:PAST_LEN + n_band, :] = s_band
        m = jnp.maximum(jnp.max(s_ctx, axis=0, keepdims=True),
                        jnp.max(s_band, axis=0, keepdims=True))
        m_scr[slot] = jnp.broadcast_to(jnp.maximum(m, sink_ref[kv:kv + 1, :]), m_scr.shape[1:])

    def value_stage(kv, blk, slot):
        head = slice(kv * D_HEAD, (kv + 1) * D_HEAD)
        keys, _ = band(blk)
        n_band = keys.stop - keys.start
        m = m_scr[slot, 0:1, :]
        p_ctx = jnp.exp(st_scr[slot, 0:PAST_LEN, :] - m).astype(BF16)
        p_band = jnp.exp(st_scr[slot, PAST_LEN:PAST_LEN + n_band, :] - m).astype(BF16)
        ot = (jnp.dot(cvt_scr[kv], p_ctx, preferred_element_type=F32)
              + jnp.dot(_with_ones(vt_scr[head, keys]), p_band, preferred_element_type=F32))
        den = ot[D_HEAD:D_HEAD + 1, :] + jnp.exp(sink_ref[kv:kv + 1, :] - m)
        o = (ot[0:D_HEAD, :] * (1.0 / den)).astype(BF16)
        for g in range(SWA_GROUP):
            hq = kv * SWA_GROUP + g
            ot_scr[hq * D_HEAD:(hq + 1) * D_HEAD, blk * bl:(blk + 1) * bl] = o[:, g * bl:(g + 1) * bl]

    problems = [(kv, blk) for kv in range(SWA_KV_HEADS) for blk in range(nb)]
    for i in range(len(problems) + 1):
        def stage(i=i):
            if i > 0:
                value_stage(*problems[i - 1], (i - 1) % 2)
            if i < len(problems):
                score_stage(*problems[i], i % 2)
        stage()
    for p in range(SWA_W // LANES):
        mswa_ref[:, p * LANES:(p + 1) * LANES] = ot_scr[p * LANES:(p + 1) * LANES, :].T


def _lat_swa_call(z_swa, cos_t, sin_t, cache_k, cache_v, sink_rows, layer):
    row_spec = lambda w: pl.BlockSpec((DEC_SEQ, w), lambda b: (b, 0))
    full = lambda a: pl.BlockSpec(a.shape, lambda b: (0,) * a.ndim)
    cache_spec = pl.BlockSpec((None, None, SWA_KV_HEADS, D_HEAD, PAST_LEN),
                              lambda b: (b, layer, 0, 0, 0))
    n_q = SWA_GROUP * SWA_BLOCK
    return pl.pallas_call(
        _lat_swa_kernel,
        grid=(DEC_BATCH,),
        in_specs=[row_spec(Z_SWA_W), full(cos_t), full(sin_t), cache_spec, cache_spec,
                  _layer_spec(sink_rows, layer)],
        out_specs=row_spec(SWA_W),
        out_shape=jax.ShapeDtypeStruct((DEC_BATCH * DEC_SEQ, SWA_W), BF16),
        scratch_shapes=[pltpu.VMEM((DEC_SEQ, SWA_W), BF16), pltpu.VMEM((DEC_SEQ, SWA_KV_W), BF16),
                        pltpu.VMEM((SWA_KV_W, DEC_SEQ), BF16), pltpu.VMEM((SWA_W, DEC_SEQ), BF16),
                        pltpu.VMEM((3 * SWA_BLOCK, n_q), F32),
                        pltpu.VMEM((PAST_LEN, SWA_KV_W), BF16),
                        pltpu.VMEM((SWA_KV_HEADS, D_HEAD + ONES_ROWS, PAST_LEN), BF16),
                        pltpu.VMEM((2, PAST_LEN + 3 * SWA_BLOCK, n_q), F32),
                        pltpu.VMEM((2, SUBLANES, n_q), F32)],
        compiler_params=_params(1),
        name="lat_swa",
    )(z_swa, cos_t, sin_t, cache_k, cache_v, sink_rows)


NAT_ROWS = DEC_SEQ // GRID_W
NAT_GROUP_ROWS = 4
NAT_GROUP_Q = NAT_GROUP_ROWS * GRID_W
NAT_RPB_ROWS = 2 * NAT_KH


def _nat_row_start(r):
    return min(max(r - NAT_KH // 2, 0), NAT_ROWS - NAT_KH)


def _nat_groups():
    groups, off = [], 0
    for g in range(NAT_ROWS // NAT_GROUP_ROWS):
        w0 = _nat_row_start(g * NAT_GROUP_ROWS)
        w1 = _nat_row_start((g + 1) * NAT_GROUP_ROWS - 1) + NAT_KH
        n_rows = w1 - w0 + (w1 - w0) % 2
        w0 = min(w0, NAT_ROWS - n_rows)
        groups.append((w0, n_rows, off))
        off += n_rows * GRID_W
    return groups, off


NAT_GROUPS, NAT_BIAS_KEYS = _nat_groups()


def _nat_build_bias(rpb_ref, bias_scr):
    shape = (GRID_W, LANES)
    q = lax.broadcasted_iota(jnp.int32, shape, 0)
    lane = lax.broadcasted_iota(jnp.int32, shape, 1)
    kc = lane & (GRID_W - 1)
    cs = jnp.clip(q - NAT_KW // 2, 0, GRID_W - NAT_KW)
    ok = (kc >= cs) & (kc < cs + NAT_KW)
    ok_lo = ok & (lane < GRID_W)
    ok_hi = ok & (lane >= GRID_W)
    neg_tile = jnp.full(shape, NEG_INF, F32)
    for hd in range(2):
        lo, hi = [], []
        for ro in range(2 * NAT_KH - 1):
            x = jnp.broadcast_to(rpb_ref[hd, ro:ro + 1, :], shape)
            lo.append(jnp.where(ok_lo, pltpu.roll(x, 0, 1, stride=1, stride_axis=0), NEG_INF))
            hi.append(jnp.where(ok_hi, pltpu.roll(x, GRID_W, 1, stride=1, stride_axis=0), NEG_INF))
        for g, (w0, n_rows, off) in enumerate(NAT_GROUPS):
            for jj in range(n_rows // 2):
                kra, krb = w0 + 2 * jj, w0 + 2 * jj + 1
                tiles = []
                for rr in range(NAT_GROUP_ROWS):
                    r = g * NAT_GROUP_ROWS + rr
                    r0 = _nat_row_start(r)
                    parts = []
                    if r0 <= kra < r0 + NAT_KH:
                        parts.append(lo[kra - r + NAT_KH - 1])
                    if r0 <= krb < r0 + NAT_KH:
                        parts.append(hi[krb - r + NAT_KH - 1])
                    tiles.append(neg_tile if not parts else (
                        parts[0] if len(parts) == 1 else jnp.maximum(parts[0], parts[1])))
                bias_scr[hd, off + jj * LANES:off + (jj + 1) * LANES, :] = (
                    jnp.concatenate(tiles, axis=0).T)


def _lat_nat_kernel(q_ref, k_ref, v_ref, ckt_ref, cvt_ref, rpb_ref, mnat_ref, bias_scr):
    @pl.when(pl.program_id(1) == 0)
    def _():
        _nat_build_bias(rpb_ref, bias_scr)

    vt = v_ref[...].astype(BF16).T
    ck2 = jnp.concatenate([ckt_ref[0], ckt_ref[1]], axis=0).astype(BF16).T
    ots = []
    for hd in range(2):
        head = slice(hd * D_HEAD, (hd + 1) * D_HEAD)
        ck = ck2[:, head]
        cvt_ones = _with_ones(cvt_ref[hd])
        ot_groups = []
        for g, (w0, n_rows, off) in enumerate(NAT_GROUPS):
            keys = slice(w0 * GRID_W, (w0 + n_rows) * GRID_W)
            parts = [(k_ref[keys, head].astype(BF16), _with_ones(vt[head, keys]),
                      bias_scr[hd, off:off + n_rows * GRID_W, :]),
                     (ck, cvt_ones, None)]
            q = (q_ref[g * NAT_GROUP_Q:(g + 1) * NAT_GROUP_Q, head] * ATT_SCALE).astype(BF16)
            ot_groups.append(_attend_t(q, parts))
        ots.append(jnp.concatenate(ot_groups, axis=1))
    mnat_ref[...] = jnp.concatenate(ots, axis=0).astype(BF16).T


def _lat_nat_call(z_nat, cache_k, cache_v, rpb_lanes, layer):
    n_pair = NAT_HEADS // 2
    col_spec = lambda off: pl.BlockSpec((DEC_SEQ, LANES), lambda p, b: (b, off + p))
    cache_spec = pl.BlockSpec((None, None, 2, D_HEAD, PAST_LEN), lambda p, b: (b, layer, p, 0, 0))
    return pl.pallas_call(
        _lat_nat_kernel,
        grid=(n_pair, DEC_BATCH),
        in_specs=[col_spec(0), col_spec(n_pair), col_spec(2 * n_pair), cache_spec, cache_spec,
                  pl.BlockSpec((None, 2, NAT_RPB_ROWS, LANES), lambda p, b: (layer, p, 0, 0))],
        out_specs=pl.BlockSpec((DEC_SEQ, LANES), lambda p, b: (b, p)),
        out_shape=jax.ShapeDtypeStruct((DEC_BATCH * DEC_SEQ, NAT_W), BF16),
        scratch_shapes=[pltpu.VMEM((2, NAT_BIAS_KEYS, NAT_GROUP_Q), F32)],
        compiler_params=_params(2),
        name="lat_nat",
    )(z_nat, z_nat, z_nat, cache_k, cache_v, rpb_lanes)


def _nat_rpb_lanes(rpb):
    n_off = 2 * NAT_KW - 1
    padded = jnp.concatenate(
        [rpb.astype(F32), jnp.zeros(rpb.shape[:-1] + (LANES - n_off,), F32)], axis=-1)
    rolled = jnp.roll(padded, -(NAT_KW - 1), axis=-1)
    pad_rows = jnp.zeros(rpb.shape[:-2] + (NAT_RPB_ROWS - rpb.shape[-2], LANES), F32)
    return jnp.concatenate([rolled, pad_rows], axis=-2)


def _rope_tables():
    t = np.arange(DEC_SEQ)[:, None]
    d = np.arange(LANES)[None, :] % D_HEAD
    pos = np.where(d < D_HEAD // 2, t // GRID_W, t % GRID_W).astype(np.float64)
    quarter = D_HEAD // 4
    freq = ROPE_BASE ** (-(d % quarter).astype(np.float64) / quarter)
    ang = (pos.astype(np.float32) * freq.astype(np.float32)).astype(np.float32)
    sign = np.where((d % (2 * quarter)) < quarter, -1.0, 1.0)
    return (jnp.asarray(np.cos(ang), dtype=F32), jnp.asarray(np.sin(ang) * sign, dtype=F32))


def _permute_in_columns(a):
    g0 = Z_ML_W
    s0 = g0 + N_GATES
    pad = jnp.zeros(a.shape[:-1] + (Z_G_W - N_GATES,), a.dtype)
    return jnp.concatenate([a[..., :g0], a[..., s0:], a[..., g0:s0], pad], axis=-1)


def kernel(x_prompt, x_sample, cache_swa_k, cache_swa_v, cache_nat_k, cache_nat_v, state_mlstm_C,
           state_mlstm_n, state_mlstm_m, c, c_ctx, w_ada, b_ada, w_in, b_in, mlstm_fbias,
           mlstm_norm_g, swa_sink, nat_rpb, w_out, ln1_g, ln1_b, w_mlp1, w_mlp2, ln2_g, ln2_b):
    xp = x_prompt.reshape(BATCH * SEQ, D_MODEL)
    xs = x_sample.reshape(DEC_BATCH * DEC_SEQ, D_MODEL)
    cvec = jnp.concatenate(
        [c, c_ctx[None, :], jnp.zeros((MOD_ROWS - DEC_BATCH - 1, D_MODEL), F32)], axis=0)
    mods = _mods_call(cvec, w_ada, b_ada).reshape(DEPTH * MOD_ROWS, 6, D_MODEL)

    w_in_t = jnp.swapaxes(w_in, 1, 2)
    b_in_p = _permute_in_columns(b_in).reshape(DEPTH, 1, Z_W)
    cache_swa_k, cache_swa_v, cache_nat_k, cache_nat_v = (
        jnp.swapaxes(a, -1, -2) for a in (cache_swa_k, cache_swa_v, cache_nat_k, cache_nat_v))
    vec = lambda a: a.reshape(DEPTH, 1, D_MODEL)
    tail_w = (w_out.astype(BF16), vec(ln1_g), vec(ln1_b), w_mlp1.astype(BF16),
              w_mlp2.astype(BF16), vec(ln2_g), vec(ln2_b))
    cos_t, sin_t = _rope_tables()
    rpb_lanes = _nat_rpb_lanes(nat_rpb)
    fb_rows = jnp.zeros((DEPTH, 1, Z_G_W), F32)
    fb_rows = fb_rows.at[:, 0, ML_HEADS:2 * ML_HEADS].set(mlstm_fbias[:, 0])
    fb_rows = fb_rows.at[:, 0, 3 * ML_HEADS:4 * ML_HEADS].set(mlstm_fbias[:, 1])
    norm_g = mlstm_norm_g.reshape(DEPTH, 1, ML_W)
    sink_ctx = jnp.repeat(swa_sink, SEQ, axis=1).reshape(DEPTH, SWA_KV_HEADS, SWA_GROUP * SEQ)
    sink_lat = jnp.repeat(swa_sink, SWA_BLOCK, axis=1).reshape(
        DEPTH, SWA_KV_HEADS, SWA_GROUP * SWA_BLOCK)

    n_st = 2 * ML_HEADS
    state_c = state_mlstm_C.reshape(DEC_BATCH, DEPTH, n_st, D_HEAD, D_HEAD)
    state_n = state_mlstm_n.reshape(DEC_BATCH, DEPTH, n_st, D_HEAD)
    state_m = state_mlstm_m.reshape(DEC_BATCH, DEPTH, n_st, 1)

    ctx_row = lambda tile: (lambda i: CTX_MOD_ROW)
    lat_row = lambda tile: (lambda i: i // (DEC_SEQ // tile))

    ctx_state = None
    for l in range(DEPTH):
        z_ml, z_swa, z_nat, z_g = _inproj_call(xp, mods, ctx_row(ROW_TILE), w_in_t, b_in_p, l)
        m_ml, m_swa, m_nat, *ctx_state = _ctx_mixers_call(
            z_ml, z_swa, z_nat, z_g, fb_rows, norm_g, sink_ctx, l, ctx_state)
        xp = _tail_call(xp, m_ml, m_swa, m_nat, mods, ctx_row(TAIL_TILE), tail_w, l)

        z_ml, z_swa, z_nat, z_g = _inproj_call(xs, mods, lat_row(ROW_TILE), w_in_t, b_in_p, l)
        m_ml = _lat_mlstm_call(z_ml, z_g, fb_rows, norm_g, state_c, state_n, state_m, l)
        m_swa = _lat_swa_call(z_swa, cos_t, sin_t, cache_swa_k, cache_swa_v, sink_lat, l)
        m_nat = _lat_nat_call(z_nat, cache_nat_k, cache_nat_v, rpb_lanes, l)
        xs = _tail_call(xs, m_ml, m_swa, m_nat, mods, lat_row(TAIL_TILE), tail_w, l)

    k_swa, v_swa, k_nat, v_nat, st_c, st_n, st_m = ctx_state
    return (xp.reshape(BATCH, SEQ, D_MODEL), xs.reshape(DEC_BATCH, DEC_SEQ, D_MODEL),
            k_swa, v_swa, k_nat, v_nat,
            st_c.reshape(BATCH, DEPTH, 2, ML_HEADS, D_HEAD, D_HEAD),
            st_n.reshape(BATCH, DEPTH, 2, ML_HEADS, D_HEAD),
            st_m[..., 0].reshape(BATCH, DEPTH, 2, ML_HEADS))
```

```python
import numpy as np
import jax
import jax.numpy as jnp
from jax import lax
from jax.experimental import pallas as pl
from jax.experimental.pallas import tpu as pltpu

D_MODEL = 1024
BATCH = 16
SEQ = 256
DEPTH = 4
DEC_BATCH = 8
DEC_SEQ = 1024
PAST_LEN = 512
GRID_W = 64
D_HEAD = 64
ML_HEADS = 4
SWA_HEADS = 6
SWA_KV_HEADS = 2
SWA_GROUP = SWA_HEADS // SWA_KV_HEADS
NAT_HEADS = 6
ML_W = ML_HEADS * D_HEAD
SWA_W = SWA_HEADS * D_HEAD
SWA_KV_W = SWA_KV_HEADS * D_HEAD
NAT_W = NAT_HEADS * D_HEAD
N_GATES = 4 * ML_HEADS
IN_DIM = 4 * ML_W + N_GATES + SWA_W + 2 * SWA_KV_W + 3 * NAT_W
D_FF = 4 * D_MODEL
SWA_WINDOW = 128
SWA_BLOCK = 128
NAT_KH = 8
NAT_KW = 16
ROPE_BASE = 10000.0
LN_EPS = 1e-5
DN_ALPHA = (2 * DEPTH) ** 0.25
ATT_SCALE = D_HEAD ** -0.5
LOG2E = 1.4426950408889634
Q_SCALE2 = ATT_SCALE * LOG2E

LANES = 128
SUBLANES = 8
Z_ML_W = 4 * ML_W
Z_SWA_W = SWA_W + 2 * SWA_KV_W
Z_NAT_W = 3 * NAT_W
Z_G_W = LANES
Z_W = Z_ML_W + Z_SWA_W + Z_NAT_W + Z_G_W
MOD_ROWS = 16
CTX_MOD_ROW = DEC_BATCH
ROW_TILE = 256
TAIL_TILE = 512
ML_CHUNK = 256
VMEM_LIMIT = 56 * 1024 * 1024

BF16 = jnp.bfloat16
F32 = jnp.float32
NEG_INF = float("-inf")


def _dot(a, b):
    return jnp.dot(a.astype(BF16), b.astype(BF16), preferred_element_type=F32)


def _dot_nt(a, b):
    return lax.dot_general(a.astype(BF16), b.astype(BF16), (((1,), (1,)), ((), ())),
                           preferred_element_type=F32)


def _dot_split3(tri, x):
    hi = x.astype(BF16)
    r1 = x - hi.astype(F32)
    mid = r1.astype(BF16)
    lo = (r1 - mid.astype(F32)).astype(BF16)
    return (jnp.dot(tri, hi, preferred_element_type=F32)
            + jnp.dot(tri, mid, preferred_element_type=F32)
            + jnp.dot(tri, lo, preferred_element_type=F32))


def _sigmoid(x):
    return 1.0 / (1.0 + jnp.exp(-x))


def _log_sigmoid(x):
    return jnp.minimum(x, 0.0) - jnp.log1p(jnp.exp(-jnp.abs(x)))


def _layer_norm(x, g, b):
    mu = jnp.mean(x, axis=-1, keepdims=True)
    xc = x - mu
    var = jnp.mean(xc * xc, axis=-1, keepdims=True)
    return xc * lax.rsqrt(var + LN_EPS) * g + b


def _params(n_axes=1):
    return pltpu.CompilerParams(dimension_semantics=("arbitrary",) * n_axes,
                                vmem_limit_bytes=VMEM_LIMIT)


def _layer_spec(a, layer, single_buffer=False):
    kwargs = {"pipeline_mode": pl.Buffered(1)} if single_buffer else {}
    return pl.BlockSpec((None,) + a.shape[1:], lambda *_: (layer,) + (0,) * (a.ndim - 1), **kwargs)


def _mods_kernel(c_ref, w_ref, b_ref, o_ref):
    c = c_ref[...]
    o_ref[...] = _dot(c * _sigmoid(c), w_ref[...]) + b_ref[...]


def _mods_call(cvec, w_ada, b_ada):
    n_col = 6
    return pl.pallas_call(
        _mods_kernel,
        grid=(DEPTH, n_col),
        in_specs=[
            pl.BlockSpec((MOD_ROWS, D_MODEL), lambda l, j: (0, 0)),
            pl.BlockSpec((None, D_MODEL, D_MODEL), lambda l, j: (l, 0, j)),
            pl.BlockSpec((None, 1, D_MODEL), lambda l, j: (l, 0, j)),
        ],
        out_specs=pl.BlockSpec((None, MOD_ROWS, D_MODEL), lambda l, j: (l, 0, j)),
        out_shape=jax.ShapeDtypeStruct((DEPTH, MOD_ROWS, 6 * D_MODEL), F32),
        compiler_params=_params(2),
        name="mods",
    )(cvec, w_ada, b_ada.reshape(DEPTH, 1, 6 * D_MODEL))


def _inproj_kernel(x_ref, mod_ref, wt_ref, b_ref, zml_ref, zswa_ref, znat_ref, zg_ref, w_scr):
    @pl.when(pl.program_id(0) == 0)
    def _():
        rest = IN_DIM - Z_ML_W - N_GATES
        w_scr[0:Z_ML_W, :] = wt_ref[0:Z_ML_W, :].astype(BF16)
        w_scr[Z_ML_W:Z_ML_W + rest, :] = wt_ref[Z_ML_W + N_GATES:IN_DIM, :].astype(BF16)
        w_scr[Z_ML_W + rest:IN_DIM, :] = wt_ref[Z_ML_W:Z_ML_W + N_GATES, :].astype(BF16)
        w_scr[IN_DIM:Z_W, :] = jnp.zeros((Z_W - IN_DIM, D_MODEL), BF16)

    h = (x_ref[...] * (1.0 + mod_ref[1:2, :]) + mod_ref[0:1, :]).astype(BF16)
    o = 0
    for ref in (zml_ref, zswa_ref, znat_ref, zg_ref):
        w = ref.shape[-1]
        z = _dot_nt(h, w_scr[o:o + w, :]) * b_ref[0:1, o:o + w] + b_ref[1:2, o:o + w]
        ref[...] = z.astype(ref.dtype)
        o += w


def _mod_spec(layer, mod_row_of_tile):
    return pl.BlockSpec((None, 6, D_MODEL),
                        lambda i: (layer * MOD_ROWS + mod_row_of_tile(i), 0, 0))


def _inproj_call(x, mods, mod_row_of_tile, w_in_t, b_in, layer, dtypes):
    rows = x.shape[0]
    widths = (Z_ML_W, Z_SWA_W, Z_NAT_W, Z_G_W)
    return pl.pallas_call(
        _inproj_kernel,
        grid=(rows // ROW_TILE,),
        in_specs=[
            pl.BlockSpec((ROW_TILE, D_MODEL), lambda i: (i, 0)),
            _mod_spec(layer, mod_row_of_tile),
            _layer_spec(w_in_t, layer, single_buffer=True),
            _layer_spec(b_in, layer),
        ],
        out_specs=[pl.BlockSpec((ROW_TILE, w), lambda i: (i, 0)) for w in widths],
        out_shape=[jax.ShapeDtypeStruct((rows, w), dt) for w, dt in zip(widths, dtypes)],
        scratch_shapes=[pltpu.VMEM((Z_W, D_MODEL), BF16)],
        compiler_params=_params(1),
        name="inproj",
    )(x, mods, w_in_t, b_in)


def _tail_kernel(x_ref, mml_ref, mswa_ref, mnat_ref, mod_ref, wout_ref, ln1g_ref, ln1b_ref,
                 w1_ref, w2_ref, ln2g_ref, ln2b_ref, o_ref):
    g_a, sh_m, sc_m, g_m = (mod_ref[i:i + 1, :] for i in (2, 3, 4, 5))
    mix = jnp.concatenate([mml_ref[...], mswa_ref[...], mnat_ref[...]], axis=1)
    proj = jnp.dot(mix, wout_ref[...], preferred_element_type=F32)
    y = _layer_norm(DN_ALPHA * x_ref[...] + g_a * proj, ln1g_ref[...], ln1b_ref[...])
    h = (y * (1.0 + sc_m) + sh_m).astype(BF16)
    f = jnp.maximum(jnp.dot(h, w1_ref[...], preferred_element_type=F32), 0.0)
    f = jnp.dot((f * f).astype(BF16), w2_ref[...], preferred_element_type=F32)
    o_ref[...] = _layer_norm(DN_ALPHA * y + g_m * f, ln2g_ref[...], ln2b_ref[...])


def _tail_call(x, mix_ml, mix_swa, mix_nat, mods, mod_row_of_tile, consts, layer):
    rows = x.shape[0]
    row_spec = lambda w: pl.BlockSpec((TAIL_TILE, w), lambda i: (i, 0))
    return pl.pallas_call(
        _tail_kernel,
        grid=(rows // TAIL_TILE,),
        in_specs=[row_spec(D_MODEL), row_spec(ML_W), row_spec(SWA_W), row_spec(NAT_W),
                  _mod_spec(layer, mod_row_of_tile)]
                 + [_layer_spec(a, layer, single_buffer=True) for a in consts],
        out_specs=row_spec(D_MODEL),
        out_shape=jax.ShapeDtypeStruct((rows, D_MODEL), F32),
        compiler_params=_params(1),
        name="tail",
    )(x, mix_ml, mix_swa, mix_nat, mods, *consts)


def _mlstm_direction(q, k, vt, a_col, b_row, li_row, neg, last, c_st, n_st, m_st):
    a_m = a_col + neg
    c_row = jnp.maximum(m_st, jnp.max(a_m, axis=0, keepdims=True))
    decay = jnp.exp(a_m - c_row)
    wprev = jnp.exp(m_st - c_row)
    mt = b_row + c_row
    st = _dot_nt(k, q) * decay
    den = jnp.sum(st, axis=0, keepdims=True)
    num = _dot(vt, st)
    if c_st is not None:
        cn = _dot_nt(jnp.concatenate([c_st, jnp.broadcast_to(n_st, (SUBLANES, D_HEAD))], axis=0),
                     q)
        num = num + wprev * cn[0:D_HEAD, :]
        den = den + wprev * cn[D_HEAD:D_HEAD + 1, :]
    ht = num * (1.0 / jnp.maximum(jnp.abs(den), jnp.exp(-mt)))
    m_new = mt[:, last:last + 1]
    ws = jnp.exp(b_row[:, last:last + 1] - b_row + li_row - m_new)
    upd = _dot(jnp.concatenate([vt * ws, jnp.broadcast_to(ws, (SUBLANES, ws.shape[1]))], axis=0),
               k)
    c_new = upd[0:D_HEAD, :]
    n_new = upd[D_HEAD:D_HEAD + 1, :]
    if c_st is not None:
        wc = wprev[:, last:last + 1]
        c_new = c_new + wc * c_st
        n_new = n_new + wc * n_st
    return ht, c_new, n_new, m_new


def _mlstm_chunk(zml_ref, g_ref, fb_ref, rows, direction, state):
    n = ML_CHUNK
    g = g_ref[rows, :]
    ls = _log_sigmoid(g + fb_ref[...])
    r = lax.broadcasted_iota(jnp.int32, (n, n), 0)
    c = lax.broadcasted_iota(jnp.int32, (n, n), 1)
    if direction == 0:
        tri, neg, last = c <= r, jnp.where(r <= c, 0.0, NEG_INF), n - 1
    else:
        tri, neg, last = c >= r, jnp.where(r >= c, 0.0, NEG_INF), 0
    b = _dot_split3(tri.astype(BF16), ls)
    a = pltpu.roll(g, ML_HEADS, 1) - b
    bt = b.T
    gt = g.T
    i_col = 2 * ML_HEADS * direction
    f_col = i_col + ML_HEADS
    hts, new_state = [], []
    for pair in range(ML_HEADS // 2):
        vt2 = zml_ref[rows, 2 * ML_W + pair * LANES:2 * ML_W + (pair + 1) * LANES].T
        for sub in range(2):
            hd = 2 * pair + sub
            feat = slice(sub * D_HEAD, (sub + 1) * D_HEAD)
            q = zml_ref[rows, hd * D_HEAD:(hd + 1) * D_HEAD].astype(BF16)
            k = zml_ref[rows, ML_W + hd * D_HEAD:ML_W + (hd + 1) * D_HEAD] * ATT_SCALE
            ht, c_new, n_new, m_new = _mlstm_direction(
                q, k, vt2[feat, :],
                a[:, f_col + hd:f_col + hd + 1], bt[f_col + hd:f_col + hd + 1, :],
                gt[i_col + hd:i_col + hd + 1, :], neg, last, *state[hd])
            hts.append(ht)
            new_state.append((c_new, n_new, m_new))
    return jnp.concatenate(hts, axis=0), new_state


def _mlstm_finish(ht, o_gate, norm_g):
    outs = []
    for hd in range(ML_HEADS):
        x = ht[hd * D_HEAD:(hd + 1) * D_HEAD, :]
        mu = jnp.mean(x, axis=0, keepdims=True)
        xc = x - mu
        var = jnp.mean(xc * xc, axis=0, keepdims=True)
        outs.append(xc * lax.rsqrt(var + LN_EPS))
    hn = jnp.concatenate(outs, axis=0).T
    return hn * norm_g * _sigmoid(o_gate)


def _store_state(c_ref, n_ref, m_ref, idx, c_new, n_new, m_new):
    c_ref[idx] = c_new
    n_ref[idx:idx + 1, :] = n_new
    m_ref[idx:idx + 1, :] = jnp.broadcast_to(m_new, (1, LANES))


ONES_ROWS = 16


def _with_ones(vt):
    ones = jnp.ones((ONES_ROWS, vt.shape[1]), BF16)
    return jnp.concatenate([vt.astype(BF16), ones], axis=0)


def _attend_t(q, parts, sink_row=None):
    sts = []
    for k, _, bias_t in parts:
        st = _dot_nt(k, q)
        sts.append(st if bias_t is None else st + bias_t)
    m = None
    for st in sts:
        mx = jnp.max(st, axis=0, keepdims=True)
        m = mx if m is None else jnp.maximum(m, mx)
    if sink_row is not None:
        m = jnp.maximum(m, sink_row)
    ot = None
    for st, (_, vt_ones, _) in zip(sts, parts):
        part = jnp.dot(vt_ones, jnp.exp2(st - m).astype(BF16), preferred_element_type=F32)
        ot = part if ot is None else ot + part
    den = ot[D_HEAD:D_HEAD + 1, :]
    if sink_row is not None:
        den = den + jnp.exp2(sink_row - m)
    return ot[0:D_HEAD, :] * (1.0 / den)


def _head_rows(pairs_t, hd):
    return pairs_t[hd // 2][(hd % 2) * D_HEAD:(hd % 2 + 1) * D_HEAD, :]


def _store_heads_t(o_ref, ots):
    for p in range(len(ots) // 2):
        o_ref[:, p * LANES:(p + 1) * LANES] = jnp.concatenate(
            [ots[2 * p], ots[2 * p + 1]], axis=0).astype(o_ref.dtype).T


N_STATE = 2 * ML_HEADS
N_CTX_STATE_OUTS = 7


def _ctx_mixers_kernel(zml_ref, zswa_ref, znat_ref, zg_ref, fb_ref, ng_ref, sink_ref, *rest):
    (mml_ref, mswa_ref, mnat_ref, sk_ref, sv_ref, nk_ref, nv_ref,
     c_ref, n_ref, m_ref) = rest[-(3 + N_CTX_STATE_OUTS):]
    rows = slice(0, SEQ)
    zero_state = [(None, jnp.zeros((1, D_HEAD), F32), jnp.zeros((1, 1), F32))] * ML_HEADS
    ht_sum = None
    for direction in range(2):
        ht, new_state = _mlstm_chunk(zml_ref, zg_ref, fb_ref, rows, direction, zero_state)
        ht_sum = ht if ht_sum is None else ht_sum + ht
        for hd in range(ML_HEADS):
            _store_state(c_ref, n_ref, m_ref, direction * ML_HEADS + hd, *new_state[hd])
    mml_ref[...] = _mlstm_finish(ht_sum, zml_ref[:, 3 * ML_W:4 * ML_W], ng_ref[...]).astype(BF16)

    scaled_q = lambda ref, hd: ref[:, hd * D_HEAD:(hd + 1) * D_HEAD].astype(BF16)
    vts = [zswa_ref[:, SWA_W + SWA_KV_W:SWA_W + 2 * SWA_KV_W].astype(BF16).T]
    ots = []
    for kv in range(SWA_KV_HEADS):
        k = zswa_ref[:, SWA_W + kv * D_HEAD:SWA_W + (kv + 1) * D_HEAD]
        sk_ref[kv] = k
        sv_ref[kv] = zswa_ref[:, SWA_W + SWA_KV_W + kv * D_HEAD:SWA_W + SWA_KV_W + (kv + 1) * D_HEAD]
        q = jnp.concatenate([scaled_q(zswa_ref, kv * SWA_GROUP + g) for g in range(SWA_GROUP)],
                            axis=0)
        ot = _attend_t(q, [(k.astype(BF16), _with_ones(_head_rows(vts, kv)), None)],
                       sink_ref[kv:kv + 1, :])
        ots += [ot[:, g * SEQ:(g + 1) * SEQ] for g in range(SWA_GROUP)]
    _store_heads_t(mswa_ref, ots)

    vts = [znat_ref[:, 2 * NAT_W + p * LANES:2 * NAT_W + (p + 1) * LANES].astype(BF16).T
           for p in range(NAT_W // LANES)]
    ots = []
    for hd in range(NAT_HEADS):
        k = znat_ref[:, NAT_W + hd * D_HEAD:NAT_W + (hd + 1) * D_HEAD]
        nk_ref[hd] = k
        nv_ref[hd] = znat_ref[:, 2 * NAT_W + hd * D_HEAD:2 * NAT_W + (hd + 1) * D_HEAD]
        ots.append(_attend_t(scaled_q(znat_ref, hd),
                             [(k.astype(BF16), _with_ones(_head_rows(vts, hd)), None)]))
    _store_heads_t(mnat_ref, ots)


CTX_STATE_DIMS = ((SWA_KV_HEADS, SEQ, D_HEAD), (SWA_KV_HEADS, SEQ, D_HEAD),
                  (NAT_HEADS, SEQ, D_HEAD), (NAT_HEADS, SEQ, D_HEAD),
                  (N_STATE, D_HEAD, D_HEAD), (N_STATE, D_HEAD), (N_STATE, LANES))


def _ctx_state_init():
    return [jnp.zeros((BATCH, DEPTH) + d, F32) for d in CTX_STATE_DIMS]


def _ctx_mixers_call(z_ml, z_swa, z_nat, z_g, fb_row, norm_g, sink_rows, layer, prev_state):
    rows = z_ml.shape[0]
    row_spec = lambda w: pl.BlockSpec((SEQ, w), lambda b: (b, 0))
    state_dims = CTX_STATE_DIMS
    state_spec = lambda d: pl.BlockSpec((None, None) + d, lambda b: (b, layer) + (0,) * len(d))
    prev = tuple(prev_state)
    n_in = 7
    return pl.pallas_call(
        _ctx_mixers_kernel,
        grid=(BATCH,),
        in_specs=[row_spec(Z_ML_W), row_spec(Z_SWA_W), row_spec(Z_NAT_W), row_spec(Z_G_W),
                  _layer_spec(fb_row, layer), _layer_spec(norm_g, layer),
                  _layer_spec(sink_rows, layer)]
                 + [pl.BlockSpec(memory_space=pl.ANY)] * len(prev),
        out_specs=[row_spec(ML_W), row_spec(SWA_W), row_spec(NAT_W)]
                  + [state_spec(d) for d in state_dims],
        out_shape=[jax.ShapeDtypeStruct((rows, ML_W), BF16),
                   jax.ShapeDtypeStruct((rows, SWA_W), BF16),
                   jax.ShapeDtypeStruct((rows, NAT_W), BF16)]
                  + [jax.ShapeDtypeStruct((BATCH, DEPTH) + d, F32) for d in state_dims],
        input_output_aliases={n_in + i: 3 + i for i in range(len(prev))},
        compiler_params=_params(1),
        name="ctx_mixers",
    )(z_ml, z_swa, z_nat, z_g, fb_row, norm_g, sink_rows, *prev)


def _lat_mlstm_kernel(zml_ref, zg_ref, fb_ref, ng_ref, c0_ref, n0_ref, m0_ref, mml_ref,
                      ht_scr, c_scr, n_scr, m_scr):
    n_chunks = DEC_SEQ // ML_CHUNK
    c_scr[...] = c0_ref[...]
    n_scr[...] = n0_ref[...]
    m_scr[...] = jnp.broadcast_to(m0_ref[...], m_scr.shape)

    def body(i, carry):
        for direction in range(2):
            chunk = i if direction == 0 else n_chunks - 1 - i
            rows = pl.ds(pl.multiple_of(chunk * ML_CHUNK, ML_CHUNK), ML_CHUNK)
            base = direction * ML_HEADS
            state = [(c_scr[base + hd], n_scr[base + hd:base + hd + 1, :],
                      m_scr[base + hd:base + hd + 1, 0:1]) for hd in range(ML_HEADS)]
            ht, new_state = _mlstm_chunk(zml_ref, zg_ref, fb_ref, rows, direction, state)
            ht_scr[direction, chunk] = ht
            for hd in range(ML_HEADS):
                _store_state(c_scr, n_scr, m_scr, base + hd, *new_state[hd])
        return carry

    lax.fori_loop(0, n_chunks, body, 0)
    for chunk in range(n_chunks):
        rows = slice(chunk * ML_CHUNK, (chunk + 1) * ML_CHUNK)
        mml_ref[rows, :] = _mlstm_finish(ht_scr[0, chunk] + ht_scr[1, chunk],
                                         zml_ref[rows, 3 * ML_W:4 * ML_W],
                                         ng_ref[...]).astype(BF16)


def _lat_mlstm_call(z_ml, z_g, fb_row, norm_g, state_c, state_n, state_m, layer):
    n_st = 2 * ML_HEADS
    n_chunks = DEC_SEQ // ML_CHUNK
    row_spec = lambda w: pl.BlockSpec((DEC_SEQ, w), lambda b: (b, 0))
    return pl.pallas_call(
        _lat_mlstm_kernel,
        grid=(DEC_BATCH,),
        in_specs=[row_spec(Z_ML_W), row_spec(Z_G_W), _layer_spec(fb_row, layer),
                  _layer_spec(norm_g, layer),
                  pl.BlockSpec((None, None, n_st, D_HEAD, D_HEAD), lambda b: (b, layer, 0, 0, 0)),
                  pl.BlockSpec((None, None, n_st, D_HEAD), lambda b: (b, layer, 0, 0)),
                  pl.BlockSpec((None, None, n_st, 1), lambda b: (b, layer, 0, 0))],
        out_specs=row_spec(ML_W),
        out_shape=jax.ShapeDtypeStruct((DEC_BATCH * DEC_SEQ, ML_W), BF16),
        scratch_shapes=[pltpu.VMEM((2, n_chunks, ML_W, ML_CHUNK), F32),
                        pltpu.VMEM((n_st, D_HEAD, D_HEAD), F32), pltpu.VMEM((n_st, D_HEAD), F32),
                        pltpu.VMEM((n_st, LANES), F32)],
        compiler_params=_params(1),
        name="lat_mlstm",
    )(z_ml, z_g, fb_row, norm_g, state_c, state_n, state_m)


def _rope(x, cos, sin_signed):
    lane = lax.broadcasted_iota(jnp.int32, x.shape, 1)
    first = (lane & 31) < 16
    partner = jnp.where(first, pltpu.roll(x, LANES - 16, 1), pltpu.roll(x, 16, 1))
    return x * cos + partner * sin_signed


def _lat_swa_kernel(zswa_ref, cos_ref, sin_ref, ckt_ref, cvt_ref, sink_ref, mswa_ref,
                    q_scr, k_scr, vt_scr, ot_scr, bias_scr, ck_scr, cvt_scr, st_scr, m_scr):
    cos = cos_ref[...]
    sin = sin_ref[...]
    for j in range(SWA_W // LANES):
        q_scr[:, j * LANES:(j + 1) * LANES] = _rope(
            zswa_ref[:, j * LANES:(j + 1) * LANES], cos, sin).astype(BF16)
    k_scr[...] = _rope(zswa_ref[:, SWA_W:SWA_W + SWA_KV_W], cos, sin).astype(BF16)
    vt_scr[...] = zswa_ref[:, SWA_W + SWA_KV_W:SWA_W + 2 * SWA_KV_W].astype(BF16).T

    bl = SWA_BLOCK
    nb = DEC_SEQ // bl
    n_q = SWA_GROUP * bl
    kj = lax.broadcasted_iota(jnp.int32, (3 * bl, n_q), 0)
    qi = lax.broadcasted_iota(jnp.int32, (3 * bl, n_q), 1) & (bl - 1)
    keep = ((kj >= qi) & (kj < 2 * bl)) | ((kj >= 2 * bl) & (kj - 2 * bl <= qi))
    bias_scr[...] = jnp.where(keep, 0.0, NEG_INF)
    ck_scr[...] = jnp.concatenate([ckt_ref[kv] for kv in range(SWA_KV_HEADS)],
                                  axis=0).astype(BF16).T
    for kv in range(SWA_KV_HEADS):
        cvt_scr[kv] = _with_ones(cvt_ref[kv])

    def band(blk):
        lo_b, hi_b = max(blk - 1, 0), min(blk + 2, nb)
        return slice(lo_b * bl, hi_b * bl), slice((lo_b - blk + 1) * bl, (hi_b - blk + 1) * bl)

    def score_stage(kv, blk, slot):
        head = slice(kv * D_HEAD, (kv + 1) * D_HEAD)
        keys, bias_rows = band(blk)
        n_band = keys.stop - keys.start
        q = jnp.concatenate(
            [q_scr[blk * bl:(blk + 1) * bl,
                   (kv * SWA_GROUP + g) * D_HEAD:(kv * SWA_GROUP + g + 1) * D_HEAD]
             for g in range(SWA_GROUP)], axis=0)
        s_ctx = _dot_nt(ck_scr[:, head], q)
        s_band = _dot_nt(k_scr[keys, head], q) + bias_scr[bias_rows, :]
        st_scr[slot, 0:PAST_LEN, :] = s_ctx
        st_scr[slot, PAST_LEN:PAST_LEN + n_band, :] = s_band
        m = jnp.maximum(jnp.max(s_ctx, axis=0, keepdims=True),
                        jnp.max(s_band, axis=0, keepdims=True))
        m_scr[slot] = jnp.broadcast_to(jnp.maximum(m, sink_ref[kv:kv + 1, :]), m_scr.shape[1:])

    def value_stage(kv, blk, slot):
        head = slice(kv * D_HEAD, (kv + 1) * D_HEAD)
        keys, _ = band(blk)
        n_band = keys.stop - keys.start
        m = m_scr[slot, 0:1, :]
        p_ctx = jnp.exp2(st_scr[slot, 0:PAST_LEN, :] - m).astype(BF16)
        p_band = jnp.exp2(st_scr[slot, PAST_LEN:PAST_LEN + n_band, :] - m).astype(BF16)
        ot = (jnp.dot(cvt_scr[kv], p_ctx, preferred_element_type=F32)
              + jnp.dot(_with_ones(vt_scr[head, keys]), p_band, preferred_element_type=F32))
        den = ot[D_HEAD:D_HEAD + 1, :] + jnp.exp2(sink_ref[kv:kv + 1, :] - m)
        o = (ot[0:D_HEAD, :] * (1.0 / den)).astype(BF16)
        for g in range(SWA_GROUP):
            hq = kv * SWA_GROUP + g
            ot_scr[hq * D_HEAD:(hq + 1) * D_HEAD, blk * bl:(blk + 1) * bl] = o[:, g * bl:(g + 1) * bl]

    problems = [(kv, blk) for kv in range(SWA_KV_HEADS) for blk in range(nb)]
    for i in range(len(problems) + 1):
        def stage(i=i):
            if i > 0:
                value_stage(*problems[i - 1], (i - 1) % 2)
            if i < len(problems):
                score_stage(*problems[i], i % 2)
        stage()
    for p in range(SWA_W // LANES):
        mswa_ref[:, p * LANES:(p + 1) * LANES] = ot_scr[p * LANES:(p + 1) * LANES, :].T


def _lat_swa_call(z_swa, cos_t, sin_t, cache_k, cache_v, sink_rows, layer):
    row_spec = lambda w: pl.BlockSpec((DEC_SEQ, w), lambda b: (b, 0))
    full = lambda a: pl.BlockSpec(a.shape, lambda b: (0,) * a.ndim)
    cache_spec = pl.BlockSpec((None, None, SWA_KV_HEADS, D_HEAD, PAST_LEN),
                              lambda b: (b, layer, 0, 0, 0))
    n_q = SWA_GROUP * SWA_BLOCK
    return pl.pallas_call(
        _lat_swa_kernel,
        grid=(DEC_BATCH,),
        in_specs=[row_spec(Z_SWA_W), full(cos_t), full(sin_t), cache_spec, cache_spec,
                  _layer_spec(sink_rows, layer)],
        out_specs=row_spec(SWA_W),
        out_shape=jax.ShapeDtypeStruct((DEC_BATCH * DEC_SEQ, SWA_W), BF16),
        scratch_shapes=[pltpu.VMEM((DEC_SEQ, SWA_W), BF16), pltpu.VMEM((DEC_SEQ, SWA_KV_W), BF16),
                        pltpu.VMEM((SWA_KV_W, DEC_SEQ), BF16), pltpu.VMEM((SWA_W, DEC_SEQ), BF16),
                        pltpu.VMEM((3 * SWA_BLOCK, n_q), F32),
                        pltpu.VMEM((PAST_LEN, SWA_KV_W), BF16),
                        pltpu.VMEM((SWA_KV_HEADS, D_HEAD + ONES_ROWS, PAST_LEN), BF16),
                        pltpu.VMEM((2, PAST_LEN + 3 * SWA_BLOCK, n_q), F32),
                        pltpu.VMEM((2, SUBLANES, n_q), F32)],
        compiler_params=_params(1),
        name="lat_swa",
    )(z_swa, cos_t, sin_t, cache_k, cache_v, sink_rows)


NAT_ROWS = DEC_SEQ // GRID_W
NAT_GROUP_ROWS = 4
NAT_GROUP_Q = NAT_GROUP_ROWS * GRID_W
NAT_RPB_ROWS = 2 * NAT_KH


def _nat_row_start(r):
    return min(max(r - NAT_KH // 2, 0), NAT_ROWS - NAT_KH)


def _nat_groups():
    groups, off = [], 0
    for g in range(NAT_ROWS // NAT_GROUP_ROWS):
        w0 = _nat_row_start(g * NAT_GROUP_ROWS)
        w1 = _nat_row_start((g + 1) * NAT_GROUP_ROWS - 1) + NAT_KH
        n_rows = w1 - w0 + (w1 - w0) % 2
        w0 = min(w0, NAT_ROWS - n_rows)
        groups.append((w0, n_rows, off))
        off += n_rows * GRID_W
    return groups, off


NAT_GROUPS, NAT_BIAS_KEYS = _nat_groups()


def _nat_build_bias(rpb_ref, bias_scr):
    shape = (GRID_W, LANES)
    q = lax.broadcasted_iota(jnp.int32, shape, 0)
    lane = lax.broadcasted_iota(jnp.int32, shape, 1)
    kc = lane & (GRID_W - 1)
    cs = jnp.clip(q - NAT_KW // 2, 0, GRID_W - NAT_KW)
    ok = (kc >= cs) & (kc < cs + NAT_KW)
    ok_lo = ok & (lane < GRID_W)
    ok_hi = ok & (lane >= GRID_W)
    neg_tile = jnp.full(shape, NEG_INF, F32)
    for hd in range(2):
        lo, hi = [], []
        for ro in range(2 * NAT_KH - 1):
            x = jnp.broadcast_to(rpb_ref[hd, ro:ro + 1, :], shape)
            lo.append(jnp.where(ok_lo, pltpu.roll(x, 0, 1, stride=1, stride_axis=0), NEG_INF))
            hi.append(jnp.where(ok_hi, pltpu.roll(x, GRID_W, 1, stride=1, stride_axis=0), NEG_INF))
        for g, (w0, n_rows, off) in enumerate(NAT_GROUPS):
            for jj in range(n_rows // 2):
                kra, krb = w0 + 2 * jj, w0 + 2 * jj + 1
                tiles = []
                for rr in range(NAT_GROUP_ROWS):
                    r = g * NAT_GROUP_ROWS + rr
                    r0 = _nat_row_start(r)
                    parts = []
                    if r0 <= kra < r0 + NAT_KH:
                        parts.append(lo[kra - r + NAT_KH - 1])
                    if r0 <= krb < r0 + NAT_KH:
                        parts.append(hi[krb - r + NAT_KH - 1])
                    tiles.append(neg_tile if not parts else (
                        parts[0] if len(parts) == 1 else jnp.maximum(parts[0], parts[1])))
                bias_scr[hd, off + jj * LANES:off + (jj + 1) * LANES, :] = (
                    jnp.concatenate(tiles, axis=0).T)


def _lat_nat_kernel(q_ref, k_ref, v_ref, ckt_ref, cvt_ref, rpb_ref, mnat_ref, bias_scr):
    @pl.when(pl.program_id(1) == 0)
    def _():
        _nat_build_bias(rpb_ref, bias_scr)

    vt = v_ref[...].astype(BF16).T
    ck2 = jnp.concatenate([ckt_ref[0], ckt_ref[1]], axis=0).astype(BF16).T
    ots = []
    for hd in range(2):
        head = slice(hd * D_HEAD, (hd + 1) * D_HEAD)
        ck = ck2[:, head]
        cvt_ones = _with_ones(cvt_ref[hd])
        ot_groups = []
        for g, (w0, n_rows, off) in enumerate(NAT_GROUPS):
            keys = slice(w0 * GRID_W, (w0 + n_rows) * GRID_W)
            parts = [(k_ref[keys, head].astype(BF16), _with_ones(vt[head, keys]),
                      bias_scr[hd, off:off + n_rows * GRID_W, :]),
                     (ck, cvt_ones, None)]
            q = q_ref[g * NAT_GROUP_Q:(g + 1) * NAT_GROUP_Q, head].astype(BF16)
            ot_groups.append(_attend_t(q, parts))
        ots.append(jnp.concatenate(ot_groups, axis=1))
    mnat_ref[...] = jnp.concatenate(ots, axis=0).astype(BF16).T


def _lat_nat_call(z_nat, cache_k, cache_v, rpb_lanes, layer):
    n_pair = NAT_HEADS // 2
    col_spec = lambda off: pl.BlockSpec((DEC_SEQ, LANES), lambda p, b: (b, off + p))
    cache_spec = pl.BlockSpec((None, None, 2, D_HEAD, PAST_LEN), lambda p, b: (b, layer, p, 0, 0))
    return pl.pallas_call(
        _lat_nat_kernel,
        grid=(n_pair, DEC_BATCH),
        in_specs=[col_spec(0), col_spec(n_pair), col_spec(2 * n_pair), cache_spec, cache_spec,
                  pl.BlockSpec((None, 2, NAT_RPB_ROWS, LANES), lambda p, b: (layer, p, 0, 0))],
        out_specs=pl.BlockSpec((DEC_SEQ, LANES), lambda p, b: (b, p)),
        out_shape=jax.ShapeDtypeStruct((DEC_BATCH * DEC_SEQ, NAT_W), BF16),
        scratch_shapes=[pltpu.VMEM((2, NAT_BIAS_KEYS, NAT_GROUP_Q), F32)],
        compiler_params=_params(2),
        name="lat_nat",
    )(z_nat, z_nat, z_nat, cache_k, cache_v, rpb_lanes)


def _nat_rpb_lanes(rpb):
    n_off = 2 * NAT_KW - 1
    padded = jnp.concatenate(
        [rpb.astype(F32), jnp.zeros(rpb.shape[:-1] + (LANES - n_off,), F32)], axis=-1)
    rolled = jnp.roll(padded, -(NAT_KW - 1), axis=-1)
    pad_rows = jnp.zeros(rpb.shape[:-2] + (NAT_RPB_ROWS - rpb.shape[-2], LANES), F32)
    return jnp.concatenate([rolled, pad_rows], axis=-2)


def _rope_tables():
    t = np.arange(DEC_SEQ)[:, None]
    d = np.arange(LANES)[None, :] % D_HEAD
    pos = np.where(d < D_HEAD // 2, t // GRID_W, t % GRID_W).astype(np.float64)
    quarter = D_HEAD // 4
    freq = ROPE_BASE ** (-(d % quarter).astype(np.float64) / quarter)
    ang = (pos.astype(np.float32) * freq.astype(np.float32)).astype(np.float32)
    sign = np.where((d % (2 * quarter)) < quarter, -1.0, 1.0)
    return (jnp.asarray(np.cos(ang), dtype=F32), jnp.asarray(np.sin(ang) * sign, dtype=F32))


def _permute_in_columns(a):
    g0 = Z_ML_W
    s0 = g0 + N_GATES
    pad = jnp.zeros(a.shape[:-1] + (Z_G_W - N_GATES,), a.dtype)
    return jnp.concatenate([a[..., :g0], a[..., s0:], a[..., g0:s0], pad], axis=-1)


def kernel(x_prompt, x_sample, cache_swa_k, cache_swa_v, cache_nat_k, cache_nat_v, state_mlstm_C,
           state_mlstm_n, state_mlstm_m, c, c_ctx, w_ada, b_ada, w_in, b_in, mlstm_fbias,
           mlstm_norm_g, swa_sink, nat_rpb, w_out, ln1_g, ln1_b, w_mlp1, w_mlp2, ln2_g, ln2_b):
    xp = x_prompt.reshape(BATCH * SEQ, D_MODEL)
    xs = x_sample.reshape(DEC_BATCH * DEC_SEQ, D_MODEL)
    cvec = jnp.concatenate(
        [c, c_ctx[None, :], jnp.zeros((MOD_ROWS - DEC_BATCH - 1, D_MODEL), F32)], axis=0)
    mods = _mods_call(cvec, w_ada, b_ada).reshape(DEPTH * MOD_ROWS, 6, D_MODEL)

    w_in_t = jnp.swapaxes(w_in, 1, 2)
    out_scale = np.ones((Z_W,), np.float32)
    out_scale[Z_ML_W:Z_ML_W + SWA_W] = Q_SCALE2
    out_scale[Z_ML_W + Z_SWA_W:Z_ML_W + Z_SWA_W + NAT_W] = Q_SCALE2
    b_in_p = jnp.stack([jnp.broadcast_to(out_scale, (DEPTH, Z_W)),
                        _permute_in_columns(b_in) * out_scale], axis=1)
    cache_swa_k, cache_swa_v, cache_nat_k, cache_nat_v = (
        jnp.swapaxes(a, -1, -2) for a in (cache_swa_k, cache_swa_v, cache_nat_k, cache_nat_v))
    vec = lambda a: a.reshape(DEPTH, 1, D_MODEL)
    tail_w = (w_out.astype(BF16), vec(ln1_g), vec(ln1_b), w_mlp1.astype(BF16),
              w_mlp2.astype(BF16), vec(ln2_g), vec(ln2_b))
    cos_t, sin_t = _rope_tables()
    rpb_lanes = _nat_rpb_lanes(nat_rpb) * LOG2E
    swa_sink = swa_sink * LOG2E
    fb_rows = jnp.zeros((DEPTH, 1, Z_G_W), F32)
    fb_rows = fb_rows.at[:, 0, ML_HEADS:2 * ML_HEADS].set(mlstm_fbias[:, 0])
    fb_rows = fb_rows.at[:, 0, 3 * ML_HEADS:4 * ML_HEADS].set(mlstm_fbias[:, 1])
    norm_g = mlstm_norm_g.reshape(DEPTH, 1, ML_W)
    sink_ctx = jnp.repeat(swa_sink, SEQ, axis=1).reshape(DEPTH, SWA_KV_HEADS, SWA_GROUP * SEQ)
    sink_lat = jnp.repeat(swa_sink, SWA_BLOCK, axis=1).reshape(
        DEPTH, SWA_KV_HEADS, SWA_GROUP * SWA_BLOCK)

    n_st = 2 * ML_HEADS
    state_c = state_mlstm_C.reshape(DEC_BATCH, DEPTH, n_st, D_HEAD, D_HEAD)
    state_n = state_mlstm_n.reshape(DEC_BATCH, DEPTH, n_st, D_HEAD)
    state_m = state_mlstm_m.reshape(DEC_BATCH, DEPTH, n_st, 1)

    ctx_row = lambda tile: (lambda i: CTX_MOD_ROW)
    lat_row = lambda tile: (lambda i: i // (DEC_SEQ // tile))

    ctx_state = _ctx_state_init()
    for l in range(DEPTH):
        z_ml, z_swa, z_nat, z_g = _inproj_call(xp, mods, ctx_row(ROW_TILE), w_in_t, b_in_p, l,
                                               (F32, F32, F32, F32))
        m_ml, m_swa, m_nat, *ctx_state = _ctx_mixers_call(
            z_ml, z_swa, z_nat, z_g, fb_rows, norm_g, sink_ctx, l, ctx_state)
        xp = _tail_call(xp, m_ml, m_swa, m_nat, mods, ctx_row(TAIL_TILE), tail_w, l)

        z_ml, z_swa, z_nat, z_g = _inproj_call(xs, mods, lat_row(ROW_TILE), w_in_t, b_in_p, l,
                                               (F32, F32, BF16, F32))
        m_ml = _lat_mlstm_call(z_ml, z_g, fb_rows, norm_g, state_c, state_n, state_m, l)
        m_swa = _lat_swa_call(z_swa, cos_t, sin_t, cache_swa_k, cache_swa_v, sink_lat, l)
        m_nat = _lat_nat_call(z_nat, cache_nat_k, cache_nat_v, rpb_lanes, l)
        xs = _tail_call(xs, m_ml, m_swa, m_nat, mods, lat_row(TAIL_TILE), tail_w, l)

    k_swa, v_swa, k_nat, v_nat, st_c, st_n, st_m = ctx_state
    return (xp.reshape(BATCH, SEQ, D_MODEL), xs.reshape(DEC_BATCH, DEC_SEQ, D_MODEL),
            k_swa, v_swa, k_nat, v_nat,
            st_c.reshape(BATCH, DEPTH, 2, ML_HEADS, D_HEAD, D_HEAD),
            st_n.reshape(BATCH, DEPTH, 2, ML_HEADS, D_HEAD),
            st_m[..., 0].reshape(BATCH, DEPTH, 2, ML_HEADS))
```

```python
import functools

import numpy as np
import jax
import jax.numpy as jnp
from jax import lax
from jax.experimental import pallas as pl
from jax.experimental.pallas import tpu as pltpu

D_MODEL = 1024
BATCH = 16
SEQ = 256
DEPTH = 4
DEC_BATCH = 8
DEC_SEQ = 1024
PAST_LEN = 512
GRID_W = 64
D_HEAD = 64
ML_HEADS = 4
SWA_HEADS = 6
SWA_KV_HEADS = 2
SWA_GROUP = SWA_HEADS // SWA_KV_HEADS
NAT_HEADS = 6
ML_W = ML_HEADS * D_HEAD
SWA_W = SWA_HEADS * D_HEAD
SWA_KV_W = SWA_KV_HEADS * D_HEAD
NAT_W = NAT_HEADS * D_HEAD
N_GATES = 4 * ML_HEADS
IN_DIM = 4 * ML_W + N_GATES + SWA_W + 2 * SWA_KV_W + 3 * NAT_W
D_FF = 4 * D_MODEL
SWA_WINDOW = 128
SWA_BLOCK = 128
NAT_KH = 8
NAT_KW = 16
ROPE_BASE = 10000.0
LN_EPS = 1e-5
DN_ALPHA = (2 * DEPTH) ** 0.25
ATT_SCALE = D_HEAD ** -0.5
LOG2E = 1.4426950408889634
Q_SCALE2 = ATT_SCALE * LOG2E

LANES = 128
SUBLANES = 8
Z_ML_W = 4 * ML_W
Z_SWA_W = SWA_W + 2 * SWA_KV_W
Z_NAT_W = 3 * NAT_W
Z_G_W = LANES
Z_W = Z_ML_W + Z_SWA_W + Z_NAT_W + Z_G_W
MOD_ROWS = 16
CTX_MOD_ROW = DEC_BATCH
ROW_TILE = 512
TAIL_TILE = 512
ML_CHUNK = 256
VMEM_LIMIT = 56 * 1024 * 1024

BF16 = jnp.bfloat16
F32 = jnp.float32
NEG_INF = float("-inf")


def _dot(a, b):
    return jnp.dot(a.astype(BF16), b.astype(BF16), preferred_element_type=F32)


def _dot_nt(a, b):
    return lax.dot_general(a.astype(BF16), b.astype(BF16), (((1,), (1,)), ((), ())),
                           preferred_element_type=F32)


def _dot_split3(tri, x):
    hi = x.astype(BF16)
    r1 = x - hi.astype(F32)
    mid = r1.astype(BF16)
    lo = (r1 - mid.astype(F32)).astype(BF16)
    return (jnp.dot(tri, hi, preferred_element_type=F32)
            + jnp.dot(tri, mid, preferred_element_type=F32)
            + jnp.dot(tri, lo, preferred_element_type=F32))


def _sigmoid(x):
    return 1.0 / (1.0 + jnp.exp(-x))


def _log_sigmoid(x):
    return jnp.minimum(x, 0.0) - jnp.log1p(jnp.exp(-jnp.abs(x)))


def _layer_norm(x, g, b):
    mu = jnp.mean(x, axis=-1, keepdims=True)
    xc = x - mu
    var = jnp.mean(xc * xc, axis=-1, keepdims=True)
    return xc * lax.rsqrt(var + LN_EPS) * g + b


def _params(n_axes=1):
    return pltpu.CompilerParams(dimension_semantics=("arbitrary",) * n_axes,
                                vmem_limit_bytes=VMEM_LIMIT)


def _layer_spec(a, layer, single_buffer=False):
    kwargs = {"pipeline_mode": pl.Buffered(1)} if single_buffer else {}
    return pl.BlockSpec((None,) + a.shape[1:], lambda *_: (layer,) + (0,) * (a.ndim - 1), **kwargs)


def _mods_kernel(c_ref, w_ref, b_ref, o_ref):
    c = c_ref[...]
    o_ref[...] = _dot(c * _sigmoid(c), w_ref[...]) + b_ref[...]


def _mods_call(cvec, w_ada, b_ada):
    n_col = 6
    return pl.pallas_call(
        _mods_kernel,
        grid=(DEPTH, n_col),
        in_specs=[
            pl.BlockSpec((MOD_ROWS, D_MODEL), lambda l, j: (0, 0)),
            pl.BlockSpec((None, D_MODEL, D_MODEL), lambda l, j: (l, 0, j)),
            pl.BlockSpec((None, 1, D_MODEL), lambda l, j: (l, 0, j)),
        ],
        out_specs=pl.BlockSpec((None, MOD_ROWS, D_MODEL), lambda l, j: (l, 0, j)),
        out_shape=jax.ShapeDtypeStruct((DEPTH, MOD_ROWS, 6 * D_MODEL), F32),
        compiler_params=_params(2),
        name="mods",
    )(cvec, w_ada, b_ada.reshape(DEPTH, 1, 6 * D_MODEL))


def _inproj_kernel(x_ref, mod_ref, wt_ref, b_ref, zml_ref, zswa_ref, znat_ref, zg_ref, w_scr):
    @pl.when(pl.program_id(0) == 0)
    def _():
        rest = IN_DIM - Z_ML_W - N_GATES
        w_scr[0:Z_ML_W, :] = wt_ref[0:Z_ML_W, :].astype(BF16)
        w_scr[Z_ML_W:Z_ML_W + rest, :] = wt_ref[Z_ML_W + N_GATES:IN_DIM, :].astype(BF16)
        w_scr[Z_ML_W + rest:IN_DIM, :] = wt_ref[Z_ML_W:Z_ML_W + N_GATES, :].astype(BF16)
        w_scr[IN_DIM:Z_W, :] = jnp.zeros((Z_W - IN_DIM, D_MODEL), BF16)

    h = (x_ref[...] * (1.0 + mod_ref[1:2, :]) + mod_ref[0:1, :]).astype(BF16)
    o = 0
    for ref in (zml_ref, zswa_ref, znat_ref, zg_ref):
        w = ref.shape[-1]
        z = _dot_nt(h, w_scr[o:o + w, :]) * b_ref[0:1, o:o + w] + b_ref[1:2, o:o + w]
        ref[...] = z.astype(ref.dtype)
        o += w


def _mod_spec(layer, mod_row_of_tile):
    return pl.BlockSpec((None, 6, D_MODEL),
                        lambda i: (layer * MOD_ROWS + mod_row_of_tile(i), 0, 0))


def _inproj_call(x, mods, mod_row_of_tile, w_in_t, b_in, layer, dtypes):
    rows = x.shape[0]
    widths = (Z_ML_W, Z_SWA_W, Z_NAT_W, Z_G_W)
    return pl.pallas_call(
        _inproj_kernel,
        grid=(rows // ROW_TILE,),
        in_specs=[
            pl.BlockSpec((ROW_TILE, D_MODEL), lambda i: (i, 0)),
            _mod_spec(layer, mod_row_of_tile),
            _layer_spec(w_in_t, layer, single_buffer=True),
            _layer_spec(b_in, layer),
        ],
        out_specs=[pl.BlockSpec((ROW_TILE, w), lambda i: (i, 0)) for w in widths],
        out_shape=[jax.ShapeDtypeStruct((rows, w), dt) for w, dt in zip(widths, dtypes)],
        scratch_shapes=[pltpu.VMEM((Z_W, D_MODEL), BF16)],
        compiler_params=_params(1),
        name="inproj",
    )(x, mods, w_in_t, b_in)


def _tail_kernel(x_ref, mml_ref, mswa_ref, mnat_ref, mod_ref, wout_ref, ln1g_ref, ln1b_ref,
                 w1_ref, w2_ref, ln2g_ref, ln2b_ref, o_ref):
    g_a, sh_m, sc_m, g_m = (mod_ref[i:i + 1, :] for i in (2, 3, 4, 5))
    mix = jnp.concatenate([mml_ref[...], mswa_ref[...], mnat_ref[...]], axis=1)
    proj = jnp.dot(mix, wout_ref[...], preferred_element_type=F32)
    y = _layer_norm(DN_ALPHA * x_ref[...] + g_a * proj, ln1g_ref[...], ln1b_ref[...])
    h = (y * (1.0 + sc_m) + sh_m).astype(BF16)
    f = jnp.maximum(jnp.dot(h, w1_ref[...], preferred_element_type=F32), 0.0)
    f = jnp.dot((f * f).astype(BF16), w2_ref[...], preferred_element_type=F32)
    o_ref[...] = _layer_norm(DN_ALPHA * y + g_m * f, ln2g_ref[...], ln2b_ref[...])


def _tail_call(x, mix_ml, mix_swa, mix_nat, mods, mod_row_of_tile, consts, layer):
    rows = x.shape[0]
    row_spec = lambda w: pl.BlockSpec((TAIL_TILE, w), lambda i: (i, 0))
    return pl.pallas_call(
        _tail_kernel,
        grid=(rows // TAIL_TILE,),
        in_specs=[row_spec(D_MODEL), row_spec(ML_W), row_spec(SWA_W), row_spec(NAT_W),
                  _mod_spec(layer, mod_row_of_tile)]
                 + [_layer_spec(a, layer, single_buffer=True) for a in consts],
        out_specs=row_spec(D_MODEL),
        out_shape=jax.ShapeDtypeStruct((rows, D_MODEL), F32),
        compiler_params=_params(1),
        name="tail",
    )(x, mix_ml, mix_swa, mix_nat, mods, *consts)


def _mlstm_direction(q, k, vt, a_col, b_row, li_row, neg, last, c_st, n_st, m_st):
    a_m = a_col + neg
    c_row = jnp.maximum(m_st, jnp.max(a_m, axis=0, keepdims=True))
    decay = jnp.exp(a_m - c_row)
    wprev = jnp.exp(m_st - c_row)
    mt = b_row + c_row
    st = _dot_nt(k, q) * decay
    den = jnp.sum(st, axis=0, keepdims=True)
    num = _dot(vt, st)
    if c_st is not None:
        cn = _dot_nt(jnp.concatenate([c_st, jnp.broadcast_to(n_st, (SUBLANES, D_HEAD))], axis=0),
                     q)
        num = num + wprev * cn[0:D_HEAD, :]
        den = den + wprev * cn[D_HEAD:D_HEAD + 1, :]
    ht = num * (1.0 / jnp.maximum(jnp.abs(den), jnp.exp(-mt)))
    m_new = mt[:, last:last + 1]
    ws = jnp.exp(b_row[:, last:last + 1] - b_row + li_row - m_new)
    upd = _dot(jnp.concatenate([vt * ws, jnp.broadcast_to(ws, (SUBLANES, ws.shape[1]))], axis=0),
               k)
    c_new = upd[0:D_HEAD, :]
    n_new = upd[D_HEAD:D_HEAD + 1, :]
    if c_st is not None:
        wc = wprev[:, last:last + 1]
        c_new = c_new + wc * c_st
        n_new = n_new + wc * n_st
    return ht, c_new, n_new, m_new


def _mlstm_chunk(zml_ref, g_ref, fb_ref, rows, direction, state):
    n = ML_CHUNK
    g = g_ref[rows, :]
    ls = _log_sigmoid(g + fb_ref[...])
    r = lax.broadcasted_iota(jnp.int32, (n, n), 0)
    c = lax.broadcasted_iota(jnp.int32, (n, n), 1)
    if direction == 0:
        tri, neg, last = c <= r, jnp.where(r <= c, 0.0, NEG_INF), n - 1
    else:
        tri, neg, last = c >= r, jnp.where(r >= c, 0.0, NEG_INF), 0
    b = _dot_split3(tri.astype(BF16), ls)
    a = pltpu.roll(g, ML_HEADS, 1) - b
    bt = b.T
    gt = g.T
    i_col = 2 * ML_HEADS * direction
    f_col = i_col + ML_HEADS
    hts, new_state = [], []
    for pair in range(ML_HEADS // 2):
        vt2 = zml_ref[rows, 2 * ML_W + pair * LANES:2 * ML_W + (pair + 1) * LANES].T
        for sub in range(2):
            hd = 2 * pair + sub
            feat = slice(sub * D_HEAD, (sub + 1) * D_HEAD)
            q = zml_ref[rows, hd * D_HEAD:(hd + 1) * D_HEAD].astype(BF16)
            k = zml_ref[rows, ML_W + hd * D_HEAD:ML_W + (hd + 1) * D_HEAD] * ATT_SCALE
            ht, c_new, n_new, m_new = _mlstm_direction(
                q, k, vt2[feat, :],
                a[:, f_col + hd:f_col + hd + 1], bt[f_col + hd:f_col + hd + 1, :],
                gt[i_col + hd:i_col + hd + 1, :], neg, last, *state[hd])
            hts.append(ht)
            new_state.append((c_new, n_new, m_new))
    return jnp.concatenate(hts, axis=0), new_state


def _mlstm_finish(ht, o_gate, norm_g):
    outs = []
    for hd in range(ML_HEADS):
        x = ht[hd * D_HEAD:(hd + 1) * D_HEAD, :]
        mu = jnp.mean(x, axis=0, keepdims=True)
        xc = x - mu
        var = jnp.mean(xc * xc, axis=0, keepdims=True)
        outs.append(xc * lax.rsqrt(var + LN_EPS))
    hn = jnp.concatenate(outs, axis=0).T
    return hn * norm_g * _sigmoid(o_gate)


def _store_state(c_ref, n_ref, m_ref, idx, c_new, n_new, m_new):
    c_ref[idx] = c_new
    n_ref[idx:idx + 1, :] = n_new
    m_ref[idx:idx + 1, :] = jnp.broadcast_to(m_new, (1, LANES))


ONES_ROWS = 16


def _with_ones(vt):
    ones = jnp.ones((ONES_ROWS, vt.shape[1]), BF16)
    return jnp.concatenate([vt.astype(BF16), ones], axis=0)


def _attend_t(q, parts, sink_row=None):
    sts = []
    for k, _, bias_t in parts:
        st = _dot_nt(k, q)
        sts.append(st if bias_t is None else st + bias_t)
    m = None
    for st in sts:
        mx = jnp.max(st, axis=0, keepdims=True)
        m = mx if m is None else jnp.maximum(m, mx)
    if sink_row is not None:
        m = jnp.maximum(m, sink_row)
    ot = None
    for st, (_, vt_ones, _) in zip(sts, parts):
        part = jnp.dot(vt_ones, jnp.exp2(st - m).astype(BF16), preferred_element_type=F32)
        ot = part if ot is None else ot + part
    den = ot[D_HEAD:D_HEAD + 1, :]
    if sink_row is not None:
        den = den + jnp.exp2(sink_row - m)
    return ot[0:D_HEAD, :] * (1.0 / den)


def _head_rows(pairs_t, hd):
    return pairs_t[hd // 2][(hd % 2) * D_HEAD:(hd % 2 + 1) * D_HEAD, :]


def _store_heads_t(o_ref, ots):
    for p in range(len(ots) // 2):
        o_ref[:, p * LANES:(p + 1) * LANES] = jnp.concatenate(
            [ots[2 * p], ots[2 * p + 1]], axis=0).astype(o_ref.dtype).T


N_STATE = 2 * ML_HEADS
N_CTX_STATE_OUTS = 7


def _ctx_mixers_kernel(zml_ref, zswa_ref, znat_ref, zg_ref, fb_ref, ng_ref, sink_ref, *rest,
                       first_layer):
    mml_ref, mswa_ref, mnat_ref, *state_refs = rest[-(3 + N_CTX_STATE_OUTS):]
    if first_layer:
        for ref in state_refs:
            ref[1:] = jnp.zeros((DEPTH - 1,) + ref.shape[1:], F32)
        state_refs = [ref.at[0] for ref in state_refs]
    sk_ref, sv_ref, nk_ref, nv_ref, c_ref, n_ref, m_ref = state_refs
    rows = slice(0, SEQ)
    zero_state = [(None, jnp.zeros((1, D_HEAD), F32), jnp.zeros((1, 1), F32))] * ML_HEADS
    ht_sum = None
    for direction in range(2):
        ht, new_state = _mlstm_chunk(zml_ref, zg_ref, fb_ref, rows, direction, zero_state)
        ht_sum = ht if ht_sum is None else ht_sum + ht
        for hd in range(ML_HEADS):
            _store_state(c_ref, n_ref, m_ref, direction * ML_HEADS + hd, *new_state[hd])
    mml_ref[...] = _mlstm_finish(ht_sum, zml_ref[:, 3 * ML_W:4 * ML_W], ng_ref[...]).astype(BF16)

    scaled_q = lambda ref, hd: ref[:, hd * D_HEAD:(hd + 1) * D_HEAD].astype(BF16)
    vts = [zswa_ref[:, SWA_W + SWA_KV_W:SWA_W + 2 * SWA_KV_W].astype(BF16).T]
    ots = []
    for kv in range(SWA_KV_HEADS):
        k = zswa_ref[:, SWA_W + kv * D_HEAD:SWA_W + (kv + 1) * D_HEAD]
        sk_ref[kv] = k
        sv_ref[kv] = zswa_ref[:, SWA_W + SWA_KV_W + kv * D_HEAD:SWA_W + SWA_KV_W + (kv + 1) * D_HEAD]
        q = jnp.concatenate([scaled_q(zswa_ref, kv * SWA_GROUP + g) for g in range(SWA_GROUP)],
                            axis=0)
        ot = _attend_t(q, [(k.astype(BF16), _with_ones(_head_rows(vts, kv)), None)],
                       sink_ref[kv:kv + 1, :])
        ots += [ot[:, g * SEQ:(g + 1) * SEQ] for g in range(SWA_GROUP)]
    _store_heads_t(mswa_ref, ots)

    vts = [znat_ref[:, 2 * NAT_W + p * LANES:2 * NAT_W + (p + 1) * LANES].astype(BF16).T
           for p in range(NAT_W // LANES)]
    ots = []
    for hd in range(NAT_HEADS):
        k = znat_ref[:, NAT_W + hd * D_HEAD:NAT_W + (hd + 1) * D_HEAD]
        nk_ref[hd] = k
        nv_ref[hd] = znat_ref[:, 2 * NAT_W + hd * D_HEAD:2 * NAT_W + (hd + 1) * D_HEAD]
        ots.append(_attend_t(scaled_q(znat_ref, hd),
                             [(k.astype(BF16), _with_ones(_head_rows(vts, hd)), None)]))
    _store_heads_t(mnat_ref, ots)


CTX_STATE_DIMS = ((SWA_KV_HEADS, SEQ, D_HEAD), (SWA_KV_HEADS, SEQ, D_HEAD),
                  (NAT_HEADS, SEQ, D_HEAD), (NAT_HEADS, SEQ, D_HEAD),
                  (N_STATE, D_HEAD, D_HEAD), (N_STATE, D_HEAD), (N_STATE, LANES))


def _ctx_mixers_call(z_ml, z_swa, z_nat, z_g, fb_row, norm_g, sink_rows, layer, prev_state):
    rows = z_ml.shape[0]
    row_spec = lambda w: pl.BlockSpec((SEQ, w), lambda b: (b, 0))
    state_dims = CTX_STATE_DIMS
    first_layer = prev_state is None
    if first_layer:
        assert layer == 0
        state_spec = lambda d: pl.BlockSpec((None, DEPTH) + d, lambda b: (b,) + (0,) * (len(d) + 1))
    else:
        state_spec = lambda d: pl.BlockSpec((None, None) + d, lambda b: (b, layer) + (0,) * len(d))
    prev = () if first_layer else tuple(prev_state)
    n_in = 7
    return pl.pallas_call(
        functools.partial(_ctx_mixers_kernel, first_layer=first_layer),
        grid=(BATCH,),
        in_specs=[row_spec(Z_ML_W), row_spec(Z_SWA_W), row_spec(Z_NAT_W), row_spec(Z_G_W),
                  _layer_spec(fb_row, layer), _layer_spec(norm_g, layer),
                  _layer_spec(sink_rows, layer)]
                 + [pl.BlockSpec(memory_space=pl.ANY)] * len(prev),
        out_specs=[row_spec(ML_W), row_spec(SWA_W), row_spec(NAT_W)]
                  + [state_spec(d) for d in state_dims],
        out_shape=[jax.ShapeDtypeStruct((rows, ML_W), BF16),
                   jax.ShapeDtypeStruct((rows, SWA_W), BF16),
                   jax.ShapeDtypeStruct((rows, NAT_W), BF16)]
                  + [jax.ShapeDtypeStruct((BATCH, DEPTH) + d, F32) for d in state_dims],
        input_output_aliases={n_in + i: 3 + i for i in range(len(prev))},
        compiler_params=_params(1),
        name="ctx_mixers",
    )(z_ml, z_swa, z_nat, z_g, fb_row, norm_g, sink_rows, *prev)


def _lat_mlstm_kernel(zml_ref, zg_ref, fb_ref, ng_ref, c0_ref, n0_ref, m0_ref, mml_ref,
                      ht_scr, c_scr, n_scr, m_scr):
    n_chunks = DEC_SEQ // ML_CHUNK
    c_scr[...] = c0_ref[...]
    n_scr[...] = n0_ref[...]
    m_scr[...] = jnp.broadcast_to(m0_ref[...], m_scr.shape)

    def body(i, carry):
        for direction in range(2):
            chunk = i if direction == 0 else n_chunks - 1 - i
            rows = pl.ds(pl.multiple_of(chunk * ML_CHUNK, ML_CHUNK), ML_CHUNK)
            base = direction * ML_HEADS
            state = [(c_scr[base + hd], n_scr[base + hd:base + hd + 1, :],
                      m_scr[base + hd:base + hd + 1, 0:1]) for hd in range(ML_HEADS)]
            ht, new_state = _mlstm_chunk(zml_ref, zg_ref, fb_ref, rows, direction, state)
            ht_scr[direction, chunk] = ht
            for hd in range(ML_HEADS):
                _store_state(c_scr, n_scr, m_scr, base + hd, *new_state[hd])
        return carry

    lax.fori_loop(0, n_chunks, body, 0)
    for chunk in range(n_chunks):
        rows = slice(chunk * ML_CHUNK, (chunk + 1) * ML_CHUNK)
        mml_ref[rows, :] = _mlstm_finish(ht_scr[0, chunk] + ht_scr[1, chunk],
                                         zml_ref[rows, 3 * ML_W:4 * ML_W],
                                         ng_ref[...]).astype(BF16)


def _lat_mlstm_call(z_ml, z_g, fb_row, norm_g, state_c, state_n, state_m, layer):
    n_st = 2 * ML_HEADS
    n_chunks = DEC_SEQ // ML_CHUNK
    row_spec = lambda w: pl.BlockSpec((DEC_SEQ, w), lambda b: (b, 0))
    return pl.pallas_call(
        _lat_mlstm_kernel,
        grid=(DEC_BATCH,),
        in_specs=[row_spec(Z_ML_W), row_spec(Z_G_W), _layer_spec(fb_row, layer),
                  _layer_spec(norm_g, layer),
                  pl.BlockSpec((None, None, n_st, D_HEAD, D_HEAD), lambda b: (b, layer, 0, 0, 0)),
                  pl.BlockSpec((None, None, n_st, D_HEAD), lambda b: (b, layer, 0, 0)),
                  pl.BlockSpec((None, None, n_st, 1), lambda b: (b, layer, 0, 0))],
        out_specs=row_spec(ML_W),
        out_shape=jax.ShapeDtypeStruct((DEC_BATCH * DEC_SEQ, ML_W), BF16),
        scratch_shapes=[pltpu.VMEM((2, n_chunks, ML_W, ML_CHUNK), F32),
                        pltpu.VMEM((n_st, D_HEAD, D_HEAD), F32), pltpu.VMEM((n_st, D_HEAD), F32),
                        pltpu.VMEM((n_st, LANES), F32)],
        compiler_params=_params(1),
        name="lat_mlstm",
    )(z_ml, z_g, fb_row, norm_g, state_c, state_n, state_m)


def _rope(x, cos, sin_signed):
    lane = lax.broadcasted_iota(jnp.int32, x.shape, 1)
    first = (lane & 31) < 16
    partner = jnp.where(first, pltpu.roll(x, LANES - 16, 1), pltpu.roll(x, 16, 1))
    return x * cos + partner * sin_signed


def _lat_swa_kernel(zswa_ref, cos_ref, sin_ref, ckt_ref, cvt_ref, sink_ref, mswa_ref,
                    q_scr, k_scr, vt_scr, ot_scr, bias_scr, ck_scr, cvt_scr, st_scr, m_scr):
    cos = cos_ref[...]
    sin = sin_ref[...]
    for j in range(SWA_W // LANES):
        q_scr[:, j * LANES:(j + 1) * LANES] = _rope(
            zswa_ref[:, j * LANES:(j + 1) * LANES], cos, sin).astype(BF16)
    k_scr[...] = _rope(zswa_ref[:, SWA_W:SWA_W + SWA_KV_W], cos, sin).astype(BF16)
    vt_scr[...] = zswa_ref[:, SWA_W + SWA_KV_W:SWA_W + 2 * SWA_KV_W].astype(BF16).T

    bl = SWA_BLOCK
    nb = DEC_SEQ // bl
    n_q = SWA_GROUP * bl
    kj = lax.broadcasted_iota(jnp.int32, (3 * bl, n_q), 0)
    qi = lax.broadcasted_iota(jnp.int32, (3 * bl, n_q), 1) & (bl - 1)
    keep = ((kj >= qi) & (kj < 2 * bl)) | ((kj >= 2 * bl) & (kj - 2 * bl <= qi))
    bias_scr[...] = jnp.where(keep, 0.0, NEG_INF)
    ck_scr[...] = jnp.concatenate([ckt_ref[kv] for kv in range(SWA_KV_HEADS)],
                                  axis=0).astype(BF16).T
    for kv in range(SWA_KV_HEADS):
        cvt_scr[kv] = _with_ones(cvt_ref[kv])

    def band(blk):
        lo_b, hi_b = max(blk - 1, 0), min(blk + 2, nb)
        return slice(lo_b * bl, hi_b * bl), slice((lo_b - blk + 1) * bl, (hi_b - blk + 1) * bl)

    def score_stage(kv, blk, slot):
        head = slice(kv * D_HEAD, (kv + 1) * D_HEAD)
        keys, bias_rows = band(blk)
        n_band = keys.stop - keys.start
        q = jnp.concatenate(
            [q_scr[blk * bl:(blk + 1) * bl,
                   (kv * SWA_GROUP + g) * D_HEAD:(kv * SWA_GROUP + g + 1) * D_HEAD]
             for g in range(SWA_GROUP)], axis=0)
        s_ctx = _dot_nt(ck_scr[:, head], q)
        s_band = _dot_nt(k_scr[keys, head], q) + bias_scr[bias_rows, :]
        st_scr[slot, 0:PAST_LEN, :] = s_ctx
        st_scr[slot, PAST_LEN:PAST_LEN + n_band, :] = s_band
        m = jnp.maximum(jnp.max(s_ctx, axis=0, keepdims=True),
                        jnp.max(s_band, axis=0, keepdims=True))
        m_scr[slot] = jnp.broadcast_to(jnp.maximum(m, sink_ref[kv:kv + 1, :]), m_scr.shape[1:])

    def value_stage(kv, blk, slot):
        head = slice(kv * D_HEAD, (kv + 1) * D_HEAD)
        keys, _ = band(blk)
        n_band = keys.stop - keys.start
        m = m_scr[slot, 0:1, :]
        p_ctx = jnp.exp2(st_scr[slot, 0:PAST_LEN, :] - m).astype(BF16)
        p_band = jnp.exp2(st_scr[slot, PAST_LEN:PAST_LEN + n_band, :] - m).astype(BF16)
        ot = (jnp.dot(cvt_scr[kv], p_ctx, preferred_element_type=F32)
              + jnp.dot(_with_ones(vt_scr[head, keys]), p_band, preferred_element_type=F32))
        den = ot[D_HEAD:D_HEAD + 1, :] + jnp.exp2(sink_ref[kv:kv + 1, :] - m)
        o = (ot[0:D_HEAD, :] * (1.0 / den)).astype(BF16)
        for g in range(SWA_GROUP):
            hq = kv * SWA_GROUP + g
            ot_scr[hq * D_HEAD:(hq + 1) * D_HEAD, blk * bl:(blk + 1) * bl] = o[:, g * bl:(g + 1) * bl]

    problems = [(kv, blk) for kv in range(SWA_KV_HEADS) for blk in range(nb)]
    for i in range(len(problems) + 1):
        def stage(i=i):
            if i > 0:
                value_stage(*problems[i - 1], (i - 1) % 2)
            if i < len(problems):
                score_stage(*problems[i], i % 2)
        stage()
    for p in range(SWA_W // LANES):
        mswa_ref[:, p * LANES:(p + 1) * LANES] = ot_scr[p * LANES:(p + 1) * LANES, :].T


def _lat_swa_call(z_swa, cos_t, sin_t, cache_k, cache_v, sink_rows, layer):
    row_spec = lambda w: pl.BlockSpec((DEC_SEQ, w), lambda b: (b, 0))
    full = lambda a: pl.BlockSpec(a.shape, lambda b: (0,) * a.ndim)
    cache_spec = pl.BlockSpec((None, None, SWA_KV_HEADS, D_HEAD, PAST_LEN),
                              lambda b: (b, layer, 0, 0, 0))
    n_q = SWA_GROUP * SWA_BLOCK
    return pl.pallas_call(
        _lat_swa_kernel,
        grid=(DEC_BATCH,),
        in_specs=[row_spec(Z_SWA_W), full(cos_t), full(sin_t), cache_spec, cache_spec,
                  _layer_spec(sink_rows, layer)],
        out_specs=row_spec(SWA_W),
        out_shape=jax.ShapeDtypeStruct((DEC_BATCH * DEC_SEQ, SWA_W), BF16),
        scratch_shapes=[pltpu.VMEM((DEC_SEQ, SWA_W), BF16), pltpu.VMEM((DEC_SEQ, SWA_KV_W), BF16),
                        pltpu.VMEM((SWA_KV_W, DEC_SEQ), BF16), pltpu.VMEM((SWA_W, DEC_SEQ), BF16),
                        pltpu.VMEM((3 * SWA_BLOCK, n_q), F32),
                        pltpu.VMEM((PAST_LEN, SWA_KV_W), BF16),
                        pltpu.VMEM((SWA_KV_HEADS, D_HEAD + ONES_ROWS, PAST_LEN), BF16),
                        pltpu.VMEM((2, PAST_LEN + 3 * SWA_BLOCK, n_q), F32),
                        pltpu.VMEM((2, SUBLANES, n_q), F32)],
        compiler_params=_params(1),
        name="lat_swa",
    )(z_swa, cos_t, sin_t, cache_k, cache_v, sink_rows)


NAT_ROWS = DEC_SEQ // GRID_W
NAT_GROUP_ROWS = 4
NAT_GROUP_Q = NAT_GROUP_ROWS * GRID_W
NAT_RPB_ROWS = 2 * NAT_KH


def _nat_row_start(r):
    return min(max(r - NAT_KH // 2, 0), NAT_ROWS - NAT_KH)


def _nat_groups():
    groups, off = [], 0
    for g in range(NAT_ROWS // NAT_GROUP_ROWS):
        w0 = _nat_row_start(g * NAT_GROUP_ROWS)
        w1 = _nat_row_start((g + 1) * NAT_GROUP_ROWS - 1) + NAT_KH
        n_rows = w1 - w0 + (w1 - w0) % 2
        w0 = min(w0, NAT_ROWS - n_rows)
        groups.append((w0, n_rows, off))
        off += n_rows * GRID_W
    return groups, off


NAT_GROUPS, NAT_BIAS_KEYS = _nat_groups()


def _nat_build_bias(rpb_ref, bias_scr):
    shape = (GRID_W, LANES)
    q = lax.broadcasted_iota(jnp.int32, shape, 0)
    lane = lax.broadcasted_iota(jnp.int32, shape, 1)
    kc = lane & (GRID_W - 1)
    cs = jnp.clip(q - NAT_KW // 2, 0, GRID_W - NAT_KW)
    ok = (kc >= cs) & (kc < cs + NAT_KW)
    ok_lo = ok & (lane < GRID_W)
    ok_hi = ok & (lane >= GRID_W)
    neg_tile = jnp.full(shape, NEG_INF, F32)
    for hd in range(2):
        lo, hi = [], []
        for ro in range(2 * NAT_KH - 1):
            x = jnp.broadcast_to(rpb_ref[hd, ro:ro + 1, :], shape)
            lo.append(jnp.where(ok_lo, pltpu.roll(x, 0, 1, stride=1, stride_axis=0), NEG_INF))
            hi.append(jnp.where(ok_hi, pltpu.roll(x, GRID_W, 1, stride=1, stride_axis=0), NEG_INF))
        for g, (w0, n_rows, off) in enumerate(NAT_GROUPS):
            for jj in range(n_rows // 2):
                kra, krb = w0 + 2 * jj, w0 + 2 * jj + 1
                tiles = []
                for rr in range(NAT_GROUP_ROWS):
                    r = g * NAT_GROUP_ROWS + rr
                    r0 = _nat_row_start(r)
                    parts = []
                    if r0 <= kra < r0 + NAT_KH:
                        parts.append(lo[kra - r + NAT_KH - 1])
                    if r0 <= krb < r0 + NAT_KH:
                        parts.append(hi[krb - r + NAT_KH - 1])
                    tiles.append(neg_tile if not parts else (
                        parts[0] if len(parts) == 1 else jnp.maximum(parts[0], parts[1])))
                bias_scr[hd, off + jj * LANES:off + (jj + 1) * LANES, :] = (
                    jnp.concatenate(tiles, axis=0).T)


def _lat_nat_kernel(q_ref, k_ref, v_ref, ckt_ref, cvt_ref, rpb_ref, mnat_ref, bias_scr):
    @pl.when(pl.program_id(1) == 0)
    def _():
        _nat_build_bias(rpb_ref, bias_scr)

    vt = v_ref[...].astype(BF16).T
    ck2 = jnp.concatenate([ckt_ref[0], ckt_ref[1]], axis=0).astype(BF16).T
    ots = []
    for hd in range(2):
        head = slice(hd * D_HEAD, (hd + 1) * D_HEAD)
        ck = ck2[:, head]
        cvt_ones = _with_ones(cvt_ref[hd])
        ot_groups = []
        for g, (w0, n_rows, off) in enumerate(NAT_GROUPS):
            keys = slice(w0 * GRID_W, (w0 + n_rows) * GRID_W)
            parts = [(k_ref[keys, head].astype(BF16), _with_ones(vt[head, keys]),
                      bias_scr[hd, off:off + n_rows * GRID_W, :]),
                     (ck, cvt_ones, None)]
            q = q_ref[g * NAT_GROUP_Q:(g + 1) * NAT_GROUP_Q, head].astype(BF16)
            ot_groups.append(_attend_t(q, parts))
        ots.append(jnp.concatenate(ot_groups, axis=1))
    mnat_ref[...] = jnp.concatenate(ots, axis=0).astype(BF16).T


def _lat_nat_call(z_nat, cache_k, cache_v, rpb_lanes, layer):
    n_pair = NAT_HEADS // 2
    col_spec = lambda off: pl.BlockSpec((DEC_SEQ, LANES), lambda p, b: (b, off + p))
    cache_spec = pl.BlockSpec((None, None, 2, D_HEAD, PAST_LEN), lambda p, b: (b, layer, p, 0, 0))
    return pl.pallas_call(
        _lat_nat_kernel,
        grid=(n_pair, DEC_BATCH),
        in_specs=[col_spec(0), col_spec(n_pair), col_spec(2 * n_pair), cache_spec, cache_spec,
                  pl.BlockSpec((None, 2, NAT_RPB_ROWS, LANES), lambda p, b: (layer, p, 0, 0))],
        out_specs=pl.BlockSpec((DEC_SEQ, LANES), lambda p, b: (b, p)),
        out_shape=jax.ShapeDtypeStruct((DEC_BATCH * DEC_SEQ, NAT_W), BF16),
        scratch_shapes=[pltpu.VMEM((2, NAT_BIAS_KEYS, NAT_GROUP_Q), F32)],
        compiler_params=_params(2),
        name="lat_nat",
    )(z_nat, z_nat, z_nat, cache_k, cache_v, rpb_lanes)


def _nat_rpb_lanes(rpb):
    n_off = 2 * NAT_KW - 1
    padded = jnp.concatenate(
        [rpb.astype(F32), jnp.zeros(rpb.shape[:-1] + (LANES - n_off,), F32)], axis=-1)
    rolled = jnp.roll(padded, -(NAT_KW - 1), axis=-1)
    pad_rows = jnp.zeros(rpb.shape[:-2] + (NAT_RPB_ROWS - rpb.shape[-2], LANES), F32)
    return jnp.concatenate([rolled, pad_rows], axis=-2)


def _rope_tables():
    t = np.arange(DEC_SEQ)[:, None]
    d = np.arange(LANES)[None, :] % D_HEAD
    pos = np.where(d < D_HEAD // 2, t // GRID_W, t % GRID_W).astype(np.float64)
    quarter = D_HEAD // 4
    freq = ROPE_BASE ** (-(d % quarter).astype(np.float64) / quarter)
    ang = (pos.astype(np.float32) * freq.astype(np.float32)).astype(np.float32)
    sign = np.where((d % (2 * quarter)) < quarter, -1.0, 1.0)
    return (jnp.asarray(np.cos(ang), dtype=F32), jnp.asarray(np.sin(ang) * sign, dtype=F32))


def _permute_in_columns(a):
    g0 = Z_ML_W
    s0 = g0 + N_GATES
    pad = jnp.zeros(a.shape[:-1] + (Z_G_W - N_GATES,), a.dtype)
    return jnp.concatenate([a[..., :g0], a[..., s0:], a[..., g0:s0], pad], axis=-1)


def kernel(x_prompt, x_sample, cache_swa_k, cache_swa_v, cache_nat_k, cache_nat_v, state_mlstm_C,
           state_mlstm_n, state_mlstm_m, c, c_ctx, w_ada, b_ada, w_in, b_in, mlstm_fbias,
           mlstm_norm_g, swa_sink, nat_rpb, w_out, ln1_g, ln1_b, w_mlp1, w_mlp2, ln2_g, ln2_b):
    xp = x_prompt.reshape(BATCH * SEQ, D_MODEL)
    xs = x_sample.reshape(DEC_BATCH * DEC_SEQ, D_MODEL)
    cvec = jnp.concatenate(
        [c, c_ctx[None, :], jnp.zeros((MOD_ROWS - DEC_BATCH - 1, D_MODEL), F32)], axis=0)
    mods = _mods_call(cvec, w_ada, b_ada).reshape(DEPTH * MOD_ROWS, 6, D_MODEL)

    w_in_t = jnp.swapaxes(w_in, 1, 2)
    out_scale = np.ones((Z_W,), np.float32)
    out_scale[Z_ML_W:Z_ML_W + SWA_W] = Q_SCALE2
    out_scale[Z_ML_W + Z_SWA_W:Z_ML_W + Z_SWA_W + NAT_W] = Q_SCALE2
    b_in_p = jnp.stack([jnp.broadcast_to(out_scale, (DEPTH, Z_W)),
                        _permute_in_columns(b_in) * out_scale], axis=1)
    cache_swa_k, cache_swa_v, cache_nat_k, cache_nat_v = (
        jnp.swapaxes(a, -1, -2) for a in (cache_swa_k, cache_swa_v, cache_nat_k, cache_nat_v))
    vec = lambda a: a.reshape(DEPTH, 1, D_MODEL)
    tail_w = (w_out.astype(BF16), vec(ln1_g), vec(ln1_b), w_mlp1.astype(BF16),
              w_mlp2.astype(BF16), vec(ln2_g), vec(ln2_b))
    cos_t, sin_t = _rope_tables()
    rpb_lanes = _nat_rpb_lanes(nat_rpb) * LOG2E
    swa_sink = swa_sink * LOG2E
    fb_rows = jnp.zeros((DEPTH, 1, Z_G_W), F32)
    fb_rows = fb_rows.at[:, 0, ML_HEADS:2 * ML_HEADS].set(mlstm_fbias[:, 0])
    fb_rows = fb_rows.at[:, 0, 3 * ML_HEADS:4 * ML_HEADS].set(mlstm_fbias[:, 1])
    norm_g = mlstm_norm_g.reshape(DEPTH, 1, ML_W)
    sink_ctx = jnp.repeat(swa_sink, SEQ, axis=1).reshape(DEPTH, SWA_KV_HEADS, SWA_GROUP * SEQ)
    sink_lat = jnp.repeat(swa_sink, SWA_BLOCK, axis=1).reshape(
        DEPTH, SWA_KV_HEADS, SWA_GROUP * SWA_BLOCK)

    n_st = 2 * ML_HEADS
    state_c = state_mlstm_C.reshape(DEC_BATCH, DEPTH, n_st, D_HEAD, D_HEAD)
    state_n = state_mlstm_n.reshape(DEC_BATCH, DEPTH, n_st, D_HEAD)
    state_m = state_mlstm_m.reshape(DEC_BATCH, DEPTH, n_st, 1)

    ctx_row = lambda tile: (lambda i: CTX_MOD_ROW)
    lat_row = lambda tile: (lambda i: i // (DEC_SEQ // tile))

    ctx_state = None
    for l in range(DEPTH):
        z_ml, z_swa, z_nat, z_g = _inproj_call(xp, mods, ctx_row(ROW_TILE), w_in_t, b_in_p, l,
                                               (F32, F32, F32, F32))
        m_ml, m_swa, m_nat, *ctx_state = _ctx_mixers_call(
            z_ml, z_swa, z_nat, z_g, fb_rows, norm_g, sink_ctx, l, ctx_state)
        xp = _tail_call(xp, m_ml, m_swa, m_nat, mods, ctx_row(TAIL_TILE), tail_w, l)

        z_ml, z_swa, z_nat, z_g = _inproj_call(xs, mods, lat_row(ROW_TILE), w_in_t, b_in_p, l,
                                               (F32, F32, BF16, F32))
        m_ml = _lat_mlstm_call(z_ml, z_g, fb_rows, norm_g, state_c, state_n, state_m, l)
        m_swa = _lat_swa_call(z_swa, cos_t, sin_t, cache_swa_k, cache_swa_v, sink_lat, l)
        m_nat = _lat_nat_call(z_nat, cache_nat_k, cache_nat_v, rpb_lanes, l)
        xs = _tail_call(xs, m_ml, m_swa, m_nat, mods, lat_row(TAIL_TILE), tail_w, l)

    k_swa, v_swa, k_nat, v_nat, st_c, st_n, st_m = ctx_state
    return (xp.reshape(BATCH, SEQ, D_MODEL), xs.reshape(DEC_BATCH, DEC_SEQ, D_MODEL),
            k_swa, v_swa, k_nat, v_nat,
            st_c.reshape(BATCH, DEPTH, 2, ML_HEADS, D_HEAD, D_HEAD),
            st_n.reshape(BATCH, DEPTH, 2, ML_HEADS, D_HEAD),
            st_m[..., 0].reshape(BATCH, DEPTH, 2, ML_HEADS))
```

```python
import functools

import numpy as np
import jax
import jax.numpy as jnp
from jax import lax
from jax.experimental import pallas as pl
from jax.experimental.pallas import tpu as pltpu

D_MODEL = 1024
BATCH = 16
SEQ = 256
DEPTH = 4
DEC_BATCH = 8
DEC_SEQ = 1024
PAST_LEN = 512
GRID_W = 64
D_HEAD = 64
ML_HEADS = 4
SWA_HEADS = 6
SWA_KV_HEADS = 2
SWA_GROUP = SWA_HEADS // SWA_KV_HEADS
NAT_HEADS = 6
ML_W = ML_HEADS * D_HEAD
SWA_W = SWA_HEADS * D_HEAD
SWA_KV_W = SWA_KV_HEADS * D_HEAD
NAT_W = NAT_HEADS * D_HEAD
N_GATES = 4 * ML_HEADS
IN_DIM = 4 * ML_W + N_GATES + SWA_W + 2 * SWA_KV_W + 3 * NAT_W
D_FF = 4 * D_MODEL
SWA_WINDOW = 128
SWA_BLOCK = 128
NAT_KH = 8
NAT_KW = 16
ROPE_BASE = 10000.0
LN_EPS = 1e-5
DN_ALPHA = (2 * DEPTH) ** 0.25
ATT_SCALE = D_HEAD ** -0.5
LOG2E = 1.4426950408889634
Q_SCALE2 = ATT_SCALE * LOG2E

LANES = 128
SUBLANES = 8
Z_ML_W = 4 * ML_W
Z_SWA_W = SWA_W + 2 * SWA_KV_W
Z_NAT_W = 3 * NAT_W
Z_G_W = LANES
Z_W = Z_ML_W + Z_SWA_W + Z_NAT_W + Z_G_W
MOD_ROWS = 16
CTX_MOD_ROW = DEC_BATCH
ROW_TILE = 512
TAIL_TILE = 512
ML_CHUNK = 256
VMEM_LIMIT = 56 * 1024 * 1024

BF16 = jnp.bfloat16
F32 = jnp.float32
NEG_INF = float("-inf")


def _dot(a, b):
    return jnp.dot(a.astype(BF16), b.astype(BF16), preferred_element_type=F32)


def _dot_nt(a, b):
    return lax.dot_general(a.astype(BF16), b.astype(BF16), (((1,), (1,)), ((), ())),
                           preferred_element_type=F32)


def _dot_split3(tri, x):
    hi = x.astype(BF16)
    r1 = x - hi.astype(F32)
    mid = r1.astype(BF16)
    lo = (r1 - mid.astype(F32)).astype(BF16)
    return (jnp.dot(tri, hi, preferred_element_type=F32)
            + jnp.dot(tri, mid, preferred_element_type=F32)
            + jnp.dot(tri, lo, preferred_element_type=F32))


def _sigmoid(x):
    return 1.0 / (1.0 + jnp.exp(-x))


def _log_sigmoid(x):
    return jnp.minimum(x, 0.0) - jnp.log1p(jnp.exp(-jnp.abs(x)))


def _layer_norm(x, g, b):
    mu = jnp.mean(x, axis=-1, keepdims=True)
    xc = x - mu
    var = jnp.mean(xc * xc, axis=-1, keepdims=True)
    return xc * lax.rsqrt(var + LN_EPS) * g + b


def _params(n_axes=1):
    return pltpu.CompilerParams(dimension_semantics=("arbitrary",) * n_axes,
                                vmem_limit_bytes=VMEM_LIMIT)


def _layer_spec(a, layer, single_buffer=False):
    kwargs = {"pipeline_mode": pl.Buffered(1)} if single_buffer else {}
    return pl.BlockSpec((None,) + a.shape[1:], lambda *_: (layer,) + (0,) * (a.ndim - 1), **kwargs)


def _mods_kernel(c_ref, w_ref, b_ref, o_ref):
    c = c_ref[...]
    o_ref[...] = _dot(c * _sigmoid(c), w_ref[...]) + b_ref[...]


def _mods_call(cvec, w_ada, b_ada):
    n_col = 6
    return pl.pallas_call(
        _mods_kernel,
        grid=(DEPTH, n_col),
        in_specs=[
            pl.BlockSpec((MOD_ROWS, D_MODEL), lambda l, j: (0, 0)),
            pl.BlockSpec((None, D_MODEL, D_MODEL), lambda l, j: (l, 0, j)),
            pl.BlockSpec((None, 1, D_MODEL), lambda l, j: (l, 0, j)),
        ],
        out_specs=pl.BlockSpec((None, MOD_ROWS, D_MODEL), lambda l, j: (l, 0, j)),
        out_shape=jax.ShapeDtypeStruct((DEPTH, MOD_ROWS, 6 * D_MODEL), F32),
        compiler_params=_params(2),
        name="mods",
    )(cvec, w_ada, b_ada.reshape(DEPTH, 1, 6 * D_MODEL))


def _inproj_kernel(x_ref, mod_ref, wt_ref, b_ref, zml_ref, zswa_ref, znat_ref, zg_ref, w_scr):
    @pl.when(pl.program_id(0) == 0)
    def _():
        rest = IN_DIM - Z_ML_W - N_GATES
        w_scr[0:Z_ML_W, :] = wt_ref[0:Z_ML_W, :].astype(BF16)
        w_scr[Z_ML_W:Z_ML_W + rest, :] = wt_ref[Z_ML_W + N_GATES:IN_DIM, :].astype(BF16)
        w_scr[Z_ML_W + rest:IN_DIM, :] = wt_ref[Z_ML_W:Z_ML_W + N_GATES, :].astype(BF16)
        w_scr[IN_DIM:Z_W, :] = jnp.zeros((Z_W - IN_DIM, D_MODEL), BF16)

    h = (x_ref[...] * (1.0 + mod_ref[1:2, :]) + mod_ref[0:1, :]).astype(BF16)
    o = 0
    for ref in (zml_ref, zswa_ref, znat_ref, zg_ref):
        w = ref.shape[-1]
        z = _dot_nt(h, w_scr[o:o + w, :]) * b_ref[0:1, o:o + w] + b_ref[1:2, o:o + w]
        ref[...] = z.astype(ref.dtype)
        o += w


def _mod_spec(layer, mod_row_of_tile):
    return pl.BlockSpec((None, 6, D_MODEL),
                        lambda i: (layer * MOD_ROWS + mod_row_of_tile(i), 0, 0))


def _inproj_call(x, mods, mod_row_of_tile, w_in_t, b_in, layer, dtypes):
    rows = x.shape[0]
    widths = (Z_ML_W, Z_SWA_W, Z_NAT_W, Z_G_W)
    return pl.pallas_call(
        _inproj_kernel,
        grid=(rows // ROW_TILE,),
        in_specs=[
            pl.BlockSpec((ROW_TILE, D_MODEL), lambda i: (i, 0)),
            _mod_spec(layer, mod_row_of_tile),
            _layer_spec(w_in_t, layer, single_buffer=True),
            _layer_spec(b_in, layer),
        ],
        out_specs=[pl.BlockSpec((ROW_TILE, w), lambda i: (i, 0)) for w in widths],
        out_shape=[jax.ShapeDtypeStruct((rows, w), dt) for w, dt in zip(widths, dtypes)],
        scratch_shapes=[pltpu.VMEM((Z_W, D_MODEL), BF16)],
        compiler_params=_params(1),
        name="inproj",
    )(x, mods, w_in_t, b_in)


def _tail_kernel(x_ref, mml_ref, mswa_ref, mnat_ref, mod_ref, wout_ref, ln1g_ref, ln1b_ref,
                 w1_ref, w2_ref, ln2g_ref, ln2b_ref, o_ref):
    g_a, sh_m, sc_m, g_m = (mod_ref[i:i + 1, :] for i in (2, 3, 4, 5))
    mix = jnp.concatenate([mml_ref[...], mswa_ref[...], mnat_ref[...]], axis=1)
    proj = jnp.dot(mix, wout_ref[...], preferred_element_type=F32)
    y = _layer_norm(DN_ALPHA * x_ref[...] + g_a * proj, ln1g_ref[...], ln1b_ref[...])
    h = (y * (1.0 + sc_m) + sh_m).astype(BF16)
    f = jnp.maximum(jnp.dot(h, w1_ref[...], preferred_element_type=F32), 0.0)
    f = jnp.dot((f * f).astype(BF16), w2_ref[...], preferred_element_type=F32)
    o_ref[...] = _layer_norm(DN_ALPHA * y + g_m * f, ln2g_ref[...], ln2b_ref[...])


def _tail_call(x, mix_ml, mix_swa, mix_nat, mods, mod_row_of_tile, consts, layer):
    rows = x.shape[0]
    row_spec = lambda w: pl.BlockSpec((TAIL_TILE, w), lambda i: (i, 0))
    return pl.pallas_call(
        _tail_kernel,
        grid=(rows // TAIL_TILE,),
        in_specs=[row_spec(D_MODEL), row_spec(ML_W), row_spec(SWA_W), row_spec(NAT_W),
                  _mod_spec(layer, mod_row_of_tile)]
                 + [_layer_spec(a, layer, single_buffer=True) for a in consts],
        out_specs=row_spec(D_MODEL),
        out_shape=jax.ShapeDtypeStruct((rows, D_MODEL), F32),
        compiler_params=_params(1),
        name="tail",
    )(x, mix_ml, mix_swa, mix_nat, mods, *consts)


def _mlstm_direction(q, k, vt, a_col, b_row, li_row, neg, last, c_st, n_st, m_st):
    a_m = a_col + neg
    c_row = jnp.maximum(m_st, jnp.max(a_m, axis=0, keepdims=True))
    decay = jnp.exp(a_m - c_row)
    wprev = jnp.exp(m_st - c_row)
    mt = b_row + c_row
    st = _dot_nt(k, q) * decay
    den = jnp.sum(st, axis=0, keepdims=True)
    num = _dot(vt, st)
    if c_st is not None:
        cn = _dot_nt(jnp.concatenate([c_st, jnp.broadcast_to(n_st, (SUBLANES, D_HEAD))], axis=0),
                     q)
        num = num + wprev * cn[0:D_HEAD, :]
        den = den + wprev * cn[D_HEAD:D_HEAD + 1, :]
    ht = num * (1.0 / jnp.maximum(jnp.abs(den), jnp.exp(-mt)))
    m_new = mt[:, last:last + 1]
    ws = jnp.exp(b_row[:, last:last + 1] - b_row + li_row - m_new)
    upd = _dot(jnp.concatenate([vt * ws, jnp.broadcast_to(ws, (SUBLANES, ws.shape[1]))], axis=0),
               k)
    c_new = upd[0:D_HEAD, :]
    n_new = upd[D_HEAD:D_HEAD + 1, :]
    if c_st is not None:
        wc = wprev[:, last:last + 1]
        c_new = c_new + wc * c_st
        n_new = n_new + wc * n_st
    return ht, c_new, n_new, m_new


def _mlstm_chunk(zml_ref, g_ref, fb_ref, rows, direction, state):
    n = ML_CHUNK
    g = g_ref[rows, :]
    ls = _log_sigmoid(g + fb_ref[...])
    r = lax.broadcasted_iota(jnp.int32, (n, n), 0)
    c = lax.broadcasted_iota(jnp.int32, (n, n), 1)
    if direction == 0:
        tri, neg, last = c <= r, jnp.where(r <= c, 0.0, NEG_INF), n - 1
    else:
        tri, neg, last = c >= r, jnp.where(r >= c, 0.0, NEG_INF), 0
    b = _dot_split3(tri.astype(BF16), ls)
    a = pltpu.roll(g, ML_HEADS, 1) - b
    bt = b.T
    gt = g.T
    i_col = 2 * ML_HEADS * direction
    f_col = i_col + ML_HEADS
    hts, new_state = [], []
    for pair in range(ML_HEADS // 2):
        vt2 = zml_ref[rows, 2 * ML_W + pair * LANES:2 * ML_W + (pair + 1) * LANES].T
        for sub in range(2):
            hd = 2 * pair + sub
            feat = slice(sub * D_HEAD, (sub + 1) * D_HEAD)
            q = zml_ref[rows, hd * D_HEAD:(hd + 1) * D_HEAD].astype(BF16)
            k = zml_ref[rows, ML_W + hd * D_HEAD:ML_W + (hd + 1) * D_HEAD] * ATT_SCALE
            ht, c_new, n_new, m_new = _mlstm_direction(
                q, k, vt2[feat, :],
                a[:, f_col + hd:f_col + hd + 1], bt[f_col + hd:f_col + hd + 1, :],
                gt[i_col + hd:i_col + hd + 1, :], neg, last, *state[hd])
            hts.append(ht)
            new_state.append((c_new, n_new, m_new))
    return jnp.concatenate(hts, axis=0), new_state


def _mlstm_finish(ht, o_gate, norm_g):
    outs = []
    for hd in range(ML_HEADS):
        x = ht[hd * D_HEAD:(hd + 1) * D_HEAD, :]
        mu = jnp.mean(x, axis=0, keepdims=True)
        xc = x - mu
        var = jnp.mean(xc * xc, axis=0, keepdims=True)
        outs.append(xc * lax.rsqrt(var + LN_EPS))
    hn = jnp.concatenate(outs, axis=0).T
    return hn * norm_g * _sigmoid(o_gate)


def _store_state(c_ref, n_ref, m_ref, idx, c_new, n_new, m_new):
    c_ref[idx] = c_new
    n_ref[idx:idx + 1, :] = n_new
    m_ref[idx:idx + 1, :] = jnp.broadcast_to(m_new, (1, LANES))


ONES_ROWS = 16


def _with_ones(vt):
    ones = jnp.ones((ONES_ROWS, vt.shape[1]), BF16)
    return jnp.concatenate([vt.astype(BF16), ones], axis=0)


def _attend_t(q, parts, sink_row=None):
    sts = []
    for k, _, bias_t in parts:
        st = _dot_nt(k, q)
        sts.append(st if bias_t is None else st + bias_t)
    m = None
    for st in sts:
        mx = jnp.max(st, axis=0, keepdims=True)
        m = mx if m is None else jnp.maximum(m, mx)
    if sink_row is not None:
        m = jnp.maximum(m, sink_row)
    ot = None
    for st, (_, vt_ones, _) in zip(sts, parts):
        part = jnp.dot(vt_ones, jnp.exp2(st - m).astype(BF16), preferred_element_type=F32)
        ot = part if ot is None else ot + part
    den = ot[D_HEAD:D_HEAD + 1, :]
    if sink_row is not None:
        den = den + jnp.exp2(sink_row - m)
    return ot[0:D_HEAD, :] * (1.0 / den)


def _head_rows(pairs_t, hd):
    return pairs_t[hd // 2][(hd % 2) * D_HEAD:(hd % 2 + 1) * D_HEAD, :]


def _store_heads_t(o_ref, ots):
    for p in range(len(ots) // 2):
        o_ref[:, p * LANES:(p + 1) * LANES] = jnp.concatenate(
            [ots[2 * p], ots[2 * p + 1]], axis=0).astype(o_ref.dtype).T


N_STATE = 2 * ML_HEADS
N_CTX_STATE_OUTS = 7


def _ctx_mixers_kernel(x_ref, mod_ref, wt_ref, b_ref, fb_ref, ng_ref, sink_ref, *rest,
                       first_layer):
    n_scratch = 5
    w_scr, zml_ref, zswa_ref, znat_ref, zg_ref = rest[-n_scratch:]
    mml_ref, mswa_ref, mnat_ref, *state_refs = rest[-(3 + N_CTX_STATE_OUTS) - n_scratch:-n_scratch]
    _inproj_kernel(x_ref, mod_ref, wt_ref, b_ref, zml_ref, zswa_ref, znat_ref, zg_ref, w_scr)
    if first_layer:
        for ref in state_refs:
            ref[1:] = jnp.zeros((DEPTH - 1,) + ref.shape[1:], F32)
        state_refs = [ref.at[0] for ref in state_refs]
    sk_ref, sv_ref, nk_ref, nv_ref, c_ref, n_ref, m_ref = state_refs
    rows = slice(0, SEQ)
    zero_state = [(None, jnp.zeros((1, D_HEAD), F32), jnp.zeros((1, 1), F32))] * ML_HEADS
    ht_sum = None
    for direction in range(2):
        ht, new_state = _mlstm_chunk(zml_ref, zg_ref, fb_ref, rows, direction, zero_state)
        ht_sum = ht if ht_sum is None else ht_sum + ht
        for hd in range(ML_HEADS):
            _store_state(c_ref, n_ref, m_ref, direction * ML_HEADS + hd, *new_state[hd])
    mml_ref[...] = _mlstm_finish(ht_sum, zml_ref[:, 3 * ML_W:4 * ML_W], ng_ref[...]).astype(BF16)

    scaled_q = lambda ref, hd: ref[:, hd * D_HEAD:(hd + 1) * D_HEAD].astype(BF16)
    vts = [zswa_ref[:, SWA_W + SWA_KV_W:SWA_W + 2 * SWA_KV_W].astype(BF16).T]
    ots = []
    for kv in range(SWA_KV_HEADS):
        k = zswa_ref[:, SWA_W + kv * D_HEAD:SWA_W + (kv + 1) * D_HEAD]
        sk_ref[kv] = k
        sv_ref[kv] = zswa_ref[:, SWA_W + SWA_KV_W + kv * D_HEAD:SWA_W + SWA_KV_W + (kv + 1) * D_HEAD]
        q = jnp.concatenate([scaled_q(zswa_ref, kv * SWA_GROUP + g) for g in range(SWA_GROUP)],
                            axis=0)
        ot = _attend_t(q, [(k.astype(BF16), _with_ones(_head_rows(vts, kv)), None)],
                       sink_ref[kv:kv + 1, :])
        ots += [ot[:, g * SEQ:(g + 1) * SEQ] for g in range(SWA_GROUP)]
    _store_heads_t(mswa_ref, ots)

    vts = [znat_ref[:, 2 * NAT_W + p * LANES:2 * NAT_W + (p + 1) * LANES].astype(BF16).T
           for p in range(NAT_W // LANES)]
    ots = []
    for hd in range(NAT_HEADS):
        k = znat_ref[:, NAT_W + hd * D_HEAD:NAT_W + (hd + 1) * D_HEAD]
        nk_ref[hd] = k
        nv_ref[hd] = znat_ref[:, 2 * NAT_W + hd * D_HEAD:2 * NAT_W + (hd + 1) * D_HEAD]
        ots.append(_attend_t(scaled_q(znat_ref, hd),
                             [(k.astype(BF16), _with_ones(_head_rows(vts, hd)), None)]))
    _store_heads_t(mnat_ref, ots)


CTX_STATE_DIMS = ((SWA_KV_HEADS, SEQ, D_HEAD), (SWA_KV_HEADS, SEQ, D_HEAD),
                  (NAT_HEADS, SEQ, D_HEAD), (NAT_HEADS, SEQ, D_HEAD),
                  (N_STATE, D_HEAD, D_HEAD), (N_STATE, D_HEAD), (N_STATE, LANES))


def _ctx_mixers_call(x, mods, w_in_t, b_in, fb_row, norm_g, sink_rows, layer, prev_state):
    rows = x.shape[0]
    row_spec = lambda w: pl.BlockSpec((SEQ, w), lambda b: (b, 0))
    state_dims = CTX_STATE_DIMS
    first_layer = prev_state is None
    if first_layer:
        assert layer == 0
        state_spec = lambda d: pl.BlockSpec((None, DEPTH) + d, lambda b: (b,) + (0,) * (len(d) + 1))
    else:
        state_spec = lambda d: pl.BlockSpec((None, None) + d, lambda b: (b, layer) + (0,) * len(d))
    prev = () if first_layer else tuple(prev_state)
    n_in = 7
    return pl.pallas_call(
        functools.partial(_ctx_mixers_kernel, first_layer=first_layer),
        grid=(BATCH,),
        in_specs=[row_spec(D_MODEL), _mod_spec(layer, lambda b: CTX_MOD_ROW),
                  _layer_spec(w_in_t, layer, single_buffer=True), _layer_spec(b_in, layer),
                  _layer_spec(fb_row, layer), _layer_spec(norm_g, layer),
                  _layer_spec(sink_rows, layer)]
                 + [pl.BlockSpec(memory_space=pl.ANY)] * len(prev),
        out_specs=[row_spec(ML_W), row_spec(SWA_W), row_spec(NAT_W)]
                  + [state_spec(d) for d in state_dims],
        out_shape=[jax.ShapeDtypeStruct((rows, ML_W), BF16),
                   jax.ShapeDtypeStruct((rows, SWA_W), BF16),
                   jax.ShapeDtypeStruct((rows, NAT_W), BF16)]
                  + [jax.ShapeDtypeStruct((BATCH, DEPTH) + d, F32) for d in state_dims],
        input_output_aliases={n_in + i: 3 + i for i in range(len(prev))},
        scratch_shapes=[pltpu.VMEM((Z_W, D_MODEL), BF16)]
                       + [pltpu.VMEM((SEQ, w), F32) for w in (Z_ML_W, Z_SWA_W, Z_NAT_W, Z_G_W)],
        compiler_params=_params(1),
        name="ctx_mixers",
    )(x, mods, w_in_t, b_in, fb_row, norm_g, sink_rows, *prev)


def _lat_mlstm_kernel(zml_ref, zg_ref, fb_ref, ng_ref, c0_ref, n0_ref, m0_ref, mml_ref,
                      ht_scr, c_scr, n_scr, m_scr):
    n_chunks = DEC_SEQ // ML_CHUNK
    c_scr[...] = c0_ref[...]
    n_scr[...] = n0_ref[...]
    m_scr[...] = jnp.broadcast_to(m0_ref[...], m_scr.shape)

    def body(i, carry):
        for direction in range(2):
            chunk = i if direction == 0 else n_chunks - 1 - i
            rows = pl.ds(pl.multiple_of(chunk * ML_CHUNK, ML_CHUNK), ML_CHUNK)
            base = direction * ML_HEADS
            state = [(c_scr[base + hd], n_scr[base + hd:base + hd + 1, :],
                      m_scr[base + hd:base + hd + 1, 0:1]) for hd in range(ML_HEADS)]
            ht, new_state = _mlstm_chunk(zml_ref, zg_ref, fb_ref, rows, direction, state)
            ht_scr[direction, chunk] = ht
            for hd in range(ML_HEADS):
                _store_state(c_scr, n_scr, m_scr, base + hd, *new_state[hd])
        return carry

    lax.fori_loop(0, n_chunks, body, 0)
    for chunk in range(n_chunks):
        rows = slice(chunk * ML_CHUNK, (chunk + 1) * ML_CHUNK)
        mml_ref[rows, :] = _mlstm_finish(ht_scr[0, chunk] + ht_scr[1, chunk],
                                         zml_ref[rows, 3 * ML_W:4 * ML_W],
                                         ng_ref[...]).astype(BF16)


def _lat_mlstm_call(z_ml, z_g, fb_row, norm_g, state_c, state_n, state_m, layer):
    n_st = 2 * ML_HEADS
    n_chunks = DEC_SEQ // ML_CHUNK
    row_spec = lambda w: pl.BlockSpec((DEC_SEQ, w), lambda b: (b, 0))
    return pl.pallas_call(
        _lat_mlstm_kernel,
        grid=(DEC_BATCH,),
        in_specs=[row_spec(Z_ML_W), row_spec(Z_G_W), _layer_spec(fb_row, layer),
                  _layer_spec(norm_g, layer),
                  pl.BlockSpec((None, None, n_st, D_HEAD, D_HEAD), lambda b: (b, layer, 0, 0, 0)),
                  pl.BlockSpec((None, None, n_st, D_HEAD), lambda b: (b, layer, 0, 0)),
                  pl.BlockSpec((None, None, n_st, 1), lambda b: (b, layer, 0, 0))],
        out_specs=row_spec(ML_W),
        out_shape=jax.ShapeDtypeStruct((DEC_BATCH * DEC_SEQ, ML_W), BF16),
        scratch_shapes=[pltpu.VMEM((2, n_chunks, ML_W, ML_CHUNK), F32),
                        pltpu.VMEM((n_st, D_HEAD, D_HEAD), F32), pltpu.VMEM((n_st, D_HEAD), F32),
                        pltpu.VMEM((n_st, LANES), F32)],
        compiler_params=_params(1),
        name="lat_mlstm",
    )(z_ml, z_g, fb_row, norm_g, state_c, state_n, state_m)


def _rope(x, cos, sin_signed):
    lane = lax.broadcasted_iota(jnp.int32, x.shape, 1)
    first = (lane & 31) < 16
    partner = jnp.where(first, pltpu.roll(x, LANES - 16, 1), pltpu.roll(x, 16, 1))
    return x * cos + partner * sin_signed


def _lat_swa_kernel(zswa_ref, cos_ref, sin_ref, ckt_ref, cvt_ref, sink_ref, mswa_ref,
                    q_scr, k_scr, vt_scr, ot_scr, bias_scr, ck_scr, cvt_scr, st_scr, m_scr):
    cos = cos_ref[...]
    sin = sin_ref[...]
    for j in range(SWA_W // LANES):
        q_scr[:, j * LANES:(j + 1) * LANES] = _rope(
            zswa_ref[:, j * LANES:(j + 1) * LANES], cos, sin).astype(BF16)
    k_scr[...] = _rope(zswa_ref[:, SWA_W:SWA_W + SWA_KV_W], cos, sin).astype(BF16)
    vt_scr[...] = zswa_ref[:, SWA_W + SWA_KV_W:SWA_W + 2 * SWA_KV_W].astype(BF16).T

    bl = SWA_BLOCK
    nb = DEC_SEQ // bl
    n_q = SWA_GROUP * bl
    kj = lax.broadcasted_iota(jnp.int32, (3 * bl, n_q), 0)
    qi = lax.broadcasted_iota(jnp.int32, (3 * bl, n_q), 1) & (bl - 1)
    keep = ((kj >= qi) & (kj < 2 * bl)) | ((kj >= 2 * bl) & (kj - 2 * bl <= qi))
    bias_scr[...] = jnp.where(keep, 0.0, NEG_INF)
    ck_scr[...] = jnp.concatenate([ckt_ref[kv] for kv in range(SWA_KV_HEADS)],
                                  axis=0).astype(BF16).T
    for kv in range(SWA_KV_HEADS):
        cvt_scr[kv] = _with_ones(cvt_ref[kv])

    def band(blk):
        lo_b, hi_b = max(blk - 1, 0), min(blk + 2, nb)
        return slice(lo_b * bl, hi_b * bl), slice((lo_b - blk + 1) * bl, (hi_b - blk + 1) * bl)

    def score_stage(kv, blk, slot):
        head = slice(kv * D_HEAD, (kv + 1) * D_HEAD)
        keys, bias_rows = band(blk)
        n_band = keys.stop - keys.start
        q = jnp.concatenate(
            [q_scr[blk * bl:(blk + 1) * bl,
                   (kv * SWA_GROUP + g) * D_HEAD:(kv * SWA_GROUP + g + 1) * D_HEAD]
             for g in range(SWA_GROUP)], axis=0)
        s_ctx = _dot_nt(ck_scr[:, head], q)
        s_band = _dot_nt(k_scr[keys, head], q) + bias_scr[bias_rows, :]
        st_scr[slot, 0:PAST_LEN, :] = s_ctx
        st_scr[slot, PAST_LEN:PAST_LEN + n_band, :] = s_band
        m = jnp.maximum(jnp.max(s_ctx, axis=0, keepdims=True),
                        jnp.max(s_band, axis=0, keepdims=True))
        m_scr[slot] = jnp.broadcast_to(jnp.maximum(m, sink_ref[kv:kv + 1, :]), m_scr.shape[1:])

    def value_stage(kv, blk, slot):
        head = slice(kv * D_HEAD, (kv + 1) * D_HEAD)
        keys, _ = band(blk)
        n_band = keys.stop - keys.start
        m = m_scr[slot, 0:1, :]
        p_ctx = jnp.exp2(st_scr[slot, 0:PAST_LEN, :] - m).astype(BF16)
        p_band = jnp.exp2(st_scr[slot, PAST_LEN:PAST_LEN + n_band, :] - m).astype(BF16)
        ot = (jnp.dot(cvt_scr[kv], p_ctx, preferred_element_type=F32)
              + jnp.dot(_with_ones(vt_scr[head, keys]), p_band, preferred_element_type=F32))
        den = ot[D_HEAD:D_HEAD + 1, :] + jnp.exp2(sink_ref[kv:kv + 1, :] - m)
        o = (ot[0:D_HEAD, :] * (1.0 / den)).astype(BF16)
        for g in range(SWA_GROUP):
            hq = kv * SWA_GROUP + g
            ot_scr[hq * D_HEAD:(hq + 1) * D_HEAD, blk * bl:(blk + 1) * bl] = o[:, g * bl:(g + 1) * bl]

    problems = [(kv, blk) for kv in range(SWA_KV_HEADS) for blk in range(nb)]
    for i in range(len(problems) + 1):
        def stage(i=i):
            if i > 0:
                value_stage(*problems[i - 1], (i - 1) % 2)
            if i < len(problems):
                score_stage(*problems[i], i % 2)
        stage()
    for p in range(SWA_W // LANES):
        mswa_ref[:, p * LANES:(p + 1) * LANES] = ot_scr[p * LANES:(p + 1) * LANES, :].T


def _lat_swa_call(z_swa, cos_t, sin_t, cache_k, cache_v, sink_rows, layer):
    row_spec = lambda w: pl.BlockSpec((DEC_SEQ, w), lambda b: (b, 0))
    full = lambda a: pl.BlockSpec(a.shape, lambda b: (0,) * a.ndim)
    cache_spec = pl.BlockSpec((None, None, SWA_KV_HEADS, D_HEAD, PAST_LEN),
                              lambda b: (b, layer, 0, 0, 0))
    n_q = SWA_GROUP * SWA_BLOCK
    return pl.pallas_call(
        _lat_swa_kernel,
        grid=(DEC_BATCH,),
        in_specs=[row_spec(Z_SWA_W), full(cos_t), full(sin_t), cache_spec, cache_spec,
                  _layer_spec(sink_rows, layer)],
        out_specs=row_spec(SWA_W),
        out_shape=jax.ShapeDtypeStruct((DEC_BATCH * DEC_SEQ, SWA_W), BF16),
        scratch_shapes=[pltpu.VMEM((DEC_SEQ, SWA_W), BF16), pltpu.VMEM((DEC_SEQ, SWA_KV_W), BF16),
                        pltpu.VMEM((SWA_KV_W, DEC_SEQ), BF16), pltpu.VMEM((SWA_W, DEC_SEQ), BF16),
                        pltpu.VMEM((3 * SWA_BLOCK, n_q), F32),
                        pltpu.VMEM((PAST_LEN, SWA_KV_W), BF16),
                        pltpu.VMEM((SWA_KV_HEADS, D_HEAD + ONES_ROWS, PAST_LEN), BF16),
                        pltpu.VMEM((2, PAST_LEN + 3 * SWA_BLOCK, n_q), F32),
                        pltpu.VMEM((2, SUBLANES, n_q), F32)],
        compiler_params=_params(1),
        name="lat_swa",
    )(z_swa, cos_t, sin_t, cache_k, cache_v, sink_rows)


NAT_ROWS = DEC_SEQ // GRID_W
NAT_GROUP_ROWS = 4
NAT_GROUP_Q = NAT_GROUP_ROWS * GRID_W
NAT_RPB_ROWS = 2 * NAT_KH


def _nat_row_start(r):
    return min(max(r - NAT_KH // 2, 0), NAT_ROWS - NAT_KH)


def _nat_groups():
    groups, off = [], 0
    for g in range(NAT_ROWS // NAT_GROUP_ROWS):
        w0 = _nat_row_start(g * NAT_GROUP_ROWS)
        w1 = _nat_row_start((g + 1) * NAT_GROUP_ROWS - 1) + NAT_KH
        n_rows = w1 - w0 + (w1 - w0) % 2
        w0 = min(w0, NAT_ROWS - n_rows)
        groups.append((w0, n_rows, off))
        off += n_rows * GRID_W
    return groups, off


NAT_GROUPS, NAT_BIAS_KEYS = _nat_groups()


def _nat_build_bias(rpb_ref, bias_scr):
    shape = (GRID_W, LANES)
    q = lax.broadcasted_iota(jnp.int32, shape, 0)
    lane = lax.broadcasted_iota(jnp.int32, shape, 1)
    kc = lane & (GRID_W - 1)
    cs = jnp.clip(q - NAT_KW // 2, 0, GRID_W - NAT_KW)
    ok = (kc >= cs) & (kc < cs + NAT_KW)
    ok_lo = ok & (lane < GRID_W)
    ok_hi = ok & (lane >= GRID_W)
    neg_tile = jnp.full(shape, NEG_INF, F32)
    for hd in range(2):
        lo, hi = [], []
        for ro in range(2 * NAT_KH - 1):
            x = jnp.broadcast_to(rpb_ref[hd, ro:ro + 1, :], shape)
            lo.append(jnp.where(ok_lo, pltpu.roll(x, 0, 1, stride=1, stride_axis=0), NEG_INF))
            hi.append(jnp.where(ok_hi, pltpu.roll(x, GRID_W, 1, stride=1, stride_axis=0), NEG_INF))
        for g, (w0, n_rows, off) in enumerate(NAT_GROUPS):
            for jj in range(n_rows // 2):
                kra, krb = w0 + 2 * jj, w0 + 2 * jj + 1
                tiles = []
                for rr in range(NAT_GROUP_ROWS):
                    r = g * NAT_GROUP_ROWS + rr
                    r0 = _nat_row_start(r)
                    parts = []
                    if r0 <= kra < r0 + NAT_KH:
                        parts.append(lo[kra - r + NAT_KH - 1])
                    if r0 <= krb < r0 + NAT_KH:
                        parts.append(hi[krb - r + NAT_KH - 1])
                    tiles.append(neg_tile if not parts else (
                        parts[0] if len(parts) == 1 else jnp.maximum(parts[0], parts[1])))
                bias_scr[hd, off + jj * LANES:off + (jj + 1) * LANES, :] = (
                    jnp.concatenate(tiles, axis=0).T)


def _lat_nat_kernel(q_ref, k_ref, v_ref, ckt_ref, cvt_ref, rpb_ref, mnat_ref, bias_scr):
    @pl.when(pl.program_id(1) == 0)
    def _():
        _nat_build_bias(rpb_ref, bias_scr)

    vt = v_ref[...].astype(BF16).T
    ck2 = jnp.concatenate([ckt_ref[0], ckt_ref[1]], axis=0).astype(BF16).T
    ots = []
    for hd in range(2):
        head = slice(hd * D_HEAD, (hd + 1) * D_HEAD)
        ck = ck2[:, head]
        cvt_ones = _with_ones(cvt_ref[hd])
        ot_groups = []
        for g, (w0, n_rows, off) in enumerate(NAT_GROUPS):
            keys = slice(w0 * GRID_W, (w0 + n_rows) * GRID_W)
            parts = [(k_ref[keys, head].astype(BF16), _with_ones(vt[head, keys]),
                      bias_scr[hd, off:off + n_rows * GRID_W, :]),
                     (ck, cvt_ones, None)]
            q = q_ref[g * NAT_GROUP_Q:(g + 1) * NAT_GROUP_Q, head].astype(BF16)
            ot_groups.append(_attend_t(q, parts))
        ots.append(jnp.concatenate(ot_groups, axis=1))
    mnat_ref[...] = jnp.concatenate(ots, axis=0).astype(BF16).T


def _lat_nat_call(z_nat, cache_k, cache_v, rpb_lanes, layer):
    n_pair = NAT_HEADS // 2
    col_spec = lambda off: pl.BlockSpec((DEC_SEQ, LANES), lambda p, b: (b, off + p))
    cache_spec = pl.BlockSpec((None, None, 2, D_HEAD, PAST_LEN), lambda p, b: (b, layer, p, 0, 0))
    return pl.pallas_call(
        _lat_nat_kernel,
        grid=(n_pair, DEC_BATCH),
        in_specs=[col_spec(0), col_spec(n_pair), col_spec(2 * n_pair), cache_spec, cache_spec,
                  pl.BlockSpec((None, 2, NAT_RPB_ROWS, LANES), lambda p, b: (layer, p, 0, 0))],
        out_specs=pl.BlockSpec((DEC_SEQ, LANES), lambda p, b: (b, p)),
        out_shape=jax.ShapeDtypeStruct((DEC_BATCH * DEC_SEQ, NAT_W), BF16),
        scratch_shapes=[pltpu.VMEM((2, NAT_BIAS_KEYS, NAT_GROUP_Q), F32)],
        compiler_params=_params(2),
        name="lat_nat",
    )(z_nat, z_nat, z_nat, cache_k, cache_v, rpb_lanes)


def _nat_rpb_lanes(rpb):
    n_off = 2 * NAT_KW - 1
    padded = jnp.concatenate(
        [rpb.astype(F32), jnp.zeros(rpb.shape[:-1] + (LANES - n_off,), F32)], axis=-1)
    rolled = jnp.roll(padded, -(NAT_KW - 1), axis=-1)
    pad_rows = jnp.zeros(rpb.shape[:-2] + (NAT_RPB_ROWS - rpb.shape[-2], LANES), F32)
    return jnp.concatenate([rolled, pad_rows], axis=-2)


def _rope_tables():
    t = np.arange(DEC_SEQ)[:, None]
    d = np.arange(LANES)[None, :] % D_HEAD
    pos = np.where(d < D_HEAD // 2, t // GRID_W, t % GRID_W).astype(np.float64)
    quarter = D_HEAD // 4
    freq = ROPE_BASE ** (-(d % quarter).astype(np.float64) / quarter)
    ang = (pos.astype(np.float32) * freq.astype(np.float32)).astype(np.float32)
    sign = np.where((d % (2 * quarter)) < quarter, -1.0, 1.0)
    return (jnp.asarray(np.cos(ang), dtype=F32), jnp.asarray(np.sin(ang) * sign, dtype=F32))


def _permute_in_columns(a):
    g0 = Z_ML_W
    s0 = g0 + N_GATES
    pad = jnp.zeros(a.shape[:-1] + (Z_G_W - N_GATES,), a.dtype)
    return jnp.concatenate([a[..., :g0], a[..., s0:], a[..., g0:s0], pad], axis=-1)


def kernel(x_prompt, x_sample, cache_swa_k, cache_swa_v, cache_nat_k, cache_nat_v, state_mlstm_C,
           state_mlstm_n, state_mlstm_m, c, c_ctx, w_ada, b_ada, w_in, b_in, mlstm_fbias,
           mlstm_norm_g, swa_sink, nat_rpb, w_out, ln1_g, ln1_b, w_mlp1, w_mlp2, ln2_g, ln2_b):
    xp = x_prompt.reshape(BATCH * SEQ, D_MODEL)
    xs = x_sample.reshape(DEC_BATCH * DEC_SEQ, D_MODEL)
    cvec = jnp.concatenate(
        [c, c_ctx[None, :], jnp.zeros((MOD_ROWS - DEC_BATCH - 1, D_MODEL), F32)], axis=0)
    mods = _mods_call(cvec, w_ada, b_ada).reshape(DEPTH * MOD_ROWS, 6, D_MODEL)

    w_in_t = jnp.swapaxes(w_in, 1, 2)
    out_scale = np.ones((Z_W,), np.float32)
    out_scale[Z_ML_W:Z_ML_W + SWA_W] = Q_SCALE2
    out_scale[Z_ML_W + Z_SWA_W:Z_ML_W + Z_SWA_W + NAT_W] = Q_SCALE2
    b_in_p = jnp.stack([jnp.broadcast_to(out_scale, (DEPTH, Z_W)),
                        _permute_in_columns(b_in) * out_scale], axis=1)
    cache_swa_k, cache_swa_v, cache_nat_k, cache_nat_v = (
        jnp.swapaxes(a, -1, -2) for a in (cache_swa_k, cache_swa_v, cache_nat_k, cache_nat_v))
    vec = lambda a: a.reshape(DEPTH, 1, D_MODEL)
    tail_w = (w_out.astype(BF16), vec(ln1_g), vec(ln1_b), w_mlp1.astype(BF16),
              w_mlp2.astype(BF16), vec(ln2_g), vec(ln2_b))
    cos_t, sin_t = _rope_tables()
    rpb_lanes = _nat_rpb_lanes(nat_rpb) * LOG2E
    swa_sink = swa_sink * LOG2E
    fb_rows = jnp.zeros((DEPTH, 1, Z_G_W), F32)
    fb_rows = fb_rows.at[:, 0, ML_HEADS:2 * ML_HEADS].set(mlstm_fbias[:, 0])
    fb_rows = fb_rows.at[:, 0, 3 * ML_HEADS:4 * ML_HEADS].set(mlstm_fbias[:, 1])
    norm_g = mlstm_norm_g.reshape(DEPTH, 1, ML_W)
    sink_ctx = jnp.repeat(swa_sink, SEQ, axis=1).reshape(DEPTH, SWA_KV_HEADS, SWA_GROUP * SEQ)
    sink_lat = jnp.repeat(swa_sink, SWA_BLOCK, axis=1).reshape(
        DEPTH, SWA_KV_HEADS, SWA_GROUP * SWA_BLOCK)

    n_st = 2 * ML_HEADS
    state_c = state_mlstm_C.reshape(DEC_BATCH, DEPTH, n_st, D_HEAD, D_HEAD)
    state_n = state_mlstm_n.reshape(DEC_BATCH, DEPTH, n_st, D_HEAD)
    state_m = state_mlstm_m.reshape(DEC_BATCH, DEPTH, n_st, 1)

    ctx_row = lambda tile: (lambda i: CTX_MOD_ROW)
    lat_row = lambda tile: (lambda i: i // (DEC_SEQ // tile))

    ctx_state = None
    for l in range(DEPTH):
        m_ml, m_swa, m_nat, *ctx_state = _ctx_mixers_call(
            xp, mods, w_in_t, b_in_p, fb_rows, norm_g, sink_ctx, l, ctx_state)
        xp = _tail_call(xp, m_ml, m_swa, m_nat, mods, ctx_row(TAIL_TILE), tail_w, l)

        z_ml, z_swa, z_nat, z_g = _inproj_call(xs, mods, lat_row(ROW_TILE), w_in_t, b_in_p, l,
                                               (F32, F32, BF16, F32))
        m_ml = _lat_mlstm_call(z_ml, z_g, fb_rows, norm_g, state_c, state_n, state_m, l)
        m_swa = _lat_swa_call(z_swa, cos_t, sin_t, cache_swa_k, cache_swa_v, sink_lat, l)
        m_nat = _lat_nat_call(z_nat, cache_nat_k, cache_nat_v, rpb_lanes, l)
        xs = _tail_call(xs, m_ml, m_swa, m_nat, mods, lat_row(TAIL_TILE), tail_w, l)

    k_swa, v_swa, k_nat, v_nat, st_c, st_n, st_m = ctx_state
    return (xp.reshape(BATCH, SEQ, D_MODEL), xs.reshape(DEC_BATCH, DEC_SEQ, D_MODEL),
            k_swa, v_swa, k_nat, v_nat,
            st_c.reshape(BATCH, DEPTH, 2, ML_HEADS, D_HEAD, D_HEAD),
            st_n.reshape(BATCH, DEPTH, 2, ML_HEADS, D_HEAD),
            st_m[..., 0].reshape(BATCH, DEPTH, 2, ML_HEADS))
```

```python
import functools

import numpy as np
import jax
import jax.numpy as jnp
from jax import lax
from jax.experimental import pallas as pl
from jax.experimental.pallas import tpu as pltpu

D_MODEL = 1024
BATCH = 16
SEQ = 256
DEPTH = 4
DEC_BATCH = 8
DEC_SEQ = 1024
PAST_LEN = 512
GRID_W = 64
D_HEAD = 64
ML_HEADS = 4
SWA_HEADS = 6
SWA_KV_HEADS = 2
SWA_GROUP = SWA_HEADS // SWA_KV_HEADS
NAT_HEADS = 6
ML_W = ML_HEADS * D_HEAD
SWA_W = SWA_HEADS * D_HEAD
SWA_KV_W = SWA_KV_HEADS * D_HEAD
NAT_W = NAT_HEADS * D_HEAD
N_GATES = 4 * ML_HEADS
IN_DIM = 4 * ML_W + N_GATES + SWA_W + 2 * SWA_KV_W + 3 * NAT_W
D_FF = 4 * D_MODEL
SWA_WINDOW = 128
SWA_BLOCK = 128
NAT_KH = 8
NAT_KW = 16
ROPE_BASE = 10000.0
LN_EPS = 1e-5
DN_ALPHA = (2 * DEPTH) ** 0.25
ATT_SCALE = D_HEAD ** -0.5
LOG2E = 1.4426950408889634
Q_SCALE2 = ATT_SCALE * LOG2E

LANES = 128
SUBLANES = 8
Z_ML_W = 4 * ML_W
Z_SWA_W = SWA_W + 2 * SWA_KV_W
Z_NAT_W = 3 * NAT_W
Z_G_W = LANES
Z_W = Z_ML_W + Z_SWA_W + Z_NAT_W + Z_G_W
Z_QK_W = 2 * ML_W
Z_VO_W = 2 * ML_W
Z_GROUPS = (Z_QK_W, Z_VO_W, Z_SWA_W, Z_NAT_W, Z_G_W)
Z_SWA_GROUP = 2
MOD_ROWS = 16
CTX_MOD_ROW = DEC_BATCH
ROW_TILE = 512
TAIL_TILE = 512
ML_CHUNK = 256
VMEM_LIMIT = 56 * 1024 * 1024

BF16 = jnp.bfloat16
F32 = jnp.float32
NEG_INF = float("-inf")
LAT_Z_DTYPES = (BF16, F32, BF16, BF16, F32)
CTX_Z_DTYPES = (BF16, F32, F32, F32, F32)


def _dot(a, b):
    return jnp.dot(a.astype(BF16), b.astype(BF16), preferred_element_type=F32)


def _dot_nt(a, b):
    return lax.dot_general(a.astype(BF16), b.astype(BF16), (((1,), (1,)), ((), ())),
                           preferred_element_type=F32)


def _dot_split3(tri, x):
    hi = x.astype(BF16)
    r1 = x - hi.astype(F32)
    mid = r1.astype(BF16)
    lo = (r1 - mid.astype(F32)).astype(BF16)
    return (jnp.dot(tri, hi, preferred_element_type=F32)
            + jnp.dot(tri, mid, preferred_element_type=F32)
            + jnp.dot(tri, lo, preferred_element_type=F32))


def _sigmoid(x):
    return 1.0 / (1.0 + jnp.exp(-x))


def _log_sigmoid(x):
    return jnp.minimum(x, 0.0) - jnp.log1p(jnp.exp(-jnp.abs(x)))


def _layer_norm(x, g, b):
    mu = jnp.mean(x, axis=-1, keepdims=True)
    xc = x - mu
    var = jnp.mean(xc * xc, axis=-1, keepdims=True)
    return xc * lax.rsqrt(var + LN_EPS) * g + b


def _params(n_axes=1, flags=None):
    return pltpu.CompilerParams(dimension_semantics=("arbitrary",) * n_axes,
                                vmem_limit_bytes=VMEM_LIMIT, flags=flags)


def _layer_spec(a, layer, single_buffer=False):
    kwargs = {"pipeline_mode": pl.Buffered(1)} if single_buffer else {}
    return pl.BlockSpec((None,) + a.shape[1:], lambda *_: (layer,) + (0,) * (a.ndim - 1), **kwargs)


def _mods_kernel(c_ref, w_ref, b_ref, o_ref):
    c = c_ref[...]
    o_ref[...] = _dot(c * _sigmoid(c), w_ref[...]) + b_ref[...]


def _mods_call(cvec, w_ada, b_ada):
    n_col = 6
    return pl.pallas_call(
        _mods_kernel,
        grid=(DEPTH, n_col),
        in_specs=[
            pl.BlockSpec((MOD_ROWS, D_MODEL), lambda l, j: (0, 0)),
            pl.BlockSpec((None, D_MODEL, D_MODEL), lambda l, j: (l, 0, j)),
            pl.BlockSpec((None, 1, D_MODEL), lambda l, j: (l, 0, j)),
        ],
        out_specs=pl.BlockSpec((None, MOD_ROWS, D_MODEL), lambda l, j: (l, 0, j)),
        out_shape=jax.ShapeDtypeStruct((DEPTH, MOD_ROWS, 6 * D_MODEL), F32),
        compiler_params=_params(2),
        name="mods",
    )(cvec, w_ada, b_ada.reshape(DEPTH, 1, 6 * D_MODEL))


def _rope(x, cos, sin_signed):
    lane = lax.broadcasted_iota(jnp.int32, x.shape, 1)
    first = (lane & 31) < 16
    partner = jnp.where(first, pltpu.roll(x, LANES - 16, 1), pltpu.roll(x, 16, 1))
    return x * cos + partner * sin_signed


def _inproj_kernel(x_ref, mod_ref, wt_ref, b_ref, *rest, rope=False):
    if rope:
        cos_ref, sin_ref, *rest = rest
    *out_refs, w_scr = rest

    @pl.when(pl.program_id(0) == 0)
    def _():
        rest = IN_DIM - Z_ML_W - N_GATES
        w_scr[0:Z_ML_W, :] = wt_ref[0:Z_ML_W, :].astype(BF16)
        w_scr[Z_ML_W:Z_ML_W + rest, :] = wt_ref[Z_ML_W + N_GATES:IN_DIM, :].astype(BF16)
        w_scr[Z_ML_W + rest:IN_DIM, :] = wt_ref[Z_ML_W:Z_ML_W + N_GATES, :].astype(BF16)
        w_scr[IN_DIM:Z_W, :] = jnp.zeros((Z_W - IN_DIM, D_MODEL), BF16)

    h = (x_ref[...] * (1.0 + mod_ref[1:2, :]) + mod_ref[0:1, :]).astype(BF16)
    o = 0
    for ref, w in zip(out_refs, Z_GROUPS):
        z = _dot_nt(h, w_scr[o:o + w, :]) * b_ref[0:1, o:o + w] + b_ref[1:2, o:o + w]
        if rope and ref is out_refs[Z_SWA_GROUP]:
            z = jnp.concatenate(
                [_rope(z[:, j * LANES:(j + 1) * LANES], cos_ref[...], sin_ref[...])
                 for j in range((SWA_W + SWA_KV_W) // LANES)] + [z[:, SWA_W + SWA_KV_W:]], axis=1)
        ref[...] = z.astype(ref.dtype)
        o += w


def _mod_spec(layer, mod_row_of_tile):
    return pl.BlockSpec((None, 6, D_MODEL),
                        lambda i: (layer * MOD_ROWS + mod_row_of_tile(i), 0, 0))


def _lat_inproj_call(x, mods, w_in_t, b_in, cos_t, sin_t, layer):
    rows = x.shape[0]
    tiles_per_seq = DEC_SEQ // ROW_TILE
    table_spec = pl.BlockSpec((ROW_TILE, LANES), lambda i: (i % tiles_per_seq, 0))
    return pl.pallas_call(
        functools.partial(_inproj_kernel, rope=True),
        grid=(rows // ROW_TILE,),
        in_specs=[
            pl.BlockSpec((ROW_TILE, D_MODEL), lambda i: (i, 0)),
            _mod_spec(layer, lambda i: i // tiles_per_seq),
            _layer_spec(w_in_t, layer, single_buffer=True),
            _layer_spec(b_in, layer),
            table_spec, table_spec,
        ],
        out_specs=[pl.BlockSpec((ROW_TILE, w), lambda i: (i, 0)) for w in Z_GROUPS],
        out_shape=[jax.ShapeDtypeStruct((rows, w), dt) for w, dt in zip(Z_GROUPS, LAT_Z_DTYPES)],
        scratch_shapes=[pltpu.VMEM((Z_W, D_MODEL), BF16)],
        compiler_params=_params(1),
        name="inproj",
    )(x, mods, w_in_t, b_in, cos_t, sin_t)


def _tail_kernel(x_ref, mml_ref, mswa_ref, mnat_ref, mod_ref, wout_ref, ln1g_ref, ln1b_ref,
                 w1_ref, w2_ref, ln2g_ref, ln2b_ref, o_ref):
    g_a, sh_m, sc_m, g_m = (mod_ref[i:i + 1, :] for i in (2, 3, 4, 5))
    mix = jnp.concatenate([mml_ref[...], mswa_ref[...], mnat_ref[...]], axis=1)
    proj = jnp.dot(mix, wout_ref[...], preferred_element_type=F32)
    y = _layer_norm(DN_ALPHA * x_ref[...] + g_a * proj, ln1g_ref[...], ln1b_ref[...])
    h = (y * (1.0 + sc_m) + sh_m).astype(BF16)
    f = jnp.maximum(jnp.dot(h, w1_ref[...], preferred_element_type=F32), 0.0)
    f = jnp.dot((f * f).astype(BF16), w2_ref[...], preferred_element_type=F32)
    o_ref[...] = _layer_norm(DN_ALPHA * y + g_m * f, ln2g_ref[...], ln2b_ref[...])


def _tail_call(x, mix_ml, mix_swa, mix_nat, mods, mod_row_of_tile, consts, layer):
    rows = x.shape[0]
    row_spec = lambda w: pl.BlockSpec((TAIL_TILE, w), lambda i: (i, 0))
    return pl.pallas_call(
        _tail_kernel,
        grid=(rows // TAIL_TILE,),
        in_specs=[row_spec(D_MODEL), row_spec(ML_W), row_spec(SWA_W), row_spec(NAT_W),
                  _mod_spec(layer, mod_row_of_tile)]
                 + [_layer_spec(a, layer, single_buffer=True) for a in consts],
        out_specs=row_spec(D_MODEL),
        out_shape=jax.ShapeDtypeStruct((rows, D_MODEL), F32),
        compiler_params=_params(1),
        name="tail",
    )(x, mix_ml, mix_swa, mix_nat, mods, *consts)


def _mlstm_direction(q, k, vt, a_col, b_row, li_row, neg, last, c_st, n_st, m_st):
    a_m = a_col + neg
    c_row = jnp.maximum(m_st, jnp.max(a_m, axis=0, keepdims=True))
    decay = jnp.exp(a_m - c_row)
    wprev = jnp.exp(m_st - c_row)
    mt = b_row + c_row
    st = _dot_nt(k, q) * decay
    den = jnp.sum(st, axis=0, keepdims=True)
    num = _dot(vt, st)
    if c_st is not None:
        cn = _dot_nt(jnp.concatenate([c_st, jnp.broadcast_to(n_st, (SUBLANES, D_HEAD))], axis=0),
                     q)
        num = num + wprev * cn[0:D_HEAD, :]
        den = den + wprev * cn[D_HEAD:D_HEAD + 1, :]
    ht = num * (1.0 / jnp.maximum(jnp.abs(den), jnp.exp(-mt)))
    m_new = mt[:, last:last + 1]
    ws = jnp.exp(b_row[:, last:last + 1] - b_row + li_row - m_new)
    upd = _dot(jnp.concatenate([vt * ws, jnp.broadcast_to(ws, (SUBLANES, ws.shape[1]))], axis=0),
               k)
    c_new = upd[0:D_HEAD, :]
    n_new = upd[D_HEAD:D_HEAD + 1, :]
    if c_st is not None:
        wc = wprev[:, last:last + 1]
        c_new = c_new + wc * c_st
        n_new = n_new + wc * n_st
    return ht, c_new, n_new, m_new


def _mlstm_chunk(zqk_ref, zvo_ref, g_ref, fb_ref, rows, n, direction, state):
    g = g_ref[rows, :]
    ls = _log_sigmoid(g + fb_ref[...])
    r = lax.broadcasted_iota(jnp.int32, (n, n), 0)
    c = lax.broadcasted_iota(jnp.int32, (n, n), 1)
    if direction == 0:
        tri, neg, last = c <= r, jnp.where(r <= c, 0.0, NEG_INF), n - 1
    else:
        tri, neg, last = c >= r, jnp.where(r >= c, 0.0, NEG_INF), 0
    b = _dot_split3(tri.astype(BF16), ls)
    a = pltpu.roll(g, ML_HEADS, 1) - b
    bt = b.T
    gt = g.T
    i_col = 2 * ML_HEADS * direction
    f_col = i_col + ML_HEADS
    hts, new_state = [], []
    for pair in range(ML_HEADS // 2):
        vt2 = zvo_ref[rows, pair * LANES:(pair + 1) * LANES].T
        for sub in range(2):
            hd = 2 * pair + sub
            feat = slice(sub * D_HEAD, (sub + 1) * D_HEAD)
            q = zqk_ref[rows, hd * D_HEAD:(hd + 1) * D_HEAD]
            k = zqk_ref[rows, ML_W + hd * D_HEAD:ML_W + (hd + 1) * D_HEAD]
            ht, c_new, n_new, m_new = _mlstm_direction(
                q, k, vt2[feat, :],
                a[:, f_col + hd:f_col + hd + 1], bt[f_col + hd:f_col + hd + 1, :],
                gt[i_col + hd:i_col + hd + 1, :], neg, last, *state[hd])
            hts.append(ht)
            new_state.append((c_new, n_new, m_new))
    return jnp.concatenate(hts, axis=0), new_state


def _mlstm_finish(ht, o_gate, norm_g):
    outs = []
    for hd in range(ML_HEADS):
        x = ht[hd * D_HEAD:(hd + 1) * D_HEAD, :]
        mu = jnp.mean(x, axis=0, keepdims=True)
        xc = x - mu
        var = jnp.mean(xc * xc, axis=0, keepdims=True)
        outs.append(xc * lax.rsqrt(var + LN_EPS))
    hn = jnp.concatenate(outs, axis=0).T
    return hn * norm_g * _sigmoid(o_gate)


def _store_state(c_ref, n_ref, m_ref, idx, c_new, n_new, m_new):
    c_ref[idx] = c_new
    n_ref[idx:idx + 1, :] = n_new
    m_ref[idx:idx + 1, :] = jnp.broadcast_to(m_new, (1, LANES))


ONES_ROWS = 16


def _with_ones(vt):
    ones = jnp.ones((ONES_ROWS, vt.shape[1]), BF16)
    return jnp.concatenate([vt.astype(BF16), ones], axis=0)


def _attend_t(q, parts, sink_row=None):
    sts = []
    for k, _, bias_t in parts:
        st = _dot_nt(k, q)
        sts.append(st if bias_t is None else st + bias_t)
    m = None
    for st in sts:
        mx = jnp.max(st, axis=0, keepdims=True)
        m = mx if m is None else jnp.maximum(m, mx)
    if sink_row is not None:
        m = jnp.maximum(m, sink_row)
    ot = None
    for st, (_, vt_ones, _) in zip(sts, parts):
        part = jnp.dot(vt_ones, jnp.exp2(st - m).astype(BF16), preferred_element_type=F32)
        ot = part if ot is None else ot + part
    den = ot[D_HEAD:D_HEAD + 1, :]
    if sink_row is not None:
        den = den + jnp.exp2(sink_row - m)
    return ot[0:D_HEAD, :] * (1.0 / den)


def _head_rows(pairs_t, hd):
    return pairs_t[hd // 2][(hd % 2) * D_HEAD:(hd % 2 + 1) * D_HEAD, :]


def _store_heads_t(o_ref, ots):
    for p in range(len(ots) // 2):
        o_ref[:, p * LANES:(p + 1) * LANES] = jnp.concatenate(
            [ots[2 * p], ots[2 * p + 1]], axis=0).astype(o_ref.dtype).T


N_STATE = 2 * ML_HEADS
N_CTX_STATE_OUTS = 7


def _ctx_mixers_kernel(x_ref, mod_ref, wt_ref, b_ref, fb_ref, ng_ref, sink_ref, *rest,
                       first_layer):
    n_scratch = 1 + len(Z_GROUPS)
    w_scr, zqk_ref, zvo_ref, zswa_ref, znat_ref, zg_ref = rest[-n_scratch:]
    mml_ref, mswa_ref, mnat_ref, *state_refs = rest[-(3 + N_CTX_STATE_OUTS) - n_scratch:-n_scratch]
    _inproj_kernel(x_ref, mod_ref, wt_ref, b_ref, zqk_ref, zvo_ref, zswa_ref, znat_ref, zg_ref,
                   w_scr)
    if first_layer:
        for ref in state_refs:
            ref[1:] = jnp.zeros((DEPTH - 1,) + ref.shape[1:], F32)
        state_refs = [ref.at[0] for ref in state_refs]
    sk_ref, sv_ref, nk_ref, nv_ref, c_ref, n_ref, m_ref = state_refs
    rows = slice(0, SEQ)
    zero_state = [(None, jnp.zeros((1, D_HEAD), F32), jnp.zeros((1, 1), F32))] * ML_HEADS
    ht_sum = None
    for direction in range(2):
        ht, new_state = _mlstm_chunk(zqk_ref, zvo_ref, zg_ref, fb_ref, rows, SEQ, direction,
                                     zero_state)
        ht_sum = ht if ht_sum is None else ht_sum + ht
        for hd in range(ML_HEADS):
            _store_state(c_ref, n_ref, m_ref, direction * ML_HEADS + hd, *new_state[hd])
    mml_ref[...] = _mlstm_finish(ht_sum, zvo_ref[:, ML_W:2 * ML_W], ng_ref[...]).astype(BF16)

    scaled_q = lambda ref, hd: ref[:, hd * D_HEAD:(hd + 1) * D_HEAD].astype(BF16)
    vts = [zswa_ref[:, SWA_W + SWA_KV_W:SWA_W + 2 * SWA_KV_W].astype(BF16).T]
    ots = []
    for kv in range(SWA_KV_HEADS):
        k = zswa_ref[:, SWA_W + kv * D_HEAD:SWA_W + (kv + 1) * D_HEAD]
        sk_ref[kv] = k
        sv_ref[kv] = zswa_ref[:, SWA_W + SWA_KV_W + kv * D_HEAD:SWA_W + SWA_KV_W + (kv + 1) * D_HEAD]
        q = jnp.concatenate([scaled_q(zswa_ref, kv * SWA_GROUP + g) for g in range(SWA_GROUP)],
                            axis=0)
        ot = _attend_t(q, [(k.astype(BF16), _with_ones(_head_rows(vts, kv)), None)],
                       sink_ref[kv:kv + 1, :])
        ots += [ot[:, g * SEQ:(g + 1) * SEQ] for g in range(SWA_GROUP)]
    _store_heads_t(mswa_ref, ots)

    vts = [znat_ref[:, 2 * NAT_W + p * LANES:2 * NAT_W + (p + 1) * LANES].astype(BF16).T
           for p in range(NAT_W // LANES)]
    ots = []
    for hd in range(NAT_HEADS):
        k = znat_ref[:, NAT_W + hd * D_HEAD:NAT_W + (hd + 1) * D_HEAD]
        nk_ref[hd] = k
        nv_ref[hd] = znat_ref[:, 2 * NAT_W + hd * D_HEAD:2 * NAT_W + (hd + 1) * D_HEAD]
        ots.append(_attend_t(scaled_q(znat_ref, hd),
                             [(k.astype(BF16), _with_ones(_head_rows(vts, hd)), None)]))
    _store_heads_t(mnat_ref, ots)


CTX_STATE_DIMS = ((SWA_KV_HEADS, SEQ, D_HEAD), (SWA_KV_HEADS, SEQ, D_HEAD),
                  (NAT_HEADS, SEQ, D_HEAD), (NAT_HEADS, SEQ, D_HEAD),
                  (N_STATE, D_HEAD, D_HEAD), (N_STATE, D_HEAD), (N_STATE, LANES))


def _ctx_mixers_call(x, mods, w_in_t, b_in, fb_row, norm_g, sink_rows, layer, prev_state):
    rows = x.shape[0]
    row_spec = lambda w: pl.BlockSpec((SEQ, w), lambda b: (b, 0))
    state_dims = CTX_STATE_DIMS
    first_layer = prev_state is None
    if first_layer:
        assert layer == 0
        state_spec = lambda d: pl.BlockSpec((None, DEPTH) + d, lambda b: (b,) + (0,) * (len(d) + 1))
    else:
        state_spec = lambda d: pl.BlockSpec((None, None) + d, lambda b: (b, layer) + (0,) * len(d))
    prev = () if first_layer else tuple(prev_state)
    n_in = 7
    return pl.pallas_call(
        functools.partial(_ctx_mixers_kernel, first_layer=first_layer),
        grid=(BATCH,),
        in_specs=[row_spec(D_MODEL), _mod_spec(layer, lambda b: CTX_MOD_ROW),
                  _layer_spec(w_in_t, layer, single_buffer=True), _layer_spec(b_in, layer),
                  _layer_spec(fb_row, layer), _layer_spec(norm_g, layer),
                  _layer_spec(sink_rows, layer)]
                 + [pl.BlockSpec(memory_space=pl.ANY)] * len(prev),
        out_specs=[row_spec(ML_W), row_spec(SWA_W), row_spec(NAT_W)]
                  + [state_spec(d) for d in state_dims],
        out_shape=[jax.ShapeDtypeStruct((rows, ML_W), BF16),
                   jax.ShapeDtypeStruct((rows, SWA_W), BF16),
                   jax.ShapeDtypeStruct((rows, NAT_W), BF16)]
                  + [jax.ShapeDtypeStruct((BATCH, DEPTH) + d, F32) for d in state_dims],
        input_output_aliases={n_in + i: 3 + i for i in range(len(prev))},
        scratch_shapes=[pltpu.VMEM((Z_W, D_MODEL), BF16)]
                       + [pltpu.VMEM((SEQ, w), dt) for w, dt in zip(Z_GROUPS, CTX_Z_DTYPES)],
        compiler_params=_params(1),
        name="ctx_mixers",
    )(x, mods, w_in_t, b_in, fb_row, norm_g, sink_rows, *prev)


def _lat_mlstm_kernel(zqk_ref, zvo_ref, zg_ref, fb_ref, ng_ref, c0_ref, n0_ref, m0_ref, mml_ref,
                      ht_scr, c_scr, n_scr, m_scr):
    n_chunks = DEC_SEQ // ML_CHUNK
    c_scr[...] = c0_ref[...]
    n_scr[...] = n0_ref[...]
    m_scr[...] = jnp.broadcast_to(m0_ref[...], m_scr.shape)

    def body(i, carry):
        for direction in range(2):
            chunk = i if direction == 0 else n_chunks - 1 - i
            rows = pl.ds(pl.multiple_of(chunk * ML_CHUNK, ML_CHUNK), ML_CHUNK)
            base = direction * ML_HEADS
            state = [(c_scr[base + hd], n_scr[base + hd:base + hd + 1, :],
                      m_scr[base + hd:base + hd + 1, 0:1]) for hd in range(ML_HEADS)]
            ht, new_state = _mlstm_chunk(zqk_ref, zvo_ref, zg_ref, fb_ref, rows, ML_CHUNK,
                                         direction, state)
            ht_scr[direction, chunk] = ht
            for hd in range(ML_HEADS):
                _store_state(c_scr, n_scr, m_scr, base + hd, *new_state[hd])
        return carry

    lax.fori_loop(0, n_chunks, body, 0)
    for chunk in range(n_chunks):
        rows = slice(chunk * ML_CHUNK, (chunk + 1) * ML_CHUNK)
        mml_ref[rows, :] = _mlstm_finish(ht_scr[0, chunk] + ht_scr[1, chunk],
                                         zvo_ref[rows, ML_W:2 * ML_W],
                                         ng_ref[...]).astype(BF16)


def _lat_mlstm_call(z_qk, z_vo, z_g, fb_row, norm_g, state_c, state_n, state_m, layer):
    n_st = 2 * ML_HEADS
    n_chunks = DEC_SEQ // ML_CHUNK
    row_spec = lambda w: pl.BlockSpec((DEC_SEQ, w), lambda b: (b, 0))
    return pl.pallas_call(
        _lat_mlstm_kernel,
        grid=(DEC_BATCH,),
        in_specs=[row_spec(Z_QK_W), row_spec(Z_VO_W), row_spec(Z_G_W),
                  _layer_spec(fb_row, layer), _layer_spec(norm_g, layer),
                  pl.BlockSpec((None, None, n_st, D_HEAD, D_HEAD), lambda b: (b, layer, 0, 0, 0)),
                  pl.BlockSpec((None, None, n_st, D_HEAD), lambda b: (b, layer, 0, 0)),
                  pl.BlockSpec((None, None, n_st, 1), lambda b: (b, layer, 0, 0))],
        out_specs=row_spec(ML_W),
        out_shape=jax.ShapeDtypeStruct((DEC_BATCH * DEC_SEQ, ML_W), BF16),
        scratch_shapes=[pltpu.VMEM((2, n_chunks, ML_W, ML_CHUNK), F32),
                        pltpu.VMEM((n_st, D_HEAD, D_HEAD), F32), pltpu.VMEM((n_st, D_HEAD), F32),
                        pltpu.VMEM((n_st, LANES), F32)],
        compiler_params=_params(1),
        name="lat_mlstm",
    )(z_qk, z_vo, z_g, fb_row, norm_g, state_c, state_n, state_m)


def _lat_swa_kernel(zswa_ref, ckt_ref, cvt_ref, sink_ref, mswa_ref,
                    vt_scr, ot_scr, bias_scr, ck_scr, cvt_scr, st_scr, m_scr):
    vt_scr[...] = zswa_ref[:, SWA_W + SWA_KV_W:SWA_W + 2 * SWA_KV_W].T

    bl = SWA_BLOCK
    nb = DEC_SEQ // bl
    n_q = SWA_GROUP * bl
    kj = lax.broadcasted_iota(jnp.int32, (3 * bl, n_q), 0)
    qi = lax.broadcasted_iota(jnp.int32, (3 * bl, n_q), 1) & (bl - 1)
    keep = ((kj >= qi) & (kj < 2 * bl)) | ((kj >= 2 * bl) & (kj - 2 * bl <= qi))
    bias_scr[...] = jnp.where(keep, 0.0, NEG_INF)
    ck_scr[...] = jnp.concatenate([ckt_ref[kv] for kv in range(SWA_KV_HEADS)],
                                  axis=0).astype(BF16).T
    for kv in range(SWA_KV_HEADS):
        cvt_scr[kv] = _with_ones(cvt_ref[kv])

    def band(blk):
        lo_b, hi_b = max(blk - 1, 0), min(blk + 2, nb)
        return slice(lo_b * bl, hi_b * bl), slice((lo_b - blk + 1) * bl, (hi_b - blk + 1) * bl)

    def score_stage(kv, blk, slot):
        head = slice(kv * D_HEAD, (kv + 1) * D_HEAD)
        keys, bias_rows = band(blk)
        n_band = keys.stop - keys.start
        q = jnp.concatenate(
            [zswa_ref[blk * bl:(blk + 1) * bl,
                      (kv * SWA_GROUP + g) * D_HEAD:(kv * SWA_GROUP + g + 1) * D_HEAD]
             for g in range(SWA_GROUP)], axis=0)
        k_band = zswa_ref[keys, SWA_W + kv * D_HEAD:SWA_W + (kv + 1) * D_HEAD]
        s_ctx = _dot_nt(ck_scr[:, head], q)
        s_band = _dot_nt(k_band, q) + bias_scr[bias_rows, :]
        st_scr[slot, 0:PAST_LEN, :] = s_ctx
        st_scr[slot, PAST_LEN:PAST_LEN + n_band, :] = s_band
        m = jnp.maximum(jnp.max(s_ctx, axis=0, keepdims=True),
                        jnp.max(s_band, axis=0, keepdims=True))
        m_scr[slot] = jnp.broadcast_to(jnp.maximum(m, sink_ref[kv:kv + 1, :]), m_scr.shape[1:])

    def value_stage(kv, blk, slot):
        head = slice(kv * D_HEAD, (kv + 1) * D_HEAD)
        keys, _ = band(blk)
        n_band = keys.stop - keys.start
        m = m_scr[slot, 0:1, :]
        p_ctx = jnp.exp2(st_scr[slot, 0:PAST_LEN, :] - m).astype(BF16)
        p_band = jnp.exp2(st_scr[slot, PAST_LEN:PAST_LEN + n_band, :] - m).astype(BF16)
        ot = (jnp.dot(cvt_scr[kv], p_ctx, preferred_element_type=F32)
              + jnp.dot(_with_ones(vt_scr[head, keys]), p_band, preferred_element_type=F32))
        den = ot[D_HEAD:D_HEAD + 1, :] + jnp.exp2(sink_ref[kv:kv + 1, :] - m)
        o = (ot[0:D_HEAD, :] * (1.0 / den)).astype(BF16)
        for g in range(SWA_GROUP):
            hq = kv * SWA_GROUP + g
            ot_scr[hq * D_HEAD:(hq + 1) * D_HEAD, blk * bl:(blk + 1) * bl] = o[:, g * bl:(g + 1) * bl]

    problems = [(kv, blk) for kv in range(SWA_KV_HEADS) for blk in range(nb)]
    for i in range(len(problems) + 1):
        def stage(i=i):
            if i > 0:
                value_stage(*problems[i - 1], (i - 1) % 2)
            if i < len(problems):
                score_stage(*problems[i], i % 2)
        stage()
    for p in range(SWA_W // LANES):
        mswa_ref[:, p * LANES:(p + 1) * LANES] = ot_scr[p * LANES:(p + 1) * LANES, :].T


def _lat_swa_call(z_swa, cache_k, cache_v, sink_rows, layer):
    row_spec = lambda w: pl.BlockSpec((DEC_SEQ, w), lambda b: (b, 0))
    cache_spec = pl.BlockSpec((None, None, SWA_KV_HEADS, D_HEAD, PAST_LEN),
                              lambda b: (b, layer, 0, 0, 0))
    n_q = SWA_GROUP * SWA_BLOCK
    return pl.pallas_call(
        _lat_swa_kernel,
        grid=(DEC_BATCH,),
        in_specs=[row_spec(Z_SWA_W), cache_spec, cache_spec, _layer_spec(sink_rows, layer)],
        out_specs=row_spec(SWA_W),
        out_shape=jax.ShapeDtypeStruct((DEC_BATCH * DEC_SEQ, SWA_W), BF16),
        scratch_shapes=[pltpu.VMEM((SWA_KV_W, DEC_SEQ), BF16), pltpu.VMEM((SWA_W, DEC_SEQ), BF16),
                        pltpu.VMEM((3 * SWA_BLOCK, n_q), F32),
                        pltpu.VMEM((PAST_LEN, SWA_KV_W), BF16),
                        pltpu.VMEM((SWA_KV_HEADS, D_HEAD + ONES_ROWS, PAST_LEN), BF16),
                        pltpu.VMEM((2, PAST_LEN + 3 * SWA_BLOCK, n_q), F32),
                        pltpu.VMEM((2, SUBLANES, n_q), F32)],
        compiler_params=_params(1),
        name="lat_swa",
    )(z_swa, cache_k, cache_v, sink_rows)


NAT_ROWS = DEC_SEQ // GRID_W
NAT_GROUP_ROWS = 4
NAT_GROUP_Q = NAT_GROUP_ROWS * GRID_W
NAT_RPB_ROWS = 2 * NAT_KH


def _nat_row_start(r):
    return min(max(r - NAT_KH // 2, 0), NAT_ROWS - NAT_KH)


def _nat_groups():
    groups, off = [], 0
    for g in range(NAT_ROWS // NAT_GROUP_ROWS):
        w0 = _nat_row_start(g * NAT_GROUP_ROWS)
        w1 = _nat_row_start((g + 1) * NAT_GROUP_ROWS - 1) + NAT_KH
        n_rows = w1 - w0 + (w1 - w0) % 2
        w0 = min(w0, NAT_ROWS - n_rows)
        groups.append((w0, n_rows, off))
        off += n_rows * GRID_W
    return groups, off


NAT_GROUPS, NAT_BIAS_KEYS = _nat_groups()


def _nat_build_bias(rpb_ref, bias_scr):
    shape = (GRID_W, LANES)
    q = lax.broadcasted_iota(jnp.int32, shape, 0)
    lane = lax.broadcasted_iota(jnp.int32, shape, 1)
    kc = lane & (GRID_W - 1)
    cs = jnp.clip(q - NAT_KW // 2, 0, GRID_W - NAT_KW)
    ok = (kc >= cs) & (kc < cs + NAT_KW)
    ok_lo = ok & (lane < GRID_W)
    ok_hi = ok & (lane >= GRID_W)
    neg_tile = jnp.full(shape, NEG_INF, F32)
    for hd in range(2):
        lo, hi = [], []
        for ro in range(2 * NAT_KH - 1):
            x = jnp.broadcast_to(rpb_ref[hd, ro:ro + 1, :], shape)
            lo.append(jnp.where(ok_lo, pltpu.roll(x, 0, 1, stride=1, stride_axis=0), NEG_INF))
            hi.append(jnp.where(ok_hi, pltpu.roll(x, GRID_W, 1, stride=1, stride_axis=0), NEG_INF))
        for g, (w0, n_rows, off) in enumerate(NAT_GROUPS):
            for jj in range(n_rows // 2):
                kra, krb = w0 + 2 * jj, w0 + 2 * jj + 1
                tiles = []
                for rr in range(NAT_GROUP_ROWS):
                    r = g * NAT_GROUP_ROWS + rr
                    r0 = _nat_row_start(r)
                    parts = []
                    if r0 <= kra < r0 + NAT_KH:
                        parts.append(lo[kra - r + NAT_KH - 1])
                    if r0 <= krb < r0 + NAT_KH:
                        parts.append(hi[krb - r + NAT_KH - 1])
                    tiles.append(neg_tile if not parts else (
                        parts[0] if len(parts) == 1 else jnp.maximum(parts[0], parts[1])))
                bias_scr[hd, off + jj * LANES:off + (jj + 1) * LANES, :] = (
                    jnp.concatenate(tiles, axis=0).T)


def _lat_nat_kernel(q_ref, k_ref, v_ref, ckt_ref, cvt_ref, rpb_ref, mnat_ref, bias_scr):
    @pl.when(pl.program_id(1) == 0)
    def _():
        _nat_build_bias(rpb_ref, bias_scr)

    vt = v_ref[...].astype(BF16).T
    ck2 = jnp.concatenate([ckt_ref[0], ckt_ref[1]], axis=0).astype(BF16).T
    ots = []
    for hd in range(2):
        head = slice(hd * D_HEAD, (hd + 1) * D_HEAD)
        ck = ck2[:, head]
        cvt_ones = _with_ones(cvt_ref[hd])
        ot_groups = []
        for g, (w0, n_rows, off) in enumerate(NAT_GROUPS):
            keys = slice(w0 * GRID_W, (w0 + n_rows) * GRID_W)
            parts = [(k_ref[keys, head].astype(BF16), _with_ones(vt[head, keys]),
                      bias_scr[hd, off:off + n_rows * GRID_W, :]),
                     (ck, cvt_ones, None)]
            q = q_ref[g * NAT_GROUP_Q:(g + 1) * NAT_GROUP_Q, head].astype(BF16)
            ot_groups.append(_attend_t(q, parts))
        ots.append(jnp.concatenate(ot_groups, axis=1))
    mnat_ref[...] = jnp.concatenate(ots, axis=0).astype(BF16).T


def _lat_nat_call(z_nat, cache_k, cache_v, rpb_lanes, layer):
    n_pair = NAT_HEADS // 2
    col_spec = lambda off: pl.BlockSpec((DEC_SEQ, LANES), lambda p, b: (b, off + p))
    cache_spec = pl.BlockSpec((None, None, 2, D_HEAD, PAST_LEN), lambda p, b: (b, layer, p, 0, 0))
    return pl.pallas_call(
        _lat_nat_kernel,
        grid=(n_pair, DEC_BATCH),
        in_specs=[col_spec(0), col_spec(n_pair), col_spec(2 * n_pair), cache_spec, cache_spec,
                  pl.BlockSpec((None, 2, NAT_RPB_ROWS, LANES), lambda p, b: (layer, p, 0, 0))],
        out_specs=pl.BlockSpec((DEC_SEQ, LANES), lambda p, b: (b, p)),
        out_shape=jax.ShapeDtypeStruct((DEC_BATCH * DEC_SEQ, NAT_W), BF16),
        scratch_shapes=[pltpu.VMEM((2, NAT_BIAS_KEYS, NAT_GROUP_Q), F32)],
        compiler_params=_params(2),
        name="lat_nat",
    )(z_nat, z_nat, z_nat, cache_k, cache_v, rpb_lanes)


def _nat_rpb_lanes(rpb):
    n_off = 2 * NAT_KW - 1
    padded = jnp.concatenate(
        [rpb.astype(F32), jnp.zeros(rpb.shape[:-1] + (LANES - n_off,), F32)], axis=-1)
    rolled = jnp.roll(padded, -(NAT_KW - 1), axis=-1)
    pad_rows = jnp.zeros(rpb.shape[:-2] + (NAT_RPB_ROWS - rpb.shape[-2], LANES), F32)
    return jnp.concatenate([rolled, pad_rows], axis=-2)


def _rope_tables():
    t = np.arange(DEC_SEQ)[:, None]
    d = np.arange(LANES)[None, :] % D_HEAD
    pos = np.where(d < D_HEAD // 2, t // GRID_W, t % GRID_W).astype(np.float64)
    quarter = D_HEAD // 4
    freq = ROPE_BASE ** (-(d % quarter).astype(np.float64) / quarter)
    ang = (pos.astype(np.float32) * freq.astype(np.float32)).astype(np.float32)
    sign = np.where((d % (2 * quarter)) < quarter, -1.0, 1.0)
    return (jnp.asarray(np.cos(ang), dtype=F32), jnp.asarray(np.sin(ang) * sign, dtype=F32))


def _permute_in_columns(a):
    g0 = Z_ML_W
    s0 = g0 + N_GATES
    pad = jnp.zeros(a.shape[:-1] + (Z_G_W - N_GATES,), a.dtype)
    return jnp.concatenate([a[..., :g0], a[..., s0:], a[..., g0:s0], pad], axis=-1)


def kernel(x_prompt, x_sample, cache_swa_k, cache_swa_v, cache_nat_k, cache_nat_v, state_mlstm_C,
           state_mlstm_n, state_mlstm_m, c, c_ctx, w_ada, b_ada, w_in, b_in, mlstm_fbias,
           mlstm_norm_g, swa_sink, nat_rpb, w_out, ln1_g, ln1_b, w_mlp1, w_mlp2, ln2_g, ln2_b):
    xp = x_prompt.reshape(BATCH * SEQ, D_MODEL)
    xs = x_sample.reshape(DEC_BATCH * DEC_SEQ, D_MODEL)
    cvec = jnp.concatenate(
        [c, c_ctx[None, :], jnp.zeros((MOD_ROWS - DEC_BATCH - 1, D_MODEL), F32)], axis=0)
    mods = _mods_call(cvec, w_ada, b_ada).reshape(DEPTH * MOD_ROWS, 6, D_MODEL)

    w_in_t = jnp.swapaxes(w_in, 1, 2)
    out_scale = np.ones((Z_W,), np.float32)
    out_scale[ML_W:2 * ML_W] = ATT_SCALE
    out_scale[Z_ML_W:Z_ML_W + SWA_W] = Q_SCALE2
    out_scale[Z_ML_W + Z_SWA_W:Z_ML_W + Z_SWA_W + NAT_W] = Q_SCALE2
    b_in_p = jnp.stack([jnp.broadcast_to(out_scale, (DEPTH, Z_W)),
                        _permute_in_columns(b_in) * out_scale], axis=1)
    cache_swa_k, cache_swa_v, cache_nat_k, cache_nat_v = (
        jnp.swapaxes(a, -1, -2) for a in (cache_swa_k, cache_swa_v, cache_nat_k, cache_nat_v))
    vec = lambda a: a.reshape(DEPTH, 1, D_MODEL)
    tail_w = (w_out.astype(BF16), vec(ln1_g), vec(ln1_b), w_mlp1.astype(BF16),
              w_mlp2.astype(BF16), vec(ln2_g), vec(ln2_b))
    cos_t, sin_t = _rope_tables()
    rpb_lanes = _nat_rpb_lanes(nat_rpb) * LOG2E
    swa_sink = swa_sink * LOG2E
    fb_rows = jnp.zeros((DEPTH, 1, Z_G_W), F32)
    fb_rows = fb_rows.at[:, 0, ML_HEADS:2 * ML_HEADS].set(mlstm_fbias[:, 0])
    fb_rows = fb_rows.at[:, 0, 3 * ML_HEADS:4 * ML_HEADS].set(mlstm_fbias[:, 1])
    norm_g = mlstm_norm_g.reshape(DEPTH, 1, ML_W)
    sink_ctx = jnp.repeat(swa_sink, SEQ, axis=1).reshape(DEPTH, SWA_KV_HEADS, SWA_GROUP * SEQ)
    sink_lat = jnp.repeat(swa_sink, SWA_BLOCK, axis=1).reshape(
        DEPTH, SWA_KV_HEADS, SWA_GROUP * SWA_BLOCK)

    n_st = 2 * ML_HEADS
    state_c = state_mlstm_C.reshape(DEC_BATCH, DEPTH, n_st, D_HEAD, D_HEAD)
    state_n = state_mlstm_n.reshape(DEC_BATCH, DEPTH, n_st, D_HEAD)
    state_m = state_mlstm_m.reshape(DEC_BATCH, DEPTH, n_st, 1)

    ctx_row = lambda tile: (lambda i: CTX_MOD_ROW)
    lat_row = lambda tile: (lambda i: i // (DEC_SEQ // tile))

    ctx_state = None
    for l in range(DEPTH):
        m_ml, m_swa, m_nat, *ctx_state = _ctx_mixers_call(
            xp, mods, w_in_t, b_in_p, fb_rows, norm_g, sink_ctx, l, ctx_state)
        xp = _tail_call(xp, m_ml, m_swa, m_nat, mods, ctx_row(TAIL_TILE), tail_w, l)

        z_qk, z_vo, z_swa, z_nat, z_g = _lat_inproj_call(xs, mods, w_in_t, b_in_p, cos_t, sin_t, l)
        m_ml = _lat_mlstm_call(z_qk, z_vo, z_g, fb_rows, norm_g, state_c, state_n, state_m, l)
        m_swa = _lat_swa_call(z_swa, cache_swa_k, cache_swa_v, sink_lat, l)
        m_nat = _lat_nat_call(z_nat, cache_nat_k, cache_nat_v, rpb_lanes, l)
        xs = _tail_call(xs, m_ml, m_swa, m_nat, mods, lat_row(TAIL_TILE), tail_w, l)

    k_swa, v_swa, k_nat, v_nat, st_c, st_n, st_m = ctx_state
    return (xp.reshape(BATCH, SEQ, D_MODEL), xs.reshape(DEC_BATCH, DEC_SEQ, D_MODEL),
            k_swa, v_swa, k_nat, v_nat,
            st_c.reshape(BATCH, DEPTH, 2, ML_HEADS, D_HEAD, D_HEAD),
            st_n.reshape(BATCH, DEPTH, 2, ML_HEADS, D_HEAD),
            st_m[..., 0].reshape(BATCH, DEPTH, 2, ML_HEADS))
```

```python
import functools

import numpy as np
import jax
import jax.numpy as jnp
from jax import lax
from jax.experimental import pallas as pl
from jax.experimental.pallas import tpu as pltpu

D_MODEL = 1024
BATCH = 16
SEQ = 256
DEPTH = 4
DEC_BATCH = 8
DEC_SEQ = 1024
PAST_LEN = 512
GRID_W = 64
D_HEAD = 64
ML_HEADS = 4
SWA_HEADS = 6
SWA_KV_HEADS = 2
SWA_GROUP = SWA_HEADS // SWA_KV_HEADS
NAT_HEADS = 6
ML_W = ML_HEADS * D_HEAD
SWA_W = SWA_HEADS * D_HEAD
SWA_KV_W = SWA_KV_HEADS * D_HEAD
NAT_W = NAT_HEADS * D_HEAD
N_GATES = 4 * ML_HEADS
IN_DIM = 4 * ML_W + N_GATES + SWA_W + 2 * SWA_KV_W + 3 * NAT_W
D_FF = 4 * D_MODEL
SWA_WINDOW = 128
SWA_BLOCK = 128
NAT_KH = 8
NAT_KW = 16
ROPE_BASE = 10000.0
LN_EPS = 1e-5
DN_ALPHA = (2 * DEPTH) ** 0.25
ATT_SCALE = D_HEAD ** -0.5
LOG2E = 1.4426950408889634
Q_SCALE2 = ATT_SCALE * LOG2E

LANES = 128
SUBLANES = 8
Z_ML_W = 4 * ML_W
Z_SWA_W = SWA_W + 2 * SWA_KV_W
Z_NAT_W = 3 * NAT_W
Z_G_W = LANES
Z_W = Z_ML_W + Z_SWA_W + Z_NAT_W + Z_G_W
Z_QK_W = 2 * ML_W
Z_VO_W = 2 * ML_W
Z_GROUPS = (Z_QK_W, Z_VO_W, Z_SWA_W, Z_NAT_W, Z_G_W)
Z_SWA_GROUP = 2
MOD_ROWS = 16
CTX_MOD_ROW = DEC_BATCH
ROW_TILE = 512
TAIL_TILE = 1024
FF_CHUNK = 1024
ML_CHUNK = 256
VMEM_LIMIT = 56 * 1024 * 1024

BF16 = jnp.bfloat16
F32 = jnp.float32
NEG_INF = float("-inf")
LAT_Z_DTYPES = (BF16, F32, BF16, BF16, F32)
CTX_Z_DTYPES = (BF16, F32, F32, F32, F32)


def _dot(a, b):
    return jnp.dot(a.astype(BF16), b.astype(BF16), preferred_element_type=F32)


def _dot_nt(a, b):
    return lax.dot_general(a.astype(BF16), b.astype(BF16), (((1,), (1,)), ((), ())),
                           preferred_element_type=F32)


def _dot_split3(tri, x):
    hi = x.astype(BF16)
    r1 = x - hi.astype(F32)
    mid = r1.astype(BF16)
    lo = (r1 - mid.astype(F32)).astype(BF16)
    return (jnp.dot(tri, hi, preferred_element_type=F32)
            + jnp.dot(tri, mid, preferred_element_type=F32)
            + jnp.dot(tri, lo, preferred_element_type=F32))


def _sigmoid(x):
    return 1.0 / (1.0 + jnp.exp(-x))


def _log_sigmoid(x):
    return jnp.minimum(x, 0.0) - jnp.log1p(jnp.exp(-jnp.abs(x)))


def _layer_norm(x, g, b):
    mu = jnp.mean(x, axis=-1, keepdims=True)
    xc = x - mu
    var = jnp.mean(xc * xc, axis=-1, keepdims=True)
    return xc * lax.rsqrt(var + LN_EPS) * g + b


def _params(n_axes=1, flags=None):
    return pltpu.CompilerParams(dimension_semantics=("arbitrary",) * n_axes,
                                vmem_limit_bytes=VMEM_LIMIT, flags=flags)


def _layer_spec(a, layer, single_buffer=False):
    kwargs = {"pipeline_mode": pl.Buffered(1)} if single_buffer else {}
    return pl.BlockSpec((None,) + a.shape[1:], lambda *_: (layer,) + (0,) * (a.ndim - 1), **kwargs)


def _mods_kernel(c_ref, w_ref, b_ref, o_ref):
    c = c_ref[...]
    o_ref[...] = _dot(c * _sigmoid(c), w_ref[...]) + b_ref[...]


def _mods_call(cvec, w_ada, b_ada):
    n_col = 6
    return pl.pallas_call(
        _mods_kernel,
        grid=(DEPTH, n_col),
        in_specs=[
            pl.BlockSpec((MOD_ROWS, D_MODEL), lambda l, j: (0, 0)),
            pl.BlockSpec((None, D_MODEL, D_MODEL), lambda l, j: (l, 0, j)),
            pl.BlockSpec((None, 1, D_MODEL), lambda l, j: (l, 0, j)),
        ],
        out_specs=pl.BlockSpec((None, MOD_ROWS, D_MODEL), lambda l, j: (l, 0, j)),
        out_shape=jax.ShapeDtypeStruct((DEPTH, MOD_ROWS, 6 * D_MODEL), F32),
        compiler_params=_params(2),
        name="mods",
    )(cvec, w_ada, b_ada.reshape(DEPTH, 1, 6 * D_MODEL))


def _rope(x, cos, sin_signed):
    lane = lax.broadcasted_iota(jnp.int32, x.shape, 1)
    first = (lane & 31) < 16
    partner = jnp.where(first, pltpu.roll(x, LANES - 16, 1), pltpu.roll(x, 16, 1))
    return x * cos + partner * sin_signed


def _inproj_kernel(x_ref, mod_ref, wt_ref, b_ref, *rest, rope=False):
    if rope:
        cos_ref, sin_ref, *rest = rest
    *out_refs, w_scr = rest

    @pl.when(pl.program_id(0) == 0)
    def _():
        rest = IN_DIM - Z_ML_W - N_GATES
        w_scr[0:Z_ML_W, :] = wt_ref[0:Z_ML_W, :].astype(BF16)
        w_scr[Z_ML_W:Z_ML_W + rest, :] = wt_ref[Z_ML_W + N_GATES:IN_DIM, :].astype(BF16)
        w_scr[Z_ML_W + rest:IN_DIM, :] = wt_ref[Z_ML_W:Z_ML_W + N_GATES, :].astype(BF16)
        w_scr[IN_DIM:Z_W, :] = jnp.zeros((Z_W - IN_DIM, D_MODEL), BF16)

    h = (x_ref[...] * (1.0 + mod_ref[1:2, :]) + mod_ref[0:1, :]).astype(BF16)
    o = 0
    for ref, w in zip(out_refs, Z_GROUPS):
        z = _dot_nt(h, w_scr[o:o + w, :]) * b_ref[0:1, o:o + w] + b_ref[1:2, o:o + w]
        if rope and ref is out_refs[Z_SWA_GROUP]:
            z = jnp.concatenate(
                [_rope(z[:, j * LANES:(j + 1) * LANES], cos_ref[...], sin_ref[...])
                 for j in range((SWA_W + SWA_KV_W) // LANES)] + [z[:, SWA_W + SWA_KV_W:]], axis=1)
        ref[...] = z.astype(ref.dtype)
        o += w


def _mod_spec(layer, mod_row_of_tile):
    return pl.BlockSpec((None, 6, D_MODEL),
                        lambda i: (layer * MOD_ROWS + mod_row_of_tile(i), 0, 0))


def _lat_inproj_call(x, mods, w_in_t, b_in, cos_t, sin_t, layer):
    rows = x.shape[0]
    tiles_per_seq = DEC_SEQ // ROW_TILE
    table_spec = pl.BlockSpec((ROW_TILE, LANES), lambda i: (i % tiles_per_seq, 0))
    return pl.pallas_call(
        functools.partial(_inproj_kernel, rope=True),
        grid=(rows // ROW_TILE,),
        in_specs=[
            pl.BlockSpec((ROW_TILE, D_MODEL), lambda i: (i, 0)),
            _mod_spec(layer, lambda i: i // tiles_per_seq),
            _layer_spec(w_in_t, layer, single_buffer=True),
            _layer_spec(b_in, layer),
            table_spec, table_spec,
        ],
        out_specs=[pl.BlockSpec((ROW_TILE, w), lambda i: (i, 0)) for w in Z_GROUPS],
        out_shape=[jax.ShapeDtypeStruct((rows, w), dt) for w, dt in zip(Z_GROUPS, LAT_Z_DTYPES)],
        scratch_shapes=[pltpu.VMEM((Z_W, D_MODEL), BF16)],
        compiler_params=_params(1),
        name="inproj",
    )(x, mods, w_in_t, b_in, cos_t, sin_t)


def _tail_kernel(x_ref, mml_ref, mswa_ref, mnat_ref, mod_ref, wout_ref, ln1g_ref, ln1b_ref,
                 w1_ref, w2_ref, ln2g_ref, ln2b_ref, o_ref):
    g_a, sh_m, sc_m, g_m = (mod_ref[i:i + 1, :] for i in (2, 3, 4, 5))
    mix = jnp.concatenate([mml_ref[...], mswa_ref[...], mnat_ref[...]], axis=1)
    proj = jnp.dot(mix, wout_ref[...], preferred_element_type=F32)
    y = _layer_norm(DN_ALPHA * x_ref[...] + g_a * proj, ln1g_ref[...], ln1b_ref[...])
    h = (y * (1.0 + sc_m) + sh_m).astype(BF16)
    mlp = None
    for c in range(D_FF // FF_CHUNK):
        cols = slice(c * FF_CHUNK, (c + 1) * FF_CHUNK)
        f = jnp.maximum(_dot_nt(h, w1_ref[cols, :]), 0.0)
        part = _dot_nt(f * f, w2_ref[:, cols])
        mlp = part if mlp is None else mlp + part
    o_ref[...] = _layer_norm(DN_ALPHA * y + g_m * mlp, ln2g_ref[...], ln2b_ref[...])


def _tail_call(x, mix_ml, mix_swa, mix_nat, mods, mod_row_of_tile, consts, layer):
    rows = x.shape[0]
    row_spec = lambda w: pl.BlockSpec((TAIL_TILE, w), lambda i: (i, 0))
    return pl.pallas_call(
        _tail_kernel,
        grid=(rows // TAIL_TILE,),
        in_specs=[row_spec(D_MODEL), row_spec(ML_W), row_spec(SWA_W), row_spec(NAT_W),
                  _mod_spec(layer, mod_row_of_tile)]
                 + [_layer_spec(a, layer, single_buffer=True) for a in consts],
        out_specs=row_spec(D_MODEL),
        out_shape=jax.ShapeDtypeStruct((rows, D_MODEL), F32),
        compiler_params=_params(1),
        name="tail",
    )(x, mix_ml, mix_swa, mix_nat, mods, *consts)


def _mlstm_direction(q, k, vt, a_col, b_row, li_row, neg, last, c_st, n_st, m_st):
    a_m = a_col + neg
    c_row = jnp.maximum(m_st, jnp.max(a_m, axis=0, keepdims=True))
    decay = jnp.exp(a_m - c_row)
    wprev = jnp.exp(m_st - c_row)
    mt = b_row + c_row
    st = _dot_nt(k, q) * decay
    den = jnp.sum(st, axis=0, keepdims=True)
    num = _dot(vt, st)
    if c_st is not None:
        cn = _dot_nt(jnp.concatenate([c_st, jnp.broadcast_to(n_st, (SUBLANES, D_HEAD))], axis=0),
                     q)
        num = num + wprev * cn[0:D_HEAD, :]
        den = den + wprev * cn[D_HEAD:D_HEAD + 1, :]
    ht = num * (1.0 / jnp.maximum(jnp.abs(den), jnp.exp(-mt)))
    m_new = mt[:, last:last + 1]
    ws = jnp.exp(b_row[:, last:last + 1] - b_row + li_row - m_new)
    upd = _dot(jnp.concatenate([vt * ws, jnp.broadcast_to(ws, (SUBLANES, ws.shape[1]))], axis=0),
               k)
    c_new = upd[0:D_HEAD, :]
    n_new = upd[D_HEAD:D_HEAD + 1, :]
    if c_st is not None:
        wc = wprev[:, last:last + 1]
        c_new = c_new + wc * c_st
        n_new = n_new + wc * n_st
    return ht, c_new, n_new, m_new


def _mlstm_chunk(zqk_ref, zvo_ref, g_ref, fb_ref, rows, n, direction, state):
    g = g_ref[rows, :]
    ls = _log_sigmoid(g + fb_ref[...])
    r = lax.broadcasted_iota(jnp.int32, (n, n), 0)
    c = lax.broadcasted_iota(jnp.int32, (n, n), 1)
    if direction == 0:
        tri, neg, last = c <= r, jnp.where(r <= c, 0.0, NEG_INF), n - 1
    else:
        tri, neg, last = c >= r, jnp.where(r >= c, 0.0, NEG_INF), 0
    b = _dot_split3(tri.astype(BF16), ls)
    a = pltpu.roll(g, ML_HEADS, 1) - b
    bt = b.T
    gt = g.T
    i_col = 2 * ML_HEADS * direction
    f_col = i_col + ML_HEADS
    hts, new_state = [], []
    for pair in range(ML_HEADS // 2):
        vt2 = zvo_ref[rows, pair * LANES:(pair + 1) * LANES].T
        for sub in range(2):
            hd = 2 * pair + sub
            feat = slice(sub * D_HEAD, (sub + 1) * D_HEAD)
            q = zqk_ref[rows, hd * D_HEAD:(hd + 1) * D_HEAD]
            k = zqk_ref[rows, ML_W + hd * D_HEAD:ML_W + (hd + 1) * D_HEAD]
            ht, c_new, n_new, m_new = _mlstm_direction(
                q, k, vt2[feat, :],
                a[:, f_col + hd:f_col + hd + 1], bt[f_col + hd:f_col + hd + 1, :],
                gt[i_col + hd:i_col + hd + 1, :], neg, last, *state[hd])
            hts.append(ht)
            new_state.append((c_new, n_new, m_new))
    return jnp.concatenate(hts, axis=0), new_state


def _mlstm_finish(ht, o_gate, norm_g):
    outs = []
    for hd in range(ML_HEADS):
        x = ht[hd * D_HEAD:(hd + 1) * D_HEAD, :]
        mu = jnp.mean(x, axis=0, keepdims=True)
        xc = x - mu
        var = jnp.mean(xc * xc, axis=0, keepdims=True)
        outs.append(xc * lax.rsqrt(var + LN_EPS))
    hn = jnp.concatenate(outs, axis=0).T
    return hn * norm_g * _sigmoid(o_gate)


def _store_state(c_ref, n_ref, m_ref, idx, c_new, n_new, m_new):
    c_ref[idx] = c_new
    n_ref[idx:idx + 1, :] = n_new
    m_ref[idx:idx + 1, :] = jnp.broadcast_to(m_new, (1, LANES))


ONES_ROWS = 16


def _with_ones(vt):
    ones = jnp.ones((ONES_ROWS, vt.shape[1]), BF16)
    return jnp.concatenate([vt.astype(BF16), ones], axis=0)


def _attend_t(q, parts, sink_row=None):
    sts = []
    for k, _, bias_t in parts:
        st = _dot_nt(k, q)
        sts.append(st if bias_t is None else st + bias_t)
    m = None
    for st in sts:
        mx = jnp.max(st, axis=0, keepdims=True)
        m = mx if m is None else jnp.maximum(m, mx)
    if sink_row is not None:
        m = jnp.maximum(m, sink_row)
    ot = None
    for st, (_, vt_ones, _) in zip(sts, parts):
        part = jnp.dot(vt_ones, jnp.exp2(st - m).astype(BF16), preferred_element_type=F32)
        ot = part if ot is None else ot + part
    den = ot[D_HEAD:D_HEAD + 1, :]
    if sink_row is not None:
        den = den + jnp.exp2(sink_row - m)
    return ot[0:D_HEAD, :] * (1.0 / den)


def _head_rows(pairs_t, hd):
    return pairs_t[hd // 2][(hd % 2) * D_HEAD:(hd % 2 + 1) * D_HEAD, :]


def _store_heads_t(o_ref, ots):
    for p in range(len(ots) // 2):
        o_ref[:, p * LANES:(p + 1) * LANES] = jnp.concatenate(
            [ots[2 * p], ots[2 * p + 1]], axis=0).astype(o_ref.dtype).T


N_STATE = 2 * ML_HEADS
N_CTX_STATE_OUTS = 7


def _ctx_mixers_kernel(x_ref, mod_ref, wt_ref, b_ref, fb_ref, ng_ref, sink_ref, *rest,
                       first_layer):
    n_scratch = 1 + len(Z_GROUPS)
    w_scr, zqk_ref, zvo_ref, zswa_ref, znat_ref, zg_ref = rest[-n_scratch:]
    mml_ref, mswa_ref, mnat_ref, *state_refs = rest[-(3 + N_CTX_STATE_OUTS) - n_scratch:-n_scratch]
    _inproj_kernel(x_ref, mod_ref, wt_ref, b_ref, zqk_ref, zvo_ref, zswa_ref, znat_ref, zg_ref,
                   w_scr)
    if first_layer:
        for ref in state_refs:
            ref[1:] = jnp.zeros((DEPTH - 1,) + ref.shape[1:], F32)
        state_refs = [ref.at[0] for ref in state_refs]
    sk_ref, sv_ref, nk_ref, nv_ref, c_ref, n_ref, m_ref = state_refs
    rows = slice(0, SEQ)
    zero_state = [(None, jnp.zeros((1, D_HEAD), F32), jnp.zeros((1, 1), F32))] * ML_HEADS
    ht_sum = None
    for direction in range(2):
        ht, new_state = _mlstm_chunk(zqk_ref, zvo_ref, zg_ref, fb_ref, rows, SEQ, direction,
                                     zero_state)
        ht_sum = ht if ht_sum is None else ht_sum + ht
        for hd in range(ML_HEADS):
            _store_state(c_ref, n_ref, m_ref, direction * ML_HEADS + hd, *new_state[hd])
    mml_ref[...] = _mlstm_finish(ht_sum, zvo_ref[:, ML_W:2 * ML_W], ng_ref[...]).astype(BF16)

    scaled_q = lambda ref, hd: ref[:, hd * D_HEAD:(hd + 1) * D_HEAD].astype(BF16)
    vts = [zswa_ref[:, SWA_W + SWA_KV_W:SWA_W + 2 * SWA_KV_W].astype(BF16).T]
    ots = []
    for kv in range(SWA_KV_HEADS):
        k = zswa_ref[:, SWA_W + kv * D_HEAD:SWA_W + (kv + 1) * D_HEAD]
        sk_ref[kv] = k
        sv_ref[kv] = zswa_ref[:, SWA_W + SWA_KV_W + kv * D_HEAD:SWA_W + SWA_KV_W + (kv + 1) * D_HEAD]
        q = jnp.concatenate([scaled_q(zswa_ref, kv * SWA_GROUP + g) for g in range(SWA_GROUP)],
                            axis=0)
        ot = _attend_t(q, [(k.astype(BF16), _with_ones(_head_rows(vts, kv)), None)],
                       sink_ref[kv:kv + 1, :])
        ots += [ot[:, g * SEQ:(g + 1) * SEQ] for g in range(SWA_GROUP)]
    _store_heads_t(mswa_ref, ots)

    vts = [znat_ref[:, 2 * NAT_W + p * LANES:2 * NAT_W + (p + 1) * LANES].astype(BF16).T
           for p in range(NAT_W // LANES)]
    ots = []
    for hd in range(NAT_HEADS):
        k = znat_ref[:, NAT_W + hd * D_HEAD:NAT_W + (hd + 1) * D_HEAD]
        nk_ref[hd] = k
        nv_ref[hd] = znat_ref[:, 2 * NAT_W + hd * D_HEAD:2 * NAT_W + (hd + 1) * D_HEAD]
        ots.append(_attend_t(scaled_q(znat_ref, hd),
                             [(k.astype(BF16), _with_ones(_head_rows(vts, hd)), None)]))
    _store_heads_t(mnat_ref, ots)


CTX_STATE_DIMS = ((SWA_KV_HEADS, SEQ, D_HEAD), (SWA_KV_HEADS, SEQ, D_HEAD),
                  (NAT_HEADS, SEQ, D_HEAD), (NAT_HEADS, SEQ, D_HEAD),
                  (N_STATE, D_HEAD, D_HEAD), (N_STATE, D_HEAD), (N_STATE, LANES))


def _ctx_mixers_call(x, mods, w_in_t, b_in, fb_row, norm_g, sink_rows, layer, prev_state):
    rows = x.shape[0]
    row_spec = lambda w: pl.BlockSpec((SEQ, w), lambda b: (b, 0))
    state_dims = CTX_STATE_DIMS
    first_layer = prev_state is None
    if first_layer:
        assert layer == 0
        state_spec = lambda d: pl.BlockSpec((None, DEPTH) + d, lambda b: (b,) + (0,) * (len(d) + 1))
    else:
        state_spec = lambda d: pl.BlockSpec((None, None) + d, lambda b: (b, layer) + (0,) * len(d))
    prev = () if first_layer else tuple(prev_state)
    n_in = 7
    return pl.pallas_call(
        functools.partial(_ctx_mixers_kernel, first_layer=first_layer),
        grid=(BATCH,),
        in_specs=[row_spec(D_MODEL), _mod_spec(layer, lambda b: CTX_MOD_ROW),
                  _layer_spec(w_in_t, layer, single_buffer=True), _layer_spec(b_in, layer),
                  _layer_spec(fb_row, layer), _layer_spec(norm_g, layer),
                  _layer_spec(sink_rows, layer)]
                 + [pl.BlockSpec(memory_space=pl.ANY)] * len(prev),
        out_specs=[row_spec(ML_W), row_spec(SWA_W), row_spec(NAT_W)]
                  + [state_spec(d) for d in state_dims],
        out_shape=[jax.ShapeDtypeStruct((rows, ML_W), BF16),
                   jax.ShapeDtypeStruct((rows, SWA_W), BF16),
                   jax.ShapeDtypeStruct((rows, NAT_W), BF16)]
                  + [jax.ShapeDtypeStruct((BATCH, DEPTH) + d, F32) for d in state_dims],
        input_output_aliases={n_in + i: 3 + i for i in range(len(prev))},
        scratch_shapes=[pltpu.VMEM((Z_W, D_MODEL), BF16)]
                       + [pltpu.VMEM((SEQ, w), dt) for w, dt in zip(Z_GROUPS, CTX_Z_DTYPES)],
        compiler_params=_params(1),
        name="ctx_mixers",
    )(x, mods, w_in_t, b_in, fb_row, norm_g, sink_rows, *prev)


def _lat_mlstm_kernel(zqk_ref, zvo_ref, zg_ref, fb_ref, ng_ref, c0_ref, n0_ref, m0_ref, mml_ref,
                      ht_scr, c_scr, n_scr, m_scr):
    n_chunks = DEC_SEQ // ML_CHUNK
    c_scr[...] = c0_ref[...]
    n_scr[...] = n0_ref[...]
    m_scr[...] = jnp.broadcast_to(m0_ref[...], m_scr.shape)

    def body(i, carry):
        for direction in range(2):
            chunk = i if direction == 0 else n_chunks - 1 - i
            rows = pl.ds(pl.multiple_of(chunk * ML_CHUNK, ML_CHUNK), ML_CHUNK)
            base = direction * ML_HEADS
            state = [(c_scr[base + hd], n_scr[base + hd:base + hd + 1, :],
                      m_scr[base + hd:base + hd + 1, 0:1]) for hd in range(ML_HEADS)]
            ht, new_state = _mlstm_chunk(zqk_ref, zvo_ref, zg_ref, fb_ref, rows, ML_CHUNK,
                                         direction, state)
            ht_scr[direction, chunk] = ht
            for hd in range(ML_HEADS):
                _store_state(c_scr, n_scr, m_scr, base + hd, *new_state[hd])
        return carry

    lax.fori_loop(0, n_chunks, body, 0)
    for chunk in range(n_chunks):
        rows = slice(chunk * ML_CHUNK, (chunk + 1) * ML_CHUNK)
        mml_ref[rows, :] = _mlstm_finish(ht_scr[0, chunk] + ht_scr[1, chunk],
                                         zvo_ref[rows, ML_W:2 * ML_W],
                                         ng_ref[...]).astype(BF16)


def _lat_mlstm_call(z_qk, z_vo, z_g, fb_row, norm_g, state_c, state_n, state_m, layer):
    n_st = 2 * ML_HEADS
    n_chunks = DEC_SEQ // ML_CHUNK
    row_spec = lambda w: pl.BlockSpec((DEC_SEQ, w), lambda b: (b, 0))
    return pl.pallas_call(
        _lat_mlstm_kernel,
        grid=(DEC_BATCH,),
        in_specs=[row_spec(Z_QK_W), row_spec(Z_VO_W), row_spec(Z_G_W),
                  _layer_spec(fb_row, layer), _layer_spec(norm_g, layer),
                  pl.BlockSpec((None, None, n_st, D_HEAD, D_HEAD), lambda b: (b, layer, 0, 0, 0)),
                  pl.BlockSpec((None, None, n_st, D_HEAD), lambda b: (b, layer, 0, 0)),
                  pl.BlockSpec((None, None, n_st, 1), lambda b: (b, layer, 0, 0))],
        out_specs=row_spec(ML_W),
        out_shape=jax.ShapeDtypeStruct((DEC_BATCH * DEC_SEQ, ML_W), BF16),
        scratch_shapes=[pltpu.VMEM((2, n_chunks, ML_W, ML_CHUNK), F32),
                        pltpu.VMEM((n_st, D_HEAD, D_HEAD), F32), pltpu.VMEM((n_st, D_HEAD), F32),
                        pltpu.VMEM((n_st, LANES), F32)],
        compiler_params=_params(1),
        name="lat_mlstm",
    )(z_qk, z_vo, z_g, fb_row, norm_g, state_c, state_n, state_m)


def _lat_swa_kernel(zswa_ref, ckt_ref, cvt_ref, sink_ref, mswa_ref,
                    vt_scr, ot_scr, bias_scr, ck_scr, cvt_scr, st_scr, m_scr):
    vt_scr[...] = zswa_ref[:, SWA_W + SWA_KV_W:SWA_W + 2 * SWA_KV_W].T

    bl = SWA_BLOCK
    nb = DEC_SEQ // bl
    n_q = SWA_GROUP * bl
    kj = lax.broadcasted_iota(jnp.int32, (3 * bl, n_q), 0)
    qi = lax.broadcasted_iota(jnp.int32, (3 * bl, n_q), 1) & (bl - 1)
    keep = ((kj >= qi) & (kj < 2 * bl)) | ((kj >= 2 * bl) & (kj - 2 * bl <= qi))
    bias_scr[...] = jnp.where(keep, 0.0, NEG_INF)
    ck_scr[...] = jnp.concatenate([ckt_ref[kv] for kv in range(SWA_KV_HEADS)],
                                  axis=0).astype(BF16).T
    for kv in range(SWA_KV_HEADS):
        cvt_scr[kv] = _with_ones(cvt_ref[kv])

    def band(blk):
        lo_b, hi_b = max(blk - 1, 0), min(blk + 2, nb)
        return slice(lo_b * bl, hi_b * bl), slice((lo_b - blk + 1) * bl, (hi_b - blk + 1) * bl)

    def score_stage(kv, blk, slot):
        head = slice(kv * D_HEAD, (kv + 1) * D_HEAD)
        keys, bias_rows = band(blk)
        n_band = keys.stop - keys.start
        q = jnp.concatenate(
            [zswa_ref[blk * bl:(blk + 1) * bl,
                      (kv * SWA_GROUP + g) * D_HEAD:(kv * SWA_GROUP + g + 1) * D_HEAD]
             for g in range(SWA_GROUP)], axis=0)
        k_band = zswa_ref[keys, SWA_W + kv * D_HEAD:SWA_W + (kv + 1) * D_HEAD]
        s_ctx = _dot_nt(ck_scr[:, head], q)
        s_band = _dot_nt(k_band, q) + bias_scr[bias_rows, :]
        st_scr[slot, 0:PAST_LEN, :] = s_ctx
        st_scr[slot, PAST_LEN:PAST_LEN + n_band, :] = s_band
        m = jnp.maximum(jnp.max(s_ctx, axis=0, keepdims=True),
                        jnp.max(s_band, axis=0, keepdims=True))
        m_scr[slot] = jnp.broadcast_to(jnp.maximum(m, sink_ref[kv:kv + 1, :]), m_scr.shape[1:])

    def value_stage(kv, blk, slot):
        head = slice(kv * D_HEAD, (kv + 1) * D_HEAD)
        keys, _ = band(blk)
        n_band = keys.stop - keys.start
        m = m_scr[slot, 0:1, :]
        p_ctx = jnp.exp2(st_scr[slot, 0:PAST_LEN, :] - m).astype(BF16)
        p_band = jnp.exp2(st_scr[slot, PAST_LEN:PAST_LEN + n_band, :] - m).astype(BF16)
        ot = (jnp.dot(cvt_scr[kv], p_ctx, preferred_element_type=F32)
              + jnp.dot(_with_ones(vt_scr[head, keys]), p_band, preferred_element_type=F32))
        den = ot[D_HEAD:D_HEAD + 1, :] + jnp.exp2(sink_ref[kv:kv + 1, :] - m)
        o = (ot[0:D_HEAD, :] * (1.0 / den)).astype(BF16)
        for g in range(SWA_GROUP):
            hq = kv * SWA_GROUP + g
            ot_scr[hq * D_HEAD:(hq + 1) * D_HEAD, blk * bl:(blk + 1) * bl] = o[:, g * bl:(g + 1) * bl]

    problems = [(kv, blk) for kv in range(SWA_KV_HEADS) for blk in range(nb)]
    for i in range(len(problems) + 1):
        def stage(i=i):
            if i > 0:
                value_stage(*problems[i - 1], (i - 1) % 2)
            if i < len(problems):
                score_stage(*problems[i], i % 2)
        stage()
    for p in range(SWA_W // LANES):
        mswa_ref[:, p * LANES:(p + 1) * LANES] = ot_scr[p * LANES:(p + 1) * LANES, :].T


def _lat_swa_call(z_swa, cache_k, cache_v, sink_rows, layer):
    row_spec = lambda w: pl.BlockSpec((DEC_SEQ, w), lambda b: (b, 0))
    cache_spec = pl.BlockSpec((None, None, SWA_KV_HEADS, D_HEAD, PAST_LEN),
                              lambda b: (b, layer, 0, 0, 0))
    n_q = SWA_GROUP * SWA_BLOCK
    return pl.pallas_call(
        _lat_swa_kernel,
        grid=(DEC_BATCH,),
        in_specs=[row_spec(Z_SWA_W), cache_spec, cache_spec, _layer_spec(sink_rows, layer)],
        out_specs=row_spec(SWA_W),
        out_shape=jax.ShapeDtypeStruct((DEC_BATCH * DEC_SEQ, SWA_W), BF16),
        scratch_shapes=[pltpu.VMEM((SWA_KV_W, DEC_SEQ), BF16), pltpu.VMEM((SWA_W, DEC_SEQ), BF16),
                        pltpu.VMEM((3 * SWA_BLOCK, n_q), F32),
                        pltpu.VMEM((PAST_LEN, SWA_KV_W), BF16),
                        pltpu.VMEM((SWA_KV_HEADS, D_HEAD + ONES_ROWS, PAST_LEN), BF16),
                        pltpu.VMEM((2, PAST_LEN + 3 * SWA_BLOCK, n_q), F32),
                        pltpu.VMEM((2, SUBLANES, n_q), F32)],
        compiler_params=_params(1),
        name="lat_swa",
    )(z_swa, cache_k, cache_v, sink_rows)


NAT_ROWS = DEC_SEQ // GRID_W
NAT_GROUP_ROWS = 4
NAT_GROUP_Q = NAT_GROUP_ROWS * GRID_W
NAT_RPB_ROWS = 2 * NAT_KH


def _nat_row_start(r):
    return min(max(r - NAT_KH // 2, 0), NAT_ROWS - NAT_KH)


def _nat_groups():
    groups, off = [], 0
    for g in range(NAT_ROWS // NAT_GROUP_ROWS):
        w0 = _nat_row_start(g * NAT_GROUP_ROWS)
        w1 = _nat_row_start((g + 1) * NAT_GROUP_ROWS - 1) + NAT_KH
        n_rows = w1 - w0 + (w1 - w0) % 2
        w0 = min(w0, NAT_ROWS - n_rows)
        groups.append((w0, n_rows, off))
        off += n_rows * GRID_W
    return groups, off


NAT_GROUPS, NAT_BIAS_KEYS = _nat_groups()


def _nat_build_bias(rpb_ref, bias_scr):
    shape = (GRID_W, LANES)
    q = lax.broadcasted_iota(jnp.int32, shape, 0)
    lane = lax.broadcasted_iota(jnp.int32, shape, 1)
    kc = lane & (GRID_W - 1)
    cs = jnp.clip(q - NAT_KW // 2, 0, GRID_W - NAT_KW)
    ok = (kc >= cs) & (kc < cs + NAT_KW)
    ok_lo = ok & (lane < GRID_W)
    ok_hi = ok & (lane >= GRID_W)
    neg_tile = jnp.full(shape, NEG_INF, F32)
    for hd in range(2):
        lo, hi = [], []
        for ro in range(2 * NAT_KH - 1):
            x = jnp.broadcast_to(rpb_ref[hd, ro:ro + 1, :], shape)
            lo.append(jnp.where(ok_lo, pltpu.roll(x, 0, 1, stride=1, stride_axis=0), NEG_INF))
            hi.append(jnp.where(ok_hi, pltpu.roll(x, GRID_W, 1, stride=1, stride_axis=0), NEG_INF))
        for g, (w0, n_rows, off) in enumerate(NAT_GROUPS):
            for jj in range(n_rows // 2):
                kra, krb = w0 + 2 * jj, w0 + 2 * jj + 1
                tiles = []
                for rr in range(NAT_GROUP_ROWS):
                    r = g * NAT_GROUP_ROWS + rr
                    r0 = _nat_row_start(r)
                    parts = []
                    if r0 <= kra < r0 + NAT_KH:
                        parts.append(lo[kra - r + NAT_KH - 1])
                    if r0 <= krb < r0 + NAT_KH:
                        parts.append(hi[krb - r + NAT_KH - 1])
                    tiles.append(neg_tile if not parts else (
                        parts[0] if len(parts) == 1 else jnp.maximum(parts[0], parts[1])))
                bias_scr[hd, off + jj * LANES:off + (jj + 1) * LANES, :] = (
                    jnp.concatenate(tiles, axis=0).T)


def _lat_nat_kernel(q_ref, k_ref, v_ref, ckt_ref, cvt_ref, rpb_ref, mnat_ref, bias_scr):
    @pl.when(pl.program_id(1) == 0)
    def _():
        _nat_build_bias(rpb_ref, bias_scr)

    vt = v_ref[...].astype(BF16).T
    ck2 = jnp.concatenate([ckt_ref[0], ckt_ref[1]], axis=0).astype(BF16).T
    ots = []
    for hd in range(2):
        head = slice(hd * D_HEAD, (hd + 1) * D_HEAD)
        ck = ck2[:, head]
        cvt_ones = _with_ones(cvt_ref[hd])
        ot_groups = []
        for g, (w0, n_rows, off) in enumerate(NAT_GROUPS):
            keys = slice(w0 * GRID_W, (w0 + n_rows) * GRID_W)
            parts = [(k_ref[keys, head].astype(BF16), _with_ones(vt[head, keys]),
                      bias_scr[hd, off:off + n_rows * GRID_W, :]),
                     (ck, cvt_ones, None)]
            q = q_ref[g * NAT_GROUP_Q:(g + 1) * NAT_GROUP_Q, head].astype(BF16)
            ot_groups.append(_attend_t(q, parts))
        ots.append(jnp.concatenate(ot_groups, axis=1))
    mnat_ref[...] = jnp.concatenate(ots, axis=0).astype(BF16).T


def _lat_nat_call(z_nat, cache_k, cache_v, rpb_lanes, layer):
    n_pair = NAT_HEADS // 2
    col_spec = lambda off: pl.BlockSpec((DEC_SEQ, LANES), lambda p, b: (b, off + p))
    cache_spec = pl.BlockSpec((None, None, 2, D_HEAD, PAST_LEN), lambda p, b: (b, layer, p, 0, 0))
    return pl.pallas_call(
        _lat_nat_kernel,
        grid=(n_pair, DEC_BATCH),
        in_specs=[col_spec(0), col_spec(n_pair), col_spec(2 * n_pair), cache_spec, cache_spec,
                  pl.BlockSpec((None, 2, NAT_RPB_ROWS, LANES), lambda p, b: (layer, p, 0, 0))],
        out_specs=pl.BlockSpec((DEC_SEQ, LANES), lambda p, b: (b, p)),
        out_shape=jax.ShapeDtypeStruct((DEC_BATCH * DEC_SEQ, NAT_W), BF16),
        scratch_shapes=[pltpu.VMEM((2, NAT_BIAS_KEYS, NAT_GROUP_Q), F32)],
        compiler_params=_params(2),
        name="lat_nat",
    )(z_nat, z_nat, z_nat, cache_k, cache_v, rpb_lanes)


def _nat_rpb_lanes(rpb):
    n_off = 2 * NAT_KW - 1
    padded = jnp.concatenate(
        [rpb.astype(F32), jnp.zeros(rpb.shape[:-1] + (LANES - n_off,), F32)], axis=-1)
    rolled = jnp.roll(padded, -(NAT_KW - 1), axis=-1)
    pad_rows = jnp.zeros(rpb.shape[:-2] + (NAT_RPB_ROWS - rpb.shape[-2], LANES), F32)
    return jnp.concatenate([rolled, pad_rows], axis=-2)


def _rope_tables():
    t = np.arange(DEC_SEQ)[:, None]
    d = np.arange(LANES)[None, :] % D_HEAD
    pos = np.where(d < D_HEAD // 2, t // GRID_W, t % GRID_W).astype(np.float64)
    quarter = D_HEAD // 4
    freq = ROPE_BASE ** (-(d % quarter).astype(np.float64) / quarter)
    ang = (pos.astype(np.float32) * freq.astype(np.float32)).astype(np.float32)
    sign = np.where((d % (2 * quarter)) < quarter, -1.0, 1.0)
    return (jnp.asarray(np.cos(ang), dtype=F32), jnp.asarray(np.sin(ang) * sign, dtype=F32))


def _permute_in_columns(a):
    g0 = Z_ML_W
    s0 = g0 + N_GATES
    pad = jnp.zeros(a.shape[:-1] + (Z_G_W - N_GATES,), a.dtype)
    return jnp.concatenate([a[..., :g0], a[..., s0:], a[..., g0:s0], pad], axis=-1)


def kernel(x_prompt, x_sample, cache_swa_k, cache_swa_v, cache_nat_k, cache_nat_v, state_mlstm_C,
           state_mlstm_n, state_mlstm_m, c, c_ctx, w_ada, b_ada, w_in, b_in, mlstm_fbias,
           mlstm_norm_g, swa_sink, nat_rpb, w_out, ln1_g, ln1_b, w_mlp1, w_mlp2, ln2_g, ln2_b):
    xp = x_prompt.reshape(BATCH * SEQ, D_MODEL)
    xs = x_sample.reshape(DEC_BATCH * DEC_SEQ, D_MODEL)
    cvec = jnp.concatenate(
        [c, c_ctx[None, :], jnp.zeros((MOD_ROWS - DEC_BATCH - 1, D_MODEL), F32)], axis=0)
    mods = _mods_call(cvec, w_ada, b_ada).reshape(DEPTH * MOD_ROWS, 6, D_MODEL)

    w_in_t = jnp.swapaxes(w_in, 1, 2)
    out_scale = np.ones((Z_W,), np.float32)
    out_scale[ML_W:2 * ML_W] = ATT_SCALE
    out_scale[Z_ML_W:Z_ML_W + SWA_W] = Q_SCALE2
    out_scale[Z_ML_W + Z_SWA_W:Z_ML_W + Z_SWA_W + NAT_W] = Q_SCALE2
    b_in_p = jnp.stack([jnp.broadcast_to(out_scale, (DEPTH, Z_W)),
                        _permute_in_columns(b_in) * out_scale], axis=1)
    cache_swa_k, cache_swa_v, cache_nat_k, cache_nat_v = (
        jnp.swapaxes(a, -1, -2) for a in (cache_swa_k, cache_swa_v, cache_nat_k, cache_nat_v))
    vec = lambda a: a.reshape(DEPTH, 1, D_MODEL)
    tail_w = (w_out.astype(BF16), vec(ln1_g), vec(ln1_b),
              jnp.swapaxes(w_mlp1, 1, 2).astype(BF16), jnp.swapaxes(w_mlp2, 1, 2).astype(BF16),
              vec(ln2_g), vec(ln2_b))
    cos_t, sin_t = _rope_tables()
    rpb_lanes = _nat_rpb_lanes(nat_rpb) * LOG2E
    swa_sink = swa_sink * LOG2E
    fb_rows = jnp.zeros((DEPTH, 1, Z_G_W), F32)
    fb_rows = fb_rows.at[:, 0, ML_HEADS:2 * ML_HEADS].set(mlstm_fbias[:, 0])
    fb_rows = fb_rows.at[:, 0, 3 * ML_HEADS:4 * ML_HEADS].set(mlstm_fbias[:, 1])
    norm_g = mlstm_norm_g.reshape(DEPTH, 1, ML_W)
    sink_ctx = jnp.repeat(swa_sink, SEQ, axis=1).reshape(DEPTH, SWA_KV_HEADS, SWA_GROUP * SEQ)
    sink_lat = jnp.repeat(swa_sink, SWA_BLOCK, axis=1).reshape(
        DEPTH, SWA_KV_HEADS, SWA_GROUP * SWA_BLOCK)

    n_st = 2 * ML_HEADS
    state_c = state_mlstm_C.reshape(DEC_BATCH, DEPTH, n_st, D_HEAD, D_HEAD)
    state_n = state_mlstm_n.reshape(DEC_BATCH, DEPTH, n_st, D_HEAD)
    state_m = state_mlstm_m.reshape(DEC_BATCH, DEPTH, n_st, 1)

    ctx_row = lambda tile: (lambda i: CTX_MOD_ROW)
    lat_row = lambda tile: (lambda i: i // (DEC_SEQ // tile))

    ctx_state = None
    for l in range(DEPTH):
        m_ml, m_swa, m_nat, *ctx_state = _ctx_mixers_call(
            xp, mods, w_in_t, b_in_p, fb_rows, norm_g, sink_ctx, l, ctx_state)
        xp = _tail_call(xp, m_ml, m_swa, m_nat, mods, ctx_row(TAIL_TILE), tail_w, l)

        z_qk, z_vo, z_swa, z_nat, z_g = _lat_inproj_call(xs, mods, w_in_t, b_in_p, cos_t, sin_t, l)
        m_ml = _lat_mlstm_call(z_qk, z_vo, z_g, fb_rows, norm_g, state_c, state_n, state_m, l)
        m_swa = _lat_swa_call(z_swa, cache_swa_k, cache_swa_v, sink_lat, l)
        m_nat = _lat_nat_call(z_nat, cache_nat_k, cache_nat_v, rpb_lanes, l)
        xs = _tail_call(xs, m_ml, m_swa, m_nat, mods, lat_row(TAIL_TILE), tail_w, l)

    k_swa, v_swa, k_nat, v_nat, st_c, st_n, st_m = ctx_state
    return (xp.reshape(BATCH, SEQ, D_MODEL), xs.reshape(DEC_BATCH, DEC_SEQ, D_MODEL),
            k_swa, v_swa, k_nat, v_nat,
            st_c.reshape(BATCH, DEPTH, 2, ML_HEADS, D_HEAD, D_HEAD),
            st_n.reshape(BATCH, DEPTH, 2, ML_HEADS, D_HEAD),
            st_m[..., 0].reshape(BATCH, DEPTH, 2, ML_HEADS))
```

```python
import functools

import numpy as np
import jax
import jax.numpy as jnp
from jax import lax
from jax.experimental import pallas as pl
from jax.experimental.pallas import tpu as pltpu

D_MODEL = 1024
BATCH = 16
SEQ = 256
DEPTH = 4
DEC_BATCH = 8
DEC_SEQ = 1024
PAST_LEN = 512
GRID_W = 64
D_HEAD = 64
ML_HEADS = 4
SWA_HEADS = 6
SWA_KV_HEADS = 2
SWA_GROUP = SWA_HEADS // SWA_KV_HEADS
NAT_HEADS = 6
ML_W = ML_HEADS * D_HEAD
SWA_W = SWA_HEADS * D_HEAD
SWA_KV_W = SWA_KV_HEADS * D_HEAD
NAT_W = NAT_HEADS * D_HEAD
N_GATES = 4 * ML_HEADS
IN_DIM = 4 * ML_W + N_GATES + SWA_W + 2 * SWA_KV_W + 3 * NAT_W
D_FF = 4 * D_MODEL
SWA_WINDOW = 128
SWA_BLOCK = 128
NAT_KH = 8
NAT_KW = 16
ROPE_BASE = 10000.0
LN_EPS = 1e-5
DN_ALPHA = (2 * DEPTH) ** 0.25
ATT_SCALE = D_HEAD ** -0.5
LOG2E = 1.4426950408889634
Q_SCALE2 = ATT_SCALE * LOG2E

LANES = 128
SUBLANES = 8
Z_ML_W = 4 * ML_W
Z_SWA_W = SWA_W + 2 * SWA_KV_W
Z_NAT_W = 3 * NAT_W
Z_G_W = LANES
Z_W = Z_ML_W + Z_SWA_W + Z_NAT_W + Z_G_W
Z_QK_W = 2 * ML_W
Z_VO_W = 2 * ML_W
Z_GROUPS = (Z_QK_W, Z_VO_W, Z_SWA_W, Z_NAT_W, Z_G_W)
Z_SWA_GROUP = 2
MOD_ROWS = 16
CTX_MOD_ROW = DEC_BATCH
ROW_TILE = 512
TAIL_TILE = 512
ML_CHUNK = 256
VMEM_LIMIT = 56 * 1024 * 1024

BF16 = jnp.bfloat16
F32 = jnp.float32
NEG_INF = float("-inf")
LAT_Z_DTYPES = (BF16, F32, BF16, BF16, F32)
CTX_Z_DTYPES = (BF16, F32, F32, F32, F32)


def _dot(a, b):
    return jnp.dot(a.astype(BF16), b.astype(BF16), preferred_element_type=F32)


def _dot_nt(a, b):
    return lax.dot_general(a.astype(BF16), b.astype(BF16), (((1,), (1,)), ((), ())),
                           preferred_element_type=F32)


def _dot_split3(tri, x):
    hi = x.astype(BF16)
    r1 = x - hi.astype(F32)
    mid = r1.astype(BF16)
    lo = (r1 - mid.astype(F32)).astype(BF16)
    return (jnp.dot(tri, hi, preferred_element_type=F32)
            + jnp.dot(tri, mid, preferred_element_type=F32)
            + jnp.dot(tri, lo, preferred_element_type=F32))


def _sigmoid(x):
    return 1.0 / (1.0 + jnp.exp(-x))


def _log_sigmoid(x):
    return jnp.minimum(x, 0.0) - jnp.log1p(jnp.exp(-jnp.abs(x)))


def _layer_norm(x, g, b):
    mu = jnp.mean(x, axis=-1, keepdims=True)
    xc = x - mu
    var = jnp.mean(xc * xc, axis=-1, keepdims=True)
    return xc * lax.rsqrt(var + LN_EPS) * g + b


def _params(n_axes=1, flags=None):
    return pltpu.CompilerParams(dimension_semantics=("arbitrary",) * n_axes,
                                vmem_limit_bytes=VMEM_LIMIT, flags=flags)


def _layer_spec(a, layer, single_buffer=False):
    kwargs = {"pipeline_mode": pl.Buffered(1)} if single_buffer else {}
    return pl.BlockSpec((None,) + a.shape[1:], lambda *_: (layer,) + (0,) * (a.ndim - 1), **kwargs)


def _mods_kernel(c_ref, w_ref, b_ref, o_ref):
    c = c_ref[...]
    o_ref[...] = _dot(c * _sigmoid(c), w_ref[...]) + b_ref[...]


def _mods_call(cvec, w_ada, b_ada):
    n_col = 6
    return pl.pallas_call(
        _mods_kernel,
        grid=(DEPTH, n_col),
        in_specs=[
            pl.BlockSpec((MOD_ROWS, D_MODEL), lambda l, j: (0, 0)),
            pl.BlockSpec((None, D_MODEL, D_MODEL), lambda l, j: (l, 0, j)),
            pl.BlockSpec((None, 1, D_MODEL), lambda l, j: (l, 0, j)),
        ],
        out_specs=pl.BlockSpec((None, MOD_ROWS, D_MODEL), lambda l, j: (l, 0, j)),
        out_shape=jax.ShapeDtypeStruct((DEPTH, MOD_ROWS, 6 * D_MODEL), F32),
        compiler_params=_params(2),
        name="mods",
    )(cvec, w_ada, b_ada.reshape(DEPTH, 1, 6 * D_MODEL))


def _rope(x, cos, sin_signed):
    lane = lax.broadcasted_iota(jnp.int32, x.shape, 1)
    first = (lane & 31) < 16
    partner = jnp.where(first, pltpu.roll(x, LANES - 16, 1), pltpu.roll(x, 16, 1))
    return x * cos + partner * sin_signed


def _inproj_kernel(x_ref, mod_ref, wt_ref, b_ref, *rest, rope=False):
    if rope:
        cos_ref, sin_ref, *rest = rest
    *out_refs, w_scr = rest

    @pl.when(pl.program_id(0) == 0)
    def _():
        rest = IN_DIM - Z_ML_W - N_GATES
        w_scr[0:Z_ML_W, :] = wt_ref[0:Z_ML_W, :].astype(BF16)
        w_scr[Z_ML_W:Z_ML_W + rest, :] = wt_ref[Z_ML_W + N_GATES:IN_DIM, :].astype(BF16)
        w_scr[Z_ML_W + rest:IN_DIM, :] = wt_ref[Z_ML_W:Z_ML_W + N_GATES, :].astype(BF16)
        w_scr[IN_DIM:Z_W, :] = jnp.zeros((Z_W - IN_DIM, D_MODEL), BF16)

    h = (x_ref[...] * (1.0 + mod_ref[1:2, :]) + mod_ref[0:1, :]).astype(BF16)
    o = 0
    for ref, w in zip(out_refs, Z_GROUPS):
        z = _dot_nt(h, w_scr[o:o + w, :]) * b_ref[0:1, o:o + w] + b_ref[1:2, o:o + w]
        if rope and ref is out_refs[Z_SWA_GROUP]:
            z = jnp.concatenate(
                [_rope(z[:, j * LANES:(j + 1) * LANES], cos_ref[...], sin_ref[...])
                 for j in range((SWA_W + SWA_KV_W) // LANES)] + [z[:, SWA_W + SWA_KV_W:]], axis=1)
        ref[...] = z.astype(ref.dtype)
        o += w


def _mod_spec(layer, mod_row_of_tile):
    return pl.BlockSpec((None, 6, D_MODEL),
                        lambda i: (layer * MOD_ROWS + mod_row_of_tile(i), 0, 0))


def _lat_inproj_call(x, mods, w_in_t, b_in, cos_t, sin_t, layer):
    rows = x.shape[0]
    tiles_per_seq = DEC_SEQ // ROW_TILE
    table_spec = pl.BlockSpec((ROW_TILE, LANES), lambda i: (i % tiles_per_seq, 0))
    return pl.pallas_call(
        functools.partial(_inproj_kernel, rope=True),
        grid=(rows // ROW_TILE,),
        in_specs=[
            pl.BlockSpec((ROW_TILE, D_MODEL), lambda i: (i, 0)),
            _mod_spec(layer, lambda i: i // tiles_per_seq),
            _layer_spec(w_in_t, layer, single_buffer=True),
            _layer_spec(b_in, layer),
            table_spec, table_spec,
        ],
        out_specs=[pl.BlockSpec((ROW_TILE, w), lambda i: (i, 0)) for w in Z_GROUPS],
        out_shape=[jax.ShapeDtypeStruct((rows, w), dt) for w, dt in zip(Z_GROUPS, LAT_Z_DTYPES)],
        scratch_shapes=[pltpu.VMEM((Z_W, D_MODEL), BF16)],
        compiler_params=_params(1),
        name="inproj",
    )(x, mods, w_in_t, b_in, cos_t, sin_t)


def _tail_kernel(x_ref, mml_ref, mswa_ref, mnat_ref, mod_ref, wout_ref, ln1g_ref, ln1b_ref,
                 w1_ref, w2_ref, ln2g_ref, ln2b_ref, o_ref):
    g_a, sh_m, sc_m, g_m = (mod_ref[i:i + 1, :] for i in (2, 3, 4, 5))
    mix = jnp.concatenate([mml_ref[...], mswa_ref[...], mnat_ref[...]], axis=1)
    proj = jnp.dot(mix, wout_ref[...], preferred_element_type=F32)
    y = _layer_norm(DN_ALPHA * x_ref[...] + g_a * proj, ln1g_ref[...], ln1b_ref[...])
    h = (y * (1.0 + sc_m) + sh_m).astype(BF16)
    f = jnp.maximum(jnp.dot(h, w1_ref[...], preferred_element_type=F32), 0.0)
    f = jnp.dot((f * f).astype(BF16), w2_ref[...], preferred_element_type=F32)
    o_ref[...] = _layer_norm(DN_ALPHA * y + g_m * f, ln2g_ref[...], ln2b_ref[...])


def _tail_call(x, mix_ml, mix_swa, mix_nat, mods, mod_row_of_tile, consts, layer):
    rows = x.shape[0]
    row_spec = lambda w: pl.BlockSpec((TAIL_TILE, w), lambda i: (i, 0))
    return pl.pallas_call(
        _tail_kernel,
        grid=(rows // TAIL_TILE,),
        in_specs=[row_spec(D_MODEL), row_spec(ML_W), row_spec(SWA_W), row_spec(NAT_W),
                  _mod_spec(layer, mod_row_of_tile)]
                 + [_layer_spec(a, layer, single_buffer=True) for a in consts],
        out_specs=row_spec(D_MODEL),
        out_shape=jax.ShapeDtypeStruct((rows, D_MODEL), F32),
        compiler_params=_params(1),
        name="tail",
    )(x, mix_ml, mix_swa, mix_nat, mods, *consts)


def _mlstm_run(units):
    kqs, cns = [], []
    for q, k, _, _, _, _, _, _, c_st, n_st, _ in units:
        kqs.append(_dot_nt(k, q))
        cns.append(None if c_st is None else _dot_nt(
            jnp.concatenate([c_st, jnp.broadcast_to(n_st, (SUBLANES, D_HEAD))], axis=0), q))
    mids = []
    for kq, (_, _, _, a_col, b_row, _, neg, _, _, _, m_st) in zip(kqs, units):
        a_m = a_col + neg
        c_row = jnp.maximum(m_st, jnp.max(a_m, axis=0, keepdims=True))
        st = kq * jnp.exp(a_m - c_row)
        mids.append((st, jnp.sum(st, axis=0, keepdims=True), jnp.exp(m_st - c_row), b_row + c_row))
    nums = [_dot(u[2], st) for u, (st, _, _, _) in zip(units, mids)]
    upds, m_news = [], []
    for (_, k, vt, _, b_row, li_row, _, last, _, _, _), (_, _, _, mt) in zip(units, mids):
        m_new = mt[:, last:last + 1]
        ws = jnp.exp(b_row[:, last:last + 1] - b_row + li_row - m_new)
        upds.append(_dot(jnp.concatenate(
            [vt * ws, jnp.broadcast_to(ws, (SUBLANES, ws.shape[1]))], axis=0), k))
        m_news.append(m_new)
    results = []
    for u, cn, (_, den, wprev, mt), num, upd, m_new in zip(units, cns, mids, nums, upds, m_news):
        last, c_st, n_st = u[7], u[8], u[9]
        c_new = upd[0:D_HEAD, :]
        n_new = upd[D_HEAD:D_HEAD + 1, :]
        if c_st is not None:
            num = num + wprev * cn[0:D_HEAD, :]
            den = den + wprev * cn[D_HEAD:D_HEAD + 1, :]
            wc = wprev[:, last:last + 1]
            c_new = c_new + wc * c_st
            n_new = n_new + wc * n_st
        ht = num * (1.0 / jnp.maximum(jnp.abs(den), jnp.exp(-mt)))
        results.append((ht, c_new, n_new, m_new))
    return results


def _mlstm_chunk_units(zqk_ref, zvo_ref, g_ref, fb_ref, rows, n, direction, state):
    g = g_ref[rows, :]
    ls = _log_sigmoid(g + fb_ref[...])
    r = lax.broadcasted_iota(jnp.int32, (n, n), 0)
    c = lax.broadcasted_iota(jnp.int32, (n, n), 1)
    if direction == 0:
        tri, neg, last = c <= r, jnp.where(r <= c, 0.0, NEG_INF), n - 1
    else:
        tri, neg, last = c >= r, jnp.where(r >= c, 0.0, NEG_INF), 0
    b = _dot_split3(tri.astype(BF16), ls)
    a = pltpu.roll(g, ML_HEADS, 1) - b
    bt = b.T
    gt = g.T
    i_col = 2 * ML_HEADS * direction
    f_col = i_col + ML_HEADS
    units = []
    for pair in range(ML_HEADS // 2):
        vt2 = zvo_ref[rows, pair * LANES:(pair + 1) * LANES].T
        for sub in range(2):
            hd = 2 * pair + sub
            feat = slice(sub * D_HEAD, (sub + 1) * D_HEAD)
            q = zqk_ref[rows, hd * D_HEAD:(hd + 1) * D_HEAD]
            k = zqk_ref[rows, ML_W + hd * D_HEAD:ML_W + (hd + 1) * D_HEAD]
            units.append((q, k, vt2[feat, :],
                          a[:, f_col + hd:f_col + hd + 1], bt[f_col + hd:f_col + hd + 1, :],
                          gt[i_col + hd:i_col + hd + 1, :], neg, last) + tuple(state[hd]))
    return units


def _mlstm_finish(ht, o_gate, norm_g):
    outs = []
    for hd in range(ML_HEADS):
        x = ht[hd * D_HEAD:(hd + 1) * D_HEAD, :]
        mu = jnp.mean(x, axis=0, keepdims=True)
        xc = x - mu
        var = jnp.mean(xc * xc, axis=0, keepdims=True)
        outs.append(xc * lax.rsqrt(var + LN_EPS))
    hn = jnp.concatenate(outs, axis=0).T
    return hn * norm_g * _sigmoid(o_gate)


def _store_state(c_ref, n_ref, m_ref, idx, c_new, n_new, m_new):
    c_ref[idx] = c_new
    n_ref[idx:idx + 1, :] = n_new
    m_ref[idx:idx + 1, :] = jnp.broadcast_to(m_new, (1, LANES))


ONES_ROWS = 16


def _with_ones(vt):
    ones = jnp.ones((ONES_ROWS, vt.shape[1]), BF16)
    return jnp.concatenate([vt.astype(BF16), ones], axis=0)


def _attend_many(problems):
    scores = []
    for q, parts, _ in problems:
        sts = []
        for k, _, bias_t in parts:
            st = _dot_nt(k, q)
            sts.append(st if bias_t is None else st + bias_t)
        scores.append(sts)
    maxima = []
    for sts, (_, _, sink_row) in zip(scores, problems):
        m = None
        for st in sts:
            mx = jnp.max(st, axis=0, keepdims=True)
            m = mx if m is None else jnp.maximum(m, mx)
        maxima.append(m if sink_row is None else jnp.maximum(m, sink_row))
    outs = []
    for sts, m, (_, parts, sink_row) in zip(scores, maxima, problems):
        ot = None
        for st, (_, vt_ones, _) in zip(sts, parts):
            part = jnp.dot(vt_ones, jnp.exp2(st - m).astype(BF16), preferred_element_type=F32)
            ot = part if ot is None else ot + part
        den = ot[D_HEAD:D_HEAD + 1, :]
        if sink_row is not None:
            den = den + jnp.exp2(sink_row - m)
        outs.append(ot[0:D_HEAD, :] * (1.0 / den))
    return outs


def _head_rows(pairs_t, hd):
    return pairs_t[hd // 2][(hd % 2) * D_HEAD:(hd % 2 + 1) * D_HEAD, :]


def _store_heads_t(o_ref, ots):
    for p in range(len(ots) // 2):
        o_ref[:, p * LANES:(p + 1) * LANES] = jnp.concatenate(
            [ots[2 * p], ots[2 * p + 1]], axis=0).astype(o_ref.dtype).T


N_STATE = 2 * ML_HEADS
N_CTX_STATE_OUTS = 7


def _ctx_mixers_kernel(x_ref, mod_ref, wt_ref, b_ref, fb_ref, ng_ref, sink_ref, *rest,
                       first_layer):
    n_scratch = 1 + len(Z_GROUPS)
    w_scr, zqk_ref, zvo_ref, zswa_ref, znat_ref, zg_ref = rest[-n_scratch:]
    mml_ref, mswa_ref, mnat_ref, *state_refs = rest[-(3 + N_CTX_STATE_OUTS) - n_scratch:-n_scratch]
    _inproj_kernel(x_ref, mod_ref, wt_ref, b_ref, zqk_ref, zvo_ref, zswa_ref, znat_ref, zg_ref,
                   w_scr)
    if first_layer:
        for ref in state_refs:
            ref[1:] = jnp.zeros((DEPTH - 1,) + ref.shape[1:], F32)
        state_refs = [ref.at[0] for ref in state_refs]
    sk_ref, sv_ref, nk_ref, nv_ref, c_ref, n_ref, m_ref = state_refs
    rows = slice(0, SEQ)
    zero_state = [(None, jnp.zeros((1, D_HEAD), F32), jnp.zeros((1, 1), F32))] * ML_HEADS
    units = []
    for direction in range(2):
        units += _mlstm_chunk_units(zqk_ref, zvo_ref, zg_ref, fb_ref, rows, SEQ, direction,
                                    zero_state)
    results = _mlstm_run(units)
    for idx, (_, c_new, n_new, m_new) in enumerate(results):
        _store_state(c_ref, n_ref, m_ref, idx, c_new, n_new, m_new)
    ht_sum = jnp.concatenate([results[hd][0] + results[ML_HEADS + hd][0]
                              for hd in range(ML_HEADS)], axis=0)
    mml_ref[...] = _mlstm_finish(ht_sum, zvo_ref[:, ML_W:2 * ML_W], ng_ref[...]).astype(BF16)

    scaled_q = lambda ref, hd: ref[:, hd * D_HEAD:(hd + 1) * D_HEAD].astype(BF16)
    swa_vts = [zswa_ref[:, SWA_W + SWA_KV_W:SWA_W + 2 * SWA_KV_W].astype(BF16).T]
    nat_vts = [znat_ref[:, 2 * NAT_W + p * LANES:2 * NAT_W + (p + 1) * LANES].astype(BF16).T
               for p in range(NAT_W // LANES)]
    problems = []
    for kv in range(SWA_KV_HEADS):
        k = zswa_ref[:, SWA_W + kv * D_HEAD:SWA_W + (kv + 1) * D_HEAD]
        sk_ref[kv] = k
        sv_ref[kv] = zswa_ref[:, SWA_W + SWA_KV_W + kv * D_HEAD:SWA_W + SWA_KV_W + (kv + 1) * D_HEAD]
        q = jnp.concatenate([scaled_q(zswa_ref, kv * SWA_GROUP + g) for g in range(SWA_GROUP)],
                            axis=0)
        problems.append((q, [(k.astype(BF16), _with_ones(_head_rows(swa_vts, kv)), None)],
                         sink_ref[kv:kv + 1, :]))
    for hd in range(NAT_HEADS):
        k = znat_ref[:, NAT_W + hd * D_HEAD:NAT_W + (hd + 1) * D_HEAD]
        nk_ref[hd] = k
        nv_ref[hd] = znat_ref[:, 2 * NAT_W + hd * D_HEAD:2 * NAT_W + (hd + 1) * D_HEAD]
        problems.append((scaled_q(znat_ref, hd),
                         [(k.astype(BF16), _with_ones(_head_rows(nat_vts, hd)), None)], None))
    ots = _attend_many(problems)
    _store_heads_t(mswa_ref, [ot[:, g * SEQ:(g + 1) * SEQ]
                              for ot in ots[:SWA_KV_HEADS] for g in range(SWA_GROUP)])
    _store_heads_t(mnat_ref, ots[SWA_KV_HEADS:])


CTX_STATE_DIMS = ((SWA_KV_HEADS, SEQ, D_HEAD), (SWA_KV_HEADS, SEQ, D_HEAD),
                  (NAT_HEADS, SEQ, D_HEAD), (NAT_HEADS, SEQ, D_HEAD),
                  (N_STATE, D_HEAD, D_HEAD), (N_STATE, D_HEAD), (N_STATE, LANES))


def _ctx_mixers_call(x, mods, w_in_t, b_in, fb_row, norm_g, sink_rows, layer, prev_state):
    rows = x.shape[0]
    row_spec = lambda w: pl.BlockSpec((SEQ, w), lambda b: (b, 0))
    state_dims = CTX_STATE_DIMS
    first_layer = prev_state is None
    if first_layer:
        assert layer == 0
        state_spec = lambda d: pl.BlockSpec((None, DEPTH) + d, lambda b: (b,) + (0,) * (len(d) + 1))
    else:
        state_spec = lambda d: pl.BlockSpec((None, None) + d, lambda b: (b, layer) + (0,) * len(d))
    prev = () if first_layer else tuple(prev_state)
    n_in = 7
    return pl.pallas_call(
        functools.partial(_ctx_mixers_kernel, first_layer=first_layer),
        grid=(BATCH,),
        in_specs=[row_spec(D_MODEL), _mod_spec(layer, lambda b: CTX_MOD_ROW),
                  _layer_spec(w_in_t, layer, single_buffer=True), _layer_spec(b_in, layer),
                  _layer_spec(fb_row, layer), _layer_spec(norm_g, layer),
                  _layer_spec(sink_rows, layer)]
                 + [pl.BlockSpec(memory_space=pl.ANY)] * len(prev),
        out_specs=[row_spec(ML_W), row_spec(SWA_W), row_spec(NAT_W)]
                  + [state_spec(d) for d in state_dims],
        out_shape=[jax.ShapeDtypeStruct((rows, ML_W), BF16),
                   jax.ShapeDtypeStruct((rows, SWA_W), BF16),
                   jax.ShapeDtypeStruct((rows, NAT_W), BF16)]
                  + [jax.ShapeDtypeStruct((BATCH, DEPTH) + d, F32) for d in state_dims],
        input_output_aliases={n_in + i: 3 + i for i in range(len(prev))},
        scratch_shapes=[pltpu.VMEM((Z_W, D_MODEL), BF16)]
                       + [pltpu.VMEM((SEQ, w), dt) for w, dt in zip(Z_GROUPS, CTX_Z_DTYPES)],
        compiler_params=_params(1),
        name="ctx_mixers",
    )(x, mods, w_in_t, b_in, fb_row, norm_g, sink_rows, *prev)


def _lat_mlstm_kernel(zqk_ref, zvo_ref, zg_ref, fb_ref, ng_ref, c0_ref, n0_ref, m0_ref, mml_ref,
                      ht_scr, c_scr, n_scr, m_scr):
    n_chunks = DEC_SEQ // ML_CHUNK
    c_scr[...] = c0_ref[...]
    n_scr[...] = n0_ref[...]
    m_scr[...] = jnp.broadcast_to(m0_ref[...], m_scr.shape)

    def body(i, carry):
        chunks = (i, n_chunks - 1 - i)
        units = []
        for direction, chunk in enumerate(chunks):
            rows = pl.ds(pl.multiple_of(chunk * ML_CHUNK, ML_CHUNK), ML_CHUNK)
            base = direction * ML_HEADS
            state = [(c_scr[base + hd], n_scr[base + hd:base + hd + 1, :],
                      m_scr[base + hd:base + hd + 1, 0:1]) for hd in range(ML_HEADS)]
            units += _mlstm_chunk_units(zqk_ref, zvo_ref, zg_ref, fb_ref, rows, ML_CHUNK,
                                        direction, state)
        results = _mlstm_run(units)
        for direction, chunk in enumerate(chunks):
            base = direction * ML_HEADS
            ht_scr[direction, chunk] = jnp.concatenate(
                [results[base + hd][0] for hd in range(ML_HEADS)], axis=0)
        for idx, (_, c_new, n_new, m_new) in enumerate(results):
            _store_state(c_scr, n_scr, m_scr, idx, c_new, n_new, m_new)
        return carry

    lax.fori_loop(0, n_chunks, body, 0)
    for chunk in range(n_chunks):
        rows = slice(chunk * ML_CHUNK, (chunk + 1) * ML_CHUNK)
        mml_ref[rows, :] = _mlstm_finish(ht_scr[0, chunk] + ht_scr[1, chunk],
                                         zvo_ref[rows, ML_W:2 * ML_W],
                                         ng_ref[...]).astype(BF16)


def _lat_mlstm_call(z_qk, z_vo, z_g, fb_row, norm_g, state_c, state_n, state_m, layer):
    n_st = 2 * ML_HEADS
    n_chunks = DEC_SEQ // ML_CHUNK
    row_spec = lambda w: pl.BlockSpec((DEC_SEQ, w), lambda b: (b, 0))
    return pl.pallas_call(
        _lat_mlstm_kernel,
        grid=(DEC_BATCH,),
        in_specs=[row_spec(Z_QK_W), row_spec(Z_VO_W), row_spec(Z_G_W),
                  _layer_spec(fb_row, layer), _layer_spec(norm_g, layer),
                  pl.BlockSpec((None, None, n_st, D_HEAD, D_HEAD), lambda b: (b, layer, 0, 0, 0)),
                  pl.BlockSpec((None, None, n_st, D_HEAD), lambda b: (b, layer, 0, 0)),
                  pl.BlockSpec((None, None, n_st, 1), lambda b: (b, layer, 0, 0))],
        out_specs=row_spec(ML_W),
        out_shape=jax.ShapeDtypeStruct((DEC_BATCH * DEC_SEQ, ML_W), BF16),
        scratch_shapes=[pltpu.VMEM((2, n_chunks, ML_W, ML_CHUNK), F32),
                        pltpu.VMEM((n_st, D_HEAD, D_HEAD), F32), pltpu.VMEM((n_st, D_HEAD), F32),
                        pltpu.VMEM((n_st, LANES), F32)],
        compiler_params=_params(1),
        name="lat_mlstm",
    )(z_qk, z_vo, z_g, fb_row, norm_g, state_c, state_n, state_m)


def _lat_swa_kernel(zswa_ref, ckt_ref, cvt_ref, sink_ref, mswa_ref,
                    vt_scr, ot_scr, bias_scr, ck_scr, cvt_scr):
    vt_scr[...] = zswa_ref[:, SWA_W + SWA_KV_W:SWA_W + 2 * SWA_KV_W].T

    bl = SWA_BLOCK
    nb = DEC_SEQ // bl
    n_q = SWA_GROUP * bl
    kj = lax.broadcasted_iota(jnp.int32, (3 * bl, n_q), 0)
    qi = lax.broadcasted_iota(jnp.int32, (3 * bl, n_q), 1) & (bl - 1)
    keep = ((kj >= qi) & (kj < 2 * bl)) | ((kj >= 2 * bl) & (kj - 2 * bl <= qi))
    bias_scr[...] = jnp.where(keep, 0.0, NEG_INF)
    ck_scr[...] = jnp.concatenate([ckt_ref[kv] for kv in range(SWA_KV_HEADS)],
                                  axis=0).astype(BF16).T
    for kv in range(SWA_KV_HEADS):
        cvt_scr[kv] = _with_ones(cvt_ref[kv])

    for kv in range(SWA_KV_HEADS):
        head = slice(kv * D_HEAD, (kv + 1) * D_HEAD)
        problems = []
        for blk in range(nb):
            lo_b, hi_b = max(blk - 1, 0), min(blk + 2, nb)
            keys = slice(lo_b * bl, hi_b * bl)
            q = jnp.concatenate(
                [zswa_ref[blk * bl:(blk + 1) * bl,
                          (kv * SWA_GROUP + g) * D_HEAD:(kv * SWA_GROUP + g + 1) * D_HEAD]
                 for g in range(SWA_GROUP)], axis=0)
            parts = [(zswa_ref[keys, SWA_W + kv * D_HEAD:SWA_W + (kv + 1) * D_HEAD],
                      _with_ones(vt_scr[head, keys]),
                      bias_scr[(lo_b - blk + 1) * bl:(hi_b - blk + 1) * bl, :]),
                     (ck_scr[:, head], cvt_scr[kv], None)]
            problems.append((q, parts, sink_ref[kv:kv + 1, :]))
        for blk, ot in enumerate(_attend_many(problems)):
            o = ot.astype(BF16)
            for g in range(SWA_GROUP):
                hq = kv * SWA_GROUP + g
                ot_scr[hq * D_HEAD:(hq + 1) * D_HEAD, blk * bl:(blk + 1) * bl] = (
                    o[:, g * bl:(g + 1) * bl])
    for p in range(SWA_W // LANES):
        mswa_ref[:, p * LANES:(p + 1) * LANES] = ot_scr[p * LANES:(p + 1) * LANES, :].T


def _lat_swa_call(z_swa, cache_k, cache_v, sink_rows, layer):
    row_spec = lambda w: pl.BlockSpec((DEC_SEQ, w), lambda b: (b, 0))
    cache_spec = pl.BlockSpec((None, None, SWA_KV_HEADS, D_HEAD, PAST_LEN),
                              lambda b: (b, layer, 0, 0, 0))
    n_q = SWA_GROUP * SWA_BLOCK
    return pl.pallas_call(
        _lat_swa_kernel,
        grid=(DEC_BATCH,),
        in_specs=[row_spec(Z_SWA_W), cache_spec, cache_spec, _layer_spec(sink_rows, layer)],
        out_specs=row_spec(SWA_W),
        out_shape=jax.ShapeDtypeStruct((DEC_BATCH * DEC_SEQ, SWA_W), BF16),
        scratch_shapes=[pltpu.VMEM((SWA_KV_W, DEC_SEQ), BF16), pltpu.VMEM((SWA_W, DEC_SEQ), BF16),
                        pltpu.VMEM((3 * SWA_BLOCK, n_q), F32),
                        pltpu.VMEM((PAST_LEN, SWA_KV_W), BF16),
                        pltpu.VMEM((SWA_KV_HEADS, D_HEAD + ONES_ROWS, PAST_LEN), BF16)],
        compiler_params=_params(1),
        name="lat_swa",
    )(z_swa, cache_k, cache_v, sink_rows)


NAT_ROWS = DEC_SEQ // GRID_W
NAT_GROUP_ROWS = 4
NAT_GROUP_Q = NAT_GROUP_ROWS * GRID_W
NAT_RPB_ROWS = 2 * NAT_KH


def _nat_row_start(r):
    return min(max(r - NAT_KH // 2, 0), NAT_ROWS - NAT_KH)


def _nat_groups():
    groups, off = [], 0
    for g in range(NAT_ROWS // NAT_GROUP_ROWS):
        w0 = _nat_row_start(g * NAT_GROUP_ROWS)
        w1 = _nat_row_start((g + 1) * NAT_GROUP_ROWS - 1) + NAT_KH
        n_rows = w1 - w0 + (w1 - w0) % 2
        w0 = min(w0, NAT_ROWS - n_rows)
        groups.append((w0, n_rows, off))
        off += n_rows * GRID_W
    return groups, off


NAT_GROUPS, NAT_BIAS_KEYS = _nat_groups()


def _nat_build_bias(rpb_ref, bias_scr):
    shape = (GRID_W, LANES)
    q = lax.broadcasted_iota(jnp.int32, shape, 0)
    lane = lax.broadcasted_iota(jnp.int32, shape, 1)
    kc = lane & (GRID_W - 1)
    cs = jnp.clip(q - NAT_KW // 2, 0, GRID_W - NAT_KW)
    ok = (kc >= cs) & (kc < cs + NAT_KW)
    ok_lo = ok & (lane < GRID_W)
    ok_hi = ok & (lane >= GRID_W)
    neg_tile = jnp.full(shape, NEG_INF, F32)
    for hd in range(2):
        lo, hi = [], []
        for ro in range(2 * NAT_KH - 1):
            x = jnp.broadcast_to(rpb_ref[hd, ro:ro + 1, :], shape)
            lo.append(jnp.where(ok_lo, pltpu.roll(x, 0, 1, stride=1, stride_axis=0), NEG_INF))
            hi.append(jnp.where(ok_hi, pltpu.roll(x, GRID_W, 1, stride=1, stride_axis=0), NEG_INF))
        for g, (w0, n_rows, off) in enumerate(NAT_GROUPS):
            for jj in range(n_rows // 2):
                kra, krb = w0 + 2 * jj, w0 + 2 * jj + 1
                tiles = []
                for rr in range(NAT_GROUP_ROWS):
                    r = g * NAT_GROUP_ROWS + rr
                    r0 = _nat_row_start(r)
                    parts = []
                    if r0 <= kra < r0 + NAT_KH:
                        parts.append(lo[kra - r + NAT_KH - 1])
                    if r0 <= krb < r0 + NAT_KH:
                        parts.append(hi[krb - r + NAT_KH - 1])
                    tiles.append(neg_tile if not parts else (
                        parts[0] if len(parts) == 1 else jnp.maximum(parts[0], parts[1])))
                bias_scr[hd, off + jj * LANES:off + (jj + 1) * LANES, :] = (
                    jnp.concatenate(tiles, axis=0).T)


def _lat_nat_kernel(q_ref, k_ref, v_ref, ckt_ref, cvt_ref, rpb_ref, mnat_ref, bias_scr):
    @pl.when(pl.program_id(1) == 0)
    def _():
        _nat_build_bias(rpb_ref, bias_scr)

    vt = v_ref[...].astype(BF16).T
    ck2 = jnp.concatenate([ckt_ref[0], ckt_ref[1]], axis=0).astype(BF16).T
    problems = []
    for hd in range(2):
        head = slice(hd * D_HEAD, (hd + 1) * D_HEAD)
        ck = ck2[:, head]
        cvt_ones = _with_ones(cvt_ref[hd])
        for g, (w0, n_rows, off) in enumerate(NAT_GROUPS):
            keys = slice(w0 * GRID_W, (w0 + n_rows) * GRID_W)
            parts = [(k_ref[keys, head].astype(BF16), _with_ones(vt[head, keys]),
                      bias_scr[hd, off:off + n_rows * GRID_W, :]),
                     (ck, cvt_ones, None)]
            q = q_ref[g * NAT_GROUP_Q:(g + 1) * NAT_GROUP_Q, head].astype(BF16)
            problems.append((q, parts, None))
    ots = _attend_many(problems)
    n_groups = len(NAT_GROUPS)
    mnat_ref[...] = jnp.concatenate(
        [jnp.concatenate(ots[hd * n_groups:(hd + 1) * n_groups], axis=1) for hd in range(2)],
        axis=0).astype(BF16).T


def _lat_nat_call(z_nat, cache_k, cache_v, rpb_lanes, layer):
    n_pair = NAT_HEADS // 2
    col_spec = lambda off: pl.BlockSpec((DEC_SEQ, LANES), lambda p, b: (b, off + p))
    cache_spec = pl.BlockSpec((None, None, 2, D_HEAD, PAST_LEN), lambda p, b: (b, layer, p, 0, 0))
    return pl.pallas_call(
        _lat_nat_kernel,
        grid=(n_pair, DEC_BATCH),
        in_specs=[col_spec(0), col_spec(n_pair), col_spec(2 * n_pair), cache_spec, cache_spec,
                  pl.BlockSpec((None, 2, NAT_RPB_ROWS, LANES), lambda p, b: (layer, p, 0, 0))],
        out_specs=pl.BlockSpec((DEC_SEQ, LANES), lambda p, b: (b, p)),
        out_shape=jax.ShapeDtypeStruct((DEC_BATCH * DEC_SEQ, NAT_W), BF16),
        scratch_shapes=[pltpu.VMEM((2, NAT_BIAS_KEYS, NAT_GROUP_Q), F32)],
        compiler_params=_params(2),
        name="lat_nat",
    )(z_nat, z_nat, z_nat, cache_k, cache_v, rpb_lanes)


def _nat_rpb_lanes(rpb):
    n_off = 2 * NAT_KW - 1
    padded = jnp.concatenate(
        [rpb.astype(F32), jnp.zeros(rpb.shape[:-1] + (LANES - n_off,), F32)], axis=-1)
    rolled = jnp.roll(padded, -(NAT_KW - 1), axis=-1)
    pad_rows = jnp.zeros(rpb.shape[:-2] + (NAT_RPB_ROWS - rpb.shape[-2], LANES), F32)
    return jnp.concatenate([rolled, pad_rows], axis=-2)


def _rope_tables():
    t = np.arange(DEC_SEQ)[:, None]
    d = np.arange(LANES)[None, :] % D_HEAD
    pos = np.where(d < D_HEAD // 2, t // GRID_W, t % GRID_W).astype(np.float64)
    quarter = D_HEAD // 4
    freq = ROPE_BASE ** (-(d % quarter).astype(np.float64) / quarter)
    ang = (pos.astype(np.float32) * freq.astype(np.float32)).astype(np.float32)
    sign = np.where((d % (2 * quarter)) < quarter, -1.0, 1.0)
    return (jnp.asarray(np.cos(ang), dtype=F32), jnp.asarray(np.sin(ang) * sign, dtype=F32))


def _permute_in_columns(a):
    g0 = Z_ML_W
    s0 = g0 + N_GATES
    pad = jnp.zeros(a.shape[:-1] + (Z_G_W - N_GATES,), a.dtype)
    return jnp.concatenate([a[..., :g0], a[..., s0:], a[..., g0:s0], pad], axis=-1)


def kernel(x_prompt, x_sample, cache_swa_k, cache_swa_v, cache_nat_k, cache_nat_v, state_mlstm_C,
           state_mlstm_n, state_mlstm_m, c, c_ctx, w_ada, b_ada, w_in, b_in, mlstm_fbias,
           mlstm_norm_g, swa_sink, nat_rpb, w_out, ln1_g, ln1_b, w_mlp1, w_mlp2, ln2_g, ln2_b):
    xp = x_prompt.reshape(BATCH * SEQ, D_MODEL)
    xs = x_sample.reshape(DEC_BATCH * DEC_SEQ, D_MODEL)
    cvec = jnp.concatenate(
        [c, c_ctx[None, :], jnp.zeros((MOD_ROWS - DEC_BATCH - 1, D_MODEL), F32)], axis=0)
    mods = _mods_call(cvec, w_ada, b_ada).reshape(DEPTH * MOD_ROWS, 6, D_MODEL)

    w_in_t = jnp.swapaxes(w_in, 1, 2)
    out_scale = np.ones((Z_W,), np.float32)
    out_scale[ML_W:2 * ML_W] = ATT_SCALE
    out_scale[Z_ML_W:Z_ML_W + SWA_W] = Q_SCALE2
    out_scale[Z_ML_W + Z_SWA_W:Z_ML_W + Z_SWA_W + NAT_W] = Q_SCALE2
    b_in_p = jnp.stack([jnp.broadcast_to(out_scale, (DEPTH, Z_W)),
                        _permute_in_columns(b_in) * out_scale], axis=1)
    cache_swa_k, cache_swa_v, cache_nat_k, cache_nat_v = (
        jnp.swapaxes(a, -1, -2) for a in (cache_swa_k, cache_swa_v, cache_nat_k, cache_nat_v))
    vec = lambda a: a.reshape(DEPTH, 1, D_MODEL)
    tail_w = (w_out.astype(BF16), vec(ln1_g), vec(ln1_b), w_mlp1.astype(BF16),
              w_mlp2.astype(BF16), vec(ln2_g), vec(ln2_b))
    cos_t, sin_t = _rope_tables()
    rpb_lanes = _nat_rpb_lanes(nat_rpb) * LOG2E
    swa_sink = swa_sink * LOG2E
    fb_rows = jnp.zeros((DEPTH, 1, Z_G_W), F32)
    fb_rows = fb_rows.at[:, 0, ML_HEADS:2 * ML_HEADS].set(mlstm_fbias[:, 0])
    fb_rows = fb_rows.at[:, 0, 3 * ML_HEADS:4 * ML_HEADS].set(mlstm_fbias[:, 1])
    norm_g = mlstm_norm_g.reshape(DEPTH, 1, ML_W)
    sink_ctx = jnp.repeat(swa_sink, SEQ, axis=1).reshape(DEPTH, SWA_KV_HEADS, SWA_GROUP * SEQ)
    sink_lat = jnp.repeat(swa_sink, SWA_BLOCK, axis=1).reshape(
        DEPTH, SWA_KV_HEADS, SWA_GROUP * SWA_BLOCK)

    n_st = 2 * ML_HEADS
    state_c = state_mlstm_C.reshape(DEC_BATCH, DEPTH, n_st, D_HEAD, D_HEAD)
    state_n = state_mlstm_n.reshape(DEC_BATCH, DEPTH, n_st, D_HEAD)
    state_m = state_mlstm_m.reshape(DEC_BATCH, DEPTH, n_st, 1)

    ctx_row = lambda tile: (lambda i: CTX_MOD_ROW)
    lat_row = lambda tile: (lambda i: i // (DEC_SEQ // tile))

    ctx_state = None
    for l in range(DEPTH):
        m_ml, m_swa, m_nat, *ctx_state = _ctx_mixers_call(
            xp, mods, w_in_t, b_in_p, fb_rows, norm_g, sink_ctx, l, ctx_state)
        xp = _tail_call(xp, m_ml, m_swa, m_nat, mods, ctx_row(TAIL_TILE), tail_w, l)

        z_qk, z_vo, z_swa, z_nat, z_g = _lat_inproj_call(xs, mods, w_in_t, b_in_p, cos_t, sin_t, l)
        m_ml = _lat_mlstm_call(z_qk, z_vo, z_g, fb_rows, norm_g, state_c, state_n, state_m, l)
        m_swa = _lat_swa_call(z_swa, cache_swa_k, cache_swa_v, sink_lat, l)
        m_nat = _lat_nat_call(z_nat, cache_nat_k, cache_nat_v, rpb_lanes, l)
        xs = _tail_call(xs, m_ml, m_swa, m_nat, mods, lat_row(TAIL_TILE), tail_w, l)

    k_swa, v_swa, k_nat, v_nat, st_c, st_n, st_m = ctx_state
    return (xp.reshape(BATCH, SEQ, D_MODEL), xs.reshape(DEC_BATCH, DEC_SEQ, D_MODEL),
            k_swa, v_swa, k_nat, v_nat,
            st_c.reshape(BATCH, DEPTH, 2, ML_HEADS, D_HEAD, D_HEAD),
            st_n.reshape(BATCH, DEPTH, 2, ML_HEADS, D_HEAD),
            st_m[..., 0].reshape(BATCH, DEPTH, 2, ML_HEADS))
```

```python
import functools

import numpy as np
import jax
import jax.numpy as jnp
from jax import lax
from jax.experimental import pallas as pl
from jax.experimental.pallas import tpu as pltpu

D_MODEL = 1024
BATCH = 16
SEQ = 256
DEPTH = 4
DEC_BATCH = 8
DEC_SEQ = 1024
PAST_LEN = 512
GRID_W = 64
D_HEAD = 64
ML_HEADS = 4
SWA_HEADS = 6
SWA_KV_HEADS = 2
SWA_GROUP = SWA_HEADS // SWA_KV_HEADS
NAT_HEADS = 6
ML_W = ML_HEADS * D_HEAD
SWA_W = SWA_HEADS * D_HEAD
SWA_KV_W = SWA_KV_HEADS * D_HEAD
NAT_W = NAT_HEADS * D_HEAD
N_GATES = 4 * ML_HEADS
IN_DIM = 4 * ML_W + N_GATES + SWA_W + 2 * SWA_KV_W + 3 * NAT_W
D_FF = 4 * D_MODEL
SWA_WINDOW = 128
SWA_BLOCK = 128
NAT_KH = 8
NAT_KW = 16
ROPE_BASE = 10000.0
LN_EPS = 1e-5
DN_ALPHA = (2 * DEPTH) ** 0.25
ATT_SCALE = D_HEAD ** -0.5
LOG2E = 1.4426950408889634
Q_SCALE2 = ATT_SCALE * LOG2E

LANES = 128
SUBLANES = 8
Z_ML_W = 4 * ML_W
Z_SWA_W = SWA_W + 2 * SWA_KV_W
Z_NAT_W = 3 * NAT_W
Z_G_W = LANES
Z_W = Z_ML_W + Z_SWA_W + Z_NAT_W + Z_G_W
Z_QK_W = 2 * ML_W
Z_VO_W = 2 * ML_W
Z_GROUPS = (Z_QK_W, Z_VO_W, Z_SWA_W, Z_NAT_W, Z_G_W)
Z_SWA_GROUP = 2
MOD_ROWS = 16
CTX_MOD_ROW = DEC_BATCH
ROW_TILE = 512
TAIL_TILE = 512
TAIL_PARTS = 2
ML_CHUNK = 256
VMEM_LIMIT = 56 * 1024 * 1024

BF16 = jnp.bfloat16
F32 = jnp.float32
NEG_INF = float("-inf")
LAT_Z_DTYPES = (BF16, F32, BF16, BF16, F32)
CTX_Z_DTYPES = (BF16, F32, F32, F32, F32)


def _dot(a, b):
    return jnp.dot(a.astype(BF16), b.astype(BF16), preferred_element_type=F32)


def _dot_nt(a, b):
    return lax.dot_general(a.astype(BF16), b.astype(BF16), (((1,), (1,)), ((), ())),
                           preferred_element_type=F32)


def _dot_split3(tri, x):
    hi = x.astype(BF16)
    r1 = x - hi.astype(F32)
    mid = r1.astype(BF16)
    lo = (r1 - mid.astype(F32)).astype(BF16)
    return (jnp.dot(tri, hi, preferred_element_type=F32)
            + jnp.dot(tri, mid, preferred_element_type=F32)
            + jnp.dot(tri, lo, preferred_element_type=F32))


def _sigmoid(x):
    return 1.0 / (1.0 + jnp.exp(-x))


def _log_sigmoid(x):
    return jnp.minimum(x, 0.0) - jnp.log1p(jnp.exp(-jnp.abs(x)))


def _layer_norm(x, g, b):
    mu = jnp.mean(x, axis=-1, keepdims=True)
    xc = x - mu
    var = jnp.mean(xc * xc, axis=-1, keepdims=True)
    return xc * lax.rsqrt(var + LN_EPS) * g + b


def _params(n_axes=1, flags=None):
    return pltpu.CompilerParams(dimension_semantics=("arbitrary",) * n_axes,
                                vmem_limit_bytes=VMEM_LIMIT, flags=flags)


def _layer_spec(a, layer, single_buffer=False):
    kwargs = {"pipeline_mode": pl.Buffered(1)} if single_buffer else {}
    return pl.BlockSpec((None,) + a.shape[1:], lambda *_: (layer,) + (0,) * (a.ndim - 1), **kwargs)


def _mods_kernel(c_ref, w_ref, b_ref, o_ref):
    c = c_ref[...]
    o_ref[...] = _dot(c * _sigmoid(c), w_ref[...]) + b_ref[...]


def _mods_call(cvec, w_ada, b_ada):
    n_col = 6
    return pl.pallas_call(
        _mods_kernel,
        grid=(DEPTH, n_col),
        in_specs=[
            pl.BlockSpec((MOD_ROWS, D_MODEL), lambda l, j: (0, 0)),
            pl.BlockSpec((None, D_MODEL, D_MODEL), lambda l, j: (l, 0, j)),
            pl.BlockSpec((None, 1, D_MODEL), lambda l, j: (l, 0, j)),
        ],
        out_specs=pl.BlockSpec((None, MOD_ROWS, D_MODEL), lambda l, j: (l, 0, j)),
        out_shape=jax.ShapeDtypeStruct((DEPTH, MOD_ROWS, 6 * D_MODEL), F32),
        compiler_params=_params(2),
        name="mods",
    )(cvec, w_ada, b_ada.reshape(DEPTH, 1, 6 * D_MODEL))


def _rope(x, cos, sin_signed):
    lane = lax.broadcasted_iota(jnp.int32, x.shape, 1)
    first = (lane & 31) < 16
    partner = jnp.where(first, pltpu.roll(x, LANES - 16, 1), pltpu.roll(x, 16, 1))
    return x * cos + partner * sin_signed


def _inproj_kernel(x_ref, mod_ref, wt_ref, b_ref, *rest, rope=False):
    if rope:
        cos_ref, sin_ref, *rest = rest
    *out_refs, w_scr = rest

    @pl.when(pl.program_id(0) == 0)
    def _():
        rest = IN_DIM - Z_ML_W - N_GATES
        w_scr[0:Z_ML_W, :] = wt_ref[0:Z_ML_W, :].astype(BF16)
        w_scr[Z_ML_W:Z_ML_W + rest, :] = wt_ref[Z_ML_W + N_GATES:IN_DIM, :].astype(BF16)
        w_scr[Z_ML_W + rest:IN_DIM, :] = wt_ref[Z_ML_W:Z_ML_W + N_GATES, :].astype(BF16)
        w_scr[IN_DIM:Z_W, :] = jnp.zeros((Z_W - IN_DIM, D_MODEL), BF16)

    h = (x_ref[...] * (1.0 + mod_ref[1:2, :]) + mod_ref[0:1, :]).astype(BF16)
    o = 0
    for ref, w in zip(out_refs, Z_GROUPS):
        z = _dot_nt(h, w_scr[o:o + w, :]) * b_ref[0:1, o:o + w] + b_ref[1:2, o:o + w]
        if rope and ref is out_refs[Z_SWA_GROUP]:
            z = jnp.concatenate(
                [_rope(z[:, j * LANES:(j + 1) * LANES], cos_ref[...], sin_ref[...])
                 for j in range((SWA_W + SWA_KV_W) // LANES)] + [z[:, SWA_W + SWA_KV_W:]], axis=1)
        ref[...] = z.astype(ref.dtype)
        o += w


def _mod_spec(layer, mod_row_of_tile):
    return pl.BlockSpec((None, 6, D_MODEL),
                        lambda i: (layer * MOD_ROWS + mod_row_of_tile(i), 0, 0))


def _lat_inproj_call(x, mods, w_in_t, b_in, cos_t, sin_t, layer):
    rows = x.shape[0]
    tiles_per_seq = DEC_SEQ // ROW_TILE
    table_spec = pl.BlockSpec((ROW_TILE, LANES), lambda i: (i % tiles_per_seq, 0))
    return pl.pallas_call(
        functools.partial(_inproj_kernel, rope=True),
        grid=(rows // ROW_TILE,),
        in_specs=[
            pl.BlockSpec((ROW_TILE, D_MODEL), lambda i: (i, 0)),
            _mod_spec(layer, lambda i: i // tiles_per_seq),
            _layer_spec(w_in_t, layer, single_buffer=True),
            _layer_spec(b_in, layer),
            table_spec, table_spec,
        ],
        out_specs=[pl.BlockSpec((ROW_TILE, w), lambda i: (i, 0)) for w in Z_GROUPS],
        out_shape=[jax.ShapeDtypeStruct((rows, w), dt) for w, dt in zip(Z_GROUPS, LAT_Z_DTYPES)],
        scratch_shapes=[pltpu.VMEM((Z_W, D_MODEL), BF16)],
        compiler_params=_params(1),
        name="inproj",
    )(x, mods, w_in_t, b_in, cos_t, sin_t)


def _tail_kernel(x_ref, mml_ref, mswa_ref, mnat_ref, mod_ref, wout_ref, ln1g_ref, ln1b_ref,
                 w1_ref, w2_ref, ln2g_ref, ln2b_ref, o_ref):
    g_a, sh_m, sc_m, g_m = (mod_ref[i:i + 1, :] for i in (2, 3, 4, 5))
    part = x_ref.shape[0] // TAIL_PARTS
    halves = tuple(slice(i * part, (i + 1) * part) for i in range(TAIL_PARTS))
    projs = [jnp.dot(jnp.concatenate([mml_ref[r, :], mswa_ref[r, :], mnat_ref[r, :]], axis=1),
                     wout_ref[...], preferred_element_type=F32) for r in halves]
    ys, fs = [], []
    for r, proj in zip(halves, projs):
        y = _layer_norm(DN_ALPHA * x_ref[r, :] + g_a * proj, ln1g_ref[...], ln1b_ref[...])
        h = (y * (1.0 + sc_m) + sh_m).astype(BF16)
        ys.append(y)
        fs.append(jnp.dot(h, w1_ref[...], preferred_element_type=F32))
    mlps = []
    for f in fs:
        f = jnp.maximum(f, 0.0)
        mlps.append(jnp.dot((f * f).astype(BF16), w2_ref[...], preferred_element_type=F32))
    for r, y, mlp in zip(halves, ys, mlps):
        o_ref[r, :] = _layer_norm(DN_ALPHA * y + g_m * mlp, ln2g_ref[...], ln2b_ref[...])


def _tail_call(x, mix_ml, mix_swa, mix_nat, mods, mod_row_of_tile, consts, layer):
    rows = x.shape[0]
    row_spec = lambda w: pl.BlockSpec((TAIL_TILE, w), lambda i: (i, 0))
    return pl.pallas_call(
        _tail_kernel,
        grid=(rows // TAIL_TILE,),
        in_specs=[row_spec(D_MODEL), row_spec(ML_W), row_spec(SWA_W), row_spec(NAT_W),
                  _mod_spec(layer, mod_row_of_tile)]
                 + [_layer_spec(a, layer, single_buffer=True) for a in consts],
        out_specs=row_spec(D_MODEL),
        out_shape=jax.ShapeDtypeStruct((rows, D_MODEL), F32),
        compiler_params=_params(1),
        name="tail",
    )(x, mix_ml, mix_swa, mix_nat, mods, *consts)


def _mlstm_run(units):
    kqs, cns = [], []
    for q, k, _, _, _, _, _, _, c_st, n_st, _ in units:
        kqs.append(_dot_nt(k, q))
        cns.append(None if c_st is None else _dot_nt(
            jnp.concatenate([c_st, jnp.broadcast_to(n_st, (SUBLANES, D_HEAD))], axis=0), q))
    mids, nums = [], []
    for kq, (_, _, vt, a_col, b_row, _, neg, _, _, _, m_st) in zip(kqs, units):
        a_m = a_col + neg
        c_row = jnp.maximum(m_st, jnp.max(a_m, axis=0, keepdims=True))
        st = kq * jnp.exp(a_m - c_row)
        mids.append((st, jnp.sum(st, axis=0, keepdims=True), jnp.exp(m_st - c_row), b_row + c_row))
        nums.append(_dot(vt, st))
    upds, m_news = [], []
    for (_, k, vt, _, b_row, li_row, _, last, _, _, _), (_, _, _, mt) in zip(units, mids):
        m_new = mt[:, last:last + 1]
        ws = jnp.exp(b_row[:, last:last + 1] - b_row + li_row - m_new)
        upds.append(_dot(jnp.concatenate(
            [vt * ws, jnp.broadcast_to(ws, (SUBLANES, ws.shape[1]))], axis=0), k))
        m_news.append(m_new)
    results = []
    for u, cn, (_, den, wprev, mt), num, upd, m_new in zip(units, cns, mids, nums, upds, m_news):
        last, c_st, n_st = u[7], u[8], u[9]
        c_new = upd[0:D_HEAD, :]
        n_new = upd[D_HEAD:D_HEAD + 1, :]
        if c_st is not None:
            num = num + wprev * cn[0:D_HEAD, :]
            den = den + wprev * cn[D_HEAD:D_HEAD + 1, :]
            wc = wprev[:, last:last + 1]
            c_new = c_new + wc * c_st
            n_new = n_new + wc * n_st
        ht = num * (1.0 / jnp.maximum(jnp.abs(den), jnp.exp(-mt)))
        results.append((ht, c_new, n_new, m_new))
    return results


def _mlstm_chunk_units(zqk_ref, zvo_ref, g_ref, fb_ref, rows, n, direction, state):
    g = g_ref[rows, :]
    ls = _log_sigmoid(g + fb_ref[...])
    r = lax.broadcasted_iota(jnp.int32, (n, n), 0)
    c = lax.broadcasted_iota(jnp.int32, (n, n), 1)
    if direction == 0:
        tri, neg, last = c <= r, jnp.where(r <= c, 0.0, NEG_INF), n - 1
    else:
        tri, neg, last = c >= r, jnp.where(r >= c, 0.0, NEG_INF), 0
    b = _dot_split3(tri.astype(BF16), ls)
    a = pltpu.roll(g, ML_HEADS, 1) - b
    bt = b.T
    gt = g.T
    i_col = 2 * ML_HEADS * direction
    f_col = i_col + ML_HEADS
    units = []
    for pair in range(ML_HEADS // 2):
        vt2 = zvo_ref[rows, pair * LANES:(pair + 1) * LANES].T
        for sub in range(2):
            hd = 2 * pair + sub
            feat = slice(sub * D_HEAD, (sub + 1) * D_HEAD)
            q = zqk_ref[rows, hd * D_HEAD:(hd + 1) * D_HEAD]
            k = zqk_ref[rows, ML_W + hd * D_HEAD:ML_W + (hd + 1) * D_HEAD]
            units.append((q, k, vt2[feat, :],
                          a[:, f_col + hd:f_col + hd + 1], bt[f_col + hd:f_col + hd + 1, :],
                          gt[i_col + hd:i_col + hd + 1, :], neg, last) + tuple(state[hd]))
    return units


def _mlstm_finish(ht, o_gate, norm_g):
    outs = []
    for hd in range(ML_HEADS):
        x = ht[hd * D_HEAD:(hd + 1) * D_HEAD, :]
        mu = jnp.mean(x, axis=0, keepdims=True)
        xc = x - mu
        var = jnp.mean(xc * xc, axis=0, keepdims=True)
        outs.append(xc * lax.rsqrt(var + LN_EPS))
    hn = jnp.concatenate(outs, axis=0).T
    return hn * norm_g * _sigmoid(o_gate)


def _store_state(c_ref, n_ref, m_ref, idx, c_new, n_new, m_new):
    c_ref[idx] = c_new
    n_ref[idx:idx + 1, :] = n_new
    m_ref[idx:idx + 1, :] = jnp.broadcast_to(m_new, (1, LANES))


ONES_ROWS = 16


def _with_ones(vt):
    ones = jnp.ones((ONES_ROWS, vt.shape[1]), BF16)
    return jnp.concatenate([vt.astype(BF16), ones], axis=0)


def _attend_many(problems):
    scores = []
    for q, parts, _ in problems:
        sts = []
        for k, _, bias_t in parts:
            st = _dot_nt(k, q)
            sts.append(st if bias_t is None else st + bias_t)
        scores.append(sts)
    maxima = []
    for sts, (_, _, sink_row) in zip(scores, problems):
        m = None
        for st in sts:
            mx = jnp.max(st, axis=0, keepdims=True)
            m = mx if m is None else jnp.maximum(m, mx)
        maxima.append(m if sink_row is None else jnp.maximum(m, sink_row))
    outs = []
    for sts, m, (_, parts, sink_row) in zip(scores, maxima, problems):
        ot = None
        for st, (_, vt_ones, _) in zip(sts, parts):
            part = jnp.dot(vt_ones, jnp.exp2(st - m).astype(BF16), preferred_element_type=F32)
            ot = part if ot is None else ot + part
        den = ot[D_HEAD:D_HEAD + 1, :]
        if sink_row is not None:
            den = den + jnp.exp2(sink_row - m)
        outs.append(ot[0:D_HEAD, :] * (1.0 / den))
    return outs


def _head_rows(pairs_t, hd):
    return pairs_t[hd // 2][(hd % 2) * D_HEAD:(hd % 2 + 1) * D_HEAD, :]


def _store_heads_t(o_ref, ots):
    for p in range(len(ots) // 2):
        o_ref[:, p * LANES:(p + 1) * LANES] = jnp.concatenate(
            [ots[2 * p], ots[2 * p + 1]], axis=0).astype(o_ref.dtype).T


N_STATE = 2 * ML_HEADS
N_CTX_STATE_OUTS = 7


def _ctx_mixers_kernel(x_ref, mod_ref, wt_ref, b_ref, fb_ref, ng_ref, sink_ref, *rest,
                       first_layer):
    n_scratch = 1 + len(Z_GROUPS)
    w_scr, zqk_ref, zvo_ref, zswa_ref, znat_ref, zg_ref = rest[-n_scratch:]
    mml_ref, mswa_ref, mnat_ref, *state_refs = rest[-(3 + N_CTX_STATE_OUTS) - n_scratch:-n_scratch]
    _inproj_kernel(x_ref, mod_ref, wt_ref, b_ref, zqk_ref, zvo_ref, zswa_ref, znat_ref, zg_ref,
                   w_scr)
    if first_layer:
        for ref in state_refs:
            ref[1:] = jnp.zeros((DEPTH - 1,) + ref.shape[1:], F32)
        state_refs = [ref.at[0] for ref in state_refs]
    sk_ref, sv_ref, nk_ref, nv_ref, c_ref, n_ref, m_ref = state_refs
    rows = slice(0, SEQ)
    zero_state = [(None, jnp.zeros((1, D_HEAD), F32), jnp.zeros((1, 1), F32))] * ML_HEADS
    units = []
    for direction in range(2):
        units += _mlstm_chunk_units(zqk_ref, zvo_ref, zg_ref, fb_ref, rows, SEQ, direction,
                                    zero_state)
    results = _mlstm_run(units)
    for idx, (_, c_new, n_new, m_new) in enumerate(results):
        _store_state(c_ref, n_ref, m_ref, idx, c_new, n_new, m_new)
    ht_sum = jnp.concatenate([results[hd][0] + results[ML_HEADS + hd][0]
                              for hd in range(ML_HEADS)], axis=0)
    mml_ref[...] = _mlstm_finish(ht_sum, zvo_ref[:, ML_W:2 * ML_W], ng_ref[...]).astype(BF16)

    scaled_q = lambda ref, hd: ref[:, hd * D_HEAD:(hd + 1) * D_HEAD].astype(BF16)
    swa_vts = [zswa_ref[:, SWA_W + SWA_KV_W:SWA_W + 2 * SWA_KV_W].astype(BF16).T]
    nat_vts = [znat_ref[:, 2 * NAT_W + p * LANES:2 * NAT_W + (p + 1) * LANES].astype(BF16).T
               for p in range(NAT_W // LANES)]
    problems = []
    for kv in range(SWA_KV_HEADS):
        k = zswa_ref[:, SWA_W + kv * D_HEAD:SWA_W + (kv + 1) * D_HEAD]
        sk_ref[kv] = k
        sv_ref[kv] = zswa_ref[:, SWA_W + SWA_KV_W + kv * D_HEAD:SWA_W + SWA_KV_W + (kv + 1) * D_HEAD]
        q = jnp.concatenate([scaled_q(zswa_ref, kv * SWA_GROUP + g) for g in range(SWA_GROUP)],
                            axis=0)
        problems.append((q, [(k.astype(BF16), _with_ones(_head_rows(swa_vts, kv)), None)],
                         sink_ref[kv:kv + 1, :]))
    for hd in range(NAT_HEADS):
        k = znat_ref[:, NAT_W + hd * D_HEAD:NAT_W + (hd + 1) * D_HEAD]
        nk_ref[hd] = k
        nv_ref[hd] = znat_ref[:, 2 * NAT_W + hd * D_HEAD:2 * NAT_W + (hd + 1) * D_HEAD]
        problems.append((scaled_q(znat_ref, hd),
                         [(k.astype(BF16), _with_ones(_head_rows(nat_vts, hd)), None)], None))
    ots = _attend_many(problems)
    _store_heads_t(mswa_ref, [ot[:, g * SEQ:(g + 1) * SEQ]
                              for ot in ots[:SWA_KV_HEADS] for g in range(SWA_GROUP)])
    _store_heads_t(mnat_ref, ots[SWA_KV_HEADS:])


CTX_STATE_DIMS = ((SWA_KV_HEADS, SEQ, D_HEAD), (SWA_KV_HEADS, SEQ, D_HEAD),
                  (NAT_HEADS, SEQ, D_HEAD), (NAT_HEADS, SEQ, D_HEAD),
                  (N_STATE, D_HEAD, D_HEAD), (N_STATE, D_HEAD), (N_STATE, LANES))


def _ctx_mixers_call(x, mods, w_in_t, b_in, fb_row, norm_g, sink_rows, layer, prev_state):
    rows = x.shape[0]
    row_spec = lambda w: pl.BlockSpec((SEQ, w), lambda b: (b, 0))
    state_dims = CTX_STATE_DIMS
    first_layer = prev_state is None
    if first_layer:
        assert layer == 0
        state_spec = lambda d: pl.BlockSpec((None, DEPTH) + d, lambda b: (b,) + (0,) * (len(d) + 1))
    else:
        state_spec = lambda d: pl.BlockSpec((None, None) + d, lambda b: (b, layer) + (0,) * len(d))
    prev = () if first_layer else tuple(prev_state)
    n_in = 7
    return pl.pallas_call(
        functools.partial(_ctx_mixers_kernel, first_layer=first_layer),
        grid=(BATCH,),
        in_specs=[row_spec(D_MODEL), _mod_spec(layer, lambda b: CTX_MOD_ROW),
                  _layer_spec(w_in_t, layer, single_buffer=True), _layer_spec(b_in, layer),
                  _layer_spec(fb_row, layer), _layer_spec(norm_g, layer),
                  _layer_spec(sink_rows, layer)]
                 + [pl.BlockSpec(memory_space=pl.ANY)] * len(prev),
        out_specs=[row_spec(ML_W), row_spec(SWA_W), row_spec(NAT_W)]
                  + [state_spec(d) for d in state_dims],
        out_shape=[jax.ShapeDtypeStruct((rows, ML_W), BF16),
                   jax.ShapeDtypeStruct((rows, SWA_W), BF16),
                   jax.ShapeDtypeStruct((rows, NAT_W), BF16)]
                  + [jax.ShapeDtypeStruct((BATCH, DEPTH) + d, F32) for d in state_dims],
        input_output_aliases={n_in + i: 3 + i for i in range(len(prev))},
        scratch_shapes=[pltpu.VMEM((Z_W, D_MODEL), BF16)]
                       + [pltpu.VMEM((SEQ, w), dt) for w, dt in zip(Z_GROUPS, CTX_Z_DTYPES)],
        compiler_params=_params(1),
        name="ctx_mixers",
    )(x, mods, w_in_t, b_in, fb_row, norm_g, sink_rows, *prev)


def _lat_mlstm_kernel(zqk_ref, zvo_ref, zg_ref, fb_ref, ng_ref, c0_ref, n0_ref, m0_ref, mml_ref,
                      ht_scr, c_scr, n_scr, m_scr):
    n_chunks = DEC_SEQ // ML_CHUNK
    c_scr[...] = c0_ref[...]
    n_scr[...] = n0_ref[...]
    m_scr[...] = jnp.broadcast_to(m0_ref[...], m_scr.shape)

    def body(i, carry):
        chunks = (i, n_chunks - 1 - i)
        units = []
        for direction, chunk in enumerate(chunks):
            rows = pl.ds(pl.multiple_of(chunk * ML_CHUNK, ML_CHUNK), ML_CHUNK)
            base = direction * ML_HEADS
            state = [(c_scr[base + hd], n_scr[base + hd:base + hd + 1, :],
                      m_scr[base + hd:base + hd + 1, 0:1]) for hd in range(ML_HEADS)]
            units += _mlstm_chunk_units(zqk_ref, zvo_ref, zg_ref, fb_ref, rows, ML_CHUNK,
                                        direction, state)
        results = _mlstm_run(units)
        for direction, chunk in enumerate(chunks):
            base = direction * ML_HEADS
            ht_scr[direction, chunk] = jnp.concatenate(
                [results[base + hd][0] for hd in range(ML_HEADS)], axis=0)
        for idx, (_, c_new, n_new, m_new) in enumerate(results):
            _store_state(c_scr, n_scr, m_scr, idx, c_new, n_new, m_new)
        return carry

    lax.fori_loop(0, n_chunks, body, 0)
    for chunk in range(n_chunks):
        rows = slice(chunk * ML_CHUNK, (chunk + 1) * ML_CHUNK)
        mml_ref[rows, :] = _mlstm_finish(ht_scr[0, chunk] + ht_scr[1, chunk],
                                         zvo_ref[rows, ML_W:2 * ML_W],
                                         ng_ref[...]).astype(BF16)


def _lat_mlstm_call(z_qk, z_vo, z_g, fb_row, norm_g, state_c, state_n, state_m, layer):
    n_st = 2 * ML_HEADS
    n_chunks = DEC_SEQ // ML_CHUNK
    row_spec = lambda w: pl.BlockSpec((DEC_SEQ, w), lambda b: (b, 0))
    return pl.pallas_call(
        _lat_mlstm_kernel,
        grid=(DEC_BATCH,),
        in_specs=[row_spec(Z_QK_W), row_spec(Z_VO_W), row_spec(Z_G_W),
                  _layer_spec(fb_row, layer), _layer_spec(norm_g, layer),
                  pl.BlockSpec((None, None, n_st, D_HEAD, D_HEAD), lambda b: (b, layer, 0, 0, 0)),
                  pl.BlockSpec((None, None, n_st, D_HEAD), lambda b: (b, layer, 0, 0)),
                  pl.BlockSpec((None, None, n_st, 1), lambda b: (b, layer, 0, 0))],
        out_specs=row_spec(ML_W),
        out_shape=jax.ShapeDtypeStruct((DEC_BATCH * DEC_SEQ, ML_W), BF16),
        scratch_shapes=[pltpu.VMEM((2, n_chunks, ML_W, ML_CHUNK), F32),
                        pltpu.VMEM((n_st, D_HEAD, D_HEAD), F32), pltpu.VMEM((n_st, D_HEAD), F32),
                        pltpu.VMEM((n_st, LANES), F32)],
        compiler_params=_params(1),
        name="lat_mlstm",
    )(z_qk, z_vo, z_g, fb_row, norm_g, state_c, state_n, state_m)


def _lat_swa_kernel(zswa_ref, ckt_ref, cvt_ref, sink_ref, mswa_ref,
                    vt_scr, ot_scr, bias_scr, ck_scr, cvt_scr):
    vt_scr[...] = zswa_ref[:, SWA_W + SWA_KV_W:SWA_W + 2 * SWA_KV_W].T

    bl = SWA_BLOCK
    nb = DEC_SEQ // bl
    n_q = SWA_GROUP * bl
    kj = lax.broadcasted_iota(jnp.int32, (3 * bl, n_q), 0)
    qi = lax.broadcasted_iota(jnp.int32, (3 * bl, n_q), 1) & (bl - 1)
    keep = ((kj >= qi) & (kj < 2 * bl)) | ((kj >= 2 * bl) & (kj - 2 * bl <= qi))
    bias_scr[...] = jnp.where(keep, 0.0, NEG_INF)
    ck_scr[...] = jnp.concatenate([ckt_ref[kv] for kv in range(SWA_KV_HEADS)],
                                  axis=0).astype(BF16).T
    for kv in range(SWA_KV_HEADS):
        cvt_scr[kv] = _with_ones(cvt_ref[kv])

    problems, where = [], []
    for kv in range(SWA_KV_HEADS):
        head = slice(kv * D_HEAD, (kv + 1) * D_HEAD)
        for blk in range(nb):
            lo_b, hi_b = max(blk - 1, 0), min(blk + 2, nb)
            keys = slice(lo_b * bl, hi_b * bl)
            q = jnp.concatenate(
                [zswa_ref[blk * bl:(blk + 1) * bl,
                          (kv * SWA_GROUP + g) * D_HEAD:(kv * SWA_GROUP + g + 1) * D_HEAD]
                 for g in range(SWA_GROUP)], axis=0)
            parts = [(zswa_ref[keys, SWA_W + kv * D_HEAD:SWA_W + (kv + 1) * D_HEAD],
                      _with_ones(vt_scr[head, keys]),
                      bias_scr[(lo_b - blk + 1) * bl:(hi_b - blk + 1) * bl, :]),
                     (ck_scr[:, head], cvt_scr[kv], None)]
            problems.append((q, parts, sink_ref[kv:kv + 1, :]))
            where.append((kv, blk))
    for (kv, blk), ot in zip(where, _attend_many(problems)):
        o = ot.astype(BF16)
        for g in range(SWA_GROUP):
            hq = kv * SWA_GROUP + g
            ot_scr[hq * D_HEAD:(hq + 1) * D_HEAD, blk * bl:(blk + 1) * bl] = (
                o[:, g * bl:(g + 1) * bl])
    for p in range(SWA_W // LANES):
        mswa_ref[:, p * LANES:(p + 1) * LANES] = ot_scr[p * LANES:(p + 1) * LANES, :].T


def _lat_swa_call(z_swa, cache_k, cache_v, sink_rows, layer):
    row_spec = lambda w: pl.BlockSpec((DEC_SEQ, w), lambda b: (b, 0))
    cache_spec = pl.BlockSpec((None, None, SWA_KV_HEADS, D_HEAD, PAST_LEN),
                              lambda b: (b, layer, 0, 0, 0))
    n_q = SWA_GROUP * SWA_BLOCK
    return pl.pallas_call(
        _lat_swa_kernel,
        grid=(DEC_BATCH,),
        in_specs=[row_spec(Z_SWA_W), cache_spec, cache_spec, _layer_spec(sink_rows, layer)],
        out_specs=row_spec(SWA_W),
        out_shape=jax.ShapeDtypeStruct((DEC_BATCH * DEC_SEQ, SWA_W), BF16),
        scratch_shapes=[pltpu.VMEM((SWA_KV_W, DEC_SEQ), BF16), pltpu.VMEM((SWA_W, DEC_SEQ), BF16),
                        pltpu.VMEM((3 * SWA_BLOCK, n_q), F32),
                        pltpu.VMEM((PAST_LEN, SWA_KV_W), BF16),
                        pltpu.VMEM((SWA_KV_HEADS, D_HEAD + ONES_ROWS, PAST_LEN), BF16)],
        compiler_params=_params(1),
        name="lat_swa",
    )(z_swa, cache_k, cache_v, sink_rows)


NAT_ROWS = DEC_SEQ // GRID_W
NAT_GROUP_ROWS = 4
NAT_GROUP_Q = NAT_GROUP_ROWS * GRID_W
NAT_RPB_ROWS = 2 * NAT_KH


def _nat_row_start(r):
    return min(max(r - NAT_KH // 2, 0), NAT_ROWS - NAT_KH)


def _nat_groups():
    groups, off = [], 0
    for g in range(NAT_ROWS // NAT_GROUP_ROWS):
        w0 = _nat_row_start(g * NAT_GROUP_ROWS)
        w1 = _nat_row_start((g + 1) * NAT_GROUP_ROWS - 1) + NAT_KH
        n_rows = w1 - w0 + (w1 - w0) % 2
        w0 = min(w0, NAT_ROWS - n_rows)
        groups.append((w0, n_rows, off))
        off += n_rows * GRID_W
    return groups, off


NAT_GROUPS, NAT_BIAS_KEYS = _nat_groups()


def _nat_build_bias(rpb_ref, bias_scr):
    shape = (GRID_W, LANES)
    q = lax.broadcasted_iota(jnp.int32, shape, 0)
    lane = lax.broadcasted_iota(jnp.int32, shape, 1)
    kc = lane & (GRID_W - 1)
    cs = jnp.clip(q - NAT_KW // 2, 0, GRID_W - NAT_KW)
    ok = (kc >= cs) & (kc < cs + NAT_KW)
    ok_lo = ok & (lane < GRID_W)
    ok_hi = ok & (lane >= GRID_W)
    neg_tile = jnp.full(shape, NEG_INF, F32)
    for hd in range(2):
        lo, hi = [], []
        for ro in range(2 * NAT_KH - 1):
            x = jnp.broadcast_to(rpb_ref[hd, ro:ro + 1, :], shape)
            lo.append(jnp.where(ok_lo, pltpu.roll(x, 0, 1, stride=1, stride_axis=0), NEG_INF))
            hi.append(jnp.where(ok_hi, pltpu.roll(x, GRID_W, 1, stride=1, stride_axis=0), NEG_INF))
        for g, (w0, n_rows, off) in enumerate(NAT_GROUPS):
            for jj in range(n_rows // 2):
                kra, krb = w0 + 2 * jj, w0 + 2 * jj + 1
                tiles = []
                for rr in range(NAT_GROUP_ROWS):
                    r = g * NAT_GROUP_ROWS + rr
                    r0 = _nat_row_start(r)
                    parts = []
                    if r0 <= kra < r0 + NAT_KH:
                        parts.append(lo[kra - r + NAT_KH - 1])
                    if r0 <= krb < r0 + NAT_KH:
                        parts.append(hi[krb - r + NAT_KH - 1])
                    tiles.append(neg_tile if not parts else (
                        parts[0] if len(parts) == 1 else jnp.maximum(parts[0], parts[1])))
                bias_scr[hd, off + jj * LANES:off + (jj + 1) * LANES, :] = (
                    jnp.concatenate(tiles, axis=0).T)


def _lat_nat_kernel(q_ref, k_ref, v_ref, ckt_ref, cvt_ref, rpb_ref, mnat_ref, bias_scr):
    @pl.when(pl.program_id(1) == 0)
    def _():
        _nat_build_bias(rpb_ref, bias_scr)

    vt = v_ref[...].astype(BF16).T
    ck2 = jnp.concatenate([ckt_ref[0], ckt_ref[1]], axis=0).astype(BF16).T
    problems = []
    for hd in range(2):
        head = slice(hd * D_HEAD, (hd + 1) * D_HEAD)
        ck = ck2[:, head]
        cvt_ones = _with_ones(cvt_ref[hd])
        for g, (w0, n_rows, off) in enumerate(NAT_GROUPS):
            keys = slice(w0 * GRID_W, (w0 + n_rows) * GRID_W)
            parts = [(k_ref[keys, head].astype(BF16), _with_ones(vt[head, keys]),
                      bias_scr[hd, off:off + n_rows * GRID_W, :]),
                     (ck, cvt_ones, None)]
            q = q_ref[g * NAT_GROUP_Q:(g + 1) * NAT_GROUP_Q, head].astype(BF16)
            problems.append((q, parts, None))
    ots = _attend_many(problems)
    n_groups = len(NAT_GROUPS)
    mnat_ref[...] = jnp.concatenate(
        [jnp.concatenate(ots[hd * n_groups:(hd + 1) * n_groups], axis=1) for hd in range(2)],
        axis=0).astype(BF16).T


def _lat_nat_call(z_nat, cache_k, cache_v, rpb_lanes, layer):
    n_pair = NAT_HEADS // 2
    col_spec = lambda off: pl.BlockSpec((DEC_SEQ, LANES), lambda p, b: (b, off + p))
    cache_spec = pl.BlockSpec((None, None, 2, D_HEAD, PAST_LEN), lambda p, b: (b, layer, p, 0, 0))
    return pl.pallas_call(
        _lat_nat_kernel,
        grid=(n_pair, DEC_BATCH),
        in_specs=[col_spec(0), col_spec(n_pair), col_spec(2 * n_pair), cache_spec, cache_spec,
                  pl.BlockSpec((None, 2, NAT_RPB_ROWS, LANES), lambda p, b: (layer, p, 0, 0))],
        out_specs=pl.BlockSpec((DEC_SEQ, LANES), lambda p, b: (b, p)),
        out_shape=jax.ShapeDtypeStruct((DEC_BATCH * DEC_SEQ, NAT_W), BF16),
        scratch_shapes=[pltpu.VMEM((2, NAT_BIAS_KEYS, NAT_GROUP_Q), F32)],
        compiler_params=_params(2),
        name="lat_nat",
    )(z_nat, z_nat, z_nat, cache_k, cache_v, rpb_lanes)


def _nat_rpb_lanes(rpb):
    n_off = 2 * NAT_KW - 1
    padded = jnp.concatenate(
        [rpb.astype(F32), jnp.zeros(rpb.shape[:-1] + (LANES - n_off,), F32)], axis=-1)
    rolled = jnp.roll(padded, -(NAT_KW - 1), axis=-1)
    pad_rows = jnp.zeros(rpb.shape[:-2] + (NAT_RPB_ROWS - rpb.shape[-2], LANES), F32)
    return jnp.concatenate([rolled, pad_rows], axis=-2)


def _rope_tables():
    t = np.arange(DEC_SEQ)[:, None]
    d = np.arange(LANES)[None, :] % D_HEAD
    pos = np.where(d < D_HEAD // 2, t // GRID_W, t % GRID_W).astype(np.float64)
    quarter = D_HEAD // 4
    freq = ROPE_BASE ** (-(d % quarter).astype(np.float64) / quarter)
    ang = (pos.astype(np.float32) * freq.astype(np.float32)).astype(np.float32)
    sign = np.where((d % (2 * quarter)) < quarter, -1.0, 1.0)
    return (jnp.asarray(np.cos(ang), dtype=F32), jnp.asarray(np.sin(ang) * sign, dtype=F32))


def _permute_in_columns(a):
    g0 = Z_ML_W
    s0 = g0 + N_GATES
    pad = jnp.zeros(a.shape[:-1] + (Z_G_W - N_GATES,), a.dtype)
    return jnp.concatenate([a[..., :g0], a[..., s0:], a[..., g0:s0], pad], axis=-1)


def kernel(x_prompt, x_sample, cache_swa_k, cache_swa_v, cache_nat_k, cache_nat_v, state_mlstm_C,
           state_mlstm_n, state_mlstm_m, c, c_ctx, w_ada, b_ada, w_in, b_in, mlstm_fbias,
           mlstm_norm_g, swa_sink, nat_rpb, w_out, ln1_g, ln1_b, w_mlp1, w_mlp2, ln2_g, ln2_b):
    xp = x_prompt.reshape(BATCH * SEQ, D_MODEL)
    xs = x_sample.reshape(DEC_BATCH * DEC_SEQ, D_MODEL)
    cvec = jnp.concatenate(
        [c, c_ctx[None, :], jnp.zeros((MOD_ROWS - DEC_BATCH - 1, D_MODEL), F32)], axis=0)
    mods = _mods_call(cvec, w_ada, b_ada).reshape(DEPTH * MOD_ROWS, 6, D_MODEL)

    w_in_t = jnp.swapaxes(w_in, 1, 2)
    out_scale = np.ones((Z_W,), np.float32)
    out_scale[ML_W:2 * ML_W] = ATT_SCALE
    out_scale[Z_ML_W:Z_ML_W + SWA_W] = Q_SCALE2
    out_scale[Z_ML_W + Z_SWA_W:Z_ML_W + Z_SWA_W + NAT_W] = Q_SCALE2
    b_in_p = jnp.stack([jnp.broadcast_to(out_scale, (DEPTH, Z_W)),
                        _permute_in_columns(b_in) * out_scale], axis=1)
    cache_swa_k, cache_swa_v, cache_nat_k, cache_nat_v = (
        jnp.swapaxes(a, -1, -2) for a in (cache_swa_k, cache_swa_v, cache_nat_k, cache_nat_v))
    vec = lambda a: a.reshape(DEPTH, 1, D_MODEL)
    tail_w = (w_out.astype(BF16), vec(ln1_g), vec(ln1_b), w_mlp1.astype(BF16),
              w_mlp2.astype(BF16), vec(ln2_g), vec(ln2_b))
    cos_t, sin_t = _rope_tables()
    rpb_lanes = _nat_rpb_lanes(nat_rpb) * LOG2E
    swa_sink = swa_sink * LOG2E
    fb_rows = jnp.zeros((DEPTH, 1, Z_G_W), F32)
    fb_rows = fb_rows.at[:, 0, ML_HEADS:2 * ML_HEADS].set(mlstm_fbias[:, 0])
    fb_rows = fb_rows.at[:, 0, 3 * ML_HEADS:4 * ML_HEADS].set(mlstm_fbias[:, 1])
    norm_g = mlstm_norm_g.reshape(DEPTH, 1, ML_W)
    sink_ctx = jnp.repeat(swa_sink, SEQ, axis=1).reshape(DEPTH, SWA_KV_HEADS, SWA_GROUP * SEQ)
    sink_lat = jnp.repeat(swa_sink, SWA_BLOCK, axis=1).reshape(
        DEPTH, SWA_KV_HEADS, SWA_GROUP * SWA_BLOCK)

    n_st = 2 * ML_HEADS
    state_c = state_mlstm_C.reshape(DEC_BATCH, DEPTH, n_st, D_HEAD, D_HEAD)
    state_n = state_mlstm_n.reshape(DEC_BATCH, DEPTH, n_st, D_HEAD)
    state_m = state_mlstm_m.reshape(DEC_BATCH, DEPTH, n_st, 1)

    ctx_row = lambda tile: (lambda i: CTX_MOD_ROW)
    lat_row = lambda tile: (lambda i: i // (DEC_SEQ // tile))

    ctx_state = None
    for l in range(DEPTH):
        m_ml, m_swa, m_nat, *ctx_state = _ctx_mixers_call(
            xp, mods, w_in_t, b_in_p, fb_rows, norm_g, sink_ctx, l, ctx_state)
        xp = _tail_call(xp, m_ml, m_swa, m_nat, mods, ctx_row(TAIL_TILE), tail_w, l)

        z_qk, z_vo, z_swa, z_nat, z_g = _lat_inproj_call(xs, mods, w_in_t, b_in_p, cos_t, sin_t, l)
        m_ml = _lat_mlstm_call(z_qk, z_vo, z_g, fb_rows, norm_g, state_c, state_n, state_m, l)
        m_swa = _lat_swa_call(z_swa, cache_swa_k, cache_swa_v, sink_lat, l)
        m_nat = _lat_nat_call(z_nat, cache_nat_k, cache_nat_v, rpb_lanes, l)
        xs = _tail_call(xs, m_ml, m_swa, m_nat, mods, lat_row(TAIL_TILE), tail_w, l)

    k_swa, v_swa, k_nat, v_nat, st_c, st_n, st_m = ctx_state
    return (xp.reshape(BATCH, SEQ, D_MODEL), xs.reshape(DEC_BATCH, DEC_SEQ, D_MODEL),
            k_swa, v_swa, k_nat, v_nat,
            st_c.reshape(BATCH, DEPTH, 2, ML_HEADS, D_HEAD, D_HEAD),
            st_n.reshape(BATCH, DEPTH, 2, ML_HEADS, D_HEAD),
            st_m[..., 0].reshape(BATCH, DEPTH, 2, ML_HEADS))
```

```python
import functools

import numpy as np
import jax
import jax.numpy as jnp
from jax import lax
from jax.experimental import pallas as pl
from jax.experimental.pallas import tpu as pltpu

D_MODEL = 1024
BATCH = 16
SEQ = 256
DEPTH = 4
DEC_BATCH = 8
DEC_SEQ = 1024
PAST_LEN = 512
GRID_W = 64
D_HEAD = 64
ML_HEADS = 4
SWA_HEADS = 6
SWA_KV_HEADS = 2
SWA_GROUP = SWA_HEADS // SWA_KV_HEADS
NAT_HEADS = 6
ML_W = ML_HEADS * D_HEAD
SWA_W = SWA_HEADS * D_HEAD
SWA_KV_W = SWA_KV_HEADS * D_HEAD
NAT_W = NAT_HEADS * D_HEAD
N_GATES = 4 * ML_HEADS
IN_DIM = 4 * ML_W + N_GATES + SWA_W + 2 * SWA_KV_W + 3 * NAT_W
D_FF = 4 * D_MODEL
SWA_WINDOW = 128
SWA_BLOCK = 128
NAT_KH = 8
NAT_KW = 16
ROPE_BASE = 10000.0
LN_EPS = 1e-5
DN_ALPHA = (2 * DEPTH) ** 0.25
ATT_SCALE = D_HEAD ** -0.5
LOG2E = 1.4426950408889634
Q_SCALE2 = ATT_SCALE * LOG2E

LANES = 128
SUBLANES = 8
Z_ML_W = 4 * ML_W
Z_SWA_W = SWA_W + 2 * SWA_KV_W
Z_NAT_W = 3 * NAT_W
Z_G_W = LANES
Z_W = Z_ML_W + Z_SWA_W + Z_NAT_W + Z_G_W
Z_QK_W = 2 * ML_W
Z_VO_W = 2 * ML_W
Z_GROUPS = (Z_QK_W, Z_VO_W, Z_SWA_W, Z_NAT_W, Z_G_W)
Z_SWA_GROUP = 2
MOD_ROWS = 16
CTX_MOD_ROW = DEC_BATCH
ROW_TILE = 512
TAIL_TILE = 512
TAIL_PARTS = 2
ML_CHUNK = 256
VMEM_LIMIT = 56 * 1024 * 1024

BF16 = jnp.bfloat16
F32 = jnp.float32
NEG_INF = float("-inf")
LAT_Z_DTYPES = (BF16, F32, BF16, BF16, F32)
CTX_Z_DTYPES = (BF16, F32, F32, F32, F32)


def _dot(a, b):
    return jnp.dot(a.astype(BF16), b.astype(BF16), preferred_element_type=F32)


def _dot_nt(a, b):
    return lax.dot_general(a.astype(BF16), b.astype(BF16), (((1,), (1,)), ((), ())),
                           preferred_element_type=F32)


def _dot_split3(tri, x):
    hi = x.astype(BF16)
    r1 = x - hi.astype(F32)
    mid = r1.astype(BF16)
    lo = (r1 - mid.astype(F32)).astype(BF16)
    return (jnp.dot(tri, hi, preferred_element_type=F32)
            + jnp.dot(tri, mid, preferred_element_type=F32)
            + jnp.dot(tri, lo, preferred_element_type=F32))


def _sigmoid(x):
    return 1.0 / (1.0 + jnp.exp(-x))


def _log_sigmoid(x):
    return jnp.minimum(x, 0.0) - jnp.log1p(jnp.exp(-jnp.abs(x)))


def _layer_norm(x, g, b):
    mu = jnp.mean(x, axis=-1, keepdims=True)
    xc = x - mu
    var = jnp.mean(xc * xc, axis=-1, keepdims=True)
    return xc * lax.rsqrt(var + LN_EPS) * g + b


def _params(n_axes=1, flags=None):
    return pltpu.CompilerParams(dimension_semantics=("arbitrary",) * n_axes,
                                vmem_limit_bytes=VMEM_LIMIT, flags=flags)


def _layer_spec(a, layer, single_buffer=False):
    kwargs = {"pipeline_mode": pl.Buffered(1)} if single_buffer else {}
    return pl.BlockSpec((None,) + a.shape[1:], lambda *_: (layer,) + (0,) * (a.ndim - 1), **kwargs)


def _mods_kernel(c_ref, w_ref, b_ref, o_ref):
    c = c_ref[...]
    o_ref[...] = _dot(c * _sigmoid(c), w_ref[...]) + b_ref[...]


def _mods_call(cvec, w_ada, b_ada):
    n_col = 6
    return pl.pallas_call(
        _mods_kernel,
        grid=(DEPTH, n_col),
        in_specs=[
            pl.BlockSpec((MOD_ROWS, D_MODEL), lambda l, j: (0, 0)),
            pl.BlockSpec((None, D_MODEL, D_MODEL), lambda l, j: (l, 0, j)),
            pl.BlockSpec((None, 1, D_MODEL), lambda l, j: (l, 0, j)),
        ],
        out_specs=pl.BlockSpec((None, MOD_ROWS, D_MODEL), lambda l, j: (l, 0, j)),
        out_shape=jax.ShapeDtypeStruct((DEPTH, MOD_ROWS, 6 * D_MODEL), F32),
        compiler_params=_params(2),
        name="mods",
    )(cvec, w_ada, b_ada.reshape(DEPTH, 1, 6 * D_MODEL))


def _rope(x, cos, sin_signed):
    lane = lax.broadcasted_iota(jnp.int32, x.shape, 1)
    first = (lane & 31) < 16
    partner = jnp.where(first, pltpu.roll(x, LANES - 16, 1), pltpu.roll(x, 16, 1))
    return x * cos + partner * sin_signed


def _inproj_kernel(x_ref, mod_ref, wt_ref, b_ref, *rest, rope=False):
    if rope:
        cos_ref, sin_ref, *rest = rest
    *out_refs, w_scr = rest

    @pl.when(pl.program_id(0) == 0)
    def _():
        rest = IN_DIM - Z_ML_W - N_GATES
        w_scr[0:Z_ML_W, :] = wt_ref[0:Z_ML_W, :].astype(BF16)
        w_scr[Z_ML_W:Z_ML_W + rest, :] = wt_ref[Z_ML_W + N_GATES:IN_DIM, :].astype(BF16)
        w_scr[Z_ML_W + rest:IN_DIM, :] = wt_ref[Z_ML_W:Z_ML_W + N_GATES, :].astype(BF16)
        w_scr[IN_DIM:Z_W, :] = jnp.zeros((Z_W - IN_DIM, D_MODEL), BF16)

    h = (x_ref[...] * (1.0 + mod_ref[1:2, :]) + mod_ref[0:1, :]).astype(BF16)
    o = 0
    for ref, w in zip(out_refs, Z_GROUPS):
        z = _dot_nt(h, w_scr[o:o + w, :]) * b_ref[0:1, o:o + w] + b_ref[1:2, o:o + w]
        if rope and ref is out_refs[Z_SWA_GROUP]:
            z = jnp.concatenate(
                [_rope(z[:, j * LANES:(j + 1) * LANES], cos_ref[...], sin_ref[...])
                 for j in range((SWA_W + SWA_KV_W) // LANES)] + [z[:, SWA_W + SWA_KV_W:]], axis=1)
        ref[...] = z.astype(ref.dtype)
        o += w


def _mod_spec(layer, mod_row_of_tile):
    return pl.BlockSpec((None, 6, D_MODEL),
                        lambda i: (layer * MOD_ROWS + mod_row_of_tile(i), 0, 0))


def _lat_inproj_call(x, mods, w_in_t, b_in, cos_t, sin_t, layer):
    rows = x.shape[0]
    tiles_per_seq = DEC_SEQ // ROW_TILE
    table_spec = pl.BlockSpec((ROW_TILE, LANES), lambda i: (i % tiles_per_seq, 0))
    return pl.pallas_call(
        functools.partial(_inproj_kernel, rope=True),
        grid=(rows // ROW_TILE,),
        in_specs=[
            pl.BlockSpec((ROW_TILE, D_MODEL), lambda i: (i, 0)),
            _mod_spec(layer, lambda i: i // tiles_per_seq),
            _layer_spec(w_in_t, layer, single_buffer=True),
            _layer_spec(b_in, layer),
            table_spec, table_spec,
        ],
        out_specs=[pl.BlockSpec((ROW_TILE, w), lambda i: (i, 0)) for w in Z_GROUPS],
        out_shape=[jax.ShapeDtypeStruct((rows, w), dt) for w, dt in zip(Z_GROUPS, LAT_Z_DTYPES)],
        scratch_shapes=[pltpu.VMEM((Z_W, D_MODEL), BF16)],
        compiler_params=_params(1),
        name="inproj",
    )(x, mods, w_in_t, b_in, cos_t, sin_t)


def _tail_kernel(x_ref, mml_ref, mswa_ref, mnat_ref, mod_ref, wout_ref, ln1g_ref, ln1b_ref,
                 w1_ref, w2_ref, ln2g_ref, ln2b_ref, o_ref):
    g_a, sh_m, sc_m, g_m = (mod_ref[i:i + 1, :] for i in (2, 3, 4, 5))
    part = x_ref.shape[0] // TAIL_PARTS
    halves = tuple(slice(i * part, (i + 1) * part) for i in range(TAIL_PARTS))
    projs = [jnp.dot(jnp.concatenate([mml_ref[r, :], mswa_ref[r, :], mnat_ref[r, :]], axis=1),
                     wout_ref[...], preferred_element_type=F32) for r in halves]
    ys, fs = [], []
    for r, proj in zip(halves, projs):
        y = _layer_norm(DN_ALPHA * x_ref[r, :] + g_a * proj, ln1g_ref[...], ln1b_ref[...])
        h = (y * (1.0 + sc_m) + sh_m).astype(BF16)
        ys.append(y)
        fs.append(jnp.dot(h, w1_ref[...], preferred_element_type=F32))
    mlps = []
    for f in fs:
        f = jnp.maximum(f, 0.0)
        mlps.append(jnp.dot((f * f).astype(BF16), w2_ref[...], preferred_element_type=F32))
    for r, y, mlp in zip(halves, ys, mlps):
        o_ref[r, :] = _layer_norm(DN_ALPHA * y + g_m * mlp, ln2g_ref[...], ln2b_ref[...])


def _tail_call(x, mix_ml, mix_swa, mix_nat, mods, mod_row_of_tile, consts, layer):
    rows = x.shape[0]
    row_spec = lambda w: pl.BlockSpec((TAIL_TILE, w), lambda i: (i, 0))
    return pl.pallas_call(
        _tail_kernel,
        grid=(rows // TAIL_TILE,),
        in_specs=[row_spec(D_MODEL), row_spec(ML_W), row_spec(SWA_W), row_spec(NAT_W),
                  _mod_spec(layer, mod_row_of_tile)]
                 + [_layer_spec(a, layer, single_buffer=True) for a in consts],
        out_specs=row_spec(D_MODEL),
        out_shape=jax.ShapeDtypeStruct((rows, D_MODEL), F32),
        compiler_params=_params(1),
        name="tail",
    )(x, mix_ml, mix_swa, mix_nat, mods, *consts)


def _mlstm_run(units):
    kqs, cns = [], []
    for q, k, _, _, _, _, _, _, c_st, n_st, _ in units:
        kqs.append(_dot_nt(k, q))
        cns.append(None if c_st is None else _dot_nt(
            jnp.concatenate([c_st, jnp.broadcast_to(n_st, (SUBLANES, D_HEAD))], axis=0), q))
    mids, nums = [], []
    for kq, (_, _, vt, a_col, b_row, _, neg, _, _, _, m_st) in zip(kqs, units):
        a_m = a_col + neg
        c_row = jnp.maximum(m_st, jnp.max(a_m, axis=0, keepdims=True))
        st = kq * jnp.exp(a_m - c_row)
        mids.append((st, jnp.sum(st, axis=0, keepdims=True), jnp.exp(m_st - c_row), b_row + c_row))
        nums.append(_dot(vt, st))
    upds, m_news = [], []
    for (_, k, vt, _, b_row, li_row, _, last, _, _, _), (_, _, _, mt) in zip(units, mids):
        m_new = mt[:, last:last + 1]
        ws = jnp.exp(b_row[:, last:last + 1] - b_row + li_row - m_new)
        upds.append(_dot(jnp.concatenate(
            [vt * ws, jnp.broadcast_to(ws, (SUBLANES, ws.shape[1]))], axis=0), k))
        m_news.append(m_new)
    results = []
    for u, cn, (_, den, wprev, mt), num, upd, m_new in zip(units, cns, mids, nums, upds, m_news):
        last, c_st, n_st = u[7], u[8], u[9]
        c_new = upd[0:D_HEAD, :]
        n_new = upd[D_HEAD:D_HEAD + 1, :]
        if c_st is not None:
            num = num + wprev * cn[0:D_HEAD, :]
            den = den + wprev * cn[D_HEAD:D_HEAD + 1, :]
            wc = wprev[:, last:last + 1]
            c_new = c_new + wc * c_st
            n_new = n_new + wc * n_st
        ht = num * (1.0 / jnp.maximum(jnp.abs(den), jnp.exp(-mt)))
        results.append((ht, c_new, n_new, m_new))
    return results


def _mlstm_units(zqk_ref, zvo_ref, g_ref, fb_ref, n, visits):
    r = lax.broadcasted_iota(jnp.int32, (n, n), 0)
    c = lax.broadcasted_iota(jnp.int32, (n, n), 1)
    gs = [g_ref[rows, :] for rows, _, _ in visits]
    lss = [_log_sigmoid(g + fb_ref[...]) for g in gs]
    bs = [_dot_split3(((c <= r) if direction == 0 else (c >= r)).astype(BF16), ls)
          for ls, (_, direction, _) in zip(lss, visits)]
    units = []
    for g, b, (rows, direction, state) in zip(gs, bs, visits):
        a = pltpu.roll(g, ML_HEADS, 1) - b
        bt = b.T
        gt = g.T
        if direction == 0:
            neg, last = jnp.where(r <= c, 0.0, NEG_INF), n - 1
        else:
            neg, last = jnp.where(r >= c, 0.0, NEG_INF), 0
        i_col = 2 * ML_HEADS * direction
        f_col = i_col + ML_HEADS
        for pair in range(ML_HEADS // 2):
            vt2 = zvo_ref[rows, pair * LANES:(pair + 1) * LANES].T
            for sub in range(2):
                hd = 2 * pair + sub
                feat = slice(sub * D_HEAD, (sub + 1) * D_HEAD)
                q = zqk_ref[rows, hd * D_HEAD:(hd + 1) * D_HEAD]
                k = zqk_ref[rows, ML_W + hd * D_HEAD:ML_W + (hd + 1) * D_HEAD]
                units.append((q, k, vt2[feat, :],
                              a[:, f_col + hd:f_col + hd + 1], bt[f_col + hd:f_col + hd + 1, :],
                              gt[i_col + hd:i_col + hd + 1, :], neg, last) + tuple(state[hd]))
    return units


def _mlstm_finish(ht, o_gate, norm_g):
    outs = []
    for hd in range(ML_HEADS):
        x = ht[hd * D_HEAD:(hd + 1) * D_HEAD, :]
        mu = jnp.mean(x, axis=0, keepdims=True)
        xc = x - mu
        var = jnp.mean(xc * xc, axis=0, keepdims=True)
        outs.append(xc * lax.rsqrt(var + LN_EPS))
    hn = jnp.concatenate(outs, axis=0).T
    return hn * norm_g * _sigmoid(o_gate)


def _store_state(c_ref, n_ref, m_ref, idx, c_new, n_new, m_new):
    c_ref[idx] = c_new
    n_ref[idx:idx + 1, :] = n_new
    m_ref[idx:idx + 1, :] = jnp.broadcast_to(m_new, (1, LANES))


ONES_ROWS = 16


def _with_ones(vt):
    ones = jnp.ones((ONES_ROWS, vt.shape[1]), BF16)
    return jnp.concatenate([vt.astype(BF16), ones], axis=0)


def _attend_many(problems):
    return _attend_values(problems, *_attend_scores(problems))


def _attend_scores(problems):
    scores = []
    for q, parts, _ in problems:
        sts = []
        for k, _, bias_t in parts:
            st = _dot_nt(k, q)
            sts.append(st if bias_t is None else st + bias_t)
        scores.append(sts)
    maxima = []
    for sts, (_, _, sink_row) in zip(scores, problems):
        m = None
        for st in sts:
            mx = jnp.max(st, axis=0, keepdims=True)
            m = mx if m is None else jnp.maximum(m, mx)
        maxima.append(m if sink_row is None else jnp.maximum(m, sink_row))
    return scores, maxima


def _attend_values(problems, scores, maxima):
    outs = []
    for sts, m, (_, parts, sink_row) in zip(scores, maxima, problems):
        ot = None
        for st, (_, vt_ones, _) in zip(sts, parts):
            part = jnp.dot(vt_ones, jnp.exp2(st - m).astype(BF16), preferred_element_type=F32)
            ot = part if ot is None else ot + part
        den = ot[D_HEAD:D_HEAD + 1, :]
        if sink_row is not None:
            den = den + jnp.exp2(sink_row - m)
        outs.append(ot[0:D_HEAD, :] * (1.0 / den))
    return outs


def _head_rows(pairs_t, hd):
    return pairs_t[hd // 2][(hd % 2) * D_HEAD:(hd % 2 + 1) * D_HEAD, :]


def _store_heads_t(o_ref, ots):
    for p in range(len(ots) // 2):
        o_ref[:, p * LANES:(p + 1) * LANES] = jnp.concatenate(
            [ots[2 * p], ots[2 * p + 1]], axis=0).astype(o_ref.dtype).T


N_STATE = 2 * ML_HEADS
N_CTX_STATE_OUTS = 7


def _ctx_mixers_kernel(x_ref, mod_ref, wt_ref, b_ref, fb_ref, ng_ref, sink_ref, *rest,
                       first_layer):
    n_scratch = 1 + len(Z_GROUPS)
    w_scr, zqk_ref, zvo_ref, zswa_ref, znat_ref, zg_ref = rest[-n_scratch:]
    mml_ref, mswa_ref, mnat_ref, *state_refs = rest[-(3 + N_CTX_STATE_OUTS) - n_scratch:-n_scratch]
    _inproj_kernel(x_ref, mod_ref, wt_ref, b_ref, zqk_ref, zvo_ref, zswa_ref, znat_ref, zg_ref,
                   w_scr)
    if first_layer:
        for ref in state_refs:
            ref[1:] = jnp.zeros((DEPTH - 1,) + ref.shape[1:], F32)
        state_refs = [ref.at[0] for ref in state_refs]
    sk_ref, sv_ref, nk_ref, nv_ref, c_ref, n_ref, m_ref = state_refs
    rows = slice(0, SEQ)
    scaled_q = lambda ref, hd: ref[:, hd * D_HEAD:(hd + 1) * D_HEAD].astype(BF16)
    swa_vts = [zswa_ref[:, SWA_W + SWA_KV_W:SWA_W + 2 * SWA_KV_W].astype(BF16).T]
    nat_vts = [znat_ref[:, 2 * NAT_W + p * LANES:2 * NAT_W + (p + 1) * LANES].astype(BF16).T
               for p in range(NAT_W // LANES)]
    problems = []
    for kv in range(SWA_KV_HEADS):
        k = zswa_ref[:, SWA_W + kv * D_HEAD:SWA_W + (kv + 1) * D_HEAD]
        sk_ref[kv] = k
        sv_ref[kv] = zswa_ref[:, SWA_W + SWA_KV_W + kv * D_HEAD:SWA_W + SWA_KV_W + (kv + 1) * D_HEAD]
        q = jnp.concatenate([scaled_q(zswa_ref, kv * SWA_GROUP + g) for g in range(SWA_GROUP)],
                            axis=0)
        problems.append((q, [(k.astype(BF16), _with_ones(_head_rows(swa_vts, kv)), None)],
                         sink_ref[kv:kv + 1, :]))
    for hd in range(NAT_HEADS):
        k = znat_ref[:, NAT_W + hd * D_HEAD:NAT_W + (hd + 1) * D_HEAD]
        nk_ref[hd] = k
        nv_ref[hd] = znat_ref[:, 2 * NAT_W + hd * D_HEAD:2 * NAT_W + (hd + 1) * D_HEAD]
        problems.append((scaled_q(znat_ref, hd),
                         [(k.astype(BF16), _with_ones(_head_rows(nat_vts, hd)), None)], None))
    scores, maxima = _attend_scores(problems)

    zero_state = [(None, jnp.zeros((1, D_HEAD), F32), jnp.zeros((1, 1), F32))] * ML_HEADS
    results = _mlstm_run(_mlstm_units(zqk_ref, zvo_ref, zg_ref, fb_ref, SEQ,
                                      [(rows, direction, zero_state) for direction in range(2)]))
    for idx, (_, c_new, n_new, m_new) in enumerate(results):
        _store_state(c_ref, n_ref, m_ref, idx, c_new, n_new, m_new)
    ht_sum = jnp.concatenate([results[hd][0] + results[ML_HEADS + hd][0]
                              for hd in range(ML_HEADS)], axis=0)
    mml_ref[...] = _mlstm_finish(ht_sum, zvo_ref[:, ML_W:2 * ML_W], ng_ref[...]).astype(BF16)

    ots = _attend_values(problems, scores, maxima)
    _store_heads_t(mswa_ref, [ot[:, g * SEQ:(g + 1) * SEQ]
                              for ot in ots[:SWA_KV_HEADS] for g in range(SWA_GROUP)])
    _store_heads_t(mnat_ref, ots[SWA_KV_HEADS:])


CTX_STATE_DIMS = ((SWA_KV_HEADS, SEQ, D_HEAD), (SWA_KV_HEADS, SEQ, D_HEAD),
                  (NAT_HEADS, SEQ, D_HEAD), (NAT_HEADS, SEQ, D_HEAD),
                  (N_STATE, D_HEAD, D_HEAD), (N_STATE, D_HEAD), (N_STATE, LANES))


def _ctx_mixers_call(x, mods, w_in_t, b_in, fb_row, norm_g, sink_rows, layer, prev_state):
    rows = x.shape[0]
    row_spec = lambda w: pl.BlockSpec((SEQ, w), lambda b: (b, 0))
    state_dims = CTX_STATE_DIMS
    first_layer = prev_state is None
    if first_layer:
        assert layer == 0
        state_spec = lambda d: pl.BlockSpec((None, DEPTH) + d, lambda b: (b,) + (0,) * (len(d) + 1))
    else:
        state_spec = lambda d: pl.BlockSpec((None, None) + d, lambda b: (b, layer) + (0,) * len(d))
    prev = () if first_layer else tuple(prev_state)
    n_in = 7
    return pl.pallas_call(
        functools.partial(_ctx_mixers_kernel, first_layer=first_layer),
        grid=(BATCH,),
        in_specs=[row_spec(D_MODEL), _mod_spec(layer, lambda b: CTX_MOD_ROW),
                  _layer_spec(w_in_t, layer, single_buffer=True), _layer_spec(b_in, layer),
                  _layer_spec(fb_row, layer), _layer_spec(norm_g, layer),
                  _layer_spec(sink_rows, layer)]
                 + [pl.BlockSpec(memory_space=pl.ANY)] * len(prev),
        out_specs=[row_spec(ML_W), row_spec(SWA_W), row_spec(NAT_W)]
                  + [state_spec(d) for d in state_dims],
        out_shape=[jax.ShapeDtypeStruct((rows, ML_W), BF16),
                   jax.ShapeDtypeStruct((rows, SWA_W), BF16),
                   jax.ShapeDtypeStruct((rows, NAT_W), BF16)]
                  + [jax.ShapeDtypeStruct((BATCH, DEPTH) + d, F32) for d in state_dims],
        input_output_aliases={n_in + i: 3 + i for i in range(len(prev))},
        scratch_shapes=[pltpu.VMEM((Z_W, D_MODEL), BF16)]
                       + [pltpu.VMEM((SEQ, w), dt) for w, dt in zip(Z_GROUPS, CTX_Z_DTYPES)],
        compiler_params=_params(1),
        name="ctx_mixers",
    )(x, mods, w_in_t, b_in, fb_row, norm_g, sink_rows, *prev)


def _lat_mlstm_kernel(zqk_ref, zvo_ref, zg_ref, fb_ref, ng_ref, c0_ref, n0_ref, m0_ref, mml_ref,
                      ht_scr, c_scr, n_scr, m_scr):
    n_chunks = DEC_SEQ // ML_CHUNK
    c_scr[...] = c0_ref[...]
    n_scr[...] = n0_ref[...]
    m_scr[...] = jnp.broadcast_to(m0_ref[...], m_scr.shape)

    def body(i, carry):
        chunks = (i, n_chunks - 1 - i)
        visits = []
        for direction, chunk in enumerate(chunks):
            rows = pl.ds(pl.multiple_of(chunk * ML_CHUNK, ML_CHUNK), ML_CHUNK)
            base = direction * ML_HEADS
            state = [(c_scr[base + hd], n_scr[base + hd:base + hd + 1, :],
                      m_scr[base + hd:base + hd + 1, 0:1]) for hd in range(ML_HEADS)]
            visits.append((rows, direction, state))
        results = _mlstm_run(_mlstm_units(zqk_ref, zvo_ref, zg_ref, fb_ref, ML_CHUNK, visits))
        for direction, chunk in enumerate(chunks):
            base = direction * ML_HEADS
            ht_scr[direction, chunk] = jnp.concatenate(
                [results[base + hd][0] for hd in range(ML_HEADS)], axis=0)
        for idx, (_, c_new, n_new, m_new) in enumerate(results):
            _store_state(c_scr, n_scr, m_scr, idx, c_new, n_new, m_new)
        return carry

    lax.fori_loop(0, n_chunks, body, 0)
    for chunk in range(n_chunks):
        rows = slice(chunk * ML_CHUNK, (chunk + 1) * ML_CHUNK)
        mml_ref[rows, :] = _mlstm_finish(ht_scr[0, chunk] + ht_scr[1, chunk],
                                         zvo_ref[rows, ML_W:2 * ML_W],
                                         ng_ref[...]).astype(BF16)


def _lat_mlstm_call(z_qk, z_vo, z_g, fb_row, norm_g, state_c, state_n, state_m, layer):
    n_st = 2 * ML_HEADS
    n_chunks = DEC_SEQ // ML_CHUNK
    row_spec = lambda w: pl.BlockSpec((DEC_SEQ, w), lambda b: (b, 0))
    return pl.pallas_call(
        _lat_mlstm_kernel,
        grid=(DEC_BATCH,),
        in_specs=[row_spec(Z_QK_W), row_spec(Z_VO_W), row_spec(Z_G_W),
                  _layer_spec(fb_row, layer), _layer_spec(norm_g, layer),
                  pl.BlockSpec((None, None, n_st, D_HEAD, D_HEAD), lambda b: (b, layer, 0, 0, 0)),
                  pl.BlockSpec((None, None, n_st, D_HEAD), lambda b: (b, layer, 0, 0)),
                  pl.BlockSpec((None, None, n_st, 1), lambda b: (b, layer, 0, 0))],
        out_specs=row_spec(ML_W),
        out_shape=jax.ShapeDtypeStruct((DEC_BATCH * DEC_SEQ, ML_W), BF16),
        scratch_shapes=[pltpu.VMEM((2, n_chunks, ML_W, ML_CHUNK), F32),
                        pltpu.VMEM((n_st, D_HEAD, D_HEAD), F32), pltpu.VMEM((n_st, D_HEAD), F32),
                        pltpu.VMEM((n_st, LANES), F32)],
        compiler_params=_params(1),
        name="lat_mlstm",
    )(z_qk, z_vo, z_g, fb_row, norm_g, state_c, state_n, state_m)


def _lat_swa_kernel(zswa_ref, ckt_ref, cvt_ref, sink_ref, mswa_ref,
                    vt_scr, ot_scr, bias_scr, ck_scr, cvt_scr):
    vt_scr[...] = zswa_ref[:, SWA_W + SWA_KV_W:SWA_W + 2 * SWA_KV_W].T

    bl = SWA_BLOCK
    nb = DEC_SEQ // bl
    n_q = SWA_GROUP * bl
    kj = lax.broadcasted_iota(jnp.int32, (3 * bl, n_q), 0)
    qi = lax.broadcasted_iota(jnp.int32, (3 * bl, n_q), 1) & (bl - 1)
    keep = ((kj >= qi) & (kj < 2 * bl)) | ((kj >= 2 * bl) & (kj - 2 * bl <= qi))
    bias_scr[...] = jnp.where(keep, 0.0, NEG_INF)
    ck_scr[...] = jnp.concatenate([ckt_ref[kv] for kv in range(SWA_KV_HEADS)],
                                  axis=0).astype(BF16).T
    for kv in range(SWA_KV_HEADS):
        cvt_scr[kv] = _with_ones(cvt_ref[kv])

    problems, where = [], []
    for kv in range(SWA_KV_HEADS):
        head = slice(kv * D_HEAD, (kv + 1) * D_HEAD)
        for blk in range(nb):
            lo_b, hi_b = max(blk - 1, 0), min(blk + 2, nb)
            keys = slice(lo_b * bl, hi_b * bl)
            q = jnp.concatenate(
                [zswa_ref[blk * bl:(blk + 1) * bl,
                          (kv * SWA_GROUP + g) * D_HEAD:(kv * SWA_GROUP + g + 1) * D_HEAD]
                 for g in range(SWA_GROUP)], axis=0)
            parts = [(zswa_ref[keys, SWA_W + kv * D_HEAD:SWA_W + (kv + 1) * D_HEAD],
                      _with_ones(vt_scr[head, keys]),
                      bias_scr[(lo_b - blk + 1) * bl:(hi_b - blk + 1) * bl, :]),
                     (ck_scr[:, head], cvt_scr[kv], None)]
            problems.append((q, parts, sink_ref[kv:kv + 1, :]))
            where.append((kv, blk))
    for (kv, blk), ot in zip(where, _attend_many(problems)):
        o = ot.astype(BF16)
        for g in range(SWA_GROUP):
            hq = kv * SWA_GROUP + g
            ot_scr[hq * D_HEAD:(hq + 1) * D_HEAD, blk * bl:(blk + 1) * bl] = (
                o[:, g * bl:(g + 1) * bl])
    for p in range(SWA_W // LANES):
        mswa_ref[:, p * LANES:(p + 1) * LANES] = ot_scr[p * LANES:(p + 1) * LANES, :].T


def _lat_swa_call(z_swa, cache_k, cache_v, sink_rows, layer):
    row_spec = lambda w: pl.BlockSpec((DEC_SEQ, w), lambda b: (b, 0))
    cache_spec = pl.BlockSpec((None, None, SWA_KV_HEADS, D_HEAD, PAST_LEN),
                              lambda b: (b, layer, 0, 0, 0))
    n_q = SWA_GROUP * SWA_BLOCK
    return pl.pallas_call(
        _lat_swa_kernel,
        grid=(DEC_BATCH,),
        in_specs=[row_spec(Z_SWA_W), cache_spec, cache_spec, _layer_spec(sink_rows, layer)],
        out_specs=row_spec(SWA_W),
        out_shape=jax.ShapeDtypeStruct((DEC_BATCH * DEC_SEQ, SWA_W), BF16),
        scratch_shapes=[pltpu.VMEM((SWA_KV_W, DEC_SEQ), BF16), pltpu.VMEM((SWA_W, DEC_SEQ), BF16),
                        pltpu.VMEM((3 * SWA_BLOCK, n_q), F32),
                        pltpu.VMEM((PAST_LEN, SWA_KV_W), BF16),
                        pltpu.VMEM((SWA_KV_HEADS, D_HEAD + ONES_ROWS, PAST_LEN), BF16)],
        compiler_params=_params(1),
        name="lat_swa",
    )(z_swa, cache_k, cache_v, sink_rows)


NAT_ROWS = DEC_SEQ // GRID_W
NAT_GROUP_ROWS = 4
NAT_GROUP_Q = NAT_GROUP_ROWS * GRID_W
NAT_RPB_ROWS = 2 * NAT_KH


def _nat_row_start(r):
    return min(max(r - NAT_KH // 2, 0), NAT_ROWS - NAT_KH)


def _nat_groups():
    groups, off = [], 0
    for g in range(NAT_ROWS // NAT_GROUP_ROWS):
        w0 = _nat_row_start(g * NAT_GROUP_ROWS)
        w1 = _nat_row_start((g + 1) * NAT_GROUP_ROWS - 1) + NAT_KH
        n_rows = w1 - w0 + (w1 - w0) % 2
        w0 = min(w0, NAT_ROWS - n_rows)
        groups.append((w0, n_rows, off))
        off += n_rows * GRID_W
    return groups, off


NAT_GROUPS, NAT_BIAS_KEYS = _nat_groups()


def _nat_build_bias(rpb_ref, bias_scr):
    shape = (GRID_W, LANES)
    q = lax.broadcasted_iota(jnp.int32, shape, 0)
    lane = lax.broadcasted_iota(jnp.int32, shape, 1)
    kc = lane & (GRID_W - 1)
    cs = jnp.clip(q - NAT_KW // 2, 0, GRID_W - NAT_KW)
    ok = (kc >= cs) & (kc < cs + NAT_KW)
    ok_lo = ok & (lane < GRID_W)
    ok_hi = ok & (lane >= GRID_W)
    neg_tile = jnp.full(shape, NEG_INF, F32)
    for hd in range(2):
        lo, hi = [], []
        for ro in range(2 * NAT_KH - 1):
            x = jnp.broadcast_to(rpb_ref[hd, ro:ro + 1, :], shape)
            lo.append(jnp.where(ok_lo, pltpu.roll(x, 0, 1, stride=1, stride_axis=0), NEG_INF))
            hi.append(jnp.where(ok_hi, pltpu.roll(x, GRID_W, 1, stride=1, stride_axis=0), NEG_INF))
        for g, (w0, n_rows, off) in enumerate(NAT_GROUPS):
            for jj in range(n_rows // 2):
                kra, krb = w0 + 2 * jj, w0 + 2 * jj + 1
                tiles = []
                for rr in range(NAT_GROUP_ROWS):
                    r = g * NAT_GROUP_ROWS + rr
                    r0 = _nat_row_start(r)
                    parts = []
                    if r0 <= kra < r0 + NAT_KH:
                        parts.append(lo[kra - r + NAT_KH - 1])
                    if r0 <= krb < r0 + NAT_KH:
                        parts.append(hi[krb - r + NAT_KH - 1])
                    tiles.append(neg_tile if not parts else (
                        parts[0] if len(parts) == 1 else jnp.maximum(parts[0], parts[1])))
                bias_scr[hd, off + jj * LANES:off + (jj + 1) * LANES, :] = (
                    jnp.concatenate(tiles, axis=0).T)


def _lat_nat_kernel(q_ref, k_ref, v_ref, ckt_ref, cvt_ref, rpb_ref, mnat_ref, bias_scr):
    @pl.when(pl.program_id(1) == 0)
    def _():
        _nat_build_bias(rpb_ref, bias_scr)

    vt = v_ref[...].astype(BF16).T
    ck2 = jnp.concatenate([ckt_ref[0], ckt_ref[1]], axis=0).astype(BF16).T
    problems = []
    for hd in range(2):
        head = slice(hd * D_HEAD, (hd + 1) * D_HEAD)
        ck = ck2[:, head]
        cvt_ones = _with_ones(cvt_ref[hd])
        for g, (w0, n_rows, off) in enumerate(NAT_GROUPS):
            keys = slice(w0 * GRID_W, (w0 + n_rows) * GRID_W)
            parts = [(k_ref[keys, head].astype(BF16), _with_ones(vt[head, keys]),
                      bias_scr[hd, off:off + n_rows * GRID_W, :]),
                     (ck, cvt_ones, None)]
            q = q_ref[g * NAT_GROUP_Q:(g + 1) * NAT_GROUP_Q, head].astype(BF16)
            problems.append((q, parts, None))
    ots = _attend_many(problems)
    n_groups = len(NAT_GROUPS)
    mnat_ref[...] = jnp.concatenate(
        [jnp.concatenate(ots[hd * n_groups:(hd + 1) * n_groups], axis=1) for hd in range(2)],
        axis=0).astype(BF16).T


def _lat_nat_call(z_nat, cache_k, cache_v, rpb_lanes, layer):
    n_pair = NAT_HEADS // 2
    col_spec = lambda off: pl.BlockSpec((DEC_SEQ, LANES), lambda p, b: (b, off + p))
    cache_spec = pl.BlockSpec((None, None, 2, D_HEAD, PAST_LEN), lambda p, b: (b, layer, p, 0, 0))
    return pl.pallas_call(
        _lat_nat_kernel,
        grid=(n_pair, DEC_BATCH),
        in_specs=[col_spec(0), col_spec(n_pair), col_spec(2 * n_pair), cache_spec, cache_spec,
                  pl.BlockSpec((None, 2, NAT_RPB_ROWS, LANES), lambda p, b: (layer, p, 0, 0))],
        out_specs=pl.BlockSpec((DEC_SEQ, LANES), lambda p, b: (b, p)),
        out_shape=jax.ShapeDtypeStruct((DEC_BATCH * DEC_SEQ, NAT_W), BF16),
        scratch_shapes=[pltpu.VMEM((2, NAT_BIAS_KEYS, NAT_GROUP_Q), F32)],
        compiler_params=_params(2),
        name="lat_nat",
    )(z_nat, z_nat, z_nat, cache_k, cache_v, rpb_lanes)


def _nat_rpb_lanes(rpb):
    n_off = 2 * NAT_KW - 1
    padded = jnp.concatenate(
        [rpb.astype(F32), jnp.zeros(rpb.shape[:-1] + (LANES - n_off,), F32)], axis=-1)
    rolled = jnp.roll(padded, -(NAT_KW - 1), axis=-1)
    pad_rows = jnp.zeros(rpb.shape[:-2] + (NAT_RPB_ROWS - rpb.shape[-2], LANES), F32)
    return jnp.concatenate([rolled, pad_rows], axis=-2)


def _rope_tables():
    t = np.arange(DEC_SEQ)[:, None]
    d = np.arange(LANES)[None, :] % D_HEAD
    pos = np.where(d < D_HEAD // 2, t // GRID_W, t % GRID_W).astype(np.float64)
    quarter = D_HEAD // 4
    freq = ROPE_BASE ** (-(d % quarter).astype(np.float64) / quarter)
    ang = (pos.astype(np.float32) * freq.astype(np.float32)).astype(np.float32)
    sign = np.where((d % (2 * quarter)) < quarter, -1.0, 1.0)
    return (jnp.asarray(np.cos(ang), dtype=F32), jnp.asarray(np.sin(ang) * sign, dtype=F32))


def _permute_in_columns(a):
    g0 = Z_ML_W
    s0 = g0 + N_GATES
    pad = jnp.zeros(a.shape[:-1] + (Z_G_W - N_GATES,), a.dtype)
    return jnp.concatenate([a[..., :g0], a[..., s0:], a[..., g0:s0], pad], axis=-1)


def kernel(x_prompt, x_sample, cache_swa_k, cache_swa_v, cache_nat_k, cache_nat_v, state_mlstm_C,
           state_mlstm_n, state_mlstm_m, c, c_ctx, w_ada, b_ada, w_in, b_in, mlstm_fbias,
           mlstm_norm_g, swa_sink, nat_rpb, w_out, ln1_g, ln1_b, w_mlp1, w_mlp2, ln2_g, ln2_b):
    xp = x_prompt.reshape(BATCH * SEQ, D_MODEL)
    xs = x_sample.reshape(DEC_BATCH * DEC_SEQ, D_MODEL)
    cvec = jnp.concatenate(
        [c, c_ctx[None, :], jnp.zeros((MOD_ROWS - DEC_BATCH - 1, D_MODEL), F32)], axis=0)
    mods = _mods_call(cvec, w_ada, b_ada).reshape(DEPTH * MOD_ROWS, 6, D_MODEL)

    w_in_t = jnp.swapaxes(w_in, 1, 2)
    out_scale = np.ones((Z_W,), np.float32)
    out_scale[ML_W:2 * ML_W] = ATT_SCALE
    out_scale[Z_ML_W:Z_ML_W + SWA_W] = Q_SCALE2
    out_scale[Z_ML_W + Z_SWA_W:Z_ML_W + Z_SWA_W + NAT_W] = Q_SCALE2
    b_in_p = jnp.stack([jnp.broadcast_to(out_scale, (DEPTH, Z_W)),
                        _permute_in_columns(b_in) * out_scale], axis=1)
    cache_swa_k, cache_swa_v, cache_nat_k, cache_nat_v = (
        jnp.swapaxes(a, -1, -2) for a in (cache_swa_k, cache_swa_v, cache_nat_k, cache_nat_v))
    vec = lambda a: a.reshape(DEPTH, 1, D_MODEL)
    tail_w = (w_out.astype(BF16), vec(ln1_g), vec(ln1_b), w_mlp1.astype(BF16),
              w_mlp2.astype(BF16), vec(ln2_g), vec(ln2_b))
    cos_t, sin_t = _rope_tables()
    rpb_lanes = _nat_rpb_lanes(nat_rpb) * LOG2E
    swa_sink = swa_sink * LOG2E
    fb_rows = jnp.zeros((DEPTH, 1, Z_G_W), F32)
    fb_rows = fb_rows.at[:, 0, ML_HEADS:2 * ML_HEADS].set(mlstm_fbias[:, 0])
    fb_rows = fb_rows.at[:, 0, 3 * ML_HEADS:4 * ML_HEADS].set(mlstm_fbias[:, 1])
    norm_g = mlstm_norm_g.reshape(DEPTH, 1, ML_W)
    sink_ctx = jnp.repeat(swa_sink, SEQ, axis=1).reshape(DEPTH, SWA_KV_HEADS, SWA_GROUP * SEQ)
    sink_lat = jnp.repeat(swa_sink, SWA_BLOCK, axis=1).reshape(
        DEPTH, SWA_KV_HEADS, SWA_GROUP * SWA_BLOCK)

    n_st = 2 * ML_HEADS
    state_c = state_mlstm_C.reshape(DEC_BATCH, DEPTH, n_st, D_HEAD, D_HEAD)
    state_n = state_mlstm_n.reshape(DEC_BATCH, DEPTH, n_st, D_HEAD)
    state_m = state_mlstm_m.reshape(DEC_BATCH, DEPTH, n_st, 1)

    ctx_row = lambda tile: (lambda i: CTX_MOD_ROW)
    lat_row = lambda tile: (lambda i: i // (DEC_SEQ // tile))

    ctx_state = None
    for l in range(DEPTH):
        m_ml, m_swa, m_nat, *ctx_state = _ctx_mixers_call(
            xp, mods, w_in_t, b_in_p, fb_rows, norm_g, sink_ctx, l, ctx_state)
        xp = _tail_call(xp, m_ml, m_swa, m_nat, mods, ctx_row(TAIL_TILE), tail_w, l)

        z_qk, z_vo, z_swa, z_nat, z_g = _lat_inproj_call(xs, mods, w_in_t, b_in_p, cos_t, sin_t, l)
        m_ml = _lat_mlstm_call(z_qk, z_vo, z_g, fb_rows, norm_g, state_c, state_n, state_m, l)
        m_swa = _lat_swa_call(z_swa, cache_swa_k, cache_swa_v, sink_lat, l)
        m_nat = _lat_nat_call(z_nat, cache_nat_k, cache_nat_v, rpb_lanes, l)
        xs = _tail_call(xs, m_ml, m_swa, m_nat, mods, lat_row(TAIL_TILE), tail_w, l)

    k_swa, v_swa, k_nat, v_nat, st_c, st_n, st_m = ctx_state
    return (xp.reshape(BATCH, SEQ, D_MODEL), xs.reshape(DEC_BATCH, DEC_SEQ, D_MODEL),
            k_swa, v_swa, k_nat, v_nat,
            st_c.reshape(BATCH, DEPTH, 2, ML_HEADS, D_HEAD, D_HEAD),
            st_n.reshape(BATCH, DEPTH, 2, ML_HEADS, D_HEAD),
            st_m[..., 0].reshape(BATCH, DEPTH, 2, ML_HEADS))
```

```python
import functools

import numpy as np
import jax
import jax.numpy as jnp
from jax import lax
from jax.experimental import pallas as pl
from jax.experimental.pallas import tpu as pltpu

D_MODEL = 1024
BATCH = 16
SEQ = 256
DEPTH = 4
DEC_BATCH = 8
DEC_SEQ = 1024
PAST_LEN = 512
GRID_W = 64
D_HEAD = 64
ML_HEADS = 4
SWA_HEADS = 6
SWA_KV_HEADS = 2
SWA_GROUP = SWA_HEADS // SWA_KV_HEADS
NAT_HEADS = 6
ML_W = ML_HEADS * D_HEAD
SWA_W = SWA_HEADS * D_HEAD
SWA_KV_W = SWA_KV_HEADS * D_HEAD
NAT_W = NAT_HEADS * D_HEAD
N_GATES = 4 * ML_HEADS
IN_DIM = 4 * ML_W + N_GATES + SWA_W + 2 * SWA_KV_W + 3 * NAT_W
D_FF = 4 * D_MODEL
SWA_WINDOW = 128
SWA_BLOCK = 128
NAT_KH = 8
NAT_KW = 16
ROPE_BASE = 10000.0
LN_EPS = 1e-5
DN_ALPHA = (2 * DEPTH) ** 0.25
ATT_SCALE = D_HEAD ** -0.5
LOG2E = 1.4426950408889634
Q_SCALE2 = ATT_SCALE * LOG2E

LANES = 128
SUBLANES = 8
Z_ML_W = 4 * ML_W
Z_SWA_W = SWA_W + 2 * SWA_KV_W
Z_NAT_W = 3 * NAT_W
Z_G_W = LANES
Z_W = Z_ML_W + Z_SWA_W + Z_NAT_W + Z_G_W
Z_QK_W = 2 * ML_W
Z_VO_W = 2 * ML_W
Z_GROUPS = (Z_QK_W, Z_VO_W, Z_SWA_W, Z_NAT_W, Z_G_W)
Z_SWA_GROUP = 2
MOD_ROWS = 16
CTX_MOD_ROW = DEC_BATCH
ROW_TILE = 512
TAIL_TILE = 512
TAIL_PARTS = 2
ML_CHUNK = 128
VMEM_LIMIT = 56 * 1024 * 1024

BF16 = jnp.bfloat16
F32 = jnp.float32
NEG_INF = float("-inf")
LAT_Z_DTYPES = (BF16, F32, BF16, BF16, F32)
CTX_Z_DTYPES = (BF16, F32, F32, F32, F32)


def _dot(a, b):
    return jnp.dot(a.astype(BF16), b.astype(BF16), preferred_element_type=F32)


def _dot_nt(a, b):
    return lax.dot_general(a.astype(BF16), b.astype(BF16), (((1,), (1,)), ((), ())),
                           preferred_element_type=F32)


def _dot_split3(tri, x):
    hi = x.astype(BF16)
    r1 = x - hi.astype(F32)
    mid = r1.astype(BF16)
    lo = (r1 - mid.astype(F32)).astype(BF16)
    return (jnp.dot(tri, hi, preferred_element_type=F32)
            + jnp.dot(tri, mid, preferred_element_type=F32)
            + jnp.dot(tri, lo, preferred_element_type=F32))


def _sigmoid(x):
    return 1.0 / (1.0 + jnp.exp(-x))


def _log_sigmoid(x):
    return jnp.minimum(x, 0.0) - jnp.log1p(jnp.exp(-jnp.abs(x)))


def _layer_norm(x, g, b):
    mu = jnp.mean(x, axis=-1, keepdims=True)
    xc = x - mu
    var = jnp.mean(xc * xc, axis=-1, keepdims=True)
    return xc * lax.rsqrt(var + LN_EPS) * g + b


def _params(n_axes=1, flags=None):
    return pltpu.CompilerParams(dimension_semantics=("arbitrary",) * n_axes,
                                vmem_limit_bytes=VMEM_LIMIT, flags=flags)


def _layer_spec(a, layer, single_buffer=False):
    kwargs = {"pipeline_mode": pl.Buffered(1)} if single_buffer else {}
    return pl.BlockSpec((None,) + a.shape[1:], lambda *_: (layer,) + (0,) * (a.ndim - 1), **kwargs)


def _mods_kernel(c_ref, w_ref, b_ref, o_ref):
    c = c_ref[...]
    o_ref[...] = _dot(c * _sigmoid(c), w_ref[...]) + b_ref[...]


def _mods_call(cvec, w_ada, b_ada):
    n_col = 6
    return pl.pallas_call(
        _mods_kernel,
        grid=(DEPTH, n_col),
        in_specs=[
            pl.BlockSpec((MOD_ROWS, D_MODEL), lambda l, j: (0, 0)),
            pl.BlockSpec((None, D_MODEL, D_MODEL), lambda l, j: (l, 0, j)),
            pl.BlockSpec((None, 1, D_MODEL), lambda l, j: (l, 0, j)),
        ],
        out_specs=pl.BlockSpec((None, MOD_ROWS, D_MODEL), lambda l, j: (l, 0, j)),
        out_shape=jax.ShapeDtypeStruct((DEPTH, MOD_ROWS, 6 * D_MODEL), F32),
        compiler_params=_params(2),
        name="mods",
    )(cvec, w_ada, b_ada.reshape(DEPTH, 1, 6 * D_MODEL))


def _rope(x, cos, sin_signed):
    lane = lax.broadcasted_iota(jnp.int32, x.shape, 1)
    first = (lane & 31) < 16
    partner = jnp.where(first, pltpu.roll(x, LANES - 16, 1), pltpu.roll(x, 16, 1))
    return x * cos + partner * sin_signed


def _inproj_kernel(x_ref, mod_ref, wt_ref, b_ref, *rest, rope=False):
    if rope:
        cos_ref, sin_ref, *rest = rest
    *out_refs, w_scr = rest

    @pl.when(pl.program_id(0) == 0)
    def _():
        rest = IN_DIM - Z_ML_W - N_GATES
        w_scr[0:Z_ML_W, :] = wt_ref[0:Z_ML_W, :].astype(BF16)
        w_scr[Z_ML_W:Z_ML_W + rest, :] = wt_ref[Z_ML_W + N_GATES:IN_DIM, :].astype(BF16)
        w_scr[Z_ML_W + rest:IN_DIM, :] = wt_ref[Z_ML_W:Z_ML_W + N_GATES, :].astype(BF16)
        w_scr[IN_DIM:Z_W, :] = jnp.zeros((Z_W - IN_DIM, D_MODEL), BF16)

    h = (x_ref[...] * (1.0 + mod_ref[1:2, :]) + mod_ref[0:1, :]).astype(BF16)
    offsets = [sum(Z_GROUPS[:i]) for i in range(len(Z_GROUPS))]
    accs = [_dot_nt(h, w_scr[o:o + w, :]) for o, w in zip(offsets, Z_GROUPS)]
    for ref, w, o, acc in zip(out_refs, Z_GROUPS, offsets, accs):
        z = acc * b_ref[0:1, o:o + w] + b_ref[1:2, o:o + w]
        if rope and ref is out_refs[Z_SWA_GROUP]:
            z = jnp.concatenate(
                [_rope(z[:, j * LANES:(j + 1) * LANES], cos_ref[...], sin_ref[...])
                 for j in range((SWA_W + SWA_KV_W) // LANES)] + [z[:, SWA_W + SWA_KV_W:]], axis=1)
        ref[...] = z.astype(ref.dtype)


def _mod_spec(layer, mod_row_of_tile):
    return pl.BlockSpec((None, 6, D_MODEL),
                        lambda i: (layer * MOD_ROWS + mod_row_of_tile(i), 0, 0))


def _lat_inproj_call(x, mods, w_in_t, b_in, cos_t, sin_t, layer):
    rows = x.shape[0]
    tiles_per_seq = DEC_SEQ // ROW_TILE
    table_spec = pl.BlockSpec((ROW_TILE, LANES), lambda i: (i % tiles_per_seq, 0))
    return pl.pallas_call(
        functools.partial(_inproj_kernel, rope=True),
        grid=(rows // ROW_TILE,),
        in_specs=[
            pl.BlockSpec((ROW_TILE, D_MODEL), lambda i: (i, 0)),
            _mod_spec(layer, lambda i: i // tiles_per_seq),
            _layer_spec(w_in_t, layer, single_buffer=True),
            _layer_spec(b_in, layer),
            table_spec, table_spec,
        ],
        out_specs=[pl.BlockSpec((ROW_TILE, w), lambda i: (i, 0)) for w in Z_GROUPS],
        out_shape=[jax.ShapeDtypeStruct((rows, w), dt) for w, dt in zip(Z_GROUPS, LAT_Z_DTYPES)],
        scratch_shapes=[pltpu.VMEM((Z_W, D_MODEL), BF16)],
        compiler_params=_params(1),
        name="inproj",
    )(x, mods, w_in_t, b_in, cos_t, sin_t)


def _tail_kernel(x_ref, mml_ref, mswa_ref, mnat_ref, mod_ref, wout_ref, ln1g_ref, ln1b_ref,
                 w1_ref, w2_ref, ln2g_ref, ln2b_ref, o_ref):
    g_a, sh_m, sc_m, g_m = (mod_ref[i:i + 1, :] for i in (2, 3, 4, 5))
    part = x_ref.shape[0] // TAIL_PARTS
    halves = tuple(slice(i * part, (i + 1) * part) for i in range(TAIL_PARTS))
    projs = [jnp.dot(jnp.concatenate([mml_ref[r, :], mswa_ref[r, :], mnat_ref[r, :]], axis=1),
                     wout_ref[...], preferred_element_type=F32) for r in halves]
    ys, fs = [], []
    for r, proj in zip(halves, projs):
        y = _layer_norm(DN_ALPHA * x_ref[r, :] + g_a * proj, ln1g_ref[...], ln1b_ref[...])
        h = (y * (1.0 + sc_m) + sh_m).astype(BF16)
        ys.append(y)
        fs.append(jnp.dot(h, w1_ref[...], preferred_element_type=F32))
    mlps = []
    for f in fs:
        f = jnp.maximum(f, 0.0)
        mlps.append(jnp.dot((f * f).astype(BF16), w2_ref[...], preferred_element_type=F32))
    for r, y, mlp in zip(halves, ys, mlps):
        o_ref[r, :] = _layer_norm(DN_ALPHA * y + g_m * mlp, ln2g_ref[...], ln2b_ref[...])


def _tail_call(x, mix_ml, mix_swa, mix_nat, mods, mod_row_of_tile, consts, layer):
    rows = x.shape[0]
    row_spec = lambda w: pl.BlockSpec((TAIL_TILE, w), lambda i: (i, 0))
    return pl.pallas_call(
        _tail_kernel,
        grid=(rows // TAIL_TILE,),
        in_specs=[row_spec(D_MODEL), row_spec(ML_W), row_spec(SWA_W), row_spec(NAT_W),
                  _mod_spec(layer, mod_row_of_tile)]
                 + [_layer_spec(a, layer, single_buffer=True) for a in consts],
        out_specs=row_spec(D_MODEL),
        out_shape=jax.ShapeDtypeStruct((rows, D_MODEL), F32),
        compiler_params=_params(1),
        name="tail",
    )(x, mix_ml, mix_swa, mix_nat, mods, *consts)


def _mlstm_scores(units):
    return [_dot_nt(u[1], u[0]) for u in units]


def _mlstm_run(units, kqs=None):
    if kqs is None:
        kqs = _mlstm_scores(units)
    cns = []
    for q, _, _, _, _, _, _, _, c_st, n_st, _ in units:
        cns.append(None if c_st is None else _dot_nt(
            jnp.concatenate([c_st, jnp.broadcast_to(n_st, (SUBLANES, D_HEAD))], axis=0), q))
    mids, nums = [], []
    for kq, (_, _, vt, a_col, b_row, _, neg, _, _, _, m_st) in zip(kqs, units):
        a_m = a_col + neg
        c_row = jnp.maximum(m_st, jnp.max(a_m, axis=0, keepdims=True))
        st = kq * jnp.exp(a_m - c_row)
        mids.append((st, jnp.sum(st, axis=0, keepdims=True), jnp.exp(m_st - c_row), b_row + c_row))
        nums.append(_dot(vt, st))
    upds, m_news = [], []
    for (_, k, vt, _, b_row, li_row, _, last, _, _, _), (_, _, _, mt) in zip(units, mids):
        m_new = mt[:, last:last + 1]
        ws = jnp.exp(b_row[:, last:last + 1] - b_row + li_row - m_new)
        upds.append(_dot(jnp.concatenate(
            [vt * ws, jnp.broadcast_to(ws, (SUBLANES, ws.shape[1]))], axis=0), k))
        m_news.append(m_new)
    results = []
    for u, cn, (_, den, wprev, mt), num, upd, m_new in zip(units, cns, mids, nums, upds, m_news):
        last, c_st, n_st = u[7], u[8], u[9]
        c_new = upd[0:D_HEAD, :]
        n_new = upd[D_HEAD:D_HEAD + 1, :]
        if c_st is not None:
            num = num + wprev * cn[0:D_HEAD, :]
            den = den + wprev * cn[D_HEAD:D_HEAD + 1, :]
            wc = wprev[:, last:last + 1]
            c_new = c_new + wc * c_st
            n_new = n_new + wc * n_st
        ht = num * (1.0 / jnp.maximum(jnp.abs(den), jnp.exp(-mt)))
        results.append((ht, c_new, n_new, m_new))
    return results


def _mlstm_units(zqk_ref, zvo_ref, g_ref, fb_ref, n, visits):
    r = lax.broadcasted_iota(jnp.int32, (n, n), 0)
    c = lax.broadcasted_iota(jnp.int32, (n, n), 1)
    gs = [g_ref[rows, :] for rows, _, _ in visits]
    lss = [_log_sigmoid(g + fb_ref[...]) for g in gs]
    bs = [_dot_split3(((c <= r) if direction == 0 else (c >= r)).astype(BF16), ls)
          for ls, (_, direction, _) in zip(lss, visits)]
    units = []
    for g, b, (rows, direction, state) in zip(gs, bs, visits):
        a = pltpu.roll(g, ML_HEADS, 1) - b
        bt = b.T
        gt = g.T
        if direction == 0:
            neg, last = jnp.where(r <= c, 0.0, NEG_INF), n - 1
        else:
            neg, last = jnp.where(r >= c, 0.0, NEG_INF), 0
        i_col = 2 * ML_HEADS * direction
        f_col = i_col + ML_HEADS
        for pair in range(ML_HEADS // 2):
            vt2 = zvo_ref[rows, pair * LANES:(pair + 1) * LANES].T
            for sub in range(2):
                hd = 2 * pair + sub
                feat = slice(sub * D_HEAD, (sub + 1) * D_HEAD)
                q = zqk_ref[rows, hd * D_HEAD:(hd + 1) * D_HEAD]
                k = zqk_ref[rows, ML_W + hd * D_HEAD:ML_W + (hd + 1) * D_HEAD]
                units.append((q, k, vt2[feat, :],
                              a[:, f_col + hd:f_col + hd + 1], bt[f_col + hd:f_col + hd + 1, :],
                              gt[i_col + hd:i_col + hd + 1, :], neg, last) + tuple(state[hd]))
    return units


def _mlstm_finish(ht, o_gate, norm_g):
    outs = []
    for hd in range(ML_HEADS):
        x = ht[hd * D_HEAD:(hd + 1) * D_HEAD, :]
        mu = jnp.mean(x, axis=0, keepdims=True)
        xc = x - mu
        var = jnp.mean(xc * xc, axis=0, keepdims=True)
        outs.append(xc * lax.rsqrt(var + LN_EPS))
    hn = jnp.concatenate(outs, axis=0).T
    return hn * norm_g * _sigmoid(o_gate)


def _mlstm_scan(zqk_ref, zvo_ref, g_ref, fb_ref, ng_ref, out_ref, n_tokens, state):
    n_chunks = n_tokens // ML_CHUNK
    step_chunks = [(i, n_chunks - 1 - i) for i in range(n_chunks)]
    no_state = [(None, None, None)] * ML_HEADS
    visits = [(slice(chunk * ML_CHUNK, (chunk + 1) * ML_CHUNK), direction, no_state)
              for chunks in step_chunks for direction, chunk in enumerate(chunks)]
    units = _mlstm_units(zqk_ref, zvo_ref, g_ref, fb_ref, ML_CHUNK, visits)
    kqs = _mlstm_scores(units)
    hts = {}
    for i, chunks in enumerate(step_chunks):
        step = slice(i * N_STATE, (i + 1) * N_STATE)
        results = _mlstm_run([u[:8] + tuple(st) for u, st in zip(units[step], state)], kqs[step])
        state = [(c_new, n_new, m_new) for _, c_new, n_new, m_new in results]
        for direction, chunk in enumerate(chunks):
            hts[direction, chunk] = jnp.concatenate(
                [results[direction * ML_HEADS + hd][0] for hd in range(ML_HEADS)], axis=0)
    for chunk in range(n_chunks):
        rows = slice(chunk * ML_CHUNK, (chunk + 1) * ML_CHUNK)
        out_ref[rows, :] = _mlstm_finish(hts[0, chunk] + hts[1, chunk],
                                         zvo_ref[rows, ML_W:2 * ML_W],
                                         ng_ref[...]).astype(out_ref.dtype)
    return state


def _store_state(c_ref, n_ref, m_ref, idx, c_new, n_new, m_new):
    c_ref[idx] = c_new
    n_ref[idx:idx + 1, :] = n_new
    m_ref[idx:idx + 1, :] = jnp.broadcast_to(m_new, (1, LANES))


ONES_ROWS = 16


def _with_ones(vt):
    ones = jnp.ones((ONES_ROWS, vt.shape[1]), BF16)
    return jnp.concatenate([vt.astype(BF16), ones], axis=0)


def _attend_many(problems):
    return _attend_values(problems, *_attend_scores(problems))


def _attend_scores(problems):
    scores = []
    for q, parts, _ in problems:
        sts = []
        for k, _, bias_t in parts:
            st = _dot_nt(k, q)
            sts.append(st if bias_t is None else st + bias_t)
        scores.append(sts)
    maxima = []
    for sts, (_, _, sink_row) in zip(scores, problems):
        m = None
        for st in sts:
            mx = jnp.max(st, axis=0, keepdims=True)
            m = mx if m is None else jnp.maximum(m, mx)
        maxima.append(m if sink_row is None else jnp.maximum(m, sink_row))
    return scores, maxima


def _attend_values(problems, scores, maxima):
    outs = []
    for sts, m, (_, parts, sink_row) in zip(scores, maxima, problems):
        ot = None
        for st, (_, vt_ones, _) in zip(sts, parts):
            part = jnp.dot(vt_ones, jnp.exp2(st - m).astype(BF16), preferred_element_type=F32)
            ot = part if ot is None else ot + part
        den = ot[D_HEAD:D_HEAD + 1, :]
        if sink_row is not None:
            den = den + jnp.exp2(sink_row - m)
        outs.append(ot[0:D_HEAD, :] * (1.0 / den))
    return outs


def _head_rows(pairs_t, hd):
    return pairs_t[hd // 2][(hd % 2) * D_HEAD:(hd % 2 + 1) * D_HEAD, :]


def _store_heads_t(o_ref, ots):
    for p in range(len(ots) // 2):
        o_ref[:, p * LANES:(p + 1) * LANES] = jnp.concatenate(
            [ots[2 * p], ots[2 * p + 1]], axis=0).astype(o_ref.dtype).T


N_STATE = 2 * ML_HEADS
N_CTX_STATE_OUTS = 7


def _ctx_mixers_kernel(x_ref, mod_ref, wt_ref, b_ref, fb_ref, ng_ref, sink_ref, *rest,
                       first_layer):
    n_scratch = 1 + len(Z_GROUPS)
    w_scr, zqk_ref, zvo_ref, zswa_ref, znat_ref, zg_ref = rest[-n_scratch:]
    mml_ref, mswa_ref, mnat_ref, *state_refs = rest[-(3 + N_CTX_STATE_OUTS) - n_scratch:-n_scratch]
    _inproj_kernel(x_ref, mod_ref, wt_ref, b_ref, zqk_ref, zvo_ref, zswa_ref, znat_ref, zg_ref,
                   w_scr)
    if first_layer:
        for ref in state_refs:
            ref[1:] = jnp.zeros((DEPTH - 1,) + ref.shape[1:], F32)
        state_refs = [ref.at[0] for ref in state_refs]
    sk_ref, sv_ref, nk_ref, nv_ref, c_ref, n_ref, m_ref = state_refs
    rows = slice(0, SEQ)
    scaled_q = lambda ref, hd: ref[:, hd * D_HEAD:(hd + 1) * D_HEAD].astype(BF16)
    swa_vts = [zswa_ref[:, SWA_W + SWA_KV_W:SWA_W + 2 * SWA_KV_W].astype(BF16).T]
    nat_vts = [znat_ref[:, 2 * NAT_W + p * LANES:2 * NAT_W + (p + 1) * LANES].astype(BF16).T
               for p in range(NAT_W // LANES)]
    problems = []
    for kv in range(SWA_KV_HEADS):
        k = zswa_ref[:, SWA_W + kv * D_HEAD:SWA_W + (kv + 1) * D_HEAD]
        sk_ref[kv] = k
        sv_ref[kv] = zswa_ref[:, SWA_W + SWA_KV_W + kv * D_HEAD:SWA_W + SWA_KV_W + (kv + 1) * D_HEAD]
        q = jnp.concatenate([scaled_q(zswa_ref, kv * SWA_GROUP + g) for g in range(SWA_GROUP)],
                            axis=0)
        problems.append((q, [(k.astype(BF16), _with_ones(_head_rows(swa_vts, kv)), None)],
                         sink_ref[kv:kv + 1, :]))
    for hd in range(NAT_HEADS):
        k = znat_ref[:, NAT_W + hd * D_HEAD:NAT_W + (hd + 1) * D_HEAD]
        nk_ref[hd] = k
        nv_ref[hd] = znat_ref[:, 2 * NAT_W + hd * D_HEAD:2 * NAT_W + (hd + 1) * D_HEAD]
        problems.append((scaled_q(znat_ref, hd),
                         [(k.astype(BF16), _with_ones(_head_rows(nat_vts, hd)), None)], None))
    scores, maxima = _attend_scores(problems)

    zero_state = [(None, jnp.zeros((1, D_HEAD), F32), jnp.zeros((1, 1), F32))] * N_STATE
    final = _mlstm_scan(zqk_ref, zvo_ref, zg_ref, fb_ref, ng_ref, mml_ref, SEQ, zero_state)
    for idx, (c_new, n_new, m_new) in enumerate(final):
        _store_state(c_ref, n_ref, m_ref, idx, c_new, n_new, m_new)

    ots = _attend_values(problems, scores, maxima)
    _store_heads_t(mswa_ref, [ot[:, g * SEQ:(g + 1) * SEQ]
                              for ot in ots[:SWA_KV_HEADS] for g in range(SWA_GROUP)])
    _store_heads_t(mnat_ref, ots[SWA_KV_HEADS:])


CTX_STATE_DIMS = ((SWA_KV_HEADS, SEQ, D_HEAD), (SWA_KV_HEADS, SEQ, D_HEAD),
                  (NAT_HEADS, SEQ, D_HEAD), (NAT_HEADS, SEQ, D_HEAD),
                  (N_STATE, D_HEAD, D_HEAD), (N_STATE, D_HEAD), (N_STATE, LANES))


def _ctx_mixers_call(x, mods, w_in_t, b_in, fb_row, norm_g, sink_rows, layer, prev_state):
    rows = x.shape[0]
    row_spec = lambda w: pl.BlockSpec((SEQ, w), lambda b: (b, 0))
    state_dims = CTX_STATE_DIMS
    first_layer = prev_state is None
    if first_layer:
        assert layer == 0
        state_spec = lambda d: pl.BlockSpec((None, DEPTH) + d, lambda b: (b,) + (0,) * (len(d) + 1))
    else:
        state_spec = lambda d: pl.BlockSpec((None, None) + d, lambda b: (b, layer) + (0,) * len(d))
    prev = () if first_layer else tuple(prev_state)
    n_in = 7
    return pl.pallas_call(
        functools.partial(_ctx_mixers_kernel, first_layer=first_layer),
        grid=(BATCH,),
        in_specs=[row_spec(D_MODEL), _mod_spec(layer, lambda b: CTX_MOD_ROW),
                  _layer_spec(w_in_t, layer, single_buffer=True), _layer_spec(b_in, layer),
                  _layer_spec(fb_row, layer), _layer_spec(norm_g, layer),
                  _layer_spec(sink_rows, layer)]
                 + [pl.BlockSpec(memory_space=pl.ANY)] * len(prev),
        out_specs=[row_spec(ML_W), row_spec(SWA_W), row_spec(NAT_W)]
                  + [state_spec(d) for d in state_dims],
        out_shape=[jax.ShapeDtypeStruct((rows, ML_W), BF16),
                   jax.ShapeDtypeStruct((rows, SWA_W), BF16),
                   jax.ShapeDtypeStruct((rows, NAT_W), BF16)]
                  + [jax.ShapeDtypeStruct((BATCH, DEPTH) + d, F32) for d in state_dims],
        input_output_aliases={n_in + i: 3 + i for i in range(len(prev))},
        scratch_shapes=[pltpu.VMEM((Z_W, D_MODEL), BF16)]
                       + [pltpu.VMEM((SEQ, w), dt) for w, dt in zip(Z_GROUPS, CTX_Z_DTYPES)],
        compiler_params=_params(1),
        name="ctx_mixers",
    )(x, mods, w_in_t, b_in, fb_row, norm_g, sink_rows, *prev)


def _lat_mlstm_kernel(zqk_ref, zvo_ref, zg_ref, fb_ref, ng_ref, c0_ref, n0_ref, m0_ref, mml_ref):
    state = [(c0_ref[idx], n0_ref[idx:idx + 1, :], m0_ref[idx:idx + 1, :])
             for idx in range(N_STATE)]
    _mlstm_scan(zqk_ref, zvo_ref, zg_ref, fb_ref, ng_ref, mml_ref, DEC_SEQ, state)


def _lat_mlstm_call(z_qk, z_vo, z_g, fb_row, norm_g, state_c, state_n, state_m, layer):
    n_st = 2 * ML_HEADS
    n_chunks = DEC_SEQ // ML_CHUNK
    row_spec = lambda w: pl.BlockSpec((DEC_SEQ, w), lambda b: (b, 0))
    return pl.pallas_call(
        _lat_mlstm_kernel,
        grid=(DEC_BATCH,),
        in_specs=[row_spec(Z_QK_W), row_spec(Z_VO_W), row_spec(Z_G_W),
                  _layer_spec(fb_row, layer), _layer_spec(norm_g, layer),
                  pl.BlockSpec((None, None, n_st, D_HEAD, D_HEAD), lambda b: (b, layer, 0, 0, 0)),
                  pl.BlockSpec((None, None, n_st, D_HEAD), lambda b: (b, layer, 0, 0)),
                  pl.BlockSpec((None, None, n_st, 1), lambda b: (b, layer, 0, 0))],
        out_specs=row_spec(ML_W),
        out_shape=jax.ShapeDtypeStruct((DEC_BATCH * DEC_SEQ, ML_W), BF16),
        compiler_params=_params(1),
        name="lat_mlstm",
    )(z_qk, z_vo, z_g, fb_row, norm_g, state_c, state_n, state_m)


def _lat_swa_kernel(zswa_ref, ckt_ref, cvt_ref, sink_ref, mswa_ref,
                    vt_scr, ot_scr, bias_scr, ck_scr, cvt_scr):
    vt_scr[...] = zswa_ref[:, SWA_W + SWA_KV_W:SWA_W + 2 * SWA_KV_W].T

    bl = SWA_BLOCK
    nb = DEC_SEQ // bl
    n_q = SWA_GROUP * bl
    kj = lax.broadcasted_iota(jnp.int32, (3 * bl, n_q), 0)
    qi = lax.broadcasted_iota(jnp.int32, (3 * bl, n_q), 1) & (bl - 1)
    keep = ((kj >= qi) & (kj < 2 * bl)) | ((kj >= 2 * bl) & (kj - 2 * bl <= qi))
    bias_scr[...] = jnp.where(keep, 0.0, NEG_INF)
    ck_scr[...] = jnp.concatenate([ckt_ref[kv] for kv in range(SWA_KV_HEADS)],
                                  axis=0).astype(BF16).T
    for kv in range(SWA_KV_HEADS):
        cvt_scr[kv] = _with_ones(cvt_ref[kv])

    problems, where = [], []
    for kv in range(SWA_KV_HEADS):
        head = slice(kv * D_HEAD, (kv + 1) * D_HEAD)
        for blk in range(nb):
            lo_b, hi_b = max(blk - 1, 0), min(blk + 2, nb)
            keys = slice(lo_b * bl, hi_b * bl)
            q = jnp.concatenate(
                [zswa_ref[blk * bl:(blk + 1) * bl,
                          (kv * SWA_GROUP + g) * D_HEAD:(kv * SWA_GROUP + g + 1) * D_HEAD]
                 for g in range(SWA_GROUP)], axis=0)
            parts = [(zswa_ref[keys, SWA_W + kv * D_HEAD:SWA_W + (kv + 1) * D_HEAD],
                      _with_ones(vt_scr[head, keys]),
                      bias_scr[(lo_b - blk + 1) * bl:(hi_b - blk + 1) * bl, :]),
                     (ck_scr[:, head], cvt_scr[kv], None)]
            problems.append((q, parts, sink_ref[kv:kv + 1, :]))
            where.append((kv, blk))
    for (kv, blk), ot in zip(where, _attend_many(problems)):
        o = ot.astype(BF16)
        for g in range(SWA_GROUP):
            hq = kv * SWA_GROUP + g
            ot_scr[hq * D_HEAD:(hq + 1) * D_HEAD, blk * bl:(blk + 1) * bl] = (
                o[:, g * bl:(g + 1) * bl])
    for p in range(SWA_W // LANES):
        mswa_ref[:, p * LANES:(p + 1) * LANES] = ot_scr[p * LANES:(p + 1) * LANES, :].T


def _lat_swa_call(z_swa, cache_k, cache_v, sink_rows, layer):
    row_spec = lambda w: pl.BlockSpec((DEC_SEQ, w), lambda b: (b, 0))
    cache_spec = pl.BlockSpec((None, None, SWA_KV_HEADS, D_HEAD, PAST_LEN),
                              lambda b: (b, layer, 0, 0, 0))
    n_q = SWA_GROUP * SWA_BLOCK
    return pl.pallas_call(
        _lat_swa_kernel,
        grid=(DEC_BATCH,),
        in_specs=[row_spec(Z_SWA_W), cache_spec, cache_spec, _layer_spec(sink_rows, layer)],
        out_specs=row_spec(SWA_W),
        out_shape=jax.ShapeDtypeStruct((DEC_BATCH * DEC_SEQ, SWA_W), BF16),
        scratch_shapes=[pltpu.VMEM((SWA_KV_W, DEC_SEQ), BF16), pltpu.VMEM((SWA_W, DEC_SEQ), BF16),
                        pltpu.VMEM((3 * SWA_BLOCK, n_q), F32),
                        pltpu.VMEM((PAST_LEN, SWA_KV_W), BF16),
                        pltpu.VMEM((SWA_KV_HEADS, D_HEAD + ONES_ROWS, PAST_LEN), BF16)],
        compiler_params=_params(1),
        name="lat_swa",
    )(z_swa, cache_k, cache_v, sink_rows)


NAT_ROWS = DEC_SEQ // GRID_W
NAT_GROUP_ROWS = 4
NAT_GROUP_Q = NAT_GROUP_ROWS * GRID_W
NAT_RPB_ROWS = 2 * NAT_KH


def _nat_row_start(r):
    return min(max(r - NAT_KH // 2, 0), NAT_ROWS - NAT_KH)


def _nat_groups():
    groups, off = [], 0
    for g in range(NAT_ROWS // NAT_GROUP_ROWS):
        w0 = _nat_row_start(g * NAT_GROUP_ROWS)
        w1 = _nat_row_start((g + 1) * NAT_GROUP_ROWS - 1) + NAT_KH
        n_rows = w1 - w0 + (w1 - w0) % 2
        w0 = min(w0, NAT_ROWS - n_rows)
        groups.append((w0, n_rows, off))
        off += n_rows * GRID_W
    return groups, off


NAT_GROUPS, NAT_BIAS_KEYS = _nat_groups()


def _nat_build_bias(rpb_ref, bias_scr):
    shape = (GRID_W, LANES)
    q = lax.broadcasted_iota(jnp.int32, shape, 0)
    lane = lax.broadcasted_iota(jnp.int32, shape, 1)
    kc = lane & (GRID_W - 1)
    cs = jnp.clip(q - NAT_KW // 2, 0, GRID_W - NAT_KW)
    ok = (kc >= cs) & (kc < cs + NAT_KW)
    ok_lo = ok & (lane < GRID_W)
    ok_hi = ok & (lane >= GRID_W)
    neg_tile = jnp.full(shape, NEG_INF, F32)
    for hd in range(2):
        lo, hi = [], []
        for ro in range(2 * NAT_KH - 1):
            x = jnp.broadcast_to(rpb_ref[hd, ro:ro + 1, :], shape)
            lo.append(jnp.where(ok_lo, pltpu.roll(x, 0, 1, stride=1, stride_axis=0), NEG_INF))
            hi.append(jnp.where(ok_hi, pltpu.roll(x, GRID_W, 1, stride=1, stride_axis=0), NEG_INF))
        for g, (w0, n_rows, off) in enumerate(NAT_GROUPS):
            for jj in range(n_rows // 2):
                kra, krb = w0 + 2 * jj, w0 + 2 * jj + 1
                tiles = []
                for rr in range(NAT_GROUP_ROWS):
                    r = g * NAT_GROUP_ROWS + rr
                    r0 = _nat_row_start(r)
                    parts = []
                    if r0 <= kra < r0 + NAT_KH:
                        parts.append(lo[kra - r + NAT_KH - 1])
                    if r0 <= krb < r0 + NAT_KH:
                        parts.append(hi[krb - r + NAT_KH - 1])
                    tiles.append(neg_tile if not parts else (
                        parts[0] if len(parts) == 1 else jnp.maximum(parts[0], parts[1])))
                bias_scr[hd, off + jj * LANES:off + (jj + 1) * LANES, :] = (
                    jnp.concatenate(tiles, axis=0).T)


def _lat_nat_kernel(q_ref, k_ref, v_ref, ckt_ref, cvt_ref, rpb_ref, mnat_ref, bias_scr):
    @pl.when(pl.program_id(1) == 0)
    def _():
        _nat_build_bias(rpb_ref, bias_scr)

    vt = v_ref[...].astype(BF16).T
    ck2 = jnp.concatenate([ckt_ref[0], ckt_ref[1]], axis=0).astype(BF16).T
    problems = []
    for hd in range(2):
        head = slice(hd * D_HEAD, (hd + 1) * D_HEAD)
        ck = ck2[:, head]
        cvt_ones = _with_ones(cvt_ref[hd])
        for g, (w0, n_rows, off) in enumerate(NAT_GROUPS):
            keys = slice(w0 * GRID_W, (w0 + n_rows) * GRID_W)
            parts = [(k_ref[keys, head].astype(BF16), _with_ones(vt[head, keys]),
                      bias_scr[hd, off:off + n_rows * GRID_W, :]),
                     (ck, cvt_ones, None)]
            q = q_ref[g * NAT_GROUP_Q:(g + 1) * NAT_GROUP_Q, head].astype(BF16)
            problems.append((q, parts, None))
    ots = _attend_many(problems)
    n_groups = len(NAT_GROUPS)
    mnat_ref[...] = jnp.concatenate(
        [jnp.concatenate(ots[hd * n_groups:(hd + 1) * n_groups], axis=1) for hd in range(2)],
        axis=0).astype(BF16).T


def _lat_nat_call(z_nat, cache_k, cache_v, rpb_lanes, layer):
    n_pair = NAT_HEADS // 2
    col_spec = lambda off: pl.BlockSpec((DEC_SEQ, LANES), lambda p, b: (b, off + p))
    cache_spec = pl.BlockSpec((None, None, 2, D_HEAD, PAST_LEN), lambda p, b: (b, layer, p, 0, 0))
    return pl.pallas_call(
        _lat_nat_kernel,
        grid=(n_pair, DEC_BATCH),
        in_specs=[col_spec(0), col_spec(n_pair), col_spec(2 * n_pair), cache_spec, cache_spec,
                  pl.BlockSpec((None, 2, NAT_RPB_ROWS, LANES), lambda p, b: (layer, p, 0, 0))],
        out_specs=pl.BlockSpec((DEC_SEQ, LANES), lambda p, b: (b, p)),
        out_shape=jax.ShapeDtypeStruct((DEC_BATCH * DEC_SEQ, NAT_W), BF16),
        scratch_shapes=[pltpu.VMEM((2, NAT_BIAS_KEYS, NAT_GROUP_Q), F32)],
        compiler_params=_params(2),
        name="lat_nat",
    )(z_nat, z_nat, z_nat, cache_k, cache_v, rpb_lanes)


def _nat_rpb_lanes(rpb):
    n_off = 2 * NAT_KW - 1
    padded = jnp.concatenate(
        [rpb.astype(F32), jnp.zeros(rpb.shape[:-1] + (LANES - n_off,), F32)], axis=-1)
    rolled = jnp.roll(padded, -(NAT_KW - 1), axis=-1)
    pad_rows = jnp.zeros(rpb.shape[:-2] + (NAT_RPB_ROWS - rpb.shape[-2], LANES), F32)
    return jnp.concatenate([rolled, pad_rows], axis=-2)


def _rope_tables():
    t = np.arange(DEC_SEQ)[:, None]
    d = np.arange(LANES)[None, :] % D_HEAD
    pos = np.where(d < D_HEAD // 2, t // GRID_W, t % GRID_W).astype(np.float64)
    quarter = D_HEAD // 4
    freq = ROPE_BASE ** (-(d % quarter).astype(np.float64) / quarter)
    ang = (pos.astype(np.float32) * freq.astype(np.float32)).astype(np.float32)
    sign = np.where((d % (2 * quarter)) < quarter, -1.0, 1.0)
    return (jnp.asarray(np.cos(ang), dtype=F32), jnp.asarray(np.sin(ang) * sign, dtype=F32))


def _permute_in_columns(a):
    g0 = Z_ML_W
    s0 = g0 + N_GATES
    pad = jnp.zeros(a.shape[:-1] + (Z_G_W - N_GATES,), a.dtype)
    return jnp.concatenate([a[..., :g0], a[..., s0:], a[..., g0:s0], pad], axis=-1)


def kernel(x_prompt, x_sample, cache_swa_k, cache_swa_v, cache_nat_k, cache_nat_v, state_mlstm_C,
           state_mlstm_n, state_mlstm_m, c, c_ctx, w_ada, b_ada, w_in, b_in, mlstm_fbias,
           mlstm_norm_g, swa_sink, nat_rpb, w_out, ln1_g, ln1_b, w_mlp1, w_mlp2, ln2_g, ln2_b):
    xp = x_prompt.reshape(BATCH * SEQ, D_MODEL)
    xs = x_sample.reshape(DEC_BATCH * DEC_SEQ, D_MODEL)
    cvec = jnp.concatenate(
        [c, c_ctx[None, :], jnp.zeros((MOD_ROWS - DEC_BATCH - 1, D_MODEL), F32)], axis=0)
    mods = _mods_call(cvec, w_ada, b_ada).reshape(DEPTH * MOD_ROWS, 6, D_MODEL)

    w_in_t = jnp.swapaxes(w_in, 1, 2)
    out_scale = np.ones((Z_W,), np.float32)
    out_scale[ML_W:2 * ML_W] = ATT_SCALE
    out_scale[Z_ML_W:Z_ML_W + SWA_W] = Q_SCALE2
    out_scale[Z_ML_W + Z_SWA_W:Z_ML_W + Z_SWA_W + NAT_W] = Q_SCALE2
    b_in_p = jnp.stack([jnp.broadcast_to(out_scale, (DEPTH, Z_W)),
                        _permute_in_columns(b_in) * out_scale], axis=1)
    cache_swa_k, cache_swa_v, cache_nat_k, cache_nat_v = (
        jnp.swapaxes(a, -1, -2) for a in (cache_swa_k, cache_swa_v, cache_nat_k, cache_nat_v))
    vec = lambda a: a.reshape(DEPTH, 1, D_MODEL)
    tail_w = (w_out.astype(BF16), vec(ln1_g), vec(ln1_b), w_mlp1.astype(BF16),
              w_mlp2.astype(BF16), vec(ln2_g), vec(ln2_b))
    cos_t, sin_t = _rope_tables()
    rpb_lanes = _nat_rpb_lanes(nat_rpb) * LOG2E
    swa_sink = swa_sink * LOG2E
    fb_rows = jnp.zeros((DEPTH, 1, Z_G_W), F32)
    fb_rows = fb_rows.at[:, 0, ML_HEADS:2 * ML_HEADS].set(mlstm_fbias[:, 0])
    fb_rows = fb_rows.at[:, 0, 3 * ML_HEADS:4 * ML_HEADS].set(mlstm_fbias[:, 1])
    norm_g = mlstm_norm_g.reshape(DEPTH, 1, ML_W)
    sink_ctx = jnp.repeat(swa_sink, SEQ, axis=1).reshape(DEPTH, SWA_KV_HEADS, SWA_GROUP * SEQ)
    sink_lat = jnp.repeat(swa_sink, SWA_BLOCK, axis=1).reshape(
        DEPTH, SWA_KV_HEADS, SWA_GROUP * SWA_BLOCK)

    n_st = 2 * ML_HEADS
    state_c = state_mlstm_C.reshape(DEC_BATCH, DEPTH, n_st, D_HEAD, D_HEAD)
    state_n = state_mlstm_n.reshape(DEC_BATCH, DEPTH, n_st, D_HEAD)
    state_m = state_mlstm_m.reshape(DEC_BATCH, DEPTH, n_st, 1)

    ctx_row = lambda tile: (lambda i: CTX_MOD_ROW)
    lat_row = lambda tile: (lambda i: i // (DEC_SEQ // tile))

    ctx_state = None
    for l in range(DEPTH):
        m_ml, m_swa, m_nat, *ctx_state = _ctx_mixers_call(
            xp, mods, w_in_t, b_in_p, fb_rows, norm_g, sink_ctx, l, ctx_state)
        xp = _tail_call(xp, m_ml, m_swa, m_nat, mods, ctx_row(TAIL_TILE), tail_w, l)

        z_qk, z_vo, z_swa, z_nat, z_g = _lat_inproj_call(xs, mods, w_in_t, b_in_p, cos_t, sin_t, l)
        m_ml = _lat_mlstm_call(z_qk, z_vo, z_g, fb_rows, norm_g, state_c, state_n, state_m, l)
        m_swa = _lat_swa_call(z_swa, cache_swa_k, cache_swa_v, sink_lat, l)
        m_nat = _lat_nat_call(z_nat, cache_nat_k, cache_nat_v, rpb_lanes, l)
        xs = _tail_call(xs, m_ml, m_swa, m_nat, mods, lat_row(TAIL_TILE), tail_w, l)

    k_swa, v_swa, k_nat, v_nat, st_c, st_n, st_m = ctx_state
    return (xp.reshape(BATCH, SEQ, D_MODEL), xs.reshape(DEC_BATCH, DEC_SEQ, D_MODEL),
            k_swa, v_swa, k_nat, v_nat,
            st_c.reshape(BATCH, DEPTH, 2, ML_HEADS, D_HEAD, D_HEAD),
            st_n.reshape(BATCH, DEPTH, 2, ML_HEADS, D_HEAD),
            st_m[..., 0].reshape(BATCH, DEPTH, 2, ML_HEADS))
```

```python
import functools

import numpy as np
import jax
import jax.numpy as jnp
from jax import lax
from jax.experimental import pallas as pl
from jax.experimental.pallas import tpu as pltpu

D_MODEL = 1024
BATCH = 16
SEQ = 256
DEPTH = 4
DEC_BATCH = 8
DEC_SEQ = 1024
PAST_LEN = 512
GRID_W = 64
D_HEAD = 64
ML_HEADS = 4
SWA_HEADS = 6
SWA_KV_HEADS = 2
SWA_GROUP = SWA_HEADS // SWA_KV_HEADS
NAT_HEADS = 6
ML_W = ML_HEADS * D_HEAD
SWA_W = SWA_HEADS * D_HEAD
SWA_KV_W = SWA_KV_HEADS * D_HEAD
NAT_W = NAT_HEADS * D_HEAD
N_GATES = 4 * ML_HEADS
IN_DIM = 4 * ML_W + N_GATES + SWA_W + 2 * SWA_KV_W + 3 * NAT_W
D_FF = 4 * D_MODEL
SWA_WINDOW = 128
SWA_BLOCK = 128
NAT_KH = 8
NAT_KW = 16
ROPE_BASE = 10000.0
LN_EPS = 1e-5
DN_ALPHA = (2 * DEPTH) ** 0.25
ATT_SCALE = D_HEAD ** -0.5
LOG2E = 1.4426950408889634
Q_SCALE2 = ATT_SCALE * LOG2E

LANES = 128
SUBLANES = 8
Z_ML_W = 4 * ML_W
Z_SWA_W = SWA_W + 2 * SWA_KV_W
Z_NAT_W = 3 * NAT_W
Z_G_W = LANES
Z_W = Z_ML_W + Z_SWA_W + Z_NAT_W + Z_G_W
Z_QK_W = 2 * ML_W
Z_VO_W = 2 * ML_W
Z_GROUPS = (Z_QK_W, Z_VO_W, Z_SWA_W, Z_NAT_W, Z_G_W)
Z_SWA_GROUP = 2
MOD_ROWS = 16
CTX_MOD_ROW = DEC_BATCH
ROW_TILE = 1024
TAIL_TILE = 1024
TAIL_PARTS = 2
ML_CHUNK = 128
VMEM_LIMIT = 58 * 1024 * 1024

BF16 = jnp.bfloat16
F32 = jnp.float32
NEG_INF = float("-inf")
LAT_Z_DTYPES = (BF16, F32, BF16, BF16, F32)
CTX_Z_DTYPES = (BF16, F32, F32, F32, F32)


def _dot(a, b):
    return jnp.dot(a.astype(BF16), b.astype(BF16), preferred_element_type=F32)


def _dot_nt(a, b):
    return lax.dot_general(a.astype(BF16), b.astype(BF16), (((1,), (1,)), ((), ())),
                           preferred_element_type=F32)


def _dot_split3(tri, x):
    hi = x.astype(BF16)
    r1 = x - hi.astype(F32)
    mid = r1.astype(BF16)
    lo = (r1 - mid.astype(F32)).astype(BF16)
    return (jnp.dot(tri, hi, preferred_element_type=F32)
            + jnp.dot(tri, mid, preferred_element_type=F32)
            + jnp.dot(tri, lo, preferred_element_type=F32))


def _sigmoid(x):
    return 1.0 / (1.0 + jnp.exp(-x))


def _log_sigmoid(x):
    return jnp.minimum(x, 0.0) - jnp.log1p(jnp.exp(-jnp.abs(x)))


def _layer_norm(x, g, b):
    mu = jnp.mean(x, axis=-1, keepdims=True)
    xc = x - mu
    var = jnp.mean(xc * xc, axis=-1, keepdims=True)
    return xc * lax.rsqrt(var + LN_EPS) * g + b


def _params(n_axes=1):
    return pltpu.CompilerParams(dimension_semantics=("arbitrary",) * n_axes,
                                vmem_limit_bytes=VMEM_LIMIT)


def _layer_spec(a, layer, single_buffer=False):
    kwargs = {"pipeline_mode": pl.Buffered(1)} if single_buffer else {}
    return pl.BlockSpec((None,) + a.shape[1:], lambda *_: (layer,) + (0,) * (a.ndim - 1), **kwargs)


def _mods_kernel(c_ref, w_ref, b_ref, o_ref):
    c = c_ref[...]
    o_ref[...] = _dot(c * _sigmoid(c), w_ref[...]) + b_ref[...]


def _mods_call(cvec, w_ada, b_ada):
    n_col = 6
    return pl.pallas_call(
        _mods_kernel,
        grid=(DEPTH, n_col),
        in_specs=[
            pl.BlockSpec((MOD_ROWS, D_MODEL), lambda l, j: (0, 0)),
            pl.BlockSpec((None, D_MODEL, D_MODEL), lambda l, j: (l, 0, j)),
            pl.BlockSpec((None, 1, D_MODEL), lambda l, j: (l, 0, j)),
        ],
        out_specs=pl.BlockSpec((None, MOD_ROWS, D_MODEL), lambda l, j: (l, 0, j)),
        out_shape=jax.ShapeDtypeStruct((DEPTH, MOD_ROWS, 6 * D_MODEL), F32),
        compiler_params=_params(2),
        name="mods",
    )(cvec, w_ada, b_ada.reshape(DEPTH, 1, 6 * D_MODEL))


def _rope(x, cos, sin_signed):
    lane = lax.broadcasted_iota(jnp.int32, x.shape, 1)
    first = (lane & 31) < 16
    partner = jnp.where(first, pltpu.roll(x, LANES - 16, 1), pltpu.roll(x, 16, 1))
    return x * cos + partner * sin_signed


def _inproj_kernel(x_ref, mod_ref, wt_ref, b_ref, *rest, rope=False):
    if rope:
        cos_ref, sin_ref, *rest = rest
    *out_refs, w_scr = rest

    @pl.when(pl.program_id(0) == 0)
    def _():
        rest = IN_DIM - Z_ML_W - N_GATES
        w_scr[0:Z_ML_W, :] = wt_ref[0:Z_ML_W, :].astype(BF16)
        w_scr[Z_ML_W:Z_ML_W + rest, :] = wt_ref[Z_ML_W + N_GATES:IN_DIM, :].astype(BF16)
        w_scr[Z_ML_W + rest:IN_DIM, :] = wt_ref[Z_ML_W:Z_ML_W + N_GATES, :].astype(BF16)
        w_scr[IN_DIM:Z_W, :] = jnp.zeros((Z_W - IN_DIM, D_MODEL), BF16)

    h = (x_ref[...] * (1.0 + mod_ref[1:2, :]) + mod_ref[0:1, :]).astype(BF16)
    offsets = [sum(Z_GROUPS[:i]) for i in range(len(Z_GROUPS))]
    accs = [_dot_nt(h, w_scr[o:o + w, :]) for o, w in zip(offsets, Z_GROUPS)]
    for ref, w, o, acc in zip(out_refs, Z_GROUPS, offsets, accs):
        z = acc * b_ref[0:1, o:o + w] + b_ref[1:2, o:o + w]
        if rope and ref is out_refs[Z_SWA_GROUP]:
            z = jnp.concatenate(
                [_rope(z[:, j * LANES:(j + 1) * LANES], cos_ref[...], sin_ref[...])
                 for j in range((SWA_W + SWA_KV_W) // LANES)] + [z[:, SWA_W + SWA_KV_W:]], axis=1)
        ref[...] = z.astype(ref.dtype)


def _mod_spec(layer, mod_row_of_tile):
    return pl.BlockSpec((None, 6, D_MODEL),
                        lambda i: (layer * MOD_ROWS + mod_row_of_tile(i), 0, 0))


def _lat_inproj_call(x, mods, w_in_t, b_in, cos_t, sin_t, layer):
    rows = x.shape[0]
    tiles_per_seq = DEC_SEQ // ROW_TILE
    table_spec = pl.BlockSpec((ROW_TILE, LANES), lambda i: (i % tiles_per_seq, 0))
    return pl.pallas_call(
        functools.partial(_inproj_kernel, rope=True),
        grid=(rows // ROW_TILE,),
        in_specs=[
            pl.BlockSpec((ROW_TILE, D_MODEL), lambda i: (i, 0)),
            _mod_spec(layer, lambda i: i // tiles_per_seq),
            _layer_spec(w_in_t, layer, single_buffer=True),
            _layer_spec(b_in, layer),
            table_spec, table_spec,
        ],
        out_specs=[pl.BlockSpec((ROW_TILE, w), lambda i: (i, 0)) for w in Z_GROUPS],
        out_shape=[jax.ShapeDtypeStruct((rows, w), dt) for w, dt in zip(Z_GROUPS, LAT_Z_DTYPES)],
        scratch_shapes=[pltpu.VMEM((Z_W, D_MODEL), BF16)],
        compiler_params=_params(1),
        name="inproj",
    )(x, mods, w_in_t, b_in, cos_t, sin_t)


def _tail_kernel(x_ref, mml_ref, mswa_ref, mnat_ref, mod_ref, wout_ref, ln1g_ref, ln1b_ref,
                 w1_ref, w2_ref, ln2g_ref, ln2b_ref, o_ref):
    g_a, sh_m, sc_m, g_m = (mod_ref[i:i + 1, :] for i in (2, 3, 4, 5))
    part = x_ref.shape[0] // TAIL_PARTS
    halves = tuple(slice(i * part, (i + 1) * part) for i in range(TAIL_PARTS))
    projs = [jnp.dot(jnp.concatenate([mml_ref[r, :], mswa_ref[r, :], mnat_ref[r, :]], axis=1),
                     wout_ref[...], preferred_element_type=F32) for r in halves]
    ys, fs = [], []
    for r, proj in zip(halves, projs):
        y = _layer_norm(DN_ALPHA * x_ref[r, :] + g_a * proj, ln1g_ref[...], ln1b_ref[...])
        h = (y * (1.0 + sc_m) + sh_m).astype(BF16)
        ys.append(y)
        fs.append(jnp.dot(h, w1_ref[...], preferred_element_type=F32))
    mlps = []
    for f in fs:
        f = jnp.maximum(f, 0.0)
        mlps.append(jnp.dot((f * f).astype(BF16), w2_ref[...], preferred_element_type=F32))
    for r, y, mlp in zip(halves, ys, mlps):
        o_ref[r, :] = _layer_norm(DN_ALPHA * y + g_m * mlp, ln2g_ref[...], ln2b_ref[...])


def _tail_call(x, mix_ml, mix_swa, mix_nat, mods, mod_row_of_tile, consts, layer):
    rows = x.shape[0]
    row_spec = lambda w: pl.BlockSpec((TAIL_TILE, w), lambda i: (i, 0))
    return pl.pallas_call(
        _tail_kernel,
        grid=(rows // TAIL_TILE,),
        in_specs=[row_spec(D_MODEL), row_spec(ML_W), row_spec(SWA_W), row_spec(NAT_W),
                  _mod_spec(layer, mod_row_of_tile)]
                 + [_layer_spec(a, layer, single_buffer=True) for a in consts],
        out_specs=row_spec(D_MODEL),
        out_shape=jax.ShapeDtypeStruct((rows, D_MODEL), F32),
        compiler_params=_params(1),
        name="tail",
    )(x, mix_ml, mix_swa, mix_nat, mods, *consts)


def _mlstm_scores(units):
    return [_dot_nt(u[1], u[0]) for u in units]


def _mlstm_run(units, kqs=None):
    if kqs is None:
        kqs = _mlstm_scores(units)
    cns = []
    for q, _, _, _, _, _, _, _, c_st, n_st, _ in units:
        cns.append(None if c_st is None else _dot_nt(
            jnp.concatenate([c_st, jnp.broadcast_to(n_st, (SUBLANES, D_HEAD))], axis=0), q))
    mids, nums = [], []
    for kq, (_, _, vt, a_col, b_row, _, neg, _, _, _, m_st) in zip(kqs, units):
        a_m = a_col + neg
        c_row = jnp.maximum(m_st, jnp.max(a_m, axis=0, keepdims=True))
        st = kq * jnp.exp(a_m - c_row)
        mids.append((st, jnp.sum(st, axis=0, keepdims=True), jnp.exp(m_st - c_row), b_row + c_row))
        nums.append(_dot(vt, st))
    upds, m_news = [], []
    for (_, k, vt, _, b_row, li_row, _, last, _, _, _), (_, _, _, mt) in zip(units, mids):
        m_new = mt[:, last:last + 1]
        ws = jnp.exp(b_row[:, last:last + 1] - b_row + li_row - m_new)
        upds.append(_dot(jnp.concatenate(
            [vt * ws, jnp.broadcast_to(ws, (SUBLANES, ws.shape[1]))], axis=0), k))
        m_news.append(m_new)
    results = []
    for u, cn, (_, den, wprev, mt), num, upd, m_new in zip(units, cns, mids, nums, upds, m_news):
        last, c_st, n_st = u[7], u[8], u[9]
        c_new = upd[0:D_HEAD, :]
        n_new = upd[D_HEAD:D_HEAD + 1, :]
        if c_st is not None:
            num = num + wprev * cn[0:D_HEAD, :]
            den = den + wprev * cn[D_HEAD:D_HEAD + 1, :]
            wc = wprev[:, last:last + 1]
            c_new = c_new + wc * c_st
            n_new = n_new + wc * n_st
        ht = num * (1.0 / jnp.maximum(jnp.abs(den), jnp.exp(-mt)))
        results.append((ht, c_new, n_new, m_new))
    return results


def _mlstm_units(zqk_ref, zvo_ref, g_ref, fb_ref, n, visits):
    r = lax.broadcasted_iota(jnp.int32, (n, n), 0)
    c = lax.broadcasted_iota(jnp.int32, (n, n), 1)
    gs = [g_ref[rows, :] for rows, _, _ in visits]
    lss = [_log_sigmoid(g + fb_ref[...]) for g in gs]
    bs = [_dot_split3(((c <= r) if direction == 0 else (c >= r)).astype(BF16), ls)
          for ls, (_, direction, _) in zip(lss, visits)]
    units = []
    for g, b, (rows, direction, state) in zip(gs, bs, visits):
        a = pltpu.roll(g, ML_HEADS, 1) - b
        bt = b.T
        gt = g.T
        if direction == 0:
            neg, last = jnp.where(r <= c, 0.0, NEG_INF), n - 1
        else:
            neg, last = jnp.where(r >= c, 0.0, NEG_INF), 0
        i_col = 2 * ML_HEADS * direction
        f_col = i_col + ML_HEADS
        for pair in range(ML_HEADS // 2):
            vt2 = zvo_ref[rows, pair * LANES:(pair + 1) * LANES].T
            for sub in range(2):
                hd = 2 * pair + sub
                feat = slice(sub * D_HEAD, (sub + 1) * D_HEAD)
                q = zqk_ref[rows, hd * D_HEAD:(hd + 1) * D_HEAD]
                k = zqk_ref[rows, ML_W + hd * D_HEAD:ML_W + (hd + 1) * D_HEAD]
                units.append((q, k, vt2[feat, :],
                              a[:, f_col + hd:f_col + hd + 1], bt[f_col + hd:f_col + hd + 1, :],
                              gt[i_col + hd:i_col + hd + 1, :], neg, last) + tuple(state[hd]))
    return units


def _mlstm_finish(ht, o_gate, norm_g):
    outs = []
    for hd in range(ML_HEADS):
        x = ht[hd * D_HEAD:(hd + 1) * D_HEAD, :]
        mu = jnp.mean(x, axis=0, keepdims=True)
        xc = x - mu
        var = jnp.mean(xc * xc, axis=0, keepdims=True)
        outs.append(xc * lax.rsqrt(var + LN_EPS))
    hn = jnp.concatenate(outs, axis=0).T
    return hn * norm_g * _sigmoid(o_gate)


def _mlstm_scan(zqk_ref, zvo_ref, g_ref, fb_ref, ng_ref, out_ref, n_tokens, state):
    n_chunks = n_tokens // ML_CHUNK
    step_chunks = [(i, n_chunks - 1 - i) for i in range(n_chunks)]
    no_state = [(None, None, None)] * ML_HEADS
    visits = [(slice(chunk * ML_CHUNK, (chunk + 1) * ML_CHUNK), direction, no_state)
              for chunks in step_chunks for direction, chunk in enumerate(chunks)]
    units = _mlstm_units(zqk_ref, zvo_ref, g_ref, fb_ref, ML_CHUNK, visits)
    kqs = _mlstm_scores(units)
    hts = {}
    for i, chunks in enumerate(step_chunks):
        step = slice(i * N_STATE, (i + 1) * N_STATE)
        results = _mlstm_run([u[:8] + tuple(st) for u, st in zip(units[step], state)], kqs[step])
        state = [(c_new, n_new, m_new) for _, c_new, n_new, m_new in results]
        for direction, chunk in enumerate(chunks):
            hts[direction, chunk] = jnp.concatenate(
                [results[direction * ML_HEADS + hd][0] for hd in range(ML_HEADS)], axis=0)
    for chunk in range(n_chunks):
        rows = slice(chunk * ML_CHUNK, (chunk + 1) * ML_CHUNK)
        out_ref[rows, :] = _mlstm_finish(hts[0, chunk] + hts[1, chunk],
                                         zvo_ref[rows, ML_W:2 * ML_W],
                                         ng_ref[...]).astype(out_ref.dtype)
    return state


def _store_state(c_ref, n_ref, m_ref, idx, c_new, n_new, m_new):
    c_ref[idx] = c_new
    n_ref[idx:idx + 1, :] = n_new
    m_ref[idx:idx + 1, :] = jnp.broadcast_to(m_new, (1, LANES))


ONES_ROWS = 16


def _with_ones(vt):
    ones = jnp.ones((ONES_ROWS, vt.shape[1]), BF16)
    return jnp.concatenate([vt.astype(BF16), ones], axis=0)


def _attend_many(problems):
    return _attend_values(problems, *_attend_scores(problems))


def _attend_scores(problems):
    scores = []
    for q, parts, _ in problems:
        sts = []
        for k, _, bias_t in parts:
            st = _dot_nt(k, q)
            sts.append(st if bias_t is None else st + bias_t)
        scores.append(sts)
    maxima = []
    for sts, (_, _, sink_row) in zip(scores, problems):
        m = None
        for st in sts:
            mx = jnp.max(st, axis=0, keepdims=True)
            m = mx if m is None else jnp.maximum(m, mx)
        maxima.append(m if sink_row is None else jnp.maximum(m, sink_row))
    return scores, maxima


def _attend_values(problems, scores, maxima):
    outs = []
    for sts, m, (_, parts, sink_row) in zip(scores, maxima, problems):
        ot = None
        for st, (_, vt_ones, _) in zip(sts, parts):
            part = jnp.dot(vt_ones, jnp.exp2(st - m).astype(BF16), preferred_element_type=F32)
            ot = part if ot is None else ot + part
        den = ot[D_HEAD:D_HEAD + 1, :]
        if sink_row is not None:
            den = den + jnp.exp2(sink_row - m)
        outs.append(ot[0:D_HEAD, :] * (1.0 / den))
    return outs


def _head_rows(pairs_t, hd):
    return pairs_t[hd // 2][(hd % 2) * D_HEAD:(hd % 2 + 1) * D_HEAD, :]


def _store_heads_t(o_ref, ots):
    for p in range(len(ots) // 2):
        o_ref[:, p * LANES:(p + 1) * LANES] = jnp.concatenate(
            [ots[2 * p], ots[2 * p + 1]], axis=0).astype(o_ref.dtype).T


N_STATE = 2 * ML_HEADS
N_CTX_STATE_OUTS = 7


def _ctx_mixers_kernel(x_ref, mod_ref, wt_ref, b_ref, fb_ref, ng_ref, sink_ref, *rest,
                       first_layer):
    n_scratch = 1 + len(Z_GROUPS)
    w_scr, zqk_ref, zvo_ref, zswa_ref, znat_ref, zg_ref = rest[-n_scratch:]
    mml_ref, mswa_ref, mnat_ref, *state_refs = rest[-(3 + N_CTX_STATE_OUTS) - n_scratch:-n_scratch]
    _inproj_kernel(x_ref, mod_ref, wt_ref, b_ref, zqk_ref, zvo_ref, zswa_ref, znat_ref, zg_ref,
                   w_scr)
    if first_layer:
        for ref in state_refs:
            ref[1:] = jnp.zeros((DEPTH - 1,) + ref.shape[1:], F32)
        state_refs = [ref.at[0] for ref in state_refs]
    sk_ref, sv_ref, nk_ref, nv_ref, c_ref, n_ref, m_ref = state_refs
    rows = slice(0, SEQ)
    scaled_q = lambda ref, hd: ref[:, hd * D_HEAD:(hd + 1) * D_HEAD].astype(BF16)
    swa_vts = [zswa_ref[:, SWA_W + SWA_KV_W:SWA_W + 2 * SWA_KV_W].astype(BF16).T]
    nat_vts = [znat_ref[:, 2 * NAT_W + p * LANES:2 * NAT_W + (p + 1) * LANES].astype(BF16).T
               for p in range(NAT_W // LANES)]
    problems = []
    for kv in range(SWA_KV_HEADS):
        k = zswa_ref[:, SWA_W + kv * D_HEAD:SWA_W + (kv + 1) * D_HEAD]
        sk_ref[kv] = k
        sv_ref[kv] = zswa_ref[:, SWA_W + SWA_KV_W + kv * D_HEAD:SWA_W + SWA_KV_W + (kv + 1) * D_HEAD]
        q = jnp.concatenate([scaled_q(zswa_ref, kv * SWA_GROUP + g) for g in range(SWA_GROUP)],
                            axis=0)
        problems.append((q, [(k.astype(BF16), _with_ones(_head_rows(swa_vts, kv)), None)],
                         sink_ref[kv:kv + 1, :]))
    for hd in range(NAT_HEADS):
        k = znat_ref[:, NAT_W + hd * D_HEAD:NAT_W + (hd + 1) * D_HEAD]
        nk_ref[hd] = k
        nv_ref[hd] = znat_ref[:, 2 * NAT_W + hd * D_HEAD:2 * NAT_W + (hd + 1) * D_HEAD]
        problems.append((scaled_q(znat_ref, hd),
                         [(k.astype(BF16), _with_ones(_head_rows(nat_vts, hd)), None)], None))
    scores, maxima = _attend_scores(problems)

    zero_state = [(None, jnp.zeros((1, D_HEAD), F32), jnp.zeros((1, 1), F32))] * N_STATE
    final = _mlstm_scan(zqk_ref, zvo_ref, zg_ref, fb_ref, ng_ref, mml_ref, SEQ, zero_state)
    for idx, (c_new, n_new, m_new) in enumerate(final):
        _store_state(c_ref, n_ref, m_ref, idx, c_new, n_new, m_new)

    ots = _attend_values(problems, scores, maxima)
    _store_heads_t(mswa_ref, [ot[:, g * SEQ:(g + 1) * SEQ]
                              for ot in ots[:SWA_KV_HEADS] for g in range(SWA_GROUP)])
    _store_heads_t(mnat_ref, ots[SWA_KV_HEADS:])


CTX_STATE_DIMS = ((SWA_KV_HEADS, SEQ, D_HEAD), (SWA_KV_HEADS, SEQ, D_HEAD),
                  (NAT_HEADS, SEQ, D_HEAD), (NAT_HEADS, SEQ, D_HEAD),
                  (N_STATE, D_HEAD, D_HEAD), (N_STATE, D_HEAD), (N_STATE, LANES))


def _ctx_mixers_call(x, mods, w_in_t, b_in, fb_row, norm_g, sink_rows, layer, prev_state):
    rows = x.shape[0]
    row_spec = lambda w: pl.BlockSpec((SEQ, w), lambda b: (b, 0))
    state_dims = CTX_STATE_DIMS
    first_layer = prev_state is None
    if first_layer:
        assert layer == 0
        state_spec = lambda d: pl.BlockSpec((None, DEPTH) + d, lambda b: (b,) + (0,) * (len(d) + 1))
    else:
        state_spec = lambda d: pl.BlockSpec((None, None) + d, lambda b: (b, layer) + (0,) * len(d))
    prev = () if first_layer else tuple(prev_state)
    n_in = 7
    return pl.pallas_call(
        functools.partial(_ctx_mixers_kernel, first_layer=first_layer),
        grid=(BATCH,),
        in_specs=[row_spec(D_MODEL), _mod_spec(layer, lambda b: CTX_MOD_ROW),
                  _layer_spec(w_in_t, layer, single_buffer=True), _layer_spec(b_in, layer),
                  _layer_spec(fb_row, layer), _layer_spec(norm_g, layer),
                  _layer_spec(sink_rows, layer)]
                 + [pl.BlockSpec(memory_space=pl.ANY)] * len(prev),
        out_specs=[row_spec(ML_W), row_spec(SWA_W), row_spec(NAT_W)]
                  + [state_spec(d) for d in state_dims],
        out_shape=[jax.ShapeDtypeStruct((rows, ML_W), BF16),
                   jax.ShapeDtypeStruct((rows, SWA_W), BF16),
                   jax.ShapeDtypeStruct((rows, NAT_W), BF16)]
                  + [jax.ShapeDtypeStruct((BATCH, DEPTH) + d, F32) for d in state_dims],
        input_output_aliases={n_in + i: 3 + i for i in range(len(prev))},
        scratch_shapes=[pltpu.VMEM((Z_W, D_MODEL), BF16)]
                       + [pltpu.VMEM((SEQ, w), dt) for w, dt in zip(Z_GROUPS, CTX_Z_DTYPES)],
        compiler_params=_params(1),
        name="ctx_mixers",
    )(x, mods, w_in_t, b_in, fb_row, norm_g, sink_rows, *prev)


def _lat_mlstm_kernel(zqk_ref, zvo_ref, zg_ref, fb_ref, ng_ref, c0_ref, n0_ref, m0_ref, mml_ref):
    state = [(c0_ref[idx], n0_ref[idx:idx + 1, :], m0_ref[idx:idx + 1, :])
             for idx in range(N_STATE)]
    _mlstm_scan(zqk_ref, zvo_ref, zg_ref, fb_ref, ng_ref, mml_ref, DEC_SEQ, state)


def _lat_mlstm_call(z_qk, z_vo, z_g, fb_row, norm_g, state_c, state_n, state_m, layer):
    n_st = 2 * ML_HEADS
    n_chunks = DEC_SEQ // ML_CHUNK
    row_spec = lambda w: pl.BlockSpec((DEC_SEQ, w), lambda b: (b, 0))
    return pl.pallas_call(
        _lat_mlstm_kernel,
        grid=(DEC_BATCH,),
        in_specs=[row_spec(Z_QK_W), row_spec(Z_VO_W), row_spec(Z_G_W),
                  _layer_spec(fb_row, layer), _layer_spec(norm_g, layer),
                  pl.BlockSpec((None, None, n_st, D_HEAD, D_HEAD), lambda b: (b, layer, 0, 0, 0)),
                  pl.BlockSpec((None, None, n_st, D_HEAD), lambda b: (b, layer, 0, 0)),
                  pl.BlockSpec((None, None, n_st, 1), lambda b: (b, layer, 0, 0))],
        out_specs=row_spec(ML_W),
        out_shape=jax.ShapeDtypeStruct((DEC_BATCH * DEC_SEQ, ML_W), BF16),
        compiler_params=_params(1),
        name="lat_mlstm",
    )(z_qk, z_vo, z_g, fb_row, norm_g, state_c, state_n, state_m)


def _lat_swa_kernel(zswa_ref, ckt_ref, cvt_ref, sink_ref, mswa_ref,
                    vt_scr, ot_scr, bias_scr, ck_scr, cvt_scr):
    vt_scr[...] = zswa_ref[:, SWA_W + SWA_KV_W:SWA_W + 2 * SWA_KV_W].T

    bl = SWA_BLOCK
    nb = DEC_SEQ // bl
    n_q = SWA_GROUP * bl
    kj = lax.broadcasted_iota(jnp.int32, (3 * bl, n_q), 0)
    qi = lax.broadcasted_iota(jnp.int32, (3 * bl, n_q), 1) & (bl - 1)
    keep = ((kj >= qi) & (kj < 2 * bl)) | ((kj >= 2 * bl) & (kj - 2 * bl <= qi))
    bias_scr[...] = jnp.where(keep, 0.0, NEG_INF)
    ck_scr[...] = jnp.concatenate([ckt_ref[kv] for kv in range(SWA_KV_HEADS)],
                                  axis=0).astype(BF16).T
    for kv in range(SWA_KV_HEADS):
        cvt_scr[kv] = _with_ones(cvt_ref[kv])

    problems, where = [], []
    for kv in range(SWA_KV_HEADS):
        head = slice(kv * D_HEAD, (kv + 1) * D_HEAD)
        for blk in range(nb):
            lo_b, hi_b = max(blk - 1, 0), min(blk + 2, nb)
            keys = slice(lo_b * bl, hi_b * bl)
            q = jnp.concatenate(
                [zswa_ref[blk * bl:(blk + 1) * bl,
                          (kv * SWA_GROUP + g) * D_HEAD:(kv * SWA_GROUP + g + 1) * D_HEAD]
                 for g in range(SWA_GROUP)], axis=0)
            parts = [(zswa_ref[keys, SWA_W + kv * D_HEAD:SWA_W + (kv + 1) * D_HEAD],
                      _with_ones(vt_scr[head, keys]),
                      bias_scr[(lo_b - blk + 1) * bl:(hi_b - blk + 1) * bl, :]),
                     (ck_scr[:, head], cvt_scr[kv], None)]
            problems.append((q, parts, sink_ref[kv:kv + 1, :]))
            where.append((kv, blk))
    for (kv, blk), ot in zip(where, _attend_many(problems)):
        o = ot.astype(BF16)
        for g in range(SWA_GROUP):
            hq = kv * SWA_GROUP + g
            ot_scr[hq * D_HEAD:(hq + 1) * D_HEAD, blk * bl:(blk + 1) * bl] = (
                o[:, g * bl:(g + 1) * bl])
    for p in range(SWA_W // LANES):
        mswa_ref[:, p * LANES:(p + 1) * LANES] = ot_scr[p * LANES:(p + 1) * LANES, :].T


def _lat_swa_call(z_swa, cache_k, cache_v, sink_rows, layer):
    row_spec = lambda w: pl.BlockSpec((DEC_SEQ, w), lambda b: (b, 0))
    cache_spec = pl.BlockSpec((None, None, SWA_KV_HEADS, D_HEAD, PAST_LEN),
                              lambda b: (b, layer, 0, 0, 0))
    n_q = SWA_GROUP * SWA_BLOCK
    return pl.pallas_call(
        _lat_swa_kernel,
        grid=(DEC_BATCH,),
        in_specs=[row_spec(Z_SWA_W), cache_spec, cache_spec, _layer_spec(sink_rows, layer)],
        out_specs=row_spec(SWA_W),
        out_shape=jax.ShapeDtypeStruct((DEC_BATCH * DEC_SEQ, SWA_W), BF16),
        scratch_shapes=[pltpu.VMEM((SWA_KV_W, DEC_SEQ), BF16), pltpu.VMEM((SWA_W, DEC_SEQ), BF16),
                        pltpu.VMEM((3 * SWA_BLOCK, n_q), F32),
                        pltpu.VMEM((PAST_LEN, SWA_KV_W), BF16),
                        pltpu.VMEM((SWA_KV_HEADS, D_HEAD + ONES_ROWS, PAST_LEN), BF16)],
        compiler_params=_params(1),
        name="lat_swa",
    )(z_swa, cache_k, cache_v, sink_rows)


NAT_ROWS = DEC_SEQ // GRID_W
NAT_GROUP_ROWS = 4
NAT_GROUP_Q = NAT_GROUP_ROWS * GRID_W
NAT_RPB_ROWS = 2 * NAT_KH


def _nat_row_start(r):
    return min(max(r - NAT_KH // 2, 0), NAT_ROWS - NAT_KH)


def _nat_groups():
    groups, off = [], 0
    for g in range(NAT_ROWS // NAT_GROUP_ROWS):
        w0 = _nat_row_start(g * NAT_GROUP_ROWS)
        w1 = _nat_row_start((g + 1) * NAT_GROUP_ROWS - 1) + NAT_KH
        n_rows = w1 - w0 + (w1 - w0) % 2
        w0 = min(w0, NAT_ROWS - n_rows)
        groups.append((w0, n_rows, off))
        off += n_rows * GRID_W
    return groups, off


NAT_GROUPS, NAT_BIAS_KEYS = _nat_groups()


def _nat_build_bias(rpb_ref, bias_scr):
    shape = (GRID_W, LANES)
    q = lax.broadcasted_iota(jnp.int32, shape, 0)
    lane = lax.broadcasted_iota(jnp.int32, shape, 1)
    kc = lane & (GRID_W - 1)
    cs = jnp.clip(q - NAT_KW // 2, 0, GRID_W - NAT_KW)
    ok = (kc >= cs) & (kc < cs + NAT_KW)
    ok_lo = ok & (lane < GRID_W)
    ok_hi = ok & (lane >= GRID_W)
    neg_tile = jnp.full(shape, NEG_INF, F32)
    for hd in range(2):
        lo, hi = [], []
        for ro in range(2 * NAT_KH - 1):
            x = jnp.broadcast_to(rpb_ref[hd, ro:ro + 1, :], shape)
            lo.append(jnp.where(ok_lo, pltpu.roll(x, 0, 1, stride=1, stride_axis=0), NEG_INF))
            hi.append(jnp.where(ok_hi, pltpu.roll(x, GRID_W, 1, stride=1, stride_axis=0), NEG_INF))
        for g, (w0, n_rows, off) in enumerate(NAT_GROUPS):
            for jj in range(n_rows // 2):
                kra, krb = w0 + 2 * jj, w0 + 2 * jj + 1
                tiles = []
                for rr in range(NAT_GROUP_ROWS):
                    r = g * NAT_GROUP_ROWS + rr
                    r0 = _nat_row_start(r)
                    parts = []
                    if r0 <= kra < r0 + NAT_KH:
                        parts.append(lo[kra - r + NAT_KH - 1])
                    if r0 <= krb < r0 + NAT_KH:
                        parts.append(hi[krb - r + NAT_KH - 1])
                    tiles.append(neg_tile if not parts else (
                        parts[0] if len(parts) == 1 else jnp.maximum(parts[0], parts[1])))
                bias_scr[hd, off + jj * LANES:off + (jj + 1) * LANES, :] = (
                    jnp.concatenate(tiles, axis=0).T)


def _lat_nat_kernel(q_ref, k_ref, v_ref, ckt_ref, cvt_ref, rpb_ref, mnat_ref, bias_scr):
    @pl.when(pl.program_id(1) == 0)
    def _():
        _nat_build_bias(rpb_ref, bias_scr)

    vt = v_ref[...].astype(BF16).T
    ck2 = jnp.concatenate([ckt_ref[0], ckt_ref[1]], axis=0).astype(BF16).T
    problems = []
    for hd in range(2):
        head = slice(hd * D_HEAD, (hd + 1) * D_HEAD)
        ck = ck2[:, head]
        cvt_ones = _with_ones(cvt_ref[hd])
        for g, (w0, n_rows, off) in enumerate(NAT_GROUPS):
            keys = slice(w0 * GRID_W, (w0 + n_rows) * GRID_W)
            parts = [(k_ref[keys, head].astype(BF16), _with_ones(vt[head, keys]),
                      bias_scr[hd, off:off + n_rows * GRID_W, :]),
                     (ck, cvt_ones, None)]
            q = q_ref[g * NAT_GROUP_Q:(g + 1) * NAT_GROUP_Q, head].astype(BF16)
            problems.append((q, parts, None))
    ots = _attend_many(problems)
    n_groups = len(NAT_GROUPS)
    mnat_ref[...] = jnp.concatenate(
        [jnp.concatenate(ots[hd * n_groups:(hd + 1) * n_groups], axis=1) for hd in range(2)],
        axis=0).astype(BF16).T


def _lat_nat_call(z_nat, cache_k, cache_v, rpb_lanes, layer):
    n_pair = NAT_HEADS // 2
    col_spec = lambda off: pl.BlockSpec((DEC_SEQ, LANES), lambda p, b: (b, off + p))
    cache_spec = pl.BlockSpec((None, None, 2, D_HEAD, PAST_LEN), lambda p, b: (b, layer, p, 0, 0))
    return pl.pallas_call(
        _lat_nat_kernel,
        grid=(n_pair, DEC_BATCH),
        in_specs=[col_spec(0), col_spec(n_pair), col_spec(2 * n_pair), cache_spec, cache_spec,
                  pl.BlockSpec((None, 2, NAT_RPB_ROWS, LANES), lambda p, b: (layer, p, 0, 0))],
        out_specs=pl.BlockSpec((DEC_SEQ, LANES), lambda p, b: (b, p)),
        out_shape=jax.ShapeDtypeStruct((DEC_BATCH * DEC_SEQ, NAT_W), BF16),
        scratch_shapes=[pltpu.VMEM((2, NAT_BIAS_KEYS, NAT_GROUP_Q), F32)],
        compiler_params=_params(2),
        name="lat_nat",
    )(z_nat, z_nat, z_nat, cache_k, cache_v, rpb_lanes)


def _nat_rpb_lanes(rpb):
    n_off = 2 * NAT_KW - 1
    padded = jnp.concatenate(
        [rpb.astype(F32), jnp.zeros(rpb.shape[:-1] + (LANES - n_off,), F32)], axis=-1)
    rolled = jnp.roll(padded, -(NAT_KW - 1), axis=-1)
    pad_rows = jnp.zeros(rpb.shape[:-2] + (NAT_RPB_ROWS - rpb.shape[-2], LANES), F32)
    return jnp.concatenate([rolled, pad_rows], axis=-2)


def _rope_tables():
    t = np.arange(DEC_SEQ)[:, None]
    d = np.arange(LANES)[None, :] % D_HEAD
    pos = np.where(d < D_HEAD // 2, t // GRID_W, t % GRID_W).astype(np.float64)
    quarter = D_HEAD // 4
    freq = ROPE_BASE ** (-(d % quarter).astype(np.float64) / quarter)
    ang = (pos.astype(np.float32) * freq.astype(np.float32)).astype(np.float32)
    sign = np.where((d % (2 * quarter)) < quarter, -1.0, 1.0)
    return (jnp.asarray(np.cos(ang), dtype=F32), jnp.asarray(np.sin(ang) * sign, dtype=F32))


def _permute_in_columns(a):
    g0 = Z_ML_W
    s0 = g0 + N_GATES
    pad = jnp.zeros(a.shape[:-1] + (Z_G_W - N_GATES,), a.dtype)
    return jnp.concatenate([a[..., :g0], a[..., s0:], a[..., g0:s0], pad], axis=-1)


def kernel(x_prompt, x_sample, cache_swa_k, cache_swa_v, cache_nat_k, cache_nat_v, state_mlstm_C,
           state_mlstm_n, state_mlstm_m, c, c_ctx, w_ada, b_ada, w_in, b_in, mlstm_fbias,
           mlstm_norm_g, swa_sink, nat_rpb, w_out, ln1_g, ln1_b, w_mlp1, w_mlp2, ln2_g, ln2_b):
    xp = x_prompt.reshape(BATCH * SEQ, D_MODEL)
    xs = x_sample.reshape(DEC_BATCH * DEC_SEQ, D_MODEL)
    cvec = jnp.concatenate(
        [c, c_ctx[None, :], jnp.zeros((MOD_ROWS - DEC_BATCH - 1, D_MODEL), F32)], axis=0)
    mods = _mods_call(cvec, w_ada, b_ada).reshape(DEPTH * MOD_ROWS, 6, D_MODEL)

    w_in_t = jnp.swapaxes(w_in, 1, 2)
    out_scale = np.ones((Z_W,), np.float32)
    out_scale[ML_W:2 * ML_W] = ATT_SCALE
    out_scale[Z_ML_W:Z_ML_W + SWA_W] = Q_SCALE2
    out_scale[Z_ML_W + Z_SWA_W:Z_ML_W + Z_SWA_W + NAT_W] = Q_SCALE2
    b_in_p = jnp.stack([jnp.broadcast_to(out_scale, (DEPTH, Z_W)),
                        _permute_in_columns(b_in) * out_scale], axis=1)
    cache_swa_k, cache_swa_v, cache_nat_k, cache_nat_v = (
        jnp.swapaxes(a, -1, -2) for a in (cache_swa_k, cache_swa_v, cache_nat_k, cache_nat_v))
    vec = lambda a: a.reshape(DEPTH, 1, D_MODEL)
    tail_w = (w_out.astype(BF16), vec(ln1_g), vec(ln1_b), w_mlp1.astype(BF16),
              w_mlp2.astype(BF16), vec(ln2_g), vec(ln2_b))
    cos_t, sin_t = _rope_tables()
    rpb_lanes = _nat_rpb_lanes(nat_rpb) * LOG2E
    swa_sink = swa_sink * LOG2E
    fb_rows = jnp.zeros((DEPTH, 1, Z_G_W), F32)
    fb_rows = fb_rows.at[:, 0, ML_HEADS:2 * ML_HEADS].set(mlstm_fbias[:, 0])
    fb_rows = fb_rows.at[:, 0, 3 * ML_HEADS:4 * ML_HEADS].set(mlstm_fbias[:, 1])
    norm_g = mlstm_norm_g.reshape(DEPTH, 1, ML_W)
    sink_ctx = jnp.repeat(swa_sink, SEQ, axis=1).reshape(DEPTH, SWA_KV_HEADS, SWA_GROUP * SEQ)
    sink_lat = jnp.repeat(swa_sink, SWA_BLOCK, axis=1).reshape(
        DEPTH, SWA_KV_HEADS, SWA_GROUP * SWA_BLOCK)

    n_st = 2 * ML_HEADS
    state_c = state_mlstm_C.reshape(DEC_BATCH, DEPTH, n_st, D_HEAD, D_HEAD)
    state_n = state_mlstm_n.reshape(DEC_BATCH, DEPTH, n_st, D_HEAD)
    state_m = state_mlstm_m.reshape(DEC_BATCH, DEPTH, n_st, 1)

    ctx_row = lambda tile: (lambda i: CTX_MOD_ROW)
    lat_row = lambda tile: (lambda i: i // (DEC_SEQ // tile))

    ctx_state = None
    for l in range(DEPTH):
        m_ml, m_swa, m_nat, *ctx_state = _ctx_mixers_call(
            xp, mods, w_in_t, b_in_p, fb_rows, norm_g, sink_ctx, l, ctx_state)
        xp = _tail_call(xp, m_ml, m_swa, m_nat, mods, ctx_row(TAIL_TILE), tail_w, l)

        z_qk, z_vo, z_swa, z_nat, z_g = _lat_inproj_call(xs, mods, w_in_t, b_in_p, cos_t, sin_t, l)
        m_ml = _lat_mlstm_call(z_qk, z_vo, z_g, fb_rows, norm_g, state_c, state_n, state_m, l)
        m_swa = _lat_swa_call(z_swa, cache_swa_k, cache_swa_v, sink_lat, l)
        m_nat = _lat_nat_call(z_nat, cache_nat_k, cache_nat_v, rpb_lanes, l)
        xs = _tail_call(xs, m_ml, m_swa, m_nat, mods, lat_row(TAIL_TILE), tail_w, l)

    k_swa, v_swa, k_nat, v_nat, st_c, st_n, st_m = ctx_state
    return (xp.reshape(BATCH, SEQ, D_MODEL), xs.reshape(DEC_BATCH, DEC_SEQ, D_MODEL),
            k_swa, v_swa, k_nat, v_nat,
            st_c.reshape(BATCH, DEPTH, 2, ML_HEADS, D_HEAD, D_HEAD),
            st_n.reshape(BATCH, DEPTH, 2, ML_HEADS, D_HEAD),
            st_m[..., 0].reshape(BATCH, DEPTH, 2, ML_HEADS))
```

```python
import functools

import numpy as np
import jax
import jax.numpy as jnp
from jax import lax
from jax.experimental import pallas as pl
from jax.experimental.pallas import tpu as pltpu

D_MODEL = 1024
BATCH = 16
SEQ = 256
DEPTH = 4
DEC_BATCH = 8
DEC_SEQ = 1024
PAST_LEN = 512
GRID_W = 64
D_HEAD = 64
ML_HEADS = 4
SWA_HEADS = 6
SWA_KV_HEADS = 2
SWA_GROUP = SWA_HEADS // SWA_KV_HEADS
NAT_HEADS = 6
ML_W = ML_HEADS * D_HEAD
SWA_W = SWA_HEADS * D_HEAD
SWA_KV_W = SWA_KV_HEADS * D_HEAD
NAT_W = NAT_HEADS * D_HEAD
N_GATES = 4 * ML_HEADS
IN_DIM = 4 * ML_W + N_GATES + SWA_W + 2 * SWA_KV_W + 3 * NAT_W
D_FF = 4 * D_MODEL
SWA_WINDOW = 128
SWA_BLOCK = 128
NAT_KH = 8
NAT_KW = 16
ROPE_BASE = 10000.0
LN_EPS = 1e-5
DN_ALPHA = (2 * DEPTH) ** 0.25
ATT_SCALE = D_HEAD ** -0.5
LOG2E = 1.4426950408889634
Q_SCALE2 = ATT_SCALE * LOG2E

LANES = 128
SUBLANES = 8
Z_ML_W = 4 * ML_W
Z_SWA_W = SWA_W + 2 * SWA_KV_W
Z_NAT_W = 3 * NAT_W
Z_G_W = LANES
Z_W = Z_ML_W + Z_SWA_W + Z_NAT_W + Z_G_W
Z_QK_W = 2 * ML_W
Z_VO_W = 2 * ML_W
Z_GROUPS = (Z_QK_W, Z_VO_W, Z_SWA_W, Z_NAT_W, Z_G_W)
Z_SWA_GROUP = 2
MOD_ROWS = 16
CTX_MOD_ROW = DEC_BATCH
ROW_TILE = 1024
TAIL_TILE = 1024
TAIL_PARTS = 2
ML_CHUNK = 128
VMEM_LIMIT = 58 * 1024 * 1024

BF16 = jnp.bfloat16
F32 = jnp.float32
NEG_INF = float("-inf")
LAT_Z_DTYPES = (BF16, F32, BF16, BF16, F32)
CTX_Z_DTYPES = (BF16, F32, F32, F32, F32)


def _dot(a, b):
    return jnp.dot(a.astype(BF16), b.astype(BF16), preferred_element_type=F32)


def _dot_nt(a, b):
    return lax.dot_general(a.astype(BF16), b.astype(BF16), (((1,), (1,)), ((), ())),
                           preferred_element_type=F32)


def _dot_split3(tri, x):
    hi = x.astype(BF16)
    r1 = x - hi.astype(F32)
    mid = r1.astype(BF16)
    lo = (r1 - mid.astype(F32)).astype(BF16)
    return (jnp.dot(tri, hi, preferred_element_type=F32)
            + jnp.dot(tri, mid, preferred_element_type=F32)
            + jnp.dot(tri, lo, preferred_element_type=F32))


def _sigmoid(x):
    return 1.0 / (1.0 + jnp.exp(-x))


def _log_sigmoid(x):
    return jnp.minimum(x, 0.0) - jnp.log1p(jnp.exp(-jnp.abs(x)))


def _layer_norm(x, g, b):
    mu = jnp.mean(x, axis=-1, keepdims=True)
    xc = x - mu
    var = jnp.mean(xc * xc, axis=-1, keepdims=True)
    return xc * lax.rsqrt(var + LN_EPS) * g + b


def _params(n_axes=1):
    return pltpu.CompilerParams(dimension_semantics=("arbitrary",) * n_axes,
                                vmem_limit_bytes=VMEM_LIMIT)


def _layer_spec(a, layer, single_buffer=False):
    kwargs = {"pipeline_mode": pl.Buffered(1)} if single_buffer else {}
    return pl.BlockSpec((None,) + a.shape[1:], lambda *_: (layer,) + (0,) * (a.ndim - 1), **kwargs)


def _mods_kernel(c_ref, w_ref, b_ref, o_ref):
    c = c_ref[...]
    o_ref[...] = _dot(c * _sigmoid(c), w_ref[...]) + b_ref[...]


def _mods_call(cvec, w_ada, b_ada):
    n_col = 6
    return pl.pallas_call(
        _mods_kernel,
        grid=(DEPTH, n_col),
        in_specs=[
            pl.BlockSpec((MOD_ROWS, D_MODEL), lambda l, j: (0, 0)),
            pl.BlockSpec((None, D_MODEL, D_MODEL), lambda l, j: (l, 0, j)),
            pl.BlockSpec((None, 1, D_MODEL), lambda l, j: (l, 0, j)),
        ],
        out_specs=pl.BlockSpec((None, MOD_ROWS, D_MODEL), lambda l, j: (l, 0, j)),
        out_shape=jax.ShapeDtypeStruct((DEPTH, MOD_ROWS, 6 * D_MODEL), F32),
        compiler_params=_params(2),
        name="mods",
    )(cvec, w_ada, b_ada.reshape(DEPTH, 1, 6 * D_MODEL))


def _rope(x, cos, sin_signed):
    lane = lax.broadcasted_iota(jnp.int32, x.shape, 1)
    first = (lane & 31) < 16
    partner = jnp.where(first, pltpu.roll(x, LANES - 16, 1), pltpu.roll(x, 16, 1))
    return x * cos + partner * sin_signed


def _inproj_kernel(x_ref, mod_ref, wt_ref, b_ref, *rest, rope=False):
    if rope:
        cos_ref, sin_ref, *rest = rest
    *out_refs, w_scr = rest

    @pl.when(pl.program_id(0) == 0)
    def _():
        rest = IN_DIM - Z_ML_W - N_GATES
        w_scr[0:Z_ML_W, :] = wt_ref[0:Z_ML_W, :].astype(BF16)
        w_scr[Z_ML_W:Z_ML_W + rest, :] = wt_ref[Z_ML_W + N_GATES:IN_DIM, :].astype(BF16)
        w_scr[Z_ML_W + rest:IN_DIM, :] = wt_ref[Z_ML_W:Z_ML_W + N_GATES, :].astype(BF16)
        w_scr[IN_DIM:Z_W, :] = jnp.zeros((Z_W - IN_DIM, D_MODEL), BF16)

    h = (x_ref[...] * (1.0 + mod_ref[1:2, :]) + mod_ref[0:1, :]).astype(BF16)
    offsets = [sum(Z_GROUPS[:i]) for i in range(len(Z_GROUPS))]
    accs = [_dot_nt(h, w_scr[o:o + w, :]) for o, w in zip(offsets, Z_GROUPS)]
    for ref, w, o, acc in zip(out_refs, Z_GROUPS, offsets, accs):
        z = acc * b_ref[0:1, o:o + w] + b_ref[1:2, o:o + w]
        if rope and ref is out_refs[Z_SWA_GROUP]:
            z = jnp.concatenate(
                [_rope(z[:, j * LANES:(j + 1) * LANES], cos_ref[...], sin_ref[...])
                 for j in range((SWA_W + SWA_KV_W) // LANES)] + [z[:, SWA_W + SWA_KV_W:]], axis=1)
        ref[...] = z.astype(ref.dtype)


def _mod_spec(layer, mod_row_of_tile):
    return pl.BlockSpec((None, 6, D_MODEL),
                        lambda i: (layer * MOD_ROWS + mod_row_of_tile(i), 0, 0))


def _lat_inproj_call(x, mods, w_in_t, b_in, cos_t, sin_t, layer):
    rows = x.shape[0]
    tiles_per_seq = DEC_SEQ // ROW_TILE
    table_spec = pl.BlockSpec((ROW_TILE, LANES), lambda i: (i % tiles_per_seq, 0))
    return pl.pallas_call(
        functools.partial(_inproj_kernel, rope=True),
        grid=(rows // ROW_TILE,),
        in_specs=[
            pl.BlockSpec((ROW_TILE, D_MODEL), lambda i: (i, 0)),
            _mod_spec(layer, lambda i: i // tiles_per_seq),
            _layer_spec(w_in_t, layer, single_buffer=True),
            _layer_spec(b_in, layer),
            table_spec, table_spec,
        ],
        out_specs=[pl.BlockSpec((ROW_TILE, w), lambda i: (i, 0)) for w in Z_GROUPS],
        out_shape=[jax.ShapeDtypeStruct((rows, w), dt) for w, dt in zip(Z_GROUPS, LAT_Z_DTYPES)],
        scratch_shapes=[pltpu.VMEM((Z_W, D_MODEL), BF16)],
        compiler_params=_params(1),
        name="inproj",
    )(x, mods, w_in_t, b_in, cos_t, sin_t)


def _tail_kernel(x_ref, mml_ref, mswa_ref, mnat_ref, mod_ref, wout_ref, ln1g_ref, ln1b_ref,
                 w1_ref, w2_ref, ln2g_ref, ln2b_ref, o_ref):
    g_a, sh_m, sc_m, g_m = (mod_ref[i:i + 1, :] for i in (2, 3, 4, 5))
    part = x_ref.shape[0] // TAIL_PARTS
    halves = tuple(slice(i * part, (i + 1) * part) for i in range(TAIL_PARTS))
    projs = [jnp.dot(jnp.concatenate([mml_ref[r, :], mswa_ref[r, :], mnat_ref[r, :]], axis=1),
                     wout_ref[...], preferred_element_type=F32) for r in halves]
    ys, fs = [], []
    for r, proj in zip(halves, projs):
        y = _layer_norm(DN_ALPHA * x_ref[r, :] + g_a * proj, ln1g_ref[...], ln1b_ref[...])
        h = (y * (1.0 + sc_m) + sh_m).astype(BF16)
        ys.append(y)
        fs.append(jnp.dot(h, w1_ref[...], preferred_element_type=F32))
    mlps = []
    for f in fs:
        f = jnp.maximum(f, 0.0)
        mlps.append(jnp.dot((f * f).astype(BF16), w2_ref[...], preferred_element_type=F32))
    for r, y, mlp in zip(halves, ys, mlps):
        o_ref[r, :] = _layer_norm(DN_ALPHA * y + g_m * mlp, ln2g_ref[...], ln2b_ref[...])


def _tail_call(x, mix_ml, mix_swa, mix_nat, mods, mod_row_of_tile, consts, layer):
    rows = x.shape[0]
    row_spec = lambda w: pl.BlockSpec((TAIL_TILE, w), lambda i: (i, 0))
    return pl.pallas_call(
        _tail_kernel,
        grid=(rows // TAIL_TILE,),
        in_specs=[row_spec(D_MODEL), row_spec(ML_W), row_spec(SWA_W), row_spec(NAT_W),
                  _mod_spec(layer, mod_row_of_tile)]
                 + [_layer_spec(a, layer, single_buffer=True) for a in consts],
        out_specs=row_spec(D_MODEL),
        out_shape=jax.ShapeDtypeStruct((rows, D_MODEL), F32),
        compiler_params=_params(1),
        name="tail",
    )(x, mix_ml, mix_swa, mix_nat, mods, *consts)


def _mlstm_scores(units):
    return [_dot_nt(u[1], u[0]) for u in units]


def _mlstm_run(units, kqs=None):
    if kqs is None:
        kqs = _mlstm_scores(units)
    cns = []
    for q, _, _, _, _, _, _, _, c_st, n_st, _ in units:
        cns.append(None if c_st is None else _dot_nt(
            jnp.concatenate([c_st, jnp.broadcast_to(n_st, (SUBLANES, D_HEAD))], axis=0), q))
    mids, nums = [], []
    for kq, (_, _, vt, a_col, b_row, _, neg, _, _, _, m_st) in zip(kqs, units):
        a_m = a_col + neg
        c_row = jnp.maximum(m_st, jnp.max(a_m, axis=0, keepdims=True))
        st = kq * jnp.exp(a_m - c_row)
        mids.append((st, jnp.sum(st, axis=0, keepdims=True), jnp.exp(m_st - c_row), b_row + c_row))
        nums.append(_dot(vt, st))
    upds, m_news = [], []
    for (_, k, vt, _, b_row, li_row, _, last, _, _, _), (_, _, _, mt) in zip(units, mids):
        m_new = mt[:, last:last + 1]
        ws = jnp.exp(b_row[:, last:last + 1] - b_row + li_row - m_new)
        upds.append(_dot(jnp.concatenate(
            [vt * ws, jnp.broadcast_to(ws, (SUBLANES, ws.shape[1]))], axis=0), k))
        m_news.append(m_new)
    results = []
    for u, cn, (_, den, wprev, mt), num, upd, m_new in zip(units, cns, mids, nums, upds, m_news):
        last, c_st, n_st = u[7], u[8], u[9]
        c_new = upd[0:D_HEAD, :]
        n_new = upd[D_HEAD:D_HEAD + 1, :]
        if c_st is not None:
            num = num + wprev * cn[0:D_HEAD, :]
            den = den + wprev * cn[D_HEAD:D_HEAD + 1, :]
            wc = wprev[:, last:last + 1]
            c_new = c_new + wc * c_st
            n_new = n_new + wc * n_st
        ht = num * (1.0 / jnp.maximum(jnp.abs(den), jnp.exp(-mt)))
        results.append((ht, c_new, n_new, m_new))
    return results


def _mlstm_units(zqk_ref, zvo_ref, g_ref, fb_ref, n, visits):
    r = lax.broadcasted_iota(jnp.int32, (n, n), 0)
    c = lax.broadcasted_iota(jnp.int32, (n, n), 1)
    gs = [g_ref[rows, :] for rows, _, _ in visits]
    lss = [_log_sigmoid(g + fb_ref[...]) for g in gs]
    bs = [_dot_split3(((c <= r) if direction == 0 else (c >= r)).astype(BF16), ls)
          for ls, (_, direction, _) in zip(lss, visits)]
    units = []
    for g, b, (rows, direction, state) in zip(gs, bs, visits):
        a = pltpu.roll(g, ML_HEADS, 1) - b
        bt = b.T
        gt = g.T
        if direction == 0:
            neg, last = jnp.where(r <= c, 0.0, NEG_INF), n - 1
        else:
            neg, last = jnp.where(r >= c, 0.0, NEG_INF), 0
        i_col = 2 * ML_HEADS * direction
        f_col = i_col + ML_HEADS
        for pair in range(ML_HEADS // 2):
            vt2 = zvo_ref[rows, pair * LANES:(pair + 1) * LANES].T
            for sub in range(2):
                hd = 2 * pair + sub
                feat = slice(sub * D_HEAD, (sub + 1) * D_HEAD)
                q = zqk_ref[rows, hd * D_HEAD:(hd + 1) * D_HEAD]
                k = zqk_ref[rows, ML_W + hd * D_HEAD:ML_W + (hd + 1) * D_HEAD]
                units.append((q, k, vt2[feat, :],
                              a[:, f_col + hd:f_col + hd + 1], bt[f_col + hd:f_col + hd + 1, :],
                              gt[i_col + hd:i_col + hd + 1, :], neg, last) + tuple(state[hd]))
    return units


def _mlstm_finish(ht, o_gate, norm_g):
    outs = []
    for hd in range(ML_HEADS):
        x = ht[hd * D_HEAD:(hd + 1) * D_HEAD, :]
        mu = jnp.mean(x, axis=0, keepdims=True)
        xc = x - mu
        var = jnp.mean(xc * xc, axis=0, keepdims=True)
        outs.append(xc * lax.rsqrt(var + LN_EPS))
    hn = jnp.concatenate(outs, axis=0).T
    return hn * norm_g * _sigmoid(o_gate)


def _mlstm_scan(zqk_ref, zvo_ref, g_ref, fb_ref, ng_ref, out_ref, n_tokens, state):
    n_chunks = n_tokens // ML_CHUNK
    step_chunks = [(i, n_chunks - 1 - i) for i in range(n_chunks)]
    no_state = [(None, None, None)] * ML_HEADS
    visits = [(slice(chunk * ML_CHUNK, (chunk + 1) * ML_CHUNK), direction, no_state)
              for chunks in step_chunks for direction, chunk in enumerate(chunks)]
    units = _mlstm_units(zqk_ref, zvo_ref, g_ref, fb_ref, ML_CHUNK, visits)
    kqs = _mlstm_scores(units)
    hts = {}
    for i, chunks in enumerate(step_chunks):
        step = slice(i * N_STATE, (i + 1) * N_STATE)
        results = _mlstm_run([u[:8] + tuple(st) for u, st in zip(units[step], state)], kqs[step])
        state = [(c_new, n_new, m_new) for _, c_new, n_new, m_new in results]
        for direction, chunk in enumerate(chunks):
            hts[direction, chunk] = jnp.concatenate(
                [results[direction * ML_HEADS + hd][0] for hd in range(ML_HEADS)], axis=0)
    for chunk in range(n_chunks):
        rows = slice(chunk * ML_CHUNK, (chunk + 1) * ML_CHUNK)
        out_ref[rows, :] = _mlstm_finish(hts[0, chunk] + hts[1, chunk],
                                         zvo_ref[rows, ML_W:2 * ML_W],
                                         ng_ref[...]).astype(out_ref.dtype)
    return state


def _store_state(c_ref, n_ref, m_ref, idx, c_new, n_new, m_new):
    c_ref[idx] = c_new
    n_ref[idx:idx + 1, :] = n_new
    m_ref[idx:idx + 1, :] = jnp.broadcast_to(m_new, (1, LANES))


ONES_ROWS = 16


def _with_ones(vt):
    ones = jnp.ones((ONES_ROWS, vt.shape[1]), BF16)
    return jnp.concatenate([vt.astype(BF16), ones], axis=0)


def _attend_many(problems):
    return _attend_values(problems, *_attend_scores(problems))


def _attend_scores(problems):
    scores = []
    for q, parts, _ in problems:
        sts = []
        for k, _, bias_t in parts:
            st = _dot_nt(k, q)
            sts.append(st if bias_t is None else st + bias_t)
        scores.append(sts)
    maxima = []
    for sts, (_, _, sink_row) in zip(scores, problems):
        m = None
        for st in sts:
            mx = jnp.max(st, axis=0, keepdims=True)
            m = mx if m is None else jnp.maximum(m, mx)
        maxima.append(m if sink_row is None else jnp.maximum(m, sink_row))
    return scores, maxima


def _attend_values(problems, scores, maxima):
    outs = []
    for sts, m, (_, parts, sink_row) in zip(scores, maxima, problems):
        ot = None
        for st, (_, vt_ones, _) in zip(sts, parts):
            part = jnp.dot(vt_ones, jnp.exp2(st - m).astype(BF16), preferred_element_type=F32)
            ot = part if ot is None else ot + part
        den = ot[D_HEAD:D_HEAD + 1, :]
        if sink_row is not None:
            den = den + jnp.exp2(sink_row - m)
        outs.append(ot[0:D_HEAD, :] * (1.0 / den))
    return outs


def _head_rows(pairs_t, hd):
    return pairs_t[hd // 2][(hd % 2) * D_HEAD:(hd % 2 + 1) * D_HEAD, :]


def _store_heads_t(o_ref, ots):
    for p in range(len(ots) // 2):
        o_ref[:, p * LANES:(p + 1) * LANES] = jnp.concatenate(
            [ots[2 * p], ots[2 * p + 1]], axis=0).astype(o_ref.dtype).T


N_STATE = 2 * ML_HEADS
N_CTX_STATE_OUTS = 7


def _ctx_mixers_kernel(x_ref, mod_ref, wt_ref, b_ref, fb_ref, ng_ref, sink_ref, *rest,
                       first_layer):
    n_scratch = 1 + len(Z_GROUPS)
    w_scr, zqk_ref, zvo_ref, zswa_ref, znat_ref, zg_ref = rest[-n_scratch:]
    mml_ref, mswa_ref, mnat_ref, *state_refs = rest[-(3 + N_CTX_STATE_OUTS) - n_scratch:-n_scratch]
    _inproj_kernel(x_ref, mod_ref, wt_ref, b_ref, zqk_ref, zvo_ref, zswa_ref, znat_ref, zg_ref,
                   w_scr)
    if first_layer:
        for ref in state_refs:
            ref[1:] = jnp.zeros((DEPTH - 1,) + ref.shape[1:], F32)
        state_refs = [ref.at[0] for ref in state_refs]
    sk_ref, sv_ref, nk_ref, nv_ref, c_ref, n_ref, m_ref = state_refs
    scaled_q = lambda ref, hd: ref[:, hd * D_HEAD:(hd + 1) * D_HEAD].astype(BF16)
    swa_vts = [zswa_ref[:, SWA_W + SWA_KV_W:SWA_W + 2 * SWA_KV_W].astype(BF16).T]
    nat_vts = [znat_ref[:, 2 * NAT_W + p * LANES:2 * NAT_W + (p + 1) * LANES].astype(BF16).T
               for p in range(NAT_W // LANES)]
    problems = []
    for kv in range(SWA_KV_HEADS):
        k = zswa_ref[:, SWA_W + kv * D_HEAD:SWA_W + (kv + 1) * D_HEAD]
        sk_ref[kv] = k
        sv_ref[kv] = zswa_ref[:, SWA_W + SWA_KV_W + kv * D_HEAD:SWA_W + SWA_KV_W + (kv + 1) * D_HEAD]
        q = jnp.concatenate([scaled_q(zswa_ref, kv * SWA_GROUP + g) for g in range(SWA_GROUP)],
                            axis=0)
        problems.append((q, [(k.astype(BF16), _with_ones(_head_rows(swa_vts, kv)), None)],
                         sink_ref[kv:kv + 1, :]))
    for hd in range(NAT_HEADS):
        k = znat_ref[:, NAT_W + hd * D_HEAD:NAT_W + (hd + 1) * D_HEAD]
        nk_ref[hd] = k
        nv_ref[hd] = znat_ref[:, 2 * NAT_W + hd * D_HEAD:2 * NAT_W + (hd + 1) * D_HEAD]
        problems.append((scaled_q(znat_ref, hd),
                         [(k.astype(BF16), _with_ones(_head_rows(nat_vts, hd)), None)], None))
    scores, maxima = _attend_scores(problems)

    zero_state = [(None, jnp.zeros((1, D_HEAD), F32), jnp.zeros((1, 1), F32))] * N_STATE
    final = _mlstm_scan(zqk_ref, zvo_ref, zg_ref, fb_ref, ng_ref, mml_ref, SEQ, zero_state)
    for idx, (c_new, n_new, m_new) in enumerate(final):
        _store_state(c_ref, n_ref, m_ref, idx, c_new, n_new, m_new)

    ots = _attend_values(problems, scores, maxima)
    _store_heads_t(mswa_ref, [ot[:, g * SEQ:(g + 1) * SEQ]
                              for ot in ots[:SWA_KV_HEADS] for g in range(SWA_GROUP)])
    _store_heads_t(mnat_ref, ots[SWA_KV_HEADS:])


CTX_STATE_DIMS = ((SWA_KV_HEADS, SEQ, D_HEAD), (SWA_KV_HEADS, SEQ, D_HEAD),
                  (NAT_HEADS, SEQ, D_HEAD), (NAT_HEADS, SEQ, D_HEAD),
                  (N_STATE, D_HEAD, D_HEAD), (N_STATE, D_HEAD), (N_STATE, LANES))


def _ctx_mixers_call(x, mods, w_in_t, b_in, fb_row, norm_g, sink_rows, layer, prev_state):
    rows = x.shape[0]
    row_spec = lambda w: pl.BlockSpec((SEQ, w), lambda b: (b, 0))
    state_dims = CTX_STATE_DIMS
    first_layer = prev_state is None
    if first_layer:
        assert layer == 0
        state_spec = lambda d: pl.BlockSpec((None, DEPTH) + d, lambda b: (b,) + (0,) * (len(d) + 1))
    else:
        state_spec = lambda d: pl.BlockSpec((None, None) + d, lambda b: (b, layer) + (0,) * len(d))
    prev = () if first_layer else tuple(prev_state)
    n_in = 7
    return pl.pallas_call(
        functools.partial(_ctx_mixers_kernel, first_layer=first_layer),
        grid=(BATCH,),
        in_specs=[row_spec(D_MODEL), _mod_spec(layer, lambda b: CTX_MOD_ROW),
                  _layer_spec(w_in_t, layer, single_buffer=True), _layer_spec(b_in, layer),
                  _layer_spec(fb_row, layer), _layer_spec(norm_g, layer),
                  _layer_spec(sink_rows, layer)]
                 + [pl.BlockSpec(memory_space=pl.ANY)] * len(prev),
        out_specs=[row_spec(ML_W), row_spec(SWA_W), row_spec(NAT_W)]
                  + [state_spec(d) for d in state_dims],
        out_shape=[jax.ShapeDtypeStruct((rows, ML_W), BF16),
                   jax.ShapeDtypeStruct((rows, SWA_W), BF16),
                   jax.ShapeDtypeStruct((rows, NAT_W), BF16)]
                  + [jax.ShapeDtypeStruct((BATCH, DEPTH) + d, F32) for d in state_dims],
        input_output_aliases={n_in + i: 3 + i for i in range(len(prev))},
        scratch_shapes=[pltpu.VMEM((Z_W, D_MODEL), BF16)]
                       + [pltpu.VMEM((SEQ, w), dt) for w, dt in zip(Z_GROUPS, CTX_Z_DTYPES)],
        compiler_params=_params(1),
        name="ctx_mixers",
    )(x, mods, w_in_t, b_in, fb_row, norm_g, sink_rows, *prev)


def _lat_mlstm_kernel(zqk_ref, zvo_ref, zg_ref, fb_ref, ng_ref, c0_ref, n0_ref, m0_ref, mml_ref):
    state = [(c0_ref[idx], n0_ref[idx:idx + 1, :], m0_ref[idx:idx + 1, :])
             for idx in range(N_STATE)]
    _mlstm_scan(zqk_ref, zvo_ref, zg_ref, fb_ref, ng_ref, mml_ref, DEC_SEQ, state)


def _lat_mlstm_call(z_qk, z_vo, z_g, fb_row, norm_g, state_c, state_n, state_m, layer):
    n_st = N_STATE
    row_spec = lambda w: pl.BlockSpec((DEC_SEQ, w), lambda b: (b, 0))
    return pl.pallas_call(
        _lat_mlstm_kernel,
        grid=(DEC_BATCH,),
        in_specs=[row_spec(Z_QK_W), row_spec(Z_VO_W), row_spec(Z_G_W),
                  _layer_spec(fb_row, layer), _layer_spec(norm_g, layer),
                  pl.BlockSpec((None, None, n_st, D_HEAD, D_HEAD), lambda b: (b, layer, 0, 0, 0)),
                  pl.BlockSpec((None, None, n_st, D_HEAD), lambda b: (b, layer, 0, 0)),
                  pl.BlockSpec((None, None, n_st, 1), lambda b: (b, layer, 0, 0))],
        out_specs=row_spec(ML_W),
        out_shape=jax.ShapeDtypeStruct((DEC_BATCH * DEC_SEQ, ML_W), BF16),
        compiler_params=_params(1),
        name="lat_mlstm",
    )(z_qk, z_vo, z_g, fb_row, norm_g, state_c, state_n, state_m)


def _lat_swa_kernel(zswa_ref, ckt_ref, cvt_ref, sink_ref, mswa_ref,
                    vt_scr, ot_scr, bias_scr, ck_scr, cvt_scr):
    vt_scr[...] = zswa_ref[:, SWA_W + SWA_KV_W:SWA_W + 2 * SWA_KV_W].T

    bl = SWA_BLOCK
    nb = DEC_SEQ // bl
    n_q = SWA_GROUP * bl
    kj = lax.broadcasted_iota(jnp.int32, (3 * bl, n_q), 0)
    qi = lax.broadcasted_iota(jnp.int32, (3 * bl, n_q), 1) & (bl - 1)
    keep = ((kj >= qi) & (kj < 2 * bl)) | ((kj >= 2 * bl) & (kj - 2 * bl <= qi))
    bias_scr[...] = jnp.where(keep, 0.0, NEG_INF)
    ck_scr[...] = jnp.concatenate([ckt_ref[kv] for kv in range(SWA_KV_HEADS)],
                                  axis=0).astype(BF16).T
    for kv in range(SWA_KV_HEADS):
        cvt_scr[kv] = _with_ones(cvt_ref[kv])

    problems, where = [], []
    for kv in range(SWA_KV_HEADS):
        head = slice(kv * D_HEAD, (kv + 1) * D_HEAD)
        for blk in range(nb):
            lo_b, hi_b = max(blk - 1, 0), min(blk + 2, nb)
            keys = slice(lo_b * bl, hi_b * bl)
            q = jnp.concatenate(
                [zswa_ref[blk * bl:(blk + 1) * bl,
                          (kv * SWA_GROUP + g) * D_HEAD:(kv * SWA_GROUP + g + 1) * D_HEAD]
                 for g in range(SWA_GROUP)], axis=0)
            parts = [(zswa_ref[keys, SWA_W + kv * D_HEAD:SWA_W + (kv + 1) * D_HEAD],
                      _with_ones(vt_scr[head, keys]),
                      bias_scr[(lo_b - blk + 1) * bl:(hi_b - blk + 1) * bl, :]),
                     (ck_scr[:, head], cvt_scr[kv], None)]
            problems.append((q, parts, sink_ref[kv:kv + 1, :]))
            where.append((kv, blk))
    for (kv, blk), ot in zip(where, _attend_many(problems)):
        o = ot.astype(BF16)
        for g in range(SWA_GROUP):
            hq = kv * SWA_GROUP + g
            ot_scr[hq * D_HEAD:(hq + 1) * D_HEAD, blk * bl:(blk + 1) * bl] = (
                o[:, g * bl:(g + 1) * bl])
    for p in range(SWA_W // LANES):
        mswa_ref[:, p * LANES:(p + 1) * LANES] = ot_scr[p * LANES:(p + 1) * LANES, :].T


def _lat_swa_call(z_swa, cache_k, cache_v, sink_rows, layer):
    row_spec = lambda w: pl.BlockSpec((DEC_SEQ, w), lambda b: (b, 0))
    cache_spec = pl.BlockSpec((None, None, SWA_KV_HEADS, D_HEAD, PAST_LEN),
                              lambda b: (b, layer, 0, 0, 0))
    n_q = SWA_GROUP * SWA_BLOCK
    return pl.pallas_call(
        _lat_swa_kernel,
        grid=(DEC_BATCH,),
        in_specs=[row_spec(Z_SWA_W), cache_spec, cache_spec, _layer_spec(sink_rows, layer)],
        out_specs=row_spec(SWA_W),
        out_shape=jax.ShapeDtypeStruct((DEC_BATCH * DEC_SEQ, SWA_W), BF16),
        scratch_shapes=[pltpu.VMEM((SWA_KV_W, DEC_SEQ), BF16), pltpu.VMEM((SWA_W, DEC_SEQ), BF16),
                        pltpu.VMEM((3 * SWA_BLOCK, n_q), F32),
                        pltpu.VMEM((PAST_LEN, SWA_KV_W), BF16),
                        pltpu.VMEM((SWA_KV_HEADS, D_HEAD + ONES_ROWS, PAST_LEN), BF16)],
        compiler_params=_params(1),
        name="lat_swa",
    )(z_swa, cache_k, cache_v, sink_rows)


NAT_ROWS = DEC_SEQ // GRID_W
NAT_GROUP_ROWS = 4
NAT_GROUP_Q = NAT_GROUP_ROWS * GRID_W
NAT_RPB_ROWS = 2 * NAT_KH


def _nat_row_start(r):
    return min(max(r - NAT_KH // 2, 0), NAT_ROWS - NAT_KH)


def _nat_groups():
    groups, off = [], 0
    for g in range(NAT_ROWS // NAT_GROUP_ROWS):
        w0 = _nat_row_start(g * NAT_GROUP_ROWS)
        w1 = _nat_row_start((g + 1) * NAT_GROUP_ROWS - 1) + NAT_KH
        n_rows = w1 - w0 + (w1 - w0) % 2
        w0 = min(w0, NAT_ROWS - n_rows)
        groups.append((w0, n_rows, off))
        off += n_rows * GRID_W
    return groups, off


NAT_GROUPS, NAT_BIAS_KEYS = _nat_groups()


def _nat_build_bias(rpb_ref, bias_scr):
    shape = (GRID_W, LANES)
    q = lax.broadcasted_iota(jnp.int32, shape, 0)
    lane = lax.broadcasted_iota(jnp.int32, shape, 1)
    kc = lane & (GRID_W - 1)
    cs = jnp.clip(q - NAT_KW // 2, 0, GRID_W - NAT_KW)
    ok = (kc >= cs) & (kc < cs + NAT_KW)
    ok_lo = ok & (lane < GRID_W)
    ok_hi = ok & (lane >= GRID_W)
    neg_tile = jnp.full(shape, NEG_INF, F32)
    for hd in range(2):
        lo, hi = [], []
        for ro in range(2 * NAT_KH - 1):
            x = jnp.broadcast_to(rpb_ref[hd, ro:ro + 1, :], shape)
            lo.append(jnp.where(ok_lo, pltpu.roll(x, 0, 1, stride=1, stride_axis=0), NEG_INF))
            hi.append(jnp.where(ok_hi, pltpu.roll(x, GRID_W, 1, stride=1, stride_axis=0), NEG_INF))
        for g, (w0, n_rows, off) in enumerate(NAT_GROUPS):
            for jj in range(n_rows // 2):
                kra, krb = w0 + 2 * jj, w0 + 2 * jj + 1
                tiles = []
                for rr in range(NAT_GROUP_ROWS):
                    r = g * NAT_GROUP_ROWS + rr
                    r0 = _nat_row_start(r)
                    parts = []
                    if r0 <= kra < r0 + NAT_KH:
                        parts.append(lo[kra - r + NAT_KH - 1])
                    if r0 <= krb < r0 + NAT_KH:
                        parts.append(hi[krb - r + NAT_KH - 1])
                    tiles.append(neg_tile if not parts else (
                        parts[0] if len(parts) == 1 else jnp.maximum(parts[0], parts[1])))
                bias_scr[hd, off + jj * LANES:off + (jj + 1) * LANES, :] = (
                    jnp.concatenate(tiles, axis=0).T)


def _lat_nat_kernel(q_ref, k_ref, v_ref, ckt_ref, cvt_ref, rpb_ref, mnat_ref, bias_scr):
    @pl.when(pl.program_id(1) == 0)
    def _():
        _nat_build_bias(rpb_ref, bias_scr)

    vt = v_ref[...].astype(BF16).T
    ck2 = jnp.concatenate([ckt_ref[0], ckt_ref[1]], axis=0).astype(BF16).T
    problems = []
    for hd in range(2):
        head = slice(hd * D_HEAD, (hd + 1) * D_HEAD)
        ck = ck2[:, head]
        cvt_ones = _with_ones(cvt_ref[hd])
        for g, (w0, n_rows, off) in enumerate(NAT_GROUPS):
            keys = slice(w0 * GRID_W, (w0 + n_rows) * GRID_W)
            parts = [(k_ref[keys, head].astype(BF16), _with_ones(vt[head, keys]),
                      bias_scr[hd, off:off + n_rows * GRID_W, :]),
                     (ck, cvt_ones, None)]
            q = q_ref[g * NAT_GROUP_Q:(g + 1) * NAT_GROUP_Q, head].astype(BF16)
            problems.append((q, parts, None))
    ots = _attend_many(problems)
    n_groups = len(NAT_GROUPS)
    mnat_ref[...] = jnp.concatenate(
        [jnp.concatenate(ots[hd * n_groups:(hd + 1) * n_groups], axis=1) for hd in range(2)],
        axis=0).astype(BF16).T


def _lat_nat_call(z_nat, cache_k, cache_v, rpb_lanes, layer):
    n_pair = NAT_HEADS // 2
    col_spec = lambda off: pl.BlockSpec((DEC_SEQ, LANES), lambda p, b: (b, off + p))
    cache_spec = pl.BlockSpec((None, None, 2, D_HEAD, PAST_LEN), lambda p, b: (b, layer, p, 0, 0))
    return pl.pallas_call(
        _lat_nat_kernel,
        grid=(n_pair, DEC_BATCH),
        in_specs=[col_spec(0), col_spec(n_pair), col_spec(2 * n_pair), cache_spec, cache_spec,
                  pl.BlockSpec((None, 2, NAT_RPB_ROWS, LANES), lambda p, b: (layer, p, 0, 0))],
        out_specs=pl.BlockSpec((DEC_SEQ, LANES), lambda p, b: (b, p)),
        out_shape=jax.ShapeDtypeStruct((DEC_BATCH * DEC_SEQ, NAT_W), BF16),
        scratch_shapes=[pltpu.VMEM((2, NAT_BIAS_KEYS, NAT_GROUP_Q), F32)],
        compiler_params=_params(2),
        name="lat_nat",
    )(z_nat, z_nat, z_nat, cache_k, cache_v, rpb_lanes)


def _nat_rpb_lanes(rpb):
    n_off = 2 * NAT_KW - 1
    padded = jnp.concatenate(
        [rpb.astype(F32), jnp.zeros(rpb.shape[:-1] + (LANES - n_off,), F32)], axis=-1)
    rolled = jnp.roll(padded, -(NAT_KW - 1), axis=-1)
    pad_rows = jnp.zeros(rpb.shape[:-2] + (NAT_RPB_ROWS - rpb.shape[-2], LANES), F32)
    return jnp.concatenate([rolled, pad_rows], axis=-2)


def _rope_tables():
    t = np.arange(DEC_SEQ)[:, None]
    d = np.arange(LANES)[None, :] % D_HEAD
    pos = np.where(d < D_HEAD // 2, t // GRID_W, t % GRID_W).astype(np.float64)
    quarter = D_HEAD // 4
    freq = ROPE_BASE ** (-(d % quarter).astype(np.float64) / quarter)
    ang = (pos.astype(np.float32) * freq.astype(np.float32)).astype(np.float32)
    sign = np.where((d % (2 * quarter)) < quarter, -1.0, 1.0)
    return (jnp.asarray(np.cos(ang), dtype=F32), jnp.asarray(np.sin(ang) * sign, dtype=F32))


def _permute_in_columns(a):
    g0 = Z_ML_W
    s0 = g0 + N_GATES
    pad = jnp.zeros(a.shape[:-1] + (Z_G_W - N_GATES,), a.dtype)
    return jnp.concatenate([a[..., :g0], a[..., s0:], a[..., g0:s0], pad], axis=-1)


def kernel(x_prompt, x_sample, cache_swa_k, cache_swa_v, cache_nat_k, cache_nat_v, state_mlstm_C,
           state_mlstm_n, state_mlstm_m, c, c_ctx, w_ada, b_ada, w_in, b_in, mlstm_fbias,
           mlstm_norm_g, swa_sink, nat_rpb, w_out, ln1_g, ln1_b, w_mlp1, w_mlp2, ln2_g, ln2_b):
    xp = x_prompt.reshape(BATCH * SEQ, D_MODEL)
    xs = x_sample.reshape(DEC_BATCH * DEC_SEQ, D_MODEL)
    cvec = jnp.concatenate(
        [c, c_ctx[None, :], jnp.zeros((MOD_ROWS - DEC_BATCH - 1, D_MODEL), F32)], axis=0)
    mods = _mods_call(cvec, w_ada, b_ada).reshape(DEPTH * MOD_ROWS, 6, D_MODEL)

    w_in_t = jnp.swapaxes(w_in, 1, 2)
    out_scale = np.ones((Z_W,), np.float32)
    out_scale[ML_W:2 * ML_W] = ATT_SCALE
    out_scale[Z_ML_W:Z_ML_W + SWA_W] = Q_SCALE2
    out_scale[Z_ML_W + Z_SWA_W:Z_ML_W + Z_SWA_W + NAT_W] = Q_SCALE2
    b_in_p = jnp.stack([jnp.broadcast_to(out_scale, (DEPTH, Z_W)),
                        _permute_in_columns(b_in) * out_scale], axis=1)
    cache_swa_k, cache_swa_v, cache_nat_k, cache_nat_v = (
        jnp.swapaxes(a, -1, -2) for a in (cache_swa_k, cache_swa_v, cache_nat_k, cache_nat_v))
    vec = lambda a: a.reshape(DEPTH, 1, D_MODEL)
    tail_w = (w_out.astype(BF16), vec(ln1_g), vec(ln1_b), w_mlp1.astype(BF16),
              w_mlp2.astype(BF16), vec(ln2_g), vec(ln2_b))
    cos_t, sin_t = _rope_tables()
    rpb_lanes = _nat_rpb_lanes(nat_rpb) * LOG2E
    swa_sink = swa_sink * LOG2E
    fb_rows = jnp.zeros((DEPTH, 1, Z_G_W), F32)
    fb_rows = fb_rows.at[:, 0, ML_HEADS:2 * ML_HEADS].set(mlstm_fbias[:, 0])
    fb_rows = fb_rows.at[:, 0, 3 * ML_HEADS:4 * ML_HEADS].set(mlstm_fbias[:, 1])
    norm_g = mlstm_norm_g.reshape(DEPTH, 1, ML_W)
    sink_ctx = jnp.repeat(swa_sink, SEQ, axis=1).reshape(DEPTH, SWA_KV_HEADS, SWA_GROUP * SEQ)
    sink_lat = jnp.repeat(swa_sink, SWA_BLOCK, axis=1).reshape(
        DEPTH, SWA_KV_HEADS, SWA_GROUP * SWA_BLOCK)

    n_st = 2 * ML_HEADS
    state_c = state_mlstm_C.reshape(DEC_BATCH, DEPTH, n_st, D_HEAD, D_HEAD)
    state_n = state_mlstm_n.reshape(DEC_BATCH, DEPTH, n_st, D_HEAD)
    state_m = state_mlstm_m.reshape(DEC_BATCH, DEPTH, n_st, 1)

    ctx_row = lambda i: CTX_MOD_ROW
    lat_row = lambda i: i // (DEC_SEQ // TAIL_TILE)

    ctx_state = None
    for l in range(DEPTH):
        m_ml, m_swa, m_nat, *ctx_state = _ctx_mixers_call(
            xp, mods, w_in_t, b_in_p, fb_rows, norm_g, sink_ctx, l, ctx_state)
        xp = _tail_call(xp, m_ml, m_swa, m_nat, mods, ctx_row, tail_w, l)

        z_qk, z_vo, z_swa, z_nat, z_g = _lat_inproj_call(xs, mods, w_in_t, b_in_p, cos_t, sin_t, l)
        m_ml = _lat_mlstm_call(z_qk, z_vo, z_g, fb_rows, norm_g, state_c, state_n, state_m, l)
        m_swa = _lat_swa_call(z_swa, cache_swa_k, cache_swa_v, sink_lat, l)
        m_nat = _lat_nat_call(z_nat, cache_nat_k, cache_nat_v, rpb_lanes, l)
        xs = _tail_call(xs, m_ml, m_swa, m_nat, mods, lat_row, tail_w, l)

    k_swa, v_swa, k_nat, v_nat, st_c, st_n, st_m = ctx_state
    return (xp.reshape(BATCH, SEQ, D_MODEL), xs.reshape(DEC_BATCH, DEC_SEQ, D_MODEL),
            k_swa, v_swa, k_nat, v_nat,
            st_c.reshape(BATCH, DEPTH, 2, ML_HEADS, D_HEAD, D_HEAD),
            st_n.reshape(BATCH, DEPTH, 2, ML_HEADS, D_HEAD),
            st_m[..., 0].reshape(BATCH, DEPTH, 2, ML_HEADS))
```

```python
import functools

import numpy as np
import jax
import jax.numpy as jnp
from jax import lax
from jax.experimental import pallas as pl
from jax.experimental.pallas import tpu as pltpu

D_MODEL = 1024
BATCH = 16
SEQ = 256
DEPTH = 4
DEC_BATCH = 8
DEC_SEQ = 1024
PAST_LEN = 512
GRID_W = 64
D_HEAD = 64
ML_HEADS = 4
SWA_HEADS = 6
SWA_KV_HEADS = 2
SWA_GROUP = SWA_HEADS // SWA_KV_HEADS
NAT_HEADS = 6
ML_W = ML_HEADS * D_HEAD
SWA_W = SWA_HEADS * D_HEAD
SWA_KV_W = SWA_KV_HEADS * D_HEAD
NAT_W = NAT_HEADS * D_HEAD
N_GATES = 4 * ML_HEADS
IN_DIM = 4 * ML_W + N_GATES + SWA_W + 2 * SWA_KV_W + 3 * NAT_W
D_FF = 4 * D_MODEL
SWA_WINDOW = 128
SWA_BLOCK = 128
NAT_KH = 8
NAT_KW = 16
ROPE_BASE = 10000.0
LN_EPS = 1e-5
DN_ALPHA = (2 * DEPTH) ** 0.25
ATT_SCALE = D_HEAD ** -0.5
LOG2E = 1.4426950408889634
Q_SCALE2 = ATT_SCALE * LOG2E

LANES = 128
SUBLANES = 8
Z_ML_W = 4 * ML_W
Z_SWA_W = SWA_W + 2 * SWA_KV_W
Z_NAT_W = 3 * NAT_W
Z_G_W = LANES
Z_W = Z_ML_W + Z_SWA_W + Z_NAT_W + Z_G_W
Z_QK_W = 2 * ML_W
Z_VO_W = 2 * ML_W
Z_GROUPS = (Z_QK_W, Z_VO_W, Z_SWA_W, Z_NAT_W, Z_G_W)
Z_SWA_GROUP = 2
MOD_ROWS = 16
CTX_MOD_ROW = DEC_BATCH
ROW_TILE = 1024
TAIL_TILE = 1024
TAIL_PARTS = 2
ML_CHUNK = 128
VMEM_LIMIT = 58 * 1024 * 1024

BF16 = jnp.bfloat16
F32 = jnp.float32
NEG_INF = float("-inf")
LAT_Z_DTYPES = (BF16, F32, BF16, BF16, F32)
CTX_Z_DTYPES = (BF16, F32, F32, F32, F32)


def _dot(a, b):
    return jnp.dot(a.astype(BF16), b.astype(BF16), preferred_element_type=F32)


def _dot_nt(a, b):
    return lax.dot_general(a.astype(BF16), b.astype(BF16), (((1,), (1,)), ((), ())),
                           preferred_element_type=F32)


def _dot_split3(tri, x):
    hi = x.astype(BF16)
    r1 = x - hi.astype(F32)
    mid = r1.astype(BF16)
    lo = (r1 - mid.astype(F32)).astype(BF16)
    return (jnp.dot(tri, hi, preferred_element_type=F32)
            + jnp.dot(tri, mid, preferred_element_type=F32)
            + jnp.dot(tri, lo, preferred_element_type=F32))


def _sigmoid(x):
    return 1.0 / (1.0 + jnp.exp(-x))


def _log_sigmoid(x):
    return jnp.minimum(x, 0.0) - jnp.log1p(jnp.exp(-jnp.abs(x)))


def _layer_norm(x, g, b):
    mu = jnp.mean(x, axis=-1, keepdims=True)
    xc = x - mu
    var = jnp.mean(xc * xc, axis=-1, keepdims=True)
    return xc * lax.rsqrt(var + LN_EPS) * g + b


def _params(n_axes=1):
    return pltpu.CompilerParams(dimension_semantics=("arbitrary",) * n_axes,
                                vmem_limit_bytes=VMEM_LIMIT)


def _layer_spec(a, layer, single_buffer=False):
    kwargs = {"pipeline_mode": pl.Buffered(1)} if single_buffer else {}
    return pl.BlockSpec((None,) + a.shape[1:], lambda *_: (layer,) + (0,) * (a.ndim - 1), **kwargs)


def _mods_kernel(c_ref, w_ref, b_ref, o_ref):
    c = c_ref[...]
    o_ref[...] = _dot(c * _sigmoid(c), w_ref[...]) + b_ref[...]


def _mods_call(cvec, w_ada, b_ada):
    n_col = 6
    return pl.pallas_call(
        _mods_kernel,
        grid=(DEPTH, n_col),
        in_specs=[
            pl.BlockSpec((MOD_ROWS, D_MODEL), lambda l, j: (0, 0)),
            pl.BlockSpec((None, D_MODEL, D_MODEL), lambda l, j: (l, 0, j)),
            pl.BlockSpec((None, 1, D_MODEL), lambda l, j: (l, 0, j)),
        ],
        out_specs=pl.BlockSpec((None, MOD_ROWS, D_MODEL), lambda l, j: (l, 0, j)),
        out_shape=jax.ShapeDtypeStruct((DEPTH, MOD_ROWS, 6 * D_MODEL), F32),
        compiler_params=_params(2),
        name="mods",
    )(cvec, w_ada, b_ada.reshape(DEPTH, 1, 6 * D_MODEL))


def _rope(x, cos, sin_signed):
    lane = lax.broadcasted_iota(jnp.int32, x.shape, 1)
    first = (lane & 31) < 16
    partner = jnp.where(first, pltpu.roll(x, LANES - 16, 1), pltpu.roll(x, 16, 1))
    return x * cos + partner * sin_signed


def _inproj_kernel(x_ref, mod_ref, wt_ref, b_ref, *rest, rope=False):
    if rope:
        cos_ref, sin_ref, *rest = rest
    *out_refs, w_scr = rest

    @pl.when(pl.program_id(0) == 0)
    def _():
        rest = IN_DIM - Z_ML_W - N_GATES
        w_scr[0:Z_ML_W, :] = wt_ref[0:Z_ML_W, :].astype(BF16)
        w_scr[Z_ML_W:Z_ML_W + rest, :] = wt_ref[Z_ML_W + N_GATES:IN_DIM, :].astype(BF16)
        w_scr[Z_ML_W + rest:IN_DIM, :] = wt_ref[Z_ML_W:Z_ML_W + N_GATES, :].astype(BF16)
        w_scr[IN_DIM:Z_W, :] = jnp.zeros((Z_W - IN_DIM, D_MODEL), BF16)

    h = (x_ref[...] * (1.0 + mod_ref[1:2, :]) + mod_ref[0:1, :]).astype(BF16)
    offsets = [sum(Z_GROUPS[:i]) for i in range(len(Z_GROUPS))]
    accs = [_dot_nt(h, w_scr[o:o + w, :]) for o, w in zip(offsets, Z_GROUPS)]
    for ref, w, o, acc in zip(out_refs, Z_GROUPS, offsets, accs):
        z = acc * b_ref[0:1, o:o + w] + b_ref[1:2, o:o + w]
        if rope and ref is out_refs[Z_SWA_GROUP]:
            z = jnp.concatenate(
                [_rope(z[:, j * LANES:(j + 1) * LANES], cos_ref[...], sin_ref[...])
                 for j in range((SWA_W + SWA_KV_W) // LANES)] + [z[:, SWA_W + SWA_KV_W:]], axis=1)
        ref[...] = z.astype(ref.dtype)


def _mod_spec(layer, mod_row_of_tile):
    return pl.BlockSpec((None, 6, D_MODEL),
                        lambda i: (layer * MOD_ROWS + mod_row_of_tile(i), 0, 0))


def _lat_inproj_call(x, mods, w_in_t, b_in, cos_t, sin_t, layer):
    rows = x.shape[0]
    tiles_per_seq = DEC_SEQ // ROW_TILE
    table_spec = pl.BlockSpec((ROW_TILE, LANES), lambda i: (i % tiles_per_seq, 0))
    return pl.pallas_call(
        functools.partial(_inproj_kernel, rope=True),
        grid=(rows // ROW_TILE,),
        in_specs=[
            pl.BlockSpec((ROW_TILE, D_MODEL), lambda i: (i, 0)),
            _mod_spec(layer, lambda i: i // tiles_per_seq),
            _layer_spec(w_in_t, layer, single_buffer=True),
            _layer_spec(b_in, layer),
            table_spec, table_spec,
        ],
        out_specs=[pl.BlockSpec((ROW_TILE, w), lambda i: (i, 0)) for w in Z_GROUPS],
        out_shape=[jax.ShapeDtypeStruct((rows, w), dt) for w, dt in zip(Z_GROUPS, LAT_Z_DTYPES)],
        scratch_shapes=[pltpu.VMEM((Z_W, D_MODEL), BF16)],
        compiler_params=_params(1),
        name="inproj",
    )(x, mods, w_in_t, b_in, cos_t, sin_t)


def _tail_kernel(x_ref, mml_ref, mswa_ref, mnat_ref, mod_ref, wout_ref, ln1g_ref, ln1b_ref,
                 w1_ref, w2_ref, ln2g_ref, ln2b_ref, o_ref):
    g_a, sh_m, sc_m, g_m = (mod_ref[i:i + 1, :] for i in (2, 3, 4, 5))
    part = x_ref.shape[0] // TAIL_PARTS
    halves = tuple(slice(i * part, (i + 1) * part) for i in range(TAIL_PARTS))
    projs = [jnp.dot(jnp.concatenate([mml_ref[r, :], mswa_ref[r, :], mnat_ref[r, :]], axis=1),
                     wout_ref[...], preferred_element_type=F32) for r in halves]
    ys, fs = [], []
    for r, proj in zip(halves, projs):
        y = _layer_norm(DN_ALPHA * x_ref[r, :] + g_a * proj, ln1g_ref[...], ln1b_ref[...])
        h = (y * (1.0 + sc_m) + sh_m).astype(BF16)
        ys.append(y)
        fs.append(jnp.dot(h, w1_ref[...], preferred_element_type=F32))
    mlps = []
    for f in fs:
        f = jnp.maximum(f, 0.0)
        mlps.append(jnp.dot((f * f).astype(BF16), w2_ref[...], preferred_element_type=F32))
    for r, y, mlp in zip(halves, ys, mlps):
        o_ref[r, :] = _layer_norm(DN_ALPHA * y + g_m * mlp, ln2g_ref[...], ln2b_ref[...])


def _tail_call(x, mix_ml, mix_swa, mix_nat, mods, mod_row_of_tile, consts, layer):
    rows = x.shape[0]
    row_spec = lambda w: pl.BlockSpec((TAIL_TILE, w), lambda i: (i, 0))
    return pl.pallas_call(
        _tail_kernel,
        grid=(rows // TAIL_TILE,),
        in_specs=[row_spec(D_MODEL), row_spec(ML_W), row_spec(SWA_W), row_spec(NAT_W),
                  _mod_spec(layer, mod_row_of_tile)]
                 + [_layer_spec(a, layer, single_buffer=True) for a in consts],
        out_specs=row_spec(D_MODEL),
        out_shape=jax.ShapeDtypeStruct((rows, D_MODEL), F32),
        compiler_params=_params(1),
        name="tail",
    )(x, mix_ml, mix_swa, mix_nat, mods, *consts)


def _mlstm_scores(units):
    return [_dot_nt(u[1], u[0]) for u in units]


def _mlstm_run(units, kqs=None):
    if kqs is None:
        kqs = _mlstm_scores(units)
    cns = []
    for q, _, _, _, _, _, _, _, c_st, n_st, _ in units:
        cns.append(None if c_st is None else _dot_nt(
            jnp.concatenate([c_st, jnp.broadcast_to(n_st, (SUBLANES, D_HEAD))], axis=0), q))
    mids, nums = [], []
    for kq, (_, _, vt, a_col, b_row, _, neg, _, _, _, m_st) in zip(kqs, units):
        a_m = a_col + neg
        c_row = jnp.maximum(m_st, jnp.max(a_m, axis=0, keepdims=True))
        st = kq * jnp.exp(a_m - c_row)
        mids.append((st, jnp.sum(st, axis=0, keepdims=True), jnp.exp(m_st - c_row), b_row + c_row))
        nums.append(_dot(vt, st))
    upds, m_news = [], []
    for (_, k, vt, _, b_row, li_row, _, last, _, _, _), (_, _, _, mt) in zip(units, mids):
        m_new = mt[:, last:last + 1]
        ws = jnp.exp(b_row[:, last:last + 1] - b_row + li_row - m_new)
        upds.append(_dot(jnp.concatenate(
            [vt * ws, jnp.broadcast_to(ws, (SUBLANES, ws.shape[1]))], axis=0), k))
        m_news.append(m_new)
    results = []
    for u, cn, (_, den, wprev, mt), num, upd, m_new in zip(units, cns, mids, nums, upds, m_news):
        last, c_st, n_st = u[7], u[8], u[9]
        c_new = upd[0:D_HEAD, :]
        n_new = upd[D_HEAD:D_HEAD + 1, :]
        if c_st is not None:
            num = num + wprev * cn[0:D_HEAD, :]
            den = den + wprev * cn[D_HEAD:D_HEAD + 1, :]
            wc = wprev[:, last:last + 1]
            c_new = c_new + wc * c_st
            n_new = n_new + wc * n_st
        ht = num * (1.0 / jnp.maximum(jnp.abs(den), jnp.exp(-mt)))
        results.append((ht, c_new, n_new, m_new))
    return results


def _mlstm_units(zqk_ref, zvo_ref, g_ref, fb_ref, n, visits):
    r = lax.broadcasted_iota(jnp.int32, (n, n), 0)
    c = lax.broadcasted_iota(jnp.int32, (n, n), 1)
    gs = [g_ref[rows, :] for rows, _, _ in visits]
    lss = [_log_sigmoid(g + fb_ref[...]) for g in gs]
    bs = [_dot_split3(((c <= r) if direction == 0 else (c >= r)).astype(BF16), ls)
          for ls, (_, direction, _) in zip(lss, visits)]
    units = []
    for g, b, (rows, direction, state) in zip(gs, bs, visits):
        a = pltpu.roll(g, ML_HEADS, 1) - b
        bt = b.T
        gt = g.T
        if direction == 0:
            neg, last = jnp.where(r <= c, 0.0, NEG_INF), n - 1
        else:
            neg, last = jnp.where(r >= c, 0.0, NEG_INF), 0
        i_col = 2 * ML_HEADS * direction
        f_col = i_col + ML_HEADS
        for pair in range(ML_HEADS // 2):
            vt2 = zvo_ref[rows, pair * LANES:(pair + 1) * LANES].T
            for sub in range(2):
                hd = 2 * pair + sub
                feat = slice(sub * D_HEAD, (sub + 1) * D_HEAD)
                q = zqk_ref[rows, hd * D_HEAD:(hd + 1) * D_HEAD]
                k = zqk_ref[rows, ML_W + hd * D_HEAD:ML_W + (hd + 1) * D_HEAD]
                units.append((q, k, vt2[feat, :],
                              a[:, f_col + hd:f_col + hd + 1], bt[f_col + hd:f_col + hd + 1, :],
                              gt[i_col + hd:i_col + hd + 1, :], neg, last) + tuple(state[hd]))
    return units


def _mlstm_finish(ht, o_gate, norm_g):
    outs = []
    for hd in range(ML_HEADS):
        x = ht[hd * D_HEAD:(hd + 1) * D_HEAD, :]
        mu = jnp.mean(x, axis=0, keepdims=True)
        xc = x - mu
        var = jnp.mean(xc * xc, axis=0, keepdims=True)
        outs.append(xc * lax.rsqrt(var + LN_EPS))
    hn = jnp.concatenate(outs, axis=0).T
    return hn * norm_g * _sigmoid(o_gate)


def _mlstm_scan(zqk_ref, zvo_ref, g_ref, fb_ref, ng_ref, out_ref, n_tokens, state):
    n_chunks = n_tokens // ML_CHUNK
    step_chunks = [(i, n_chunks - 1 - i) for i in range(n_chunks)]
    no_state = [(None, None, None)] * ML_HEADS
    visits = [(slice(chunk * ML_CHUNK, (chunk + 1) * ML_CHUNK), direction, no_state)
              for chunks in step_chunks for direction, chunk in enumerate(chunks)]
    units = _mlstm_units(zqk_ref, zvo_ref, g_ref, fb_ref, ML_CHUNK, visits)
    kqs = _mlstm_scores(units)
    hts = {}
    for i, chunks in enumerate(step_chunks):
        step = slice(i * N_STATE, (i + 1) * N_STATE)
        results = _mlstm_run([u[:8] + tuple(st) for u, st in zip(units[step], state)], kqs[step])
        state = [(c_new, n_new, m_new) for _, c_new, n_new, m_new in results]
        for direction, chunk in enumerate(chunks):
            hts[direction, chunk] = jnp.concatenate(
                [results[direction * ML_HEADS + hd][0] for hd in range(ML_HEADS)], axis=0)
    for chunk in range(n_chunks):
        rows = slice(chunk * ML_CHUNK, (chunk + 1) * ML_CHUNK)
        out_ref[rows, :] = _mlstm_finish(hts[0, chunk] + hts[1, chunk],
                                         zvo_ref[rows, ML_W:2 * ML_W],
                                         ng_ref[...]).astype(out_ref.dtype)
    return state


def _store_state(c_ref, n_ref, m_ref, idx, c_new, n_new, m_new):
    c_ref[idx] = c_new
    n_ref[idx:idx + 1, :] = n_new
    m_ref[idx:idx + 1, :] = jnp.broadcast_to(m_new, (1, LANES))


ONES_ROWS = 16


def _with_ones(vt):
    ones = jnp.ones((ONES_ROWS, vt.shape[1]), BF16)
    return jnp.concatenate([vt.astype(BF16), ones], axis=0)


def _attend_many(problems):
    return _attend_values(problems, *_attend_scores(problems))


def _attend_scores(problems):
    scores = []
    for q, parts, _ in problems:
        sts = []
        for k, _, bias_t in parts:
            st = _dot_nt(k, q)
            sts.append(st if bias_t is None else st + bias_t)
        scores.append(sts)
    maxima = []
    for sts, (_, _, sink_row) in zip(scores, problems):
        m = None
        for st in sts:
            mx = jnp.max(st, axis=0, keepdims=True)
            m = mx if m is None else jnp.maximum(m, mx)
        maxima.append(m if sink_row is None else jnp.maximum(m, sink_row))
    return scores, maxima


def _attend_values(problems, scores, maxima):
    outs = []
    for sts, m, (_, parts, sink_row) in zip(scores, maxima, problems):
        ot = None
        for st, (_, vt_ones, _) in zip(sts, parts):
            part = jnp.dot(vt_ones, jnp.exp2(st - m).astype(BF16), preferred_element_type=F32)
            ot = part if ot is None else ot + part
        den = ot[D_HEAD:D_HEAD + 1, :]
        if sink_row is not None:
            den = den + jnp.exp2(sink_row - m)
        outs.append(ot[0:D_HEAD, :] * (1.0 / den))
    return outs


def _head_rows(pairs_t, hd):
    return pairs_t[hd // 2][(hd % 2) * D_HEAD:(hd % 2 + 1) * D_HEAD, :]


def _store_heads_t(o_ref, ots):
    for p in range(len(ots) // 2):
        o_ref[:, p * LANES:(p + 1) * LANES] = jnp.concatenate(
            [ots[2 * p], ots[2 * p + 1]], axis=0).astype(o_ref.dtype).T


N_STATE = 2 * ML_HEADS
N_CTX_STATE_OUTS = 7


def _ctx_mixers_kernel(x_ref, mod_ref, wt_ref, b_ref, fb_ref, ng_ref, sink_ref, *rest,
                       first_layer):
    n_scratch = 1 + len(Z_GROUPS)
    w_scr, zqk_ref, zvo_ref, zswa_ref, znat_ref, zg_ref = rest[-n_scratch:]
    mml_ref, mswa_ref, mnat_ref, *state_refs = rest[-(3 + N_CTX_STATE_OUTS) - n_scratch:-n_scratch]
    _inproj_kernel(x_ref, mod_ref, wt_ref, b_ref, zqk_ref, zvo_ref, zswa_ref, znat_ref, zg_ref,
                   w_scr)
    if first_layer:
        for ref in state_refs:
            ref[1:] = jnp.zeros((DEPTH - 1,) + ref.shape[1:], F32)
        state_refs = [ref.at[0] for ref in state_refs]
    sk_ref, sv_ref, nk_ref, nv_ref, c_ref, n_ref, m_ref = state_refs
    scaled_q = lambda ref, hd: ref[:, hd * D_HEAD:(hd + 1) * D_HEAD].astype(BF16)
    swa_vts = [zswa_ref[:, SWA_W + SWA_KV_W:SWA_W + 2 * SWA_KV_W].astype(BF16).T]
    nat_vts = [znat_ref[:, 2 * NAT_W + p * LANES:2 * NAT_W + (p + 1) * LANES].astype(BF16).T
               for p in range(NAT_W // LANES)]
    problems = []
    for kv in range(SWA_KV_HEADS):
        k = zswa_ref[:, SWA_W + kv * D_HEAD:SWA_W + (kv + 1) * D_HEAD]
        sk_ref[kv] = k
        sv_ref[kv] = zswa_ref[:, SWA_W + SWA_KV_W + kv * D_HEAD:SWA_W + SWA_KV_W + (kv + 1) * D_HEAD]
        q = jnp.concatenate([scaled_q(zswa_ref, kv * SWA_GROUP + g) for g in range(SWA_GROUP)],
                            axis=0)
        problems.append((q, [(k.astype(BF16), _with_ones(_head_rows(swa_vts, kv)), None)],
                         sink_ref[kv:kv + 1, :]))
    for hd in range(NAT_HEADS):
        k = znat_ref[:, NAT_W + hd * D_HEAD:NAT_W + (hd + 1) * D_HEAD]
        nk_ref[hd] = k
        nv_ref[hd] = znat_ref[:, 2 * NAT_W + hd * D_HEAD:2 * NAT_W + (hd + 1) * D_HEAD]
        problems.append((scaled_q(znat_ref, hd),
                         [(k.astype(BF16), _with_ones(_head_rows(nat_vts, hd)), None)], None))
    scores, maxima = _attend_scores(problems)

    zero_state = [(None, jnp.zeros((1, D_HEAD), F32), jnp.zeros((1, 1), F32))] * N_STATE
    final = _mlstm_scan(zqk_ref, zvo_ref, zg_ref, fb_ref, ng_ref, mml_ref, SEQ, zero_state)
    for idx, (c_new, n_new, m_new) in enumerate(final):
        _store_state(c_ref, n_ref, m_ref, idx, c_new, n_new, m_new)

    ots = _attend_values(problems, scores, maxima)
    _store_heads_t(mswa_ref, [ot[:, g * SEQ:(g + 1) * SEQ]
                              for ot in ots[:SWA_KV_HEADS] for g in range(SWA_GROUP)])
    _store_heads_t(mnat_ref, ots[SWA_KV_HEADS:])


CTX_STATE_DIMS = ((SWA_KV_HEADS, SEQ, D_HEAD), (SWA_KV_HEADS, SEQ, D_HEAD),
                  (NAT_HEADS, SEQ, D_HEAD), (NAT_HEADS, SEQ, D_HEAD),
                  (N_STATE, D_HEAD, D_HEAD), (N_STATE, D_HEAD), (N_STATE, LANES))


def _ctx_mixers_call(x, mods, w_in_t, b_in, fb_row, norm_g, sink_rows, layer, prev_state):
    rows = x.shape[0]
    row_spec = lambda w: pl.BlockSpec((SEQ, w), lambda b: (b, 0))
    state_dims = CTX_STATE_DIMS
    first_layer = prev_state is None
    if first_layer:
        assert layer == 0
        state_spec = lambda d: pl.BlockSpec((None, DEPTH) + d, lambda b: (b,) + (0,) * (len(d) + 1))
    else:
        state_spec = lambda d: pl.BlockSpec((None, None) + d, lambda b: (b, layer) + (0,) * len(d))
    prev = () if first_layer else tuple(prev_state)
    n_in = 7
    return pl.pallas_call(
        functools.partial(_ctx_mixers_kernel, first_layer=first_layer),
        grid=(BATCH,),
        in_specs=[row_spec(D_MODEL), _mod_spec(layer, lambda b: CTX_MOD_ROW),
                  _layer_spec(w_in_t, layer, single_buffer=True), _layer_spec(b_in, layer),
                  _layer_spec(fb_row, layer), _layer_spec(norm_g, layer),
                  _layer_spec(sink_rows, layer)]
                 + [pl.BlockSpec(memory_space=pl.ANY)] * len(prev),
        out_specs=[row_spec(ML_W), row_spec(SWA_W), row_spec(NAT_W)]
                  + [state_spec(d) for d in state_dims],
        out_shape=[jax.ShapeDtypeStruct((rows, ML_W), BF16),
                   jax.ShapeDtypeStruct((rows, SWA_W), BF16),
                   jax.ShapeDtypeStruct((rows, NAT_W), BF16)]
                  + [jax.ShapeDtypeStruct((BATCH, DEPTH) + d, F32) for d in state_dims],
        input_output_aliases={n_in + i: 3 + i for i in range(len(prev))},
        scratch_shapes=[pltpu.VMEM((Z_W, D_MODEL), BF16)]
                       + [pltpu.VMEM((SEQ, w), dt) for w, dt in zip(Z_GROUPS, CTX_Z_DTYPES)],
        compiler_params=_params(1),
        name="ctx_mixers",
    )(x, mods, w_in_t, b_in, fb_row, norm_g, sink_rows, *prev)


def _lat_mlstm_kernel(zqk_ref, zvo_ref, zg_ref, fb_ref, ng_ref, c0_ref, n0_ref, m0_ref, mml_ref):
    state = [(c0_ref[idx], n0_ref[idx:idx + 1, :], m0_ref[idx:idx + 1, :])
             for idx in range(N_STATE)]
    _mlstm_scan(zqk_ref, zvo_ref, zg_ref, fb_ref, ng_ref, mml_ref, DEC_SEQ, state)


def _lat_mlstm_call(z_qk, z_vo, z_g, fb_row, norm_g, state_c, state_n, state_m, layer):
    n_st = N_STATE
    row_spec = lambda w: pl.BlockSpec((DEC_SEQ, w), lambda b: (b, 0))
    return pl.pallas_call(
        _lat_mlstm_kernel,
        grid=(DEC_BATCH,),
        in_specs=[row_spec(Z_QK_W), row_spec(Z_VO_W), row_spec(Z_G_W),
                  _layer_spec(fb_row, layer), _layer_spec(norm_g, layer),
                  pl.BlockSpec((None, None, n_st, D_HEAD, D_HEAD), lambda b: (b, layer, 0, 0, 0)),
                  pl.BlockSpec((None, None, n_st, D_HEAD), lambda b: (b, layer, 0, 0)),
                  pl.BlockSpec((None, None, n_st, 1), lambda b: (b, layer, 0, 0))],
        out_specs=row_spec(ML_W),
        out_shape=jax.ShapeDtypeStruct((DEC_BATCH * DEC_SEQ, ML_W), BF16),
        compiler_params=_params(1),
        name="lat_mlstm",
    )(z_qk, z_vo, z_g, fb_row, norm_g, state_c, state_n, state_m)


def _lat_swa_kernel(zswa_ref, ckt_ref, cvt_ref, sink_ref, mswa_ref,
                    vt_scr, ot_scr, bias_scr, ck_scr, cvt_scr):
    vt_scr[...] = zswa_ref[:, SWA_W + SWA_KV_W:SWA_W + 2 * SWA_KV_W].T

    bl = SWA_BLOCK
    nb = DEC_SEQ // bl
    n_q = SWA_GROUP * bl
    kj = lax.broadcasted_iota(jnp.int32, (3 * bl, n_q), 0)
    qi = lax.broadcasted_iota(jnp.int32, (3 * bl, n_q), 1) & (bl - 1)
    keep = ((kj >= qi) & (kj < 2 * bl)) | ((kj >= 2 * bl) & (kj - 2 * bl <= qi))
    bias_scr[...] = jnp.where(keep, 0.0, NEG_INF)
    ck_scr[...] = jnp.concatenate([ckt_ref[kv] for kv in range(SWA_KV_HEADS)],
                                  axis=0).astype(BF16).T
    for kv in range(SWA_KV_HEADS):
        cvt_scr[kv] = _with_ones(cvt_ref[kv])

    problems, where = [], []
    for kv in range(SWA_KV_HEADS):
        head = slice(kv * D_HEAD, (kv + 1) * D_HEAD)
        for blk in range(nb):
            lo_b, hi_b = max(blk - 1, 0), min(blk + 2, nb)
            keys = slice(lo_b * bl, hi_b * bl)
            q = jnp.concatenate(
                [zswa_ref[blk * bl:(blk + 1) * bl,
                          (kv * SWA_GROUP + g) * D_HEAD:(kv * SWA_GROUP + g + 1) * D_HEAD]
                 for g in range(SWA_GROUP)], axis=0)
            parts = [(zswa_ref[keys, SWA_W + kv * D_HEAD:SWA_W + (kv + 1) * D_HEAD],
                      _with_ones(vt_scr[head, keys]),
                      bias_scr[(lo_b - blk + 1) * bl:(hi_b - blk + 1) * bl, :]),
                     (ck_scr[:, head], cvt_scr[kv], None)]
            problems.append((q, parts, sink_ref[kv:kv + 1, :]))
            where.append((kv, blk))
    for (kv, blk), ot in zip(where, _attend_many(problems)):
        o = ot.astype(BF16)
        for g in range(SWA_GROUP):
            hq = kv * SWA_GROUP + g
            ot_scr[hq * D_HEAD:(hq + 1) * D_HEAD, blk * bl:(blk + 1) * bl] = (
                o[:, g * bl:(g + 1) * bl])
    for p in range(SWA_W // LANES):
        mswa_ref[:, p * LANES:(p + 1) * LANES] = ot_scr[p * LANES:(p + 1) * LANES, :].T


def _lat_swa_call(z_swa, cache_k, cache_v, sink_rows, layer):
    row_spec = lambda w: pl.BlockSpec((DEC_SEQ, w), lambda b: (b, 0))
    cache_spec = pl.BlockSpec((None, None, SWA_KV_HEADS, D_HEAD, PAST_LEN),
                              lambda b: (b, layer, 0, 0, 0))
    n_q = SWA_GROUP * SWA_BLOCK
    return pl.pallas_call(
        _lat_swa_kernel,
        grid=(DEC_BATCH,),
        in_specs=[row_spec(Z_SWA_W), cache_spec, cache_spec, _layer_spec(sink_rows, layer)],
        out_specs=row_spec(SWA_W),
        out_shape=jax.ShapeDtypeStruct((DEC_BATCH * DEC_SEQ, SWA_W), BF16),
        scratch_shapes=[pltpu.VMEM((SWA_KV_W, DEC_SEQ), BF16), pltpu.VMEM((SWA_W, DEC_SEQ), BF16),
                        pltpu.VMEM((3 * SWA_BLOCK, n_q), F32),
                        pltpu.VMEM((PAST_LEN, SWA_KV_W), BF16),
                        pltpu.VMEM((SWA_KV_HEADS, D_HEAD + ONES_ROWS, PAST_LEN), BF16)],
        compiler_params=_params(1),
        name="lat_swa",
    )(z_swa, cache_k, cache_v, sink_rows)


NAT_ROWS = DEC_SEQ // GRID_W
NAT_GROUP_ROWS = 4
NAT_GROUP_Q = NAT_GROUP_ROWS * GRID_W
NAT_RPB_ROWS = 2 * NAT_KH


def _nat_row_start(r):
    return min(max(r - NAT_KH // 2, 0), NAT_ROWS - NAT_KH)


def _nat_groups():
    groups, off = [], 0
    for g in range(NAT_ROWS // NAT_GROUP_ROWS):
        w0 = _nat_row_start(g * NAT_GROUP_ROWS)
        w1 = _nat_row_start((g + 1) * NAT_GROUP_ROWS - 1) + NAT_KH
        n_rows = w1 - w0 + (w1 - w0) % 2
        w0 = min(w0, NAT_ROWS - n_rows)
        groups.append((w0, n_rows, off))
        off += n_rows * GRID_W
    return groups, off


NAT_GROUPS, NAT_BIAS_KEYS = _nat_groups()


def _nat_build_bias(rpb_ref, bias_scr):
    shape = (GRID_W, LANES)
    q = lax.broadcasted_iota(jnp.int32, shape, 0)
    lane = lax.broadcasted_iota(jnp.int32, shape, 1)
    kc = lane & (GRID_W - 1)
    cs = jnp.clip(q - NAT_KW // 2, 0, GRID_W - NAT_KW)
    ok = (kc >= cs) & (kc < cs + NAT_KW)
    ok_lo = ok & (lane < GRID_W)
    ok_hi = ok & (lane >= GRID_W)
    neg_tile = jnp.full(shape, NEG_INF, F32)
    for hd in range(rpb_ref.shape[0]):
        lo, hi = [], []
        for ro in range(2 * NAT_KH - 1):
            x = jnp.broadcast_to(rpb_ref[hd, ro:ro + 1, :], shape)
            lo.append(jnp.where(ok_lo, pltpu.roll(x, 0, 1, stride=1, stride_axis=0), NEG_INF))
            hi.append(jnp.where(ok_hi, pltpu.roll(x, GRID_W, 1, stride=1, stride_axis=0), NEG_INF))
        for g, (w0, n_rows, off) in enumerate(NAT_GROUPS):
            for jj in range(n_rows // 2):
                kra, krb = w0 + 2 * jj, w0 + 2 * jj + 1
                tiles = []
                for rr in range(NAT_GROUP_ROWS):
                    r = g * NAT_GROUP_ROWS + rr
                    r0 = _nat_row_start(r)
                    parts = []
                    if r0 <= kra < r0 + NAT_KH:
                        parts.append(lo[kra - r + NAT_KH - 1])
                    if r0 <= krb < r0 + NAT_KH:
                        parts.append(hi[krb - r + NAT_KH - 1])
                    tiles.append(neg_tile if not parts else (
                        parts[0] if len(parts) == 1 else jnp.maximum(parts[0], parts[1])))
                bias_scr[hd, off + jj * LANES:off + (jj + 1) * LANES, :] = (
                    jnp.concatenate(tiles, axis=0).T)


def _lat_nat_kernel(q_ref, k_ref, v_ref, ckt_ref, cvt_ref, rpb_ref, mnat_ref, bias_scr):
    @pl.when(pl.program_id(0) == 0)
    def _():
        _nat_build_bias(rpb_ref, bias_scr)

    n_groups = len(NAT_GROUPS)
    for pair in range(NAT_HEADS // 2):
        lanes = slice(pair * LANES, (pair + 1) * LANES)
        vt = v_ref[:, lanes].astype(BF16).T
        ck2 = jnp.concatenate([ckt_ref[2 * pair], ckt_ref[2 * pair + 1]], axis=0).astype(BF16).T
        problems = []
        for sub in range(2):
            hd = 2 * pair + sub
            head = slice(sub * D_HEAD, (sub + 1) * D_HEAD)
            cols = slice(hd * D_HEAD, (hd + 1) * D_HEAD)
            ck = ck2[:, head]
            cvt_ones = _with_ones(cvt_ref[hd])
            for g, (w0, n_rows, off) in enumerate(NAT_GROUPS):
                keys = slice(w0 * GRID_W, (w0 + n_rows) * GRID_W)
                parts = [(k_ref[keys, cols].astype(BF16), _with_ones(vt[head, keys]),
                          bias_scr[hd, off:off + n_rows * GRID_W, :]),
                         (ck, cvt_ones, None)]
                q = q_ref[g * NAT_GROUP_Q:(g + 1) * NAT_GROUP_Q, cols].astype(BF16)
                problems.append((q, parts, None))
        ots = _attend_many(problems)
        mnat_ref[:, lanes] = jnp.concatenate(
            [jnp.concatenate(ots[sub * n_groups:(sub + 1) * n_groups], axis=1)
             for sub in range(2)], axis=0).astype(BF16).T


def _lat_nat_call(z_nat, cache_k, cache_v, rpb_lanes, layer):
    col_spec = lambda j: pl.BlockSpec((DEC_SEQ, NAT_W), lambda b: (b, j))
    cache_spec = pl.BlockSpec((None, None, NAT_HEADS, D_HEAD, PAST_LEN),
                              lambda b: (b, layer, 0, 0, 0))
    return pl.pallas_call(
        _lat_nat_kernel,
        grid=(DEC_BATCH,),
        in_specs=[col_spec(0), col_spec(1), col_spec(2), cache_spec, cache_spec,
                  _layer_spec(rpb_lanes, layer)],
        out_specs=pl.BlockSpec((DEC_SEQ, NAT_W), lambda b: (b, 0)),
        out_shape=jax.ShapeDtypeStruct((DEC_BATCH * DEC_SEQ, NAT_W), BF16),
        scratch_shapes=[pltpu.VMEM((NAT_HEADS, NAT_BIAS_KEYS, NAT_GROUP_Q), F32)],
        compiler_params=_params(1),
        name="lat_nat",
    )(z_nat, z_nat, z_nat, cache_k, cache_v, rpb_lanes)


def _nat_rpb_lanes(rpb):
    n_off = 2 * NAT_KW - 1
    padded = jnp.concatenate(
        [rpb.astype(F32), jnp.zeros(rpb.shape[:-1] + (LANES - n_off,), F32)], axis=-1)
    rolled = jnp.roll(padded, -(NAT_KW - 1), axis=-1)
    pad_rows = jnp.zeros(rpb.shape[:-2] + (NAT_RPB_ROWS - rpb.shape[-2], LANES), F32)
    return jnp.concatenate([rolled, pad_rows], axis=-2)


def _rope_tables():
    t = np.arange(DEC_SEQ)[:, None]
    d = np.arange(LANES)[None, :] % D_HEAD
    pos = np.where(d < D_HEAD // 2, t // GRID_W, t % GRID_W).astype(np.float64)
    quarter = D_HEAD // 4
    freq = ROPE_BASE ** (-(d % quarter).astype(np.float64) / quarter)
    ang = (pos.astype(np.float32) * freq.astype(np.float32)).astype(np.float32)
    sign = np.where((d % (2 * quarter)) < quarter, -1.0, 1.0)
    return (jnp.asarray(np.cos(ang), dtype=F32), jnp.asarray(np.sin(ang) * sign, dtype=F32))


def _permute_in_columns(a):
    g0 = Z_ML_W
    s0 = g0 + N_GATES
    pad = jnp.zeros(a.shape[:-1] + (Z_G_W - N_GATES,), a.dtype)
    return jnp.concatenate([a[..., :g0], a[..., s0:], a[..., g0:s0], pad], axis=-1)


def kernel(x_prompt, x_sample, cache_swa_k, cache_swa_v, cache_nat_k, cache_nat_v, state_mlstm_C,
           state_mlstm_n, state_mlstm_m, c, c_ctx, w_ada, b_ada, w_in, b_in, mlstm_fbias,
           mlstm_norm_g, swa_sink, nat_rpb, w_out, ln1_g, ln1_b, w_mlp1, w_mlp2, ln2_g, ln2_b):
    xp = x_prompt.reshape(BATCH * SEQ, D_MODEL)
    xs = x_sample.reshape(DEC_BATCH * DEC_SEQ, D_MODEL)
    cvec = jnp.concatenate(
        [c, c_ctx[None, :], jnp.zeros((MOD_ROWS - DEC_BATCH - 1, D_MODEL), F32)], axis=0)
    mods = _mods_call(cvec, w_ada, b_ada).reshape(DEPTH * MOD_ROWS, 6, D_MODEL)

    w_in_t = jnp.swapaxes(w_in, 1, 2)
    out_scale = np.ones((Z_W,), np.float32)
    out_scale[ML_W:2 * ML_W] = ATT_SCALE
    out_scale[Z_ML_W:Z_ML_W + SWA_W] = Q_SCALE2
    out_scale[Z_ML_W + Z_SWA_W:Z_ML_W + Z_SWA_W + NAT_W] = Q_SCALE2
    b_in_p = jnp.stack([jnp.broadcast_to(out_scale, (DEPTH, Z_W)),
                        _permute_in_columns(b_in) * out_scale], axis=1)
    cache_swa_k, cache_swa_v, cache_nat_k, cache_nat_v = (
        jnp.swapaxes(a, -1, -2) for a in (cache_swa_k, cache_swa_v, cache_nat_k, cache_nat_v))
    vec = lambda a: a.reshape(DEPTH, 1, D_MODEL)
    tail_w = (w_out.astype(BF16), vec(ln1_g), vec(ln1_b), w_mlp1.astype(BF16),
              w_mlp2.astype(BF16), vec(ln2_g), vec(ln2_b))
    cos_t, sin_t = _rope_tables()
    rpb_lanes = _nat_rpb_lanes(nat_rpb) * LOG2E
    swa_sink = swa_sink * LOG2E
    fb_rows = jnp.zeros((DEPTH, 1, Z_G_W), F32)
    fb_rows = fb_rows.at[:, 0, ML_HEADS:2 * ML_HEADS].set(mlstm_fbias[:, 0])
    fb_rows = fb_rows.at[:, 0, 3 * ML_HEADS:4 * ML_HEADS].set(mlstm_fbias[:, 1])
    norm_g = mlstm_norm_g.reshape(DEPTH, 1, ML_W)
    sink_ctx = jnp.repeat(swa_sink, SEQ, axis=1).reshape(DEPTH, SWA_KV_HEADS, SWA_GROUP * SEQ)
    sink_lat = jnp.repeat(swa_sink, SWA_BLOCK, axis=1).reshape(
        DEPTH, SWA_KV_HEADS, SWA_GROUP * SWA_BLOCK)

    n_st = 2 * ML_HEADS
    state_c = state_mlstm_C.reshape(DEC_BATCH, DEPTH, n_st, D_HEAD, D_HEAD)
    state_n = state_mlstm_n.reshape(DEC_BATCH, DEPTH, n_st, D_HEAD)
    state_m = state_mlstm_m.reshape(DEC_BATCH, DEPTH, n_st, 1)

    ctx_row = lambda i: CTX_MOD_ROW
    lat_row = lambda i: i // (DEC_SEQ // TAIL_TILE)

    ctx_state = None
    for l in range(DEPTH):
        m_ml, m_swa, m_nat, *ctx_state = _ctx_mixers_call(
            xp, mods, w_in_t, b_in_p, fb_rows, norm_g, sink_ctx, l, ctx_state)
        xp = _tail_call(xp, m_ml, m_swa, m_nat, mods, ctx_row, tail_w, l)

        z_qk, z_vo, z_swa, z_nat, z_g = _lat_inproj_call(xs, mods, w_in_t, b_in_p, cos_t, sin_t, l)
        m_ml = _lat_mlstm_call(z_qk, z_vo, z_g, fb_rows, norm_g, state_c, state_n, state_m, l)
        m_swa = _lat_swa_call(z_swa, cache_swa_k, cache_swa_v, sink_lat, l)
        m_nat = _lat_nat_call(z_nat, cache_nat_k, cache_nat_v, rpb_lanes, l)
        xs = _tail_call(xs, m_ml, m_swa, m_nat, mods, lat_row, tail_w, l)

    k_swa, v_swa, k_nat, v_nat, st_c, st_n, st_m = ctx_state
    return (xp.reshape(BATCH, SEQ, D_MODEL), xs.reshape(DEC_BATCH, DEC_SEQ, D_MODEL),
            k_swa, v_swa, k_nat, v_nat,
            st_c.reshape(BATCH, DEPTH, 2, ML_HEADS, D_HEAD, D_HEAD),
            st_n.reshape(BATCH, DEPTH, 2, ML_HEADS, D_HEAD),
            st_m[..., 0].reshape(BATCH, DEPTH, 2, ML_HEADS))
```
